```python
import math
import jax, jax.numpy as jnp
from jax import lax
import numpy as np

D_MODEL = 1024
BATCH = 32
SEQ = 2048
DEPTH = 4

N_MIXERS = 3
D_FF = 4 * D_MODEL
D_RNN = D_MODEL
HA = 8
HDA = D_RNN // HA
CONV_W = 4
RG_C = 8.0
SGU_CHUNK = 128
D_SGU = D_MODEL
GB = 8
DGB = D_SGU // GB
HC = 8
DK = 128
DV = 128
DC = HC * DV
GDN_CHUNK = 64

kernel_name = "hybrid_rglru_sgu_gdn_encoder"

F32 = jnp.float32


def rms_norm(x, g, eps=1e-6):
    xf = x.astype(F32)
    y = xf * lax.rsqrt(jnp.mean(xf * xf, axis=-1, keepdims=True) + eps)
    return (y * g.astype(F32)).astype(x.dtype)


def layer_norm(x, g, b, eps=1e-5):
    xf = x.astype(F32)
    mu = jnp.mean(xf, axis=-1, keepdims=True)
    var = jnp.mean(jnp.square(xf - mu), axis=-1, keepdims=True)
    return ((xf - mu) * lax.rsqrt(var + eps) * g.astype(F32) + b.astype(F32)).astype(x.dtype)


def l2_normalize(x, eps=1e-6):
    xf = x.astype(F32)
    return xf * lax.rsqrt(jnp.sum(xf * xf, axis=-1, keepdims=True) + eps)


def centred_dwconv(x, w):
    K, C = w.shape
    left = K // 2
    return lax.conv_general_dilated(x, w[:, None, :].astype(x.dtype), window_strides=(1,),
                                    padding=[(left, K - 1 - left)],
                                    dimension_numbers=('NWC', 'WIO', 'NWC'),
                                    feature_group_count=C)


def linear_scan(a, b, reverse):
    def combine(c1, c2):
        a1, b1 = c1
        a2, b2 = c2
        return a1 * a2, a2 * b1 + b2
    _, h = lax.associative_scan(combine, (a, b), axis=1, reverse=reverse)
    return h


def rglru_direction(xr, gate_w, gate_b, lam, reverse):
    Bsz, S, _ = xr.shape
    xh = xr.reshape(Bsz, S, HA, HDA)
    gates = jnp.einsum('bshi,ghij->gbshj', xh, gate_w) + gate_b[:, None, None]
    gates = jax.nn.sigmoid(gates.astype(F32)).reshape(2, Bsz, S, D_RNN)
    r, ig = gates[0], gates[1]
    log_a = -RG_C * r * jax.nn.softplus(-lam.astype(F32))
    a = jnp.exp(log_a)
    mult = jnp.sqrt(-jnp.expm1(2.0 * log_a))
    bx = mult * ig * xr.astype(F32)
    return linear_scan(a, bx, reverse)


def mixer_rglru(h, w_in, conv_w, conv_b, gate_w, gate_b, lam, w_out):
    z = h @ w_in
    gate, xr = z[..., :D_RNN], z[..., D_RNN:]
    xr = centred_dwconv(xr, conv_w) + conv_b
    y = (rglru_direction(xr, gate_w[0], gate_b[0], lam[0], reverse=False)
         + rglru_direction(xr, gate_w[1], gate_b[1], lam[1], reverse=True))
    y = y.astype(h.dtype) * jax.nn.gelu(gate)
    return y @ w_out


def mixer_sgu(h, w_in, ln_g, ln_b, w_s, b_s, w_out):
    Bsz, S, _ = h.shape
    n_chunks = S // SGU_CHUNK
    z = jax.nn.gelu(h @ w_in)
    u, v = z[..., :D_SGU], z[..., D_SGU:]
    v = layer_norm(v, ln_g, ln_b).reshape(Bsz, n_chunks, SGU_CHUNK, GB, DGB)
    vs = jnp.einsum('gpq,bnqgc->bnpgc', w_s, v) + b_s.T[None, None, :, :, None]
    y = u * vs.reshape(Bsz, S, D_SGU)
    return y @ w_out


def chunk_gated_delta(q, k, v, g, beta):
    Bsz, S, H, _ = q.shape
    C = GDN_CHUNK
    N = S // C
    ch = lambda t: t.reshape(Bsz, N, C, H, t.shape[-1]).transpose(0, 3, 1, 2, 4).astype(F32)
    chs = lambda t: t.reshape(Bsz, N, C, H).transpose(0, 3, 1, 2).astype(F32)
    q, k, v = ch(q), ch(k), ch(v)
    g, beta = chs(g), chs(beta)
    gc = jnp.cumsum(g, axis=-1)
    tril = jnp.tril(jnp.ones((C, C), bool))
    strict = jnp.tril(jnp.ones((C, C), bool), -1)
    diff = gc[..., :, None] - gc[..., None, :]
    decay = jnp.where(tril, jnp.exp(jnp.where(tril, diff, 0.0)), 0.0)
    k_beta = k * beta[..., None]
    v_beta = v * beta[..., None]
    A = jnp.where(strict, jnp.einsum('bhnid,bhnjd->bhnij', k_beta, k) * decay, 0.0)
    eye = jnp.eye(C, dtype=F32)
    T = lax.linalg.triangular_solve(eye + A, jnp.broadcast_to(eye, A.shape), left_side=True,
                                    lower=True, unit_diagonal=True)
    u = jnp.einsum('bhnij,bhnjd->bhnid', T, v_beta)
    w = jnp.einsum('bhnij,bhnjd->bhnid', T, k_beta * jnp.exp(gc)[..., None])
    qk = jnp.einsum('bhnid,bhnjd->bhnij', q, k) * decay

    def step(state, xs):
        q_c, k_c, u_c, w_c, qk_c, gc_c = xs
        v_new = u_c - jnp.einsum('bhck,bhkv->bhcv', w_c, state)
        o = (jnp.einsum('bhck,bhkv->bhcv', q_c * jnp.exp(gc_c)[..., None], state)
             + jnp.einsum('bhij,bhjv->bhiv', qk_c, v_new))
        g_last = gc_c[..., -1]
        state = (state * jnp.exp(g_last)[..., None, None]
                 + jnp.einsum('bhck,bhcv->bhkv', k_c * jnp.exp(g_last[..., None] - gc_c)[..., None], v_new))
        return state, o

    mv = lambda t: jnp.moveaxis(t, 2, 0)
    state0 = jnp.zeros((Bsz, H, q.shape[-1], v.shape[-1]), F32)
    _, o = lax.scan(step, state0, (mv(q), mv(k), mv(u), mv(w), mv(qk), mv(gc)))
    return o.transpose(1, 0, 3, 2, 4).reshape(Bsz, S, H, v.shape[-1])


def mixer_gdn(h, w_in, conv_w, a_log, dt_bias, norm_g, w_out):
    Bsz, S, _ = h.shape
    z = h @ w_in
    qkv = jax.nn.silu(centred_dwconv(z[..., :3 * DC], conv_w))
    gate = z[..., 3 * DC:4 * DC].reshape(Bsz, S, HC, DV)
    a_logit = z[..., 4 * DC:4 * DC + 2 * HC].reshape(Bsz, S, 2, HC).astype(F32)
    b_logit = z[..., 4 * DC + 2 * HC:].reshape(Bsz, S, 2, HC).astype(F32)
    q = l2_normalize(qkv[..., :DC].reshape(Bsz, S, HC, DK)) * (DK ** -0.5)
    k = l2_normalize(qkv[..., DC:2 * DC].reshape(Bsz, S, HC, DK))
    v = qkv[..., 2 * DC:].reshape(Bsz, S, HC, DV)
    g = -jnp.exp(a_log.astype(F32)) * jax.nn.softplus(a_logit + dt_bias.astype(F32))
    beta = jax.nn.sigmoid(b_logit)
    o_f = chunk_gated_delta(q, k, v, g[:, :, 0], beta[:, :, 0])
    fl = lambda t: jnp.flip(t, axis=1)
    o_b = fl(chunk_gated_delta(fl(q), fl(k), fl(v), fl(g[:, :, 1]), fl(beta[:, :, 1])))
    o = (o_f + o_b).astype(h.dtype)
    o = rms_norm(o, norm_g) * jax.nn.silu(gate)
    return o.reshape(Bsz, S, DC) @ w_out


def sqrelu_mlp(h, w_up, w_down):
    return jnp.square(jax.nn.relu(h @ w_up)) @ w_down


def _fwd_setup_inputs(seed: int = 0) -> dict:
    key = jax.random.key(seed)
    ks = jax.random.split(key, 32)
    nA = (DEPTH + 2) // 3
    nB = (DEPTH + 1) // 3
    nC = DEPTH // 3
    nrm = lambda k, shape, scale: jax.random.normal(k, shape, F32) * scale
    gain = lambda k, shape: 1.0 + 0.02 * jax.random.normal(k, shape, F32)

    x = nrm(ks[0], (BATCH, SEQ, D_MODEL), 1.0)
    norm_mix_g = gain(ks[1], (DEPTH, D_MODEL))
    norm_mlp_g = gain(ks[2], (DEPTH, D_MODEL))
    mlp_w_up = nrm(ks[3], (DEPTH, D_MODEL, D_FF), D_MODEL ** -0.5)
    mlp_w_down = nrm(ks[4], (DEPTH, D_FF, D_MODEL), D_FF ** -0.5)
    norm_final_g = gain(ks[5], (D_MODEL,))

    a_w_in = nrm(ks[6], (nA, D_MODEL, 2 * D_RNN), D_MODEL ** -0.5)
    a_conv_w = nrm(ks[7], (nA, CONV_W, D_RNN), CONV_W ** -0.5)
    a_conv_b = nrm(ks[8], (nA, D_RNN), 0.02)
    a_gate_w = nrm(ks[9], (nA, 2, 2, HA, HDA, HDA), HDA ** -0.5)
    a_gate_b = nrm(ks[10], (nA, 2, 2, HA, HDA), 0.02)
    a0 = jax.random.uniform(ks[11], (nA, 2, D_RNN), F32, minval=0.9, maxval=0.999)
    a_base = a0 ** (1.0 / RG_C)
    a_lambda = jnp.log(a_base) - jnp.log1p(-a_base)
    a_w_out = nrm(ks[12], (nA, D_RNN, D_MODEL), D_RNN ** -0.5)

    b_w_in = nrm(ks[13], (nB, D_MODEL, 2 * D_SGU), D_MODEL ** -0.5)
    b_ln_g = gain(ks[14], (nB, D_SGU))
    b_ln_b = nrm(ks[15], (nB, D_SGU), 0.02)
    b_w_s = nrm(ks[16], (nB, GB, SGU_CHUNK, SGU_CHUNK), SGU_CHUNK ** -0.5)
    b_b_s = gain(ks[17], (nB, GB, SGU_CHUNK))
    b_w_out = nrm(ks[18], (nB, D_SGU, D_MODEL), D_SGU ** -0.5)

    c_w_in = nrm(ks[19], (nC, D_MODEL, 4 * DC + 4 * HC), D_MODEL ** -0.5)
    c_conv_w = nrm(ks[20], (nC, CONV_W, 3 * DC), CONV_W ** -0.5)
    c_a_log = jnp.log(jax.random.uniform(ks[21], (nC, 2, HC), F32, minval=1.0, maxval=16.0))
    dt = jnp.exp(jax.random.uniform(ks[22], (nC, 2, HC), F32, minval=math.log(1e-3), maxval=math.log(1e-1)))
    c_dt_bias = dt + jnp.log(-jnp.expm1(-dt))
    c_norm_g = gain(ks[23], (nC, DV))
    c_w_out = nrm(ks[24], (nC, DC, D_MODEL), DC ** -0.5)

    return {"x": x, "norm_mix_g": norm_mix_g, "norm_mlp_g": norm_mlp_g,
            "mlp_w_up": mlp_w_up, "mlp_w_down": mlp_w_down, "norm_final_g": norm_final_g,
            "a_w_in": a_w_in, "a_conv_w": a_conv_w, "a_conv_b": a_conv_b,
            "a_gate_w": a_gate_w, "a_gate_b": a_gate_b, "a_lambda": a_lambda, "a_w_out": a_w_out,
            "b_w_in": b_w_in, "b_ln_g": b_ln_g, "b_ln_b": b_ln_b, "b_w_s": b_w_s,
            "b_b_s": b_b_s, "b_w_out": b_w_out,
            "c_w_in": c_w_in, "c_conv_w": c_conv_w, "c_a_log": c_a_log, "c_dt_bias": c_dt_bias,
            "c_norm_g": c_norm_g, "c_w_out": c_w_out}


def _fwd_reference(x, norm_mix_g, norm_mlp_g, mlp_w_up, mlp_w_down, norm_final_g,
              a_w_in, a_conv_w, a_conv_b, a_gate_w, a_gate_b, a_lambda, a_w_out,
              b_w_in, b_ln_g, b_ln_b, b_w_s, b_b_s, b_w_out,
              c_w_in, c_conv_w, c_a_log, c_dt_bias, c_norm_g, c_w_out):
    for i in range(DEPTH):
        kind, j = i % N_MIXERS, i // N_MIXERS
        hn = rms_norm(x, norm_mix_g[i])
        if kind == 0:
            m = mixer_rglru(hn, a_w_in[j], a_conv_w[j], a_conv_b[j], a_gate_w[j], a_gate_b[j],
                            a_lambda[j], a_w_out[j])
        elif kind == 1:
            m = mixer_sgu(hn, b_w_in[j], b_ln_g[j], b_ln_b[j], b_w_s[j], b_b_s[j], b_w_out[j])
        else:
            m = mixer_gdn(hn, c_w_in[j], c_conv_w[j], c_a_log[j], c_dt_bias[j], c_norm_g[j], c_w_out[j])
        x = x + m
        x = x + sqrelu_mlp(rms_norm(x, norm_mlp_g[i]), mlp_w_up[i], mlp_w_down[i])
    return rms_norm(x, norm_final_g)


import jax as _jax
import jax.numpy as _jnp

TWIN_FORMAT = 'train_step'
FWD_PARAMS = ['x', 'norm_mix_g', 'norm_mlp_g', 'mlp_w_up', 'mlp_w_down', 'norm_final_g', 'a_w_in', 'a_conv_w', 'a_conv_b', 'a_gate_w', 'a_gate_b', 'a_lambda', 'a_w_out', 'b_w_in', 'b_ln_g', 'b_ln_b', 'b_w_s', 'b_b_s', 'b_w_out', 'c_w_in', 'c_conv_w', 'c_a_log', 'c_dt_bias', 'c_norm_g', 'c_w_out']
TWIN_WEIGHTS = ['norm_mix_g', 'norm_mlp_g', 'mlp_w_up', 'mlp_w_down', 'norm_final_g', 'a_w_in', 'a_conv_w', 'a_conv_b', 'a_gate_w', 'a_gate_b', 'a_lambda', 'a_w_out', 'b_w_in', 'b_ln_g', 'b_ln_b', 'b_w_s', 'b_b_s', 'b_w_out', 'c_w_in', 'c_conv_w', 'c_a_log', 'c_dt_bias', 'c_norm_g', 'c_w_out']
TWIN_DIFF_INPUT = 'x'
TWIN_INPUTS = ['x', 'norm_mix_g', 'norm_mlp_g', 'mlp_w_up', 'mlp_w_down', 'norm_final_g', 'a_w_in', 'a_conv_w', 'a_conv_b', 'a_gate_w', 'a_gate_b', 'a_lambda', 'a_w_out', 'b_w_in', 'b_ln_g', 'b_ln_b', 'b_w_s', 'b_b_s', 'b_w_out', 'c_w_in', 'c_conv_w', 'c_a_log', 'c_dt_bias', 'c_norm_g', 'c_w_out', 'loss_target', 'm_norm_mix_g', 'm_norm_mlp_g', 'm_mlp_w_up', 'm_mlp_w_down', 'm_norm_final_g', 'm_a_w_in', 'm_a_conv_w', 'm_a_conv_b', 'm_a_gate_w', 'm_a_gate_b', 'm_a_lambda', 'm_a_w_out', 'm_b_w_in', 'm_b_ln_g', 'm_b_ln_b', 'm_b_w_s', 'm_b_b_s', 'm_b_w_out', 'm_c_w_in', 'm_c_conv_w', 'm_c_a_log', 'm_c_dt_bias', 'm_c_norm_g', 'm_c_w_out', 'v_norm_mix_g', 'v_norm_mlp_g', 'v_mlp_w_up', 'v_mlp_w_down', 'v_norm_final_g', 'v_a_w_in', 'v_a_conv_w', 'v_a_conv_b', 'v_a_gate_w', 'v_a_gate_b', 'v_a_lambda', 'v_a_w_out', 'v_b_w_in', 'v_b_ln_g', 'v_b_ln_b', 'v_b_w_s', 'v_b_b_s', 'v_b_w_out', 'v_c_w_in', 'v_c_conv_w', 'v_c_a_log', 'v_c_dt_bias', 'v_c_norm_g', 'v_c_w_out']
TWIN_OUTPUTS = ['loss', 'grad_x', 'grad_norm_mix_g', 'grad_norm_mlp_g', 'grad_mlp_w_up', 'grad_mlp_w_down', 'grad_norm_final_g', 'grad_a_w_in', 'grad_a_conv_w', 'grad_a_conv_b', 'grad_a_gate_w', 'grad_a_gate_b', 'grad_a_lambda', 'grad_a_w_out', 'grad_b_w_in', 'grad_b_ln_g', 'grad_b_ln_b', 'grad_b_w_s', 'grad_b_b_s', 'grad_b_w_out', 'grad_c_w_in', 'grad_c_conv_w', 'grad_c_a_log', 'grad_c_dt_bias', 'grad_c_norm_g', 'grad_c_w_out', 'delta_norm_mix_g', 'delta_norm_mlp_g', 'delta_mlp_w_up', 'delta_mlp_w_down', 'delta_norm_final_g', 'delta_a_w_in', 'delta_a_conv_w', 'delta_a_conv_b', 'delta_a_gate_w', 'delta_a_gate_b', 'delta_a_lambda', 'delta_a_w_out', 'delta_b_w_in', 'delta_b_ln_g', 'delta_b_ln_b', 'delta_b_w_s', 'delta_b_b_s', 'delta_b_w_out', 'delta_c_w_in', 'delta_c_conv_w', 'delta_c_a_log', 'delta_c_dt_bias', 'delta_c_norm_g', 'delta_c_w_out', 'new_m_norm_mix_g', 'new_m_norm_mlp_g', 'new_m_mlp_w_up', 'new_m_mlp_w_down', 'new_m_norm_final_g', 'new_m_a_w_in', 'new_m_a_conv_w', 'new_m_a_conv_b', 'new_m_a_gate_w', 'new_m_a_gate_b', 'new_m_a_lambda', 'new_m_a_w_out', 'new_m_b_w_in', 'new_m_b_ln_g', 'new_m_b_ln_b', 'new_m_b_w_s', 'new_m_b_b_s', 'new_m_b_w_out', 'new_m_c_w_in', 'new_m_c_conv_w', 'new_m_c_a_log', 'new_m_c_dt_bias', 'new_m_c_norm_g', 'new_m_c_w_out', 'new_v_norm_mix_g', 'new_v_norm_mlp_g', 'new_v_mlp_w_up', 'new_v_mlp_w_down', 'new_v_norm_final_g', 'new_v_a_w_in', 'new_v_a_conv_w', 'new_v_a_conv_b', 'new_v_a_gate_w', 'new_v_a_gate_b', 'new_v_a_lambda', 'new_v_a_w_out', 'new_v_b_w_in', 'new_v_b_ln_g', 'new_v_b_ln_b', 'new_v_b_w_s', 'new_v_b_b_s', 'new_v_b_w_out', 'new_v_c_w_in', 'new_v_c_conv_w', 'new_v_c_a_log', 'new_v_c_dt_bias', 'new_v_c_norm_g', 'new_v_c_w_out']
TWIN_LEAF_KINDS = {'loss': 'loss', 'grad_x': 'grad_x', 'grad_norm_mix_g': 'grad_w', 'grad_norm_mlp_g': 'grad_w', 'grad_mlp_w_up': 'grad_w', 'grad_mlp_w_down': 'grad_w', 'grad_norm_final_g': 'grad_w', 'grad_a_w_in': 'grad_w', 'grad_a_conv_w': 'grad_w', 'grad_a_conv_b': 'grad_w', 'grad_a_gate_w': 'grad_w', 'grad_a_gate_b': 'grad_w', 'grad_a_lambda': 'grad_w', 'grad_a_w_out': 'grad_w', 'grad_b_w_in': 'grad_w', 'grad_b_ln_g': 'grad_w', 'grad_b_ln_b': 'grad_w', 'grad_b_w_s': 'grad_w', 'grad_b_b_s': 'grad_w', 'grad_b_w_out': 'grad_w', 'grad_c_w_in': 'grad_w', 'grad_c_conv_w': 'grad_w', 'grad_c_a_log': 'grad_w', 'grad_c_dt_bias': 'grad_w', 'grad_c_norm_g': 'grad_w', 'grad_c_w_out': 'grad_w', 'delta_norm_mix_g': 'delta_w', 'delta_norm_mlp_g': 'delta_w', 'delta_mlp_w_up': 'delta_w', 'delta_mlp_w_down': 'delta_w', 'delta_norm_final_g': 'delta_w', 'delta_a_w_in': 'delta_w', 'delta_a_conv_w': 'delta_w', 'delta_a_conv_b': 'delta_w', 'delta_a_gate_w': 'delta_w', 'delta_a_gate_b': 'delta_w', 'delta_a_lambda': 'delta_w', 'delta_a_w_out': 'delta_w', 'delta_b_w_in': 'delta_w', 'delta_b_ln_g': 'delta_w', 'delta_b_ln_b': 'delta_w', 'delta_b_w_s': 'delta_w', 'delta_b_b_s': 'delta_w', 'delta_b_w_out': 'delta_w', 'delta_c_w_in': 'delta_w', 'delta_c_conv_w': 'delta_w', 'delta_c_a_log': 'delta_w', 'delta_c_dt_bias': 'delta_w', 'delta_c_norm_g': 'delta_w', 'delta_c_w_out': 'delta_w', 'new_m_norm_mix_g': 'new_m', 'new_m_norm_mlp_g': 'new_m', 'new_m_mlp_w_up': 'new_m', 'new_m_mlp_w_down': 'new_m', 'new_m_norm_final_g': 'new_m', 'new_m_a_w_in': 'new_m', 'new_m_a_conv_w': 'new_m', 'new_m_a_conv_b': 'new_m', 'new_m_a_gate_w': 'new_m', 'new_m_a_gate_b': 'new_m', 'new_m_a_lambda': 'new_m', 'new_m_a_w_out': 'new_m', 'new_m_b_w_in': 'new_m', 'new_m_b_ln_g': 'new_m', 'new_m_b_ln_b': 'new_m', 'new_m_b_w_s': 'new_m', 'new_m_b_b_s': 'new_m', 'new_m_b_w_out': 'new_m', 'new_m_c_w_in': 'new_m', 'new_m_c_conv_w': 'new_m', 'new_m_c_a_log': 'new_m', 'new_m_c_dt_bias': 'new_m', 'new_m_c_norm_g': 'new_m', 'new_m_c_w_out': 'new_m', 'new_v_norm_mix_g': 'new_v', 'new_v_norm_mlp_g': 'new_v', 'new_v_mlp_w_up': 'new_v', 'new_v_mlp_w_down': 'new_v', 'new_v_norm_final_g': 'new_v', 'new_v_a_w_in': 'new_v', 'new_v_a_conv_w': 'new_v', 'new_v_a_conv_b': 'new_v', 'new_v_a_gate_w': 'new_v', 'new_v_a_gate_b': 'new_v', 'new_v_a_lambda': 'new_v', 'new_v_a_w_out': 'new_v', 'new_v_b_w_in': 'new_v', 'new_v_b_ln_g': 'new_v', 'new_v_b_ln_b': 'new_v', 'new_v_b_w_s': 'new_v', 'new_v_b_b_s': 'new_v', 'new_v_b_w_out': 'new_v', 'new_v_c_w_in': 'new_v', 'new_v_c_conv_w': 'new_v', 'new_v_c_a_log': 'new_v', 'new_v_c_dt_bias': 'new_v', 'new_v_c_norm_g': 'new_v', 'new_v_c_w_out': 'new_v'}


def _forward(args):
    return _fwd_reference(*[args[k] for k in FWD_PARAMS])


def _output_shape():
    out = _jax.eval_shape(lambda: _forward(_fwd_setup_inputs(0)))
    return out.shape, out.dtype

N_MICROBATCH = 1
ADAM_LR = 0.001
ADAM_B1 = 0.9
ADAM_B2 = 0.999
ADAM_EPS = 1e-08
ADAM_WD = 0.01
ADAM_STEP = 10
PER_EXAMPLE_BATCH_AXIS = {'x': 0, 'loss_target': 0}
SHARED_INPUTS = []
_WEIGHT_DTYPES = {'norm_mix_g': _jnp.float32, 'norm_mlp_g': _jnp.float32, 'mlp_w_up': _jnp.float32, 'mlp_w_down': _jnp.float32, 'norm_final_g': _jnp.float32, 'a_w_in': _jnp.float32, 'a_conv_w': _jnp.float32, 'a_conv_b': _jnp.float32, 'a_gate_w': _jnp.float32, 'a_gate_b': _jnp.float32, 'a_lambda': _jnp.float32, 'a_w_out': _jnp.float32, 'b_w_in': _jnp.float32, 'b_ln_g': _jnp.float32, 'b_ln_b': _jnp.float32, 'b_w_s': _jnp.float32, 'b_b_s': _jnp.float32, 'b_w_out': _jnp.float32, 'c_w_in': _jnp.float32, 'c_conv_w': _jnp.float32, 'c_a_log': _jnp.float32, 'c_dt_bias': _jnp.float32, 'c_norm_g': _jnp.float32, 'c_w_out': _jnp.float32}
MOMENT_SCALE = {'norm_mix_g': 2.797243e-01, 'norm_mlp_g': 2.026724e-01, 'mlp_w_up': 9.891262e-02, 'mlp_w_down': 2.369832e-01, 'norm_final_g': 6.611936e+01, 'a_w_in': 2.506130e-01, 'a_conv_w': 2.406574e-01, 'a_conv_b': 2.658660e+00, 'a_gate_w': 5.131115e-02, 'a_gate_b': 4.656005e-02, 'a_lambda': 8.575251e-02, 'a_w_out': 2.805972e-01, 'b_w_in': 1.432184e-01, 'b_ln_g': 1.020738e-01, 'b_ln_b': 1.058816e-01, 'b_w_s': 1.050822e-01, 'b_b_s': 1.077577e-01, 'b_w_out': 1.882745e-01, 'c_w_in': 6.830748e-02, 'c_conv_w': 6.414676e-02, 'c_a_log': 1.388529e-01, 'c_dt_bias': 1.362355e-01, 'c_norm_g': 2.125947e-01, 'c_w_out': 9.198768e-02}


def _to_microbatches(a, axis):
    t = _jnp.moveaxis(a, axis, 0)
    t = t.reshape((N_MICROBATCH, t.shape[0] // N_MICROBATCH) + t.shape[1:])
    return _jnp.moveaxis(t, 1, axis + 1)


def setup_inputs(seed: int = 0) -> dict:
    inp = _fwd_setup_inputs(seed)
    key = _jax.random.fold_in(_jax.random.key(seed), 7919)
    shape, _ = _output_shape()
    out = dict(inp)
    out["loss_target"] = _jax.random.normal(_jax.random.fold_in(key, 0), shape, _jnp.float32)
    for i, name in enumerate(TWIN_WEIGHTS):
        w = inp[name].astype(_jnp.float32)
        if MOMENT_SCALE is None:
            s = _jnp.sqrt(_jnp.mean(_jnp.square(w)) + 1e-30)
        else:
            s = MOMENT_SCALE[name]
        km, kv = _jax.random.split(_jax.random.fold_in(key, i + 1))
        out[name] = w
        out["m_" + name] = s * _jax.random.normal(km, w.shape, _jnp.float32)
        out["v_" + name] = (s * s) * _jax.random.uniform(kv, w.shape, _jnp.float32, 0.5, 1.5)
    if N_MICROBATCH > 1:
        for name, axis in PER_EXAMPLE_BATCH_AXIS.items():
            out[name] = _to_microbatches(out[name], axis)
    return {'x': out['x'], 'norm_mix_g': out['norm_mix_g'], 'norm_mlp_g': out['norm_mlp_g'], 'mlp_w_up': out['mlp_w_up'], 'mlp_w_down': out['mlp_w_down'], 'norm_final_g': out['norm_final_g'], 'a_w_in': out['a_w_in'], 'a_conv_w': out['a_conv_w'], 'a_conv_b': out['a_conv_b'], 'a_gate_w': out['a_gate_w'], 'a_gate_b': out['a_gate_b'], 'a_lambda': out['a_lambda'], 'a_w_out': out['a_w_out'], 'b_w_in': out['b_w_in'], 'b_ln_g': out['b_ln_g'], 'b_ln_b': out['b_ln_b'], 'b_w_s': out['b_w_s'], 'b_b_s': out['b_b_s'], 'b_w_out': out['b_w_out'], 'c_w_in': out['c_w_in'], 'c_conv_w': out['c_conv_w'], 'c_a_log': out['c_a_log'], 'c_dt_bias': out['c_dt_bias'], 'c_norm_g': out['c_norm_g'], 'c_w_out': out['c_w_out'], 'loss_target': out['loss_target'], 'm_norm_mix_g': out['m_norm_mix_g'], 'm_norm_mlp_g': out['m_norm_mlp_g'], 'm_mlp_w_up': out['m_mlp_w_up'], 'm_mlp_w_down': out['m_mlp_w_down'], 'm_norm_final_g': out['m_norm_final_g'], 'm_a_w_in': out['m_a_w_in'], 'm_a_conv_w': out['m_a_conv_w'], 'm_a_conv_b': out['m_a_conv_b'], 'm_a_gate_w': out['m_a_gate_w'], 'm_a_gate_b': out['m_a_gate_b'], 'm_a_lambda': out['m_a_lambda'], 'm_a_w_out': out['m_a_w_out'], 'm_b_w_in': out['m_b_w_in'], 'm_b_ln_g': out['m_b_ln_g'], 'm_b_ln_b': out['m_b_ln_b'], 'm_b_w_s': out['m_b_w_s'], 'm_b_b_s': out['m_b_b_s'], 'm_b_w_out': out['m_b_w_out'], 'm_c_w_in': out['m_c_w_in'], 'm_c_conv_w': out['m_c_conv_w'], 'm_c_a_log': out['m_c_a_log'], 'm_c_dt_bias': out['m_c_dt_bias'], 'm_c_norm_g': out['m_c_norm_g'], 'm_c_w_out': out['m_c_w_out'], 'v_norm_mix_g': out['v_norm_mix_g'], 'v_norm_mlp_g': out['v_norm_mlp_g'], 'v_mlp_w_up': out['v_mlp_w_up'], 'v_mlp_w_down': out['v_mlp_w_down'], 'v_norm_final_g': out['v_norm_final_g'], 'v_a_w_in': out['v_a_w_in'], 'v_a_conv_w': out['v_a_conv_w'], 'v_a_conv_b': out['v_a_conv_b'], 'v_a_gate_w': out['v_a_gate_w'], 'v_a_gate_b': out['v_a_gate_b'], 'v_a_lambda': out['v_a_lambda'], 'v_a_w_out': out['v_a_w_out'], 'v_b_w_in': out['v_b_w_in'], 'v_b_ln_g': out['v_b_ln_g'], 'v_b_ln_b': out['v_b_ln_b'], 'v_b_w_s': out['v_b_w_s'], 'v_b_b_s': out['v_b_b_s'], 'v_b_w_out': out['v_b_w_out'], 'v_c_w_in': out['v_c_w_in'], 'v_c_conv_w': out['v_c_conv_w'], 'v_c_a_log': out['v_c_a_log'], 'v_c_dt_bias': out['v_c_dt_bias'], 'v_c_norm_g': out['v_c_norm_g'], 'v_c_w_out': out['v_c_w_out']}


def _loss(weights, diff, rest, loss_target):
    with _jax.named_scope("forward"):
        args = {**rest, TWIN_DIFF_INPUT: diff, **{k: w.astype(_WEIGHT_DTYPES[k]) for k, w in weights.items()}}
        y = _forward(args)
    with _jax.named_scope("loss_head"):
        err = _jnp.square(y.astype(_jnp.float32) - loss_target)
        return 0.5 * _jnp.sum(_jnp.mean(err, axis=-1)) if err.ndim else 0.5 * err


def _adamw(w, g, m, v):
    m = ADAM_B1 * m + (1.0 - ADAM_B1) * g
    v = ADAM_B2 * v + (1.0 - ADAM_B2) * _jnp.square(g)
    m_hat = m / (1.0 - ADAM_B1 ** ADAM_STEP)
    v_hat = v / (1.0 - ADAM_B2 ** ADAM_STEP)
    delta = -ADAM_LR * (m_hat / (_jnp.sqrt(v_hat) + ADAM_EPS) + ADAM_WD * w)
    return delta, m, v


def reference(x, norm_mix_g, norm_mlp_g, mlp_w_up, mlp_w_down, norm_final_g, a_w_in, a_conv_w, a_conv_b, a_gate_w, a_gate_b, a_lambda, a_w_out, b_w_in, b_ln_g, b_ln_b, b_w_s, b_b_s, b_w_out, c_w_in, c_conv_w, c_a_log, c_dt_bias, c_norm_g, c_w_out, loss_target, m_norm_mix_g, m_norm_mlp_g, m_mlp_w_up, m_mlp_w_down, m_norm_final_g, m_a_w_in, m_a_conv_w, m_a_conv_b, m_a_gate_w, m_a_gate_b, m_a_lambda, m_a_w_out, m_b_w_in, m_b_ln_g, m_b_ln_b, m_b_w_s, m_b_b_s, m_b_w_out, m_c_w_in, m_c_conv_w, m_c_a_log, m_c_dt_bias, m_c_norm_g, m_c_w_out, v_norm_mix_g, v_norm_mlp_g, v_mlp_w_up, v_mlp_w_down, v_norm_final_g, v_a_w_in, v_a_conv_w, v_a_conv_b, v_a_gate_w, v_a_gate_b, v_a_lambda, v_a_w_out, v_b_w_in, v_b_ln_g, v_b_ln_b, v_b_w_s, v_b_b_s, v_b_w_out, v_c_w_in, v_c_conv_w, v_c_a_log, v_c_dt_bias, v_c_norm_g, v_c_w_out):
    given = dict(x=x, norm_mix_g=norm_mix_g, norm_mlp_g=norm_mlp_g, mlp_w_up=mlp_w_up, mlp_w_down=mlp_w_down, norm_final_g=norm_final_g, a_w_in=a_w_in, a_conv_w=a_conv_w, a_conv_b=a_conv_b, a_gate_w=a_gate_w, a_gate_b=a_gate_b, a_lambda=a_lambda, a_w_out=a_w_out, b_w_in=b_w_in, b_ln_g=b_ln_g, b_ln_b=b_ln_b, b_w_s=b_w_s, b_b_s=b_b_s, b_w_out=b_w_out, c_w_in=c_w_in, c_conv_w=c_conv_w, c_a_log=c_a_log, c_dt_bias=c_dt_bias, c_norm_g=c_norm_g, c_w_out=c_w_out, loss_target=loss_target, m_norm_mix_g=m_norm_mix_g, m_norm_mlp_g=m_norm_mlp_g, m_mlp_w_up=m_mlp_w_up, m_mlp_w_down=m_mlp_w_down, m_norm_final_g=m_norm_final_g, m_a_w_in=m_a_w_in, m_a_conv_w=m_a_conv_w, m_a_conv_b=m_a_conv_b, m_a_gate_w=m_a_gate_w, m_a_gate_b=m_a_gate_b, m_a_lambda=m_a_lambda, m_a_w_out=m_a_w_out, m_b_w_in=m_b_w_in, m_b_ln_g=m_b_ln_g, m_b_ln_b=m_b_ln_b, m_b_w_s=m_b_w_s, m_b_b_s=m_b_b_s, m_b_w_out=m_b_w_out, m_c_w_in=m_c_w_in, m_c_conv_w=m_c_conv_w, m_c_a_log=m_c_a_log, m_c_dt_bias=m_c_dt_bias, m_c_norm_g=m_c_norm_g, m_c_w_out=m_c_w_out, v_norm_mix_g=v_norm_mix_g, v_norm_mlp_g=v_norm_mlp_g, v_mlp_w_up=v_mlp_w_up, v_mlp_w_down=v_mlp_w_down, v_norm_final_g=v_norm_final_g, v_a_w_in=v_a_w_in, v_a_conv_w=v_a_conv_w, v_a_conv_b=v_a_conv_b, v_a_gate_w=v_a_gate_w, v_a_gate_b=v_a_gate_b, v_a_lambda=v_a_lambda, v_a_w_out=v_a_w_out, v_b_w_in=v_b_w_in, v_b_ln_g=v_b_ln_g, v_b_ln_b=v_b_ln_b, v_b_w_s=v_b_w_s, v_b_b_s=v_b_b_s, v_b_w_out=v_b_w_out, v_c_w_in=v_c_w_in, v_c_conv_w=v_c_conv_w, v_c_a_log=v_c_a_log, v_c_dt_bias=v_c_dt_bias, v_c_norm_g=v_c_norm_g, v_c_w_out=v_c_w_out)
    weights = {n: given[n] for n in TWIN_WEIGHTS}
    shared = {n: given[n] for n in SHARED_INPUTS}
    per_example = {n: given[n] for n in ['x']}
    grad_fn = _jax.value_and_grad(_loss, argnums=(0, 1))

    def one_microbatch(ex, loss_target):
        ex = dict(ex)
        diff = ex.pop(TWIN_DIFF_INPUT)
        return grad_fn(weights, diff, {**shared, **ex}, loss_target)

    if N_MICROBATCH == 1:
        loss, (grad_w, grad_x) = one_microbatch(per_example, given["loss_target"])
    else:
        def body(carry, xs):
            loss_sum, grad_sum = carry
            l_k, (gw_k, gx_k) = one_microbatch(xs[0], xs[1])
            with _jax.named_scope("update"):
                return (loss_sum + l_k, _jax.tree.map(_jnp.add, grad_sum, gw_k)), gx_k

        init = (_jnp.zeros((), _jnp.float32), _jax.tree.map(_jnp.zeros_like, weights))
        (loss, grad_w), grad_x = _jax.lax.scan(body, init, (per_example, given["loss_target"]))
    with _jax.named_scope("update"):
        delta_w, new_m, new_v = {}, {}, {}
        for n in TWIN_WEIGHTS:
            delta_w[n], new_m[n], new_v[n] = _adamw(weights[n], grad_w[n], given["m_" + n], given["v_" + n])
    return (loss, grad_x, *[grad_w[n] for n in TWIN_WEIGHTS], *[delta_w[n] for n in TWIN_WEIGHTS],
            *[new_m[n] for n in TWIN_WEIGHTS], *[new_v[n] for n in TWIN_WEIGHTS])
```

```python
import functools

import jax
import jax.numpy as jnp
from jax import lax
from jax.experimental import pallas as pl
from jax.experimental.pallas import tpu as pltpu

F32 = jnp.float32
BF = jnp.bfloat16
HP = lax.Precision.HIGHEST

LANES = 128
VMEM_LIMIT = 56 * 1024 * 1024
RG_C = 8.0
SGU_CHUNK = 128
GDN_CHUNK = 64
CONV_W = 4
N_MIXERS = 3

ADAM_LR = 0.001
ADAM_B1 = 0.9
ADAM_B2 = 0.999
ADAM_EPS = 1e-08
ADAM_WD = 0.01
ADAM_STEP = 10


def _params(sem=None):
    return pltpu.CompilerParams(dimension_semantics=sem, vmem_limit_bytes=VMEM_LIMIT)


def _shift_impl(x, s):
    if s == 0:
        return x
    n = x.shape[0]
    row = lax.broadcasted_iota(jnp.int32, x.shape, 0)
    if s > 0:
        return jnp.where(row >= s, pltpu.roll(x, s, 0), 0.0)
    return jnp.where(row < n + s, pltpu.roll(x, n + s, 0), 0.0)


@functools.partial(jax.custom_vjp, nondiff_argnums=(1,))
def _shift(x, s):
    return _shift_impl(x, s)


def _shift_fwd(x, s):
    return _shift_impl(x, s), None


def _shift_bwd(s, _, g):
    return (_shift_impl(g, -s),)


_shift.defvjp(_shift_fwd, _shift_bwd)


def _chunk_cumsum_impl(x, rev, chunk):
    n = x.shape[0]
    rc = lax.broadcasted_iota(jnp.int32, x.shape, 0) & (chunk - 1)
    sh = 1
    while sh < chunk:
        if rev:
            x = x + jnp.where(rc < chunk - sh, pltpu.roll(x, n - sh, 0), 0.0)
        else:
            x = x + jnp.where(rc >= sh, pltpu.roll(x, sh, 0), 0.0)
        sh *= 2
    return x


@functools.partial(jax.custom_vjp, nondiff_argnums=(1, 2))
def _chunk_cumsum(x, rev, chunk):
    return _chunk_cumsum_impl(x, rev, chunk)


def _chunk_cumsum_fwd(x, rev, chunk):
    return _chunk_cumsum_impl(x, rev, chunk), None


def _chunk_cumsum_bwd(rev, chunk, _, g):
    return (_chunk_cumsum_impl(g, not rev, chunk),)


_chunk_cumsum.defvjp(_chunk_cumsum_fwd, _chunk_cumsum_bwd)


def _rms(x, g, eps=1e-6):
    return x * lax.rsqrt(jnp.mean(x * x, axis=-1, keepdims=True) + eps) * g


def _bdot(a, b):
    return jnp.dot(a.astype(BF), b.astype(BF), preferred_element_type=F32)


def _conv4(z, rows):
    out = rows[0] * _shift(z, 2)
    for k in range(1, CONV_W):
        out = out + rows[k] * _shift(z, 2 - k)
    return out


_DIMS = {"nn": ((1,), (0,)), "nt": ((1,), (1,)), "tn": ((0,), (0,))}


def _mm(a, b, mode, name, *, out_dtype=F32, epi=None, extra=None, tm=1024, tn=1024, tk=512):
    if mode == "tn":
        K, M = a.shape
    else:
        M, K = a.shape
    N = b.shape[0] if mode == "nt" else b.shape[1]
    tm, tn, tk = min(tm, M), min(tn, N), min(tk, K)
    assert M % tm == 0 and N % tn == 0 and K % tk == 0, (name, M, N, K)
    nk = K // tk
    a_spec = pl.BlockSpec((tk, tm), lambda i, j, k: (k, i)) if mode == "tn" else pl.BlockSpec((tm, tk), lambda i, j, k: (i, k))
    b_spec = pl.BlockSpec((tn, tk), lambda i, j, k: (j, k)) if mode == "nt" else pl.BlockSpec((tk, tn), lambda i, j, k: (k, j))
    o_spec = pl.BlockSpec((tm, tn), lambda i, j, k: (i, j))
    ins, specs = [a, b], [a_spec, b_spec]
    if epi in ("add", "relu2_bwd"):
        ins.append(extra)
        specs.append(o_spec)

    def body(*refs):
        a_ref, b_ref = refs[0], refs[1]
        e_ref = refs[2] if len(ins) == 3 else None
        o_ref, acc = refs[-2], refs[-1]
        k = pl.program_id(2)

        @pl.when(k == 0)
        def _():
            acc[...] = jnp.zeros_like(acc)

        acc[...] += lax.dot_general(a_ref[...].astype(BF), b_ref[...].astype(BF), (_DIMS[mode], ((), ())),
                                    preferred_element_type=F32)

        @pl.when(k == nk - 1)
        def _():
            r = acc[...]
            if epi == "relu2":
                r = jnp.square(jnp.maximum(r, 0.0))
            elif epi == "add":
                r = r + e_ref[...]
            elif epi == "relu2_bwd":
                r = r * (2.0 * jnp.sqrt(e_ref[...].astype(F32)))
            o_ref[...] = r.astype(out_dtype)

    return pl.pallas_call(
        body, name=name, grid=(M // tm, N // tn, nk), in_specs=specs, out_specs=o_spec,
        out_shape=jax.ShapeDtypeStruct((M, N), out_dtype), scratch_shapes=[pltpu.VMEM((tm, tn), F32)],
        compiler_params=_params(("parallel", "parallel", "arbitrary")))(*ins)


def _rows_tile(T):
    return min(512, T)


def _rms_fwd(x, g, name):
    T, D = x.shape
    tr = _rows_tile(T)

    def body(x_ref, g_ref, o_ref):
        o_ref[...] = _rms(x_ref[...], g_ref[...]).astype(BF)

    return pl.pallas_call(
        body, name=name, grid=(T // tr,),
        in_specs=[pl.BlockSpec((tr, D), lambda i: (i, 0)), pl.BlockSpec((1, D), lambda i: (0, 0))],
        out_specs=pl.BlockSpec((tr, D), lambda i: (i, 0)), out_shape=jax.ShapeDtypeStruct((T, D), BF),
        compiler_params=_params(("parallel",)))(x, g.reshape(1, D))


def _rms_bwd(x, g, dhn, dres, name):
    T, D = x.shape
    tr = _rows_tile(T)

    def body(x_ref, g_ref, dhn_ref, dres_ref, dx_ref, dg_ref):
        _, vjp = jax.vjp(_rms, x_ref[...], g_ref[...])
        dx, dg = vjp(dhn_ref[...])
        dx_ref[...] = dres_ref[...] + dx

        @pl.when(pl.program_id(0) == 0)
        def _():
            dg_ref[...] = jnp.zeros_like(dg_ref)

        dg_ref[...] += dg

    row = pl.BlockSpec((tr, D), lambda i: (i, 0))
    vec = pl.BlockSpec((1, D), lambda i: (0, 0))
    dx, dg = pl.pallas_call(
        body, name=name, grid=(T // tr,), in_specs=[row, vec, row, row], out_specs=[row, vec],
        out_shape=[jax.ShapeDtypeStruct((T, D), F32), jax.ShapeDtypeStruct((1, D), F32)],
        compiler_params=_params(("arbitrary",)))(x, g.reshape(1, D), dhn, dres)
    return dx, dg.reshape(D)


def _final_loss(x, g, tgt, name):
    T, D = x.shape
    tr = _rows_tile(T)

    def body(x_ref, g_ref, t_ref, l_ref, dx_ref, dg_ref):
        y, vjp = jax.vjp(_rms, x_ref[...], g_ref[...])
        err = y - t_ref[...]
        dx, dg = vjp(err * (1.0 / D))
        dx_ref[...] = dx

        @pl.when(pl.program_id(0) == 0)
        def _():
            dg_ref[...] = jnp.zeros_like(dg_ref)
            l_ref[...] = jnp.zeros_like(l_ref)

        dg_ref[...] += dg
        l_ref[...] += (0.5 / D) * jnp.sum(jnp.sum(err * err, axis=1, keepdims=True), axis=0, keepdims=True)

    row = pl.BlockSpec((tr, D), lambda i: (i, 0))
    vec = pl.BlockSpec((1, D), lambda i: (0, 0))
    loss, dx, dg = pl.pallas_call(
        body, name=name, grid=(T // tr,), in_specs=[row, vec, row],
        out_specs=[pl.BlockSpec((1, LANES), lambda i: (0, 0)), row, vec],
        out_shape=[jax.ShapeDtypeStruct((1, LANES), F32), jax.ShapeDtypeStruct((T, D), F32),
                   jax.ShapeDtypeStruct((1, D), F32)],
        compiler_params=_params(("arbitrary",)))(x, g.reshape(1, D), tgt)
    return loss[0, 0], dx, dg.reshape(D)


def _a_pre(zx, cws, cb, gws, gbs, lams):
    xr = _conv4(zx, cws) + cb
    xb = xr.astype(BF)
    out = []
    for d in range(2):
        r = jax.nn.sigmoid(_bdot(xb, gws[2 * d]) + gbs[2 * d])
        ig = jax.nn.sigmoid(_bdot(xb, gws[2 * d + 1]) + gbs[2 * d + 1])
        log_a = -RG_C * r * jax.nn.softplus(-lams[d])
        a = jnp.exp(log_a)
        one_m_a2 = -jnp.tanh(log_a) * (a * a + 1.0)
        out += [a, jnp.sqrt(one_m_a2) * ig * xr]
    return tuple(out)


def _a_post(h0, h1, zg):
    return (h0 + h1) * jax.nn.gelu(zg)


def _scan_tiles(a_ref, b_ref, h_ref, reverse):
    S, C = a_ref.shape
    row = lax.broadcasted_iota(jnp.int32, (8, C), 0)

    def step(i, carry):
        blk = (S // 8 - 1 - i) if reverse else i
        t = pl.multiple_of(blk * 8, 8)
        a = a_ref[pl.ds(t, 8), :]
        b = b_ref[pl.ds(t, 8), :]
        for sh in (1, 2, 4):
            if reverse:
                m, r = row < 8 - sh, 8 - sh
            else:
                m, r = row >= sh, sh
            a_s = jnp.where(m, pltpu.roll(a, r, 0), 1.0)
            b_s = jnp.where(m, pltpu.roll(b, r, 0), 0.0)
            b = a * b_s + b
            a = a * a_s
        hh = a * carry + b
        h_ref[pl.ds(t, 8), :] = hh
        edge = hh[0:1, :] if reverse else hh[7:8, :]
        return jnp.broadcast_to(edge, (8, C))

    lax.fori_loop(0, S // 8, step, jnp.zeros((8, C), F32))


def _a_load_params(cw_ref, cb_ref, gw_ref, gb_ref, lam_ref):
    cws = [cw_ref[k:k + 1, :] for k in range(CONV_W)]
    gws = [gw_ref[d, g] for d in range(2) for g in range(2)]
    gbs = [gb_ref[d, g] for d in range(2) for g in range(2)]
    lams = [lam_ref[d:d + 1, :] for d in range(2)]
    return cws, cb_ref[...], gws, gbs, lams


def _a_in_specs(S, H):
    zg = pl.BlockSpec((None, S, LANES), lambda h, b: (b, 0, h))
    zx = pl.BlockSpec((None, S, LANES), lambda h, b: (b, 0, H + h))
    cw = pl.BlockSpec((CONV_W, LANES), lambda h, b: (0, h))
    cb = pl.BlockSpec((1, LANES), lambda h, b: (0, h))
    gw = pl.BlockSpec((2, 2, None, LANES, LANES), lambda h, b: (0, 0, h, 0, 0))
    gb = pl.BlockSpec((2, 2, None, 1, LANES), lambda h, b: (0, 0, h, 0, 0))
    lam = pl.BlockSpec((2, LANES), lambda h, b: (0, h))
    return zg, zx, cw, cb, gw, gb, lam


def _a_core_fwd(z, cw, cb, gw, gb, lam, name):
    Bq, S, D2 = z.shape
    D = D2 // 2
    H = D // LANES

    def body(zg_ref, zx_ref, cw_ref, cb_ref, gw_ref, gb_ref, lam_ref, y_ref, a_s, b_s, h_s):
        ab = _a_pre(zx_ref[...], *_a_load_params(cw_ref, cb_ref, gw_ref, gb_ref, lam_ref))
        for d in range(2):
            a_s[d] = ab[2 * d]
            b_s[d] = ab[2 * d + 1]
            _scan_tiles(a_s.at[d], b_s.at[d], h_s.at[d], reverse=(d == 1))
        y_ref[...] = _a_post(h_s[0], h_s[1], zg_ref[...]).astype(BF)

    seq = pltpu.VMEM((2, S, LANES), F32)
    return pl.pallas_call(
        body, name=name, grid=(H, Bq), in_specs=list(_a_in_specs(S, H)),
        out_specs=pl.BlockSpec((None, S, LANES), lambda h, b: (b, 0, h)),
        out_shape=jax.ShapeDtypeStruct((Bq, S, D), BF), scratch_shapes=[seq, seq, seq],
        compiler_params=_params(("parallel", "arbitrary")))(z, z, cw, cb.reshape(1, D), gw, gb.reshape(2, 2, H, 1, LANES), lam)


def _a_core_bwd(z, dy, cw, cb, gw, gb, lam, name):
    Bq, S, D2 = z.shape
    D = D2 // 2
    H = D // LANES

    def body(zg_ref, zx_ref, dy_ref, cw_ref, cb_ref, gw_ref, gb_ref, lam_ref,
             dzg_ref, dzx_ref, dcw_ref, dcb_ref, dgw_ref, dgb_ref, dlam_ref, a_s, b_s, h_s, l_s):
        prm = _a_load_params(cw_ref, cb_ref, gw_ref, gb_ref, lam_ref)
        zx = zx_ref[...]
        ab = _a_pre(zx, *prm)
        for d in range(2):
            a_s[d] = ab[2 * d]
            b_s[d] = ab[2 * d + 1]
            _scan_tiles(a_s.at[d], b_s.at[d], h_s.at[d], reverse=(d == 1))
        _, post_vjp = jax.vjp(_a_post, h_s[0], h_s[1], zg_ref[...])
        dh0, dh1, dzg = post_vjp(dy_ref[...])
        dzg_ref[...] = dzg.astype(BF)
        cot = []
        for d, dh in ((0, dh0), (1, dh1)):
            toward = -1 if d == 0 else 1
            b_s[d] = dh
            a_s[d] = _shift(a_s[d], toward)
            _scan_tiles(a_s.at[d], b_s.at[d], l_s.at[d], reverse=(d == 0))
            lam_t = l_s[d]
            cot += [lam_t * _shift(h_s[d], -toward), lam_t]
        _, pre_vjp = jax.vjp(_a_pre, zx, *prm)
        dzx, dcws, dcb, dgws, dgbs, dlams = pre_vjp(tuple(cot))
        dzx_ref[...] = dzx.astype(BF)

        @pl.when(pl.program_id(1) == 0)
        def _():
            for r in (dcw_ref, dcb_ref, dgw_ref, dgb_ref, dlam_ref):
                r[...] = jnp.zeros_like(r)

        for k in range(CONV_W):
            dcw_ref[k:k + 1, :] += dcws[k]
        dcb_ref[...] += dcb
        for d in range(2):
            dlam_ref[d:d + 1, :] += dlams[d]
            for g in range(2):
                dgw_ref[d, g] += dgws[2 * d + g]
                dgb_ref[d, g] += dgbs[2 * d + g]

    zg, zx, cws, cbs, gws, gbs, lams = _a_in_specs(S, H)
    dyspec = pl.BlockSpec((None, S, LANES), lambda h, b: (b, 0, h))
    seq = pltpu.VMEM((2, S, LANES), F32)
    dzg, dzx, dcw, dcb, dgw, dgb, dlam = pl.pallas_call(
        body, name=name, grid=(H, Bq), in_specs=[zg, zx, dyspec, cws, cbs, gws, gbs, lams],
        out_specs=[dyspec, dyspec, cws, cbs, gws, gbs, lams],
        out_shape=[jax.ShapeDtypeStruct((Bq, S, D), BF), jax.ShapeDtypeStruct((Bq, S, D), BF),
                   jax.ShapeDtypeStruct((CONV_W, D), F32), jax.ShapeDtypeStruct((1, D), F32),
                   jax.ShapeDtypeStruct((2, 2, H, LANES, LANES), F32), jax.ShapeDtypeStruct((2, 2, H, 1, LANES), F32),
                   jax.ShapeDtypeStruct((2, D), F32)],
        scratch_shapes=[seq, seq, seq, seq],
        compiler_params=_params(("parallel", "arbitrary")))(z, z, dy, cw, cb.reshape(1, D), gw, gb.reshape(2, 2, H, 1, LANES), lam)
    dz = jnp.concatenate([dzg, dzx], axis=-1)
    return dz, dcw, dcb.reshape(D), dgw, dgb.reshape(2, 2, H, LANES), dlam


def _b_fn(z, lng, lnb, wss, bsf):
    D = z.shape[1] // 2
    zz = jax.nn.gelu(z)
    u, v = zz[:, :D], zz[:, D:]
    mu = jnp.mean(v, axis=-1, keepdims=True)
    var = jnp.mean(jnp.square(v - mu), axis=-1, keepdims=True)
    vn = ((v - mu) * lax.rsqrt(var + 1e-5) * lng + lnb).astype(BF)
    vs = jnp.concatenate([_bdot(wss[g], vn[:, g * LANES:(g + 1) * LANES]) for g in range(D // LANES)], axis=1)
    return u * (vs + bsf)


def _b_specs(D, GB):
    row = lambda w: pl.BlockSpec((SGU_CHUNK, w), lambda i: (i, 0))
    vec = pl.BlockSpec((1, D), lambda i: (0, 0))
    ws = pl.BlockSpec((GB, SGU_CHUNK, SGU_CHUNK), lambda i: (0, 0, 0))
    bsf = pl.BlockSpec((SGU_CHUNK, D), lambda i: (0, 0))
    return row, vec, ws, bsf


def _b_core_fwd(z, lng, lnb, ws, bsf, name):
    T, D2 = z.shape
    D = D2 // 2
    GB = D // LANES
    row, vec, wspec, bspec = _b_specs(D, GB)

    def body(z_ref, lng_ref, lnb_ref, ws_ref, bsf_ref, y_ref):
        wss = [ws_ref[g] for g in range(GB)]
        y_ref[...] = _b_fn(z_ref[...], lng_ref[...], lnb_ref[...], wss, bsf_ref[...]).astype(BF)

    return pl.pallas_call(
        body, name=name, grid=(T // SGU_CHUNK,), in_specs=[row(D2), vec, vec, wspec, bspec], out_specs=row(D),
        out_shape=jax.ShapeDtypeStruct((T, D), BF), compiler_params=_params(("parallel",)))(
            z, lng.reshape(1, D), lnb.reshape(1, D), ws, bsf)


def _b_core_bwd(z, dy, lng, lnb, ws, bsf, name):
    T, D2 = z.shape
    D = D2 // 2
    GB = D // LANES
    row, vec, wspec, bspec = _b_specs(D, GB)

    def body(z_ref, dy_ref, lng_ref, lnb_ref, ws_ref, bsf_ref, dz_ref, dlng_ref, dlnb_ref, dws_ref, dbsf_ref):
        wss = [ws_ref[g] for g in range(GB)]
        _, vjp = jax.vjp(_b_fn, z_ref[...], lng_ref[...], lnb_ref[...], wss, bsf_ref[...])
        dz, dlng, dlnb, dwss, dbsf = vjp(dy_ref[...])
        dz_ref[...] = dz.astype(BF)

        @pl.when(pl.program_id(0) == 0)
        def _():
            for r in (dlng_ref, dlnb_ref, dws_ref, dbsf_ref):
                r[...] = jnp.zeros_like(r)

        dlng_ref[...] += dlng
        dlnb_ref[...] += dlnb
        dbsf_ref[...] += dbsf
        for g in range(GB):
            dws_ref[g] += dwss[g]

    dz, dlng, dlnb, dws, dbsf = pl.pallas_call(
        body, name=name, grid=(T // SGU_CHUNK,), in_specs=[row(D2), row(D), vec, vec, wspec, bspec],
        out_specs=[row(D2), vec, vec, wspec, bspec],
        out_shape=[jax.ShapeDtypeStruct((T, D2), BF), jax.ShapeDtypeStruct((1, D), F32), jax.ShapeDtypeStruct((1, D), F32),
                   jax.ShapeDtypeStruct((GB, SGU_CHUNK, SGU_CHUNK), F32), jax.ShapeDtypeStruct((SGU_CHUNK, D), F32)],
        compiler_params=_params(("arbitrary",)))(z, dy, lng.reshape(1, D), lnb.reshape(1, D), ws, bsf)
    return dz, dlng.reshape(D), dlnb.reshape(D), dws, dbsf


def _lane_is(j):
    return lax.broadcasted_iota(jnp.int32, (1, LANES), 1) == j


def _lane_col(x, j):
    return jnp.sum(jnp.where(_lane_is(j), x, 0.0), axis=1, keepdims=True)


def _c_pre(zq, zk, zv, zs, cwq, cwk, cwv, pcs, head, HC):
    q = jax.nn.silu(_conv4(zq, cwq))
    k = jax.nn.silu(_conv4(zk, cwk))
    v = jax.nn.silu(_conv4(zv, cwv))
    q = q * lax.rsqrt(jnp.sum(q * q, axis=-1, keepdims=True) + 1e-6) * (LANES ** -0.5)
    k = k * lax.rsqrt(jnp.sum(k * k, axis=-1, keepdims=True) + 1e-6)
    gbp = jnp.zeros_like(zs)
    for d in range(2):
        a_logit = _lane_col(zs, d * HC + head)
        b_logit = _lane_col(zs, 2 * HC + d * HC + head)
        g = -jnp.exp(pcs[d]) * jax.nn.softplus(a_logit + pcs[2 + d])
        beta = jnp.broadcast_to(jax.nn.sigmoid(b_logit), g.shape)
        gbp = gbp + jnp.where(_lane_is(2 * d), g, 0.0) + jnp.where(_lane_is(2 * d + 1), beta, 0.0)
    return q, k, v, gbp


def _tri_inv(a):
    C = a.shape[-1]
    eye = (lax.broadcasted_iota(jnp.int32, (1, C, C), 1) == lax.broadcasted_iota(jnp.int32, (1, C, C), 2)).astype(F32)
    r = eye - a
    p = a
    n = 2
    while n < C:
        p = jnp.einsum("nij,njk->nik", p, p, preferred_element_type=F32, precision=HP)
        r = r + jnp.einsum("nij,njk->nik", r, p, preferred_element_type=F32, precision=HP)
        n *= 2
    return r


def _c_phase1(q, k, v, gbp, rev):
    S = q.shape[0]
    C = GDN_CHUNK
    N = S // C
    col = 2 if rev else 0
    gB = jnp.broadcast_to(_lane_col(gbp, col), (S, LANES))
    bB = jnp.broadcast_to(_lane_col(gbp, col + 1), (S, LANES))
    r3 = lambda t: t.reshape(N, C, LANES)
    gc3 = r3(_chunk_cumsum(gB, rev, C))
    q3, k3, v3, b3, g3 = r3(q), r3(k), r3(v), r3(bB), r3(gB)
    ri = lax.broadcasted_iota(jnp.int32, (1, C, C), 1)
    ci = lax.broadcasted_iota(jnp.int32, (1, C, C), 2)
    incl = (ri <= ci) if rev else (ri >= ci)
    strict = (ri < ci) if rev else (ri > ci)
    ones = jnp.full((N, C, LANES), 1.0 / LANES, F32)
    nt = lambda x, y: jnp.einsum("nil,njl->nij", x, y, preferred_element_type=F32, precision=HP)
    diff = nt(gc3, ones) - nt(ones, gc3)
    decay = jnp.where(incl, jnp.exp(jnp.where(incl, diff, 0.0)), 0.0)
    bnt = lambda x, y: jnp.einsum("nid,njd->nij", x.astype(BF), y.astype(BF), preferred_element_type=F32)
    bnn = lambda x, y: jnp.einsum("nij,njd->nid", x.astype(BF), y.astype(BF), preferred_element_type=F32)
    kb = k3 * b3
    vb = v3 * b3
    A = jnp.where(strict, bnt(kb, k3) * decay, 0.0)
    T = _tri_inv(A)
    egc = jnp.exp(gc3)
    u = bnn(T, vb)
    w = bnn(T, kb * egc)
    qk = bnt(q3, k3) * decay
    qe = q3 * egc
    glast = jnp.sum(g3, axis=1, keepdims=True)
    kd = k3 * jnp.exp(glast - gc3)
    eg = jnp.exp(glast)
    return u, w, qe, qk, kd, eg


def _c_step(state, u, w, qe, qk, kd, eg):
    sb = state.astype(BF)
    vn = u - jnp.dot(w.astype(BF), sb, preferred_element_type=F32)
    o = jnp.dot(qe.astype(BF), sb, preferred_element_type=F32) + _bdot(qk, vn)
    new = state * eg + lax.dot_general(kd.astype(BF), vn.astype(BF), (((0,), (0,)), ((), ())), preferred_element_type=F32)
    return new, o


def _c_post(o, zg, ng):
    return _rms(o, ng) * jax.nn.silu(zg)


def _c_pre_specs(S, H):
    col = lambda c0: pl.BlockSpec((None, S, LANES), lambda h, b: (b, 0, c0 * H + h))
    zs = pl.BlockSpec((None, S, LANES), lambda h, b: (b, 0, 0))
    cw = lambda c0: pl.BlockSpec((None, CONV_W, LANES), lambda h, b: (c0, 0, h))
    pc = pl.BlockSpec((None, 4, LANES), lambda h, b: (h, 0, 0))
    return col, zs, cw, pc


def _c_pre_fwd(z, zs, cw3, pc, name):
    Bq, S, D4 = z.shape
    D = D4 // 4
    H = D // LANES
    col, zss, cw, pcs = _c_pre_specs(S, H)

    def body(zq_ref, zk_ref, zv_ref, zs_ref, cwq_ref, cwk_ref, cwv_ref, pc_ref, q_ref, k_ref, v_ref, gbp_ref):
        rows = lambda r: [r[i:i + 1, :] for i in range(r.shape[0])]
        q, k, v, gbp = _c_pre(zq_ref[...], zk_ref[...], zv_ref[...], zs_ref[...], rows(cwq_ref), rows(cwk_ref),
                              rows(cwv_ref), rows(pc_ref), pl.program_id(0), H)
        q_ref[...] = q
        k_ref[...] = k
        v_ref[...] = v
        gbp_ref[...] = gbp

    out = pl.BlockSpec((None, S, LANES), lambda h, b: (b, 0, h))
    shp = jax.ShapeDtypeStruct((Bq, S, D), F32)
    return pl.pallas_call(
        body, name=name, grid=(H, Bq), in_specs=[col(0), col(1), col(2), zss, cw(0), cw(1), cw(2), pcs],
        out_specs=[out] * 4, out_shape=[shp] * 4, compiler_params=_params(("parallel", "arbitrary")))(
            z, z, z, zs, cw3, cw3, cw3, pc)


def _c_pre_bwd(z, zs, cw3, pc, dq, dk, dv, dgbp, name):
    Bq, S, D4 = z.shape
    D = D4 // 4
    H = D // LANES
    col, zss, cw, pcs = _c_pre_specs(S, H)

    def body(zq_ref, zk_ref, zv_ref, zs_ref, cwq_ref, cwk_ref, cwv_ref, pc_ref, dq_ref, dk_ref, dv_ref, dgbp_ref,
             dzq_ref, dzk_ref, dzv_ref, dzs_ref, dcw_ref, dpc_ref):
        rows = lambda r: [r[i:i + 1, :] for i in range(r.shape[0])]
        fn = functools.partial(_c_pre, head=pl.program_id(0), HC=H)
        _, vjp = jax.vjp(fn, zq_ref[...], zk_ref[...], zv_ref[...], zs_ref[...], rows(cwq_ref), rows(cwk_ref),
                         rows(cwv_ref), rows(pc_ref))
        dzq, dzk, dzv, dzs, dcwq, dcwk, dcwv, dpcs = vjp((dq_ref[...], dk_ref[...], dv_ref[...], dgbp_ref[...]))
        dzq_ref[...] = dzq.astype(BF)
        dzk_ref[...] = dzk.astype(BF)
        dzv_ref[...] = dzv.astype(BF)
        dzs_ref[...] = dzs

        @pl.when(pl.program_id(1) == 0)
        def _():
            dcw_ref[...] = jnp.zeros_like(dcw_ref)
            dpc_ref[...] = jnp.zeros_like(dpc_ref)

        for c, dc in enumerate((dcwq, dcwk, dcwv)):
            for i in range(CONV_W):
                dcw_ref[c, i:i + 1, :] += dc[i]
        for i in range(4):
            dpc_ref[i:i + 1, :] += dpcs[i]

    out = pl.BlockSpec((None, S, LANES), lambda h, b: (b, 0, h))
    dzs_spec = pl.BlockSpec((None, None, S, LANES), lambda h, b: (h, b, 0, 0))
    dcw_spec = pl.BlockSpec((3, CONV_W, LANES), lambda h, b: (0, 0, h))
    bshape = jax.ShapeDtypeStruct((Bq, S, D), BF)
    dzq, dzk, dzv, dzs, dcw3, dpc = pl.pallas_call(
        body, name=name, grid=(H, Bq),
        in_specs=[col(0), col(1), col(2), zss, cw(0), cw(1), cw(2), pcs, out, out, out, out],
        out_specs=[out, out, out, dzs_spec, dcw_spec, pcs],
        out_shape=[bshape, bshape, bshape, jax.ShapeDtypeStruct((H, Bq, S, LANES), F32),
                   jax.ShapeDtypeStruct((3, CONV_W, D), F32), jax.ShapeDtypeStruct((H, 4, LANES), F32)],
        compiler_params=_params(("parallel", "arbitrary")))(z, z, z, zs, cw3, cw3, cw3, pc, dq, dk, dv, dgbp)
    return dzq, dzk, dzv, dzs, dcw3, dpc


def _c_mid_scratch(S):
    N = S // GDN_CHUNK
    seq = pltpu.VMEM((N, GDN_CHUNK, LANES), F32)
    return [seq, seq, seq, pltpu.VMEM((N, GDN_CHUNK, GDN_CHUNK), F32), seq, pltpu.VMEM((N, 1, LANES), F32)]


PHASE1_CHUNKS = 8


def _c_phase1_blocks(in_refs, p_refs, rev):
    S = in_refs[0].shape[0]
    nb = min(PHASE1_CHUNKS, S // GDN_CHUNK)
    rows = nb * GDN_CHUNK

    def blk(i, carry):
        r0 = pl.multiple_of(i * rows, rows)
        c0 = pl.multiple_of(i * nb, nb)
        vals = _c_phase1(*[r[pl.ds(r0, rows), :] for r in in_refs], rev)
        for r, val in zip(p_refs, vals):
            r[pl.ds(c0, nb)] = val
        return carry

    lax.fori_loop(0, S // rows, blk, 0)


def _c_phase1_blocks_vjp(in_refs, dp_refs, out_refs, rev, accumulate):
    S = in_refs[0].shape[0]
    nb = min(PHASE1_CHUNKS, S // GDN_CHUNK)
    rows = nb * GDN_CHUNK

    def blk(i, carry):
        r0 = pl.multiple_of(i * rows, rows)
        c0 = pl.multiple_of(i * nb, nb)
        _, vjp = jax.vjp(functools.partial(_c_phase1, rev=rev), *[r[pl.ds(r0, rows), :] for r in in_refs])
        cots = vjp(tuple(r[pl.ds(c0, nb)] for r in dp_refs))
        for r, c in zip(out_refs, cots):
            if accumulate:
                r[pl.ds(r0, rows), :] += c
            else:
                r[pl.ds(r0, rows), :] = c
        return carry

    lax.fori_loop(0, S // rows, blk, 0)


def _c_sweep(p_refs, o_ref, st_ref, rev):
    N = p_refs[0].shape[0]

    def step(i, state):
        n = (N - 1 - i) if rev else i
        if st_ref is not None:
            st_ref[n] = state
        new, o = _c_step(state, *[r[n] for r in p_refs])
        if o_ref is not None:
            o_ref[n] += o
        return new

    lax.fori_loop(0, N, step, jnp.zeros((LANES, LANES), F32))


def _c_mid_fwd(q, k, v, gbp, name):
    Bq, S, D = q.shape
    H = D // LANES
    N = S // GDN_CHUNK
    blk = pl.BlockSpec((None, S, LANES), lambda h, b: (b, 0, h))
    blk3 = pl.BlockSpec((None, N, GDN_CHUNK, LANES), lambda h, b: (b, 0, 0, h))

    def body(q_ref, k_ref, v_ref, gbp_ref, o3, *p_refs):
        o3[...] = jnp.zeros_like(o3)
        for rev in (False, True):
            _c_phase1_blocks((q_ref, k_ref, v_ref, gbp_ref), p_refs, rev)
            _c_sweep(p_refs, o3, None, rev)

    o = pl.pallas_call(
        body, name=name, grid=(H, Bq), in_specs=[blk] * 4, out_specs=blk3,
        out_shape=jax.ShapeDtypeStruct((Bq, N, GDN_CHUNK, D), F32),
        scratch_shapes=_c_mid_scratch(S), compiler_params=_params(("parallel", "parallel")))(q, k, v, gbp)
    return o.reshape(Bq, S, D)


def _c_mid_bwd(q, k, v, gbp, do, name):
    Bq, S, D = q.shape
    H = D // LANES
    N = S // GDN_CHUNK
    blk = pl.BlockSpec((None, S, LANES), lambda h, b: (b, 0, h))
    blk3 = pl.BlockSpec((None, N, GDN_CHUNK, LANES), lambda h, b: (b, 0, 0, h))

    def body(q_ref, k_ref, v_ref, gbp_ref, do3, dq_ref, dk_ref, dv_ref, dgbp_ref, *scr):
        p_refs, dp_refs, st_ref = scr[:6], scr[6:12], scr[12]
        in_refs = (q_ref, k_ref, v_ref, gbp_ref)
        out_refs = (dq_ref, dk_ref, dv_ref, dgbp_ref)
        for rev in (False, True):
            _c_phase1_blocks(in_refs, p_refs, rev)
            _c_sweep(p_refs, None, st_ref, rev)

            def back(i, dstate):
                n = i if rev else (N - 1 - i)
                _, vjp = jax.vjp(_c_step, st_ref[n], *[r[n] for r in p_refs])
                cots = vjp((dstate, do3[n]))
                for r, c in zip(dp_refs, cots[1:]):
                    r[n] = c
                return cots[0]

            lax.fori_loop(0, N, back, jnp.zeros((LANES, LANES), F32))
            _c_phase1_blocks_vjp(in_refs, dp_refs, out_refs, rev, accumulate=rev)

    shp = jax.ShapeDtypeStruct((Bq, S, D), F32)
    scratch = _c_mid_scratch(S) + _c_mid_scratch(S) + [pltpu.VMEM((N, LANES, LANES), F32)]
    return pl.pallas_call(
        body, name=name, grid=(H, Bq), in_specs=[blk] * 4 + [blk3], out_specs=[blk] * 4, out_shape=[shp] * 4,
        scratch_shapes=scratch, compiler_params=_params(("parallel", "parallel")))(
            q, k, v, gbp, do.reshape(Bq, N, GDN_CHUNK, D))


def _c_post_fwd(o, z, ng, name):
    Bq, S, D = o.shape
    H = D // LANES
    blk = pl.BlockSpec((None, S, LANES), lambda h, b: (b, 0, h))
    gate = pl.BlockSpec((None, S, LANES), lambda h, b: (b, 0, 3 * H + h))
    vec = pl.BlockSpec((1, LANES), lambda h, b: (0, 0))

    def body(o_ref, zg_ref, ng_ref, y_ref):
        y_ref[...] = _c_post(o_ref[...], zg_ref[...], ng_ref[...]).astype(BF)

    return pl.pallas_call(
        body, name=name, grid=(H, Bq), in_specs=[blk, gate, vec], out_specs=blk,
        out_shape=jax.ShapeDtypeStruct((Bq, S, D), BF), compiler_params=_params(("parallel", "parallel")))(
            o, z, ng.reshape(1, LANES))


def _c_post_bwd(o, z, ng, dy, name):
    Bq, S, D = o.shape
    H = D // LANES
    blk = pl.BlockSpec((None, S, LANES), lambda b, h: (b, 0, h))
    gate = pl.BlockSpec((None, S, LANES), lambda b, h: (b, 0, 3 * H + h))
    vec = pl.BlockSpec((1, LANES), lambda b, h: (0, 0))

    def body(o_ref, zg_ref, ng_ref, dy_ref, do_ref, dzg_ref, dng_ref):
        _, vjp = jax.vjp(_c_post, o_ref[...], zg_ref[...], ng_ref[...])
        do, dzg, dng = vjp(dy_ref[...])
        do_ref[...] = do
        dzg_ref[...] = dzg.astype(BF)

        @pl.when((pl.program_id(0) == 0) & (pl.program_id(1) == 0))
        def _():
            dng_ref[...] = jnp.zeros_like(dng_ref)

        dng_ref[...] += dng

    do, dzg, dng = pl.pallas_call(
        body, name=name, grid=(Bq, H), in_specs=[blk, gate, vec, blk], out_specs=[blk, blk, vec],
        out_shape=[jax.ShapeDtypeStruct((Bq, S, D), F32), jax.ShapeDtypeStruct((Bq, S, D), BF),
                   jax.ShapeDtypeStruct((1, LANES), F32)],
        compiler_params=_params(("arbitrary", "arbitrary")))(o, z, ng.reshape(1, LANES), dy)
    return do, dzg, dng.reshape(LANES)


def _c_param_rows(a_log, dt_bias):
    p = jnp.concatenate([a_log, dt_bias], axis=0).T
    return jnp.broadcast_to(p[:, :, None], p.shape + (LANES,)).astype(F32)


def _local_step(x, tgt, W):
    Bq, S, D = x.shape
    T = Bq * S
    H = D // LANES
    L = W["norm_mix_g"].shape[0]
    seq = lambda t: t.reshape(Bq, S, t.shape[-1])
    flat = lambda t: t.reshape(T, t.shape[-1])

    xs = flat(x)
    saved = []
    for i in range(L):
        kind, j = i % N_MIXERS, i // N_MIXERS
        tag = f"l{i}"
        sv = {"x": xs}
        hn = _rms_fwd(xs, W["norm_mix_g"][i], f"{tag}_mix_norm")
        sv["hn"] = hn
        if kind == 0:
            z = _mm(hn, W["a_w_in"][j], "nn", f"{tag}_a_in")
            y = _a_core_fwd(seq(z), W["a_conv_w"][j], W["a_conv_b"][j], W["a_gate_w"][j], W["a_gate_b"][j],
                            W["a_lambda"][j], f"{tag}_a_core")
            sv["z"] = z
            w_out = W["a_w_out"][j]
        elif kind == 1:
            z = _mm(hn, W["b_w_in"][j], "nn", f"{tag}_b_in")
            bsf = jnp.repeat(W["b_b_s"][j].T, LANES, axis=1)
            y = _b_core_fwd(z, W["b_ln_g"][j], W["b_ln_b"][j], W["b_w_s"][j], bsf, f"{tag}_b_core")
            sv["z"], sv["bsf"] = z, bsf
            w_out = W["b_w_out"][j]
        else:
            w_in = W["c_w_in"][j]
            w_small = jnp.pad(w_in[:, 4 * D:], ((0, 0), (0, LANES - 4 * H)))
            z = _mm(hn, w_in[:, :4 * D], "nn", f"{tag}_c_in")
            zs = _mm(hn, w_small, "nn", f"{tag}_c_in_small")
            cw3 = W["c_conv_w"][j].reshape(CONV_W, 3, D).transpose(1, 0, 2)
            pc = _c_param_rows(W["c_a_log"][j], W["c_dt_bias"][j])
            q, k, v, gbp = _c_pre_fwd(seq(z), seq(zs), cw3, pc, f"{tag}_c_pre")
            o = _c_mid_fwd(q, k, v, gbp, f"{tag}_c_mid")
            y = _c_post_fwd(o, seq(z), W["c_norm_g"][j], f"{tag}_c_post")
            sv.update(z=z, zs=zs, cw3=cw3, pc=pc, q=q, k=k, v=v, gbp=gbp, o=o, w_small=w_small)
            w_out = W["c_w_out"][j]
        y = flat(y)
        sv["y"] = y
        x1 = _mm(y, w_out, "nn", f"{tag}_mix_out", epi="add", extra=xs)
        sv["x1"] = x1
        hn2 = _rms_fwd(x1, W["norm_mlp_g"][i], f"{tag}_mlp_norm")
        act = _mm(hn2, W["mlp_w_up"][i], "nn", f"{tag}_mlp_up", out_dtype=BF, epi="relu2")
        xs = _mm(act, W["mlp_w_down"][i], "nn", f"{tag}_mlp_down", epi="add", extra=x1)
        sv["hn2"], sv["act"] = hn2, act
        saved.append(sv)

    loss, dx, dgf = _final_loss(xs, W["norm_final_g"], flat(tgt), "final_loss")

    G = {"norm_final_g": dgf}
    per_layer = {n: [None] * L for n in ("norm_mix_g", "norm_mlp_g", "mlp_w_up", "mlp_w_down")}
    mixer = {}
    for i in reversed(range(L)):
        kind, j = i % N_MIXERS, i // N_MIXERS
        tag = f"l{i}"
        sv = saved[i]
        dhid = _mm(dx, W["mlp_w_down"][i], "nt", f"{tag}_mlp_dhid", out_dtype=BF, epi="relu2_bwd", extra=sv["act"])
        per_layer["mlp_w_down"][i] = _mm(sv["act"], dx, "tn", f"{tag}_mlp_dwdown")
        per_layer["mlp_w_up"][i] = _mm(sv["hn2"], dhid, "tn", f"{tag}_mlp_dwup")
        dhn2 = _mm(dhid, W["mlp_w_up"][i], "nt", f"{tag}_mlp_dhn")
        dx, per_layer["norm_mlp_g"][i] = _rms_bwd(sv["x1"], W["norm_mlp_g"][i], dhn2, dx, f"{tag}_mlp_norm_bwd")
        g = {}
        if kind == 0:
            dy = _mm(dx, W["a_w_out"][j], "nt", f"{tag}_a_dy")
            g["a_w_out"] = _mm(sv["y"], dx, "tn", f"{tag}_a_dwout")
            dz, g["a_conv_w"], g["a_conv_b"], g["a_gate_w"], g["a_gate_b"], g["a_lambda"] = _a_core_bwd(
                seq(sv["z"]), seq(dy), W["a_conv_w"][j], W["a_conv_b"][j], W["a_gate_w"][j], W["a_gate_b"][j],
                W["a_lambda"][j], f"{tag}_a_core_bwd")
            dz = flat(dz)
            g["a_w_in"] = _mm(sv["hn"], dz, "tn", f"{tag}_a_dwin")
            dhn = _mm(dz, W["a_w_in"][j], "nt", f"{tag}_a_dhn")
        elif kind == 1:
            dy = _mm(dx, W["b_w_out"][j], "nt", f"{tag}_b_dy")
            g["b_w_out"] = _mm(sv["y"], dx, "tn", f"{tag}_b_dwout")
            dz, g["b_ln_g"], g["b_ln_b"], g["b_w_s"], dbsf = _b_core_bwd(
                sv["z"], dy, W["b_ln_g"][j], W["b_ln_b"][j], W["b_w_s"][j], sv["bsf"], f"{tag}_b_core_bwd")
            g["b_b_s"] = dbsf.reshape(SGU_CHUNK, H, LANES).sum(-1).T
            g["b_w_in"] = _mm(sv["hn"], dz, "tn", f"{tag}_b_dwin")
            dhn = _mm(dz, W["b_w_in"][j], "nt", f"{tag}_b_dhn")
        else:
            dy = _mm(dx, W["c_w_out"][j], "nt", f"{tag}_c_dy")
            g["c_w_out"] = _mm(sv["y"], dx, "tn", f"{tag}_c_dwout")
            do, dzg, g["c_norm_g"] = _c_post_bwd(sv["o"], seq(sv["z"]), W["c_norm_g"][j], seq(dy), f"{tag}_c_post_bwd")
            dq, dk, dv, dgbp = _c_mid_bwd(sv["q"], sv["k"], sv["v"], sv["gbp"], do, f"{tag}_c_mid_bwd")
            dzq, dzk, dzv, dzs_h, dcw3, dpc = _c_pre_bwd(seq(sv["z"]), seq(sv["zs"]), sv["cw3"], sv["pc"], dq, dk, dv, dgbp,
                                                         f"{tag}_c_pre_bwd")
            dz = flat(jnp.concatenate([dzq, dzk, dzv, dzg], axis=-1))
            dzs = flat(dzs_h.sum(0)).astype(BF)
            g["c_conv_w"] = dcw3.transpose(1, 0, 2).reshape(CONV_W, 3 * D)
            dpc = dpc.sum(-1)
            g["c_a_log"], g["c_dt_bias"] = dpc[:, :2].T, dpc[:, 2:].T
            dw_main = _mm(sv["hn"], dz, "tn", f"{tag}_c_dwin")
            dw_small = _mm(sv["hn"], dzs, "tn", f"{tag}_c_dwin_small")
            g["c_w_in"] = jnp.concatenate([dw_main, dw_small[:, :4 * H]], axis=1)
            dhn = _mm(dz, W["c_w_in"][j][:, :4 * D], "nt", f"{tag}_c_dhn")
            dhn = _mm(dzs, sv["w_small"], "nt", f"{tag}_c_dhn_small", epi="add", extra=dhn)
        dx, per_layer["norm_mix_g"][i] = _rms_bwd(sv["x"], W["norm_mix_g"][i], dhn, dx, f"{tag}_mix_norm_bwd")
        for n, val in g.items():
            mixer.setdefault(n, {})[j] = val

    for n, vals in per_layer.items():
        G[n] = jnp.stack(vals)
    for n, by_j in mixer.items():
        G[n] = jnp.stack([by_j[j] for j in sorted(by_j)])
    return loss, dx.reshape(Bq, S, D), G


MESH = pl.DeviceIdType.MESH
N_CHIPS = 4
HBM_SPEC = pl.BlockSpec(memory_space=pltpu.HBM)


def _place():
    x, y, c = lax.axis_index("x"), lax.axis_index("y"), lax.axis_index("c")
    others = [(1 - x, y), (x, 1 - y), (1 - x, 1 - y)]
    return x, y, c, others


def _all_gather_xy(bufs, name):
    n = len(bufs)

    def body(*refs):
        ins, outs = refs[:n], refs[n:2 * n]
        send, recv, fsend, frecv, lsem = refs[2 * n:]
        x, y, c, others = _place()
        p = 2 * x + y
        half = lambda b, cc: pl.ds(cc * (ins[b].shape[0] // 2), ins[b].shape[0] // 2)
        local = [pltpu.make_async_copy(ins[b], outs[b].at[p], lsem.at[b]) for b in range(n)]
        for cp in local:
            cp.start()

        def ici(b, j):
            qx, qy = others[j]
            return pltpu.make_async_remote_copy(
                src_ref=ins[b].at[half(b, c)], dst_ref=outs[b].at[p, half(b, c)], send_sem=send.at[b, j],
                recv_sem=recv.at[b, j], device_id=(qx, qy, c), device_id_type=MESH)

        def landed(b, j, cc):
            qx, qy = others[j]
            return outs[b].at[2 * qx + qy, half(b, cc)]

        def d2d(b, j):
            return pltpu.make_async_remote_copy(
                src_ref=landed(b, j, c), dst_ref=landed(b, j, c), send_sem=fsend.at[b, j], recv_sem=frecv.at[b, j],
                device_id=(x, y, 1 - c), device_id_type=MESH)

        pairs = [(b, j) for b in range(n) for j in range(3)]
        for b, j in pairs:
            ici(b, j).start()
        for b, j in pairs:
            pltpu.make_async_remote_copy(
                src_ref=ins[b].at[half(b, c)], dst_ref=landed(b, j, c), send_sem=send.at[b, j], recv_sem=recv.at[b, j],
                device_id=(x, y, c), device_id_type=MESH).wait_recv()
            d2d(b, j).start()
        for b, j in pairs:
            pltpu.make_async_remote_copy(
                src_ref=landed(b, j, 1 - c), dst_ref=landed(b, j, 1 - c), send_sem=fsend.at[b, j], recv_sem=frecv.at[b, j],
                device_id=(x, y, 1 - c), device_id_type=MESH).wait_recv()
        for b, j in pairs:
            ici(b, j).wait_send()
            d2d(b, j).wait_send()
        for cp in local:
            cp.wait()

    return pl.pallas_call(
        body, name=name, in_specs=[HBM_SPEC] * n, out_specs=[HBM_SPEC] * n,
        out_shape=[jax.ShapeDtypeStruct((N_CHIPS,) + b.shape, b.dtype) for b in bufs],
        scratch_shapes=[pltpu.SemaphoreType.DMA((n, 3))] * 4 + [pltpu.SemaphoreType.DMA((n,))],
        compiler_params=pltpu.CompilerParams(has_side_effects=True))(*bufs)


def _swap_halves(g, name):
    nq, _, h, cols = g.shape

    def body(g_ref, o_ref, send, recv):
        x, y, c, _ = _place()
        cp = pltpu.make_async_remote_copy(src_ref=g_ref.at[:, 1 - c], dst_ref=o_ref, send_sem=send, recv_sem=recv,
                                          device_id=(x, y, 1 - c), device_id_type=MESH)
        cp.start()
        cp.wait()

    return pl.pallas_call(
        body, name=name, in_specs=[HBM_SPEC], out_specs=HBM_SPEC, out_shape=jax.ShapeDtypeStruct((nq, h, cols), g.dtype),
        scratch_shapes=[pltpu.SemaphoreType.DMA, pltpu.SemaphoreType.DMA],
        compiler_params=pltpu.CompilerParams(has_side_effects=True))(g)


def _pair_sum(g, got, name):
    nq, _, h, cols = g.shape
    tr = min(512, h)

    def body(c_ref, g_ref, r_ref, o_ref):
        o_ref[...] = g_ref[...] + r_ref[...]

    spec = pl.BlockSpec((None, tr, cols), lambda q, i, c_ref: (q, i, 0))
    return pl.pallas_call(
        body, name=name,
        grid_spec=pltpu.PrefetchScalarGridSpec(
            num_scalar_prefetch=1, grid=(nq, h // tr),
            in_specs=[pl.BlockSpec((None, None, tr, cols), lambda q, i, c_ref: (q, c_ref[0], i, 0)), spec], out_specs=spec),
        out_shape=jax.ShapeDtypeStruct((nq, h, cols), g.dtype),
        compiler_params=_params(("parallel", "parallel")))(lax.axis_index("c").astype(jnp.int32).reshape(1), g, got)


def _scatter_xy(p_sum, name):
    nq, h, cols = p_sum.shape

    def body(p_ref, o_ref, send, recv, lsem):
        x, y, c, others = _place()
        me = 2 * x + y
        local = pltpu.make_async_copy(p_ref.at[me], o_ref.at[me], lsem)
        local.start()
        cps = []
        for j, (qx, qy) in enumerate(others):
            cps.append(pltpu.make_async_remote_copy(
                src_ref=p_ref.at[2 * qx + qy], dst_ref=o_ref.at[me], send_sem=send.at[j], recv_sem=recv.at[j],
                device_id=(qx, qy, c), device_id_type=MESH))
            cps[-1].start()
        for j, (qx, qy) in enumerate(others):
            pltpu.make_async_remote_copy(
                src_ref=p_ref.at[me], dst_ref=o_ref.at[2 * qx + qy], send_sem=send.at[j], recv_sem=recv.at[j],
                device_id=(qx, qy, c), device_id_type=MESH).wait_recv()
        for cp in cps:
            cp.wait_send()
        local.wait()

    return pl.pallas_call(
        body, name=name, in_specs=[HBM_SPEC], out_specs=HBM_SPEC, out_shape=jax.ShapeDtypeStruct((nq, h, cols), p_sum.dtype),
        scratch_shapes=[pltpu.SemaphoreType.DMA((3,)), pltpu.SemaphoreType.DMA((3,)), pltpu.SemaphoreType.DMA],
        compiler_params=pltpu.CompilerParams(has_side_effects=True))(p_sum)


def _chip_sum(r4, name):
    nq, h, cols = r4.shape
    tr = min(512, h)

    def body(r_ref, o_ref):
        o_ref[...] = ((r_ref[0] + r_ref[1]) + r_ref[2]) + r_ref[3]

    return pl.pallas_call(
        body, name=name, grid=(h // tr,), in_specs=[pl.BlockSpec((nq, tr, cols), lambda i: (0, i, 0))],
        out_specs=pl.BlockSpec((tr, cols), lambda i: (i, 0)), out_shape=jax.ShapeDtypeStruct((h, cols), r4.dtype),
        compiler_params=_params(("parallel",)))(r4)


def _join_halves(r, name):
    h, cols = r.shape

    def body(r_ref, o_ref, send, recv, lsem):
        x, y, c, _ = _place()
        local = pltpu.make_async_copy(r_ref, o_ref.at[c], lsem)
        local.start()
        cp = pltpu.make_async_remote_copy(src_ref=r_ref, dst_ref=o_ref.at[c], send_sem=send, recv_sem=recv,
                                          device_id=(x, y, 1 - c), device_id_type=MESH)
        cp.start()
        pltpu.make_async_remote_copy(src_ref=r_ref, dst_ref=o_ref.at[1 - c], send_sem=send, recv_sem=recv,
                                     device_id=(x, y, 1 - c), device_id_type=MESH).wait_recv()
        cp.wait_send()
        local.wait()

    return pl.pallas_call(
        body, name=name, in_specs=[HBM_SPEC], out_specs=HBM_SPEC, out_shape=jax.ShapeDtypeStruct((2, h, cols), r.dtype),
        scratch_shapes=[pltpu.SemaphoreType.DMA, pltpu.SemaphoreType.DMA, pltpu.SemaphoreType.DMA],
        compiler_params=pltpu.CompilerParams(has_side_effects=True))(r)


def _reduce_scatter(g, tag):
    nq, rows, cols = g.shape
    g = g.reshape(nq, 2, rows // 2, cols)
    got = _swap_halves(g, f"{tag}_swap")
    pair = _pair_sum(g, got, f"{tag}_pair_sum")
    r4 = _scatter_xy(pair, f"{tag}_scatter")
    r = _chip_sum(r4, f"{tag}_chip_sum")
    return _join_halves(r, f"{tag}_join").reshape(rows, cols)


def _adamw(w, g, m, v, name):
    rows, cols = w.shape
    tr = rows
    for cand in (512, 344, 256, 128, 64, 32, 16, 8):
        if rows % cand == 0:
            tr = cand
            break

    def body(w_ref, g_ref, m_ref, v_ref, d_ref, nm_ref, nv_ref):
        g_ = g_ref[...]
        m_ = ADAM_B1 * m_ref[...] + (1.0 - ADAM_B1) * g_
        v_ = ADAM_B2 * v_ref[...] + (1.0 - ADAM_B2) * jnp.square(g_)
        m_hat = m_ / (1.0 - ADAM_B1 ** ADAM_STEP)
        v_hat = v_ / (1.0 - ADAM_B2 ** ADAM_STEP)
        d_ref[...] = -ADAM_LR * (m_hat / (jnp.sqrt(v_hat) + ADAM_EPS) + ADAM_WD * w_ref[...])
        nm_ref[...] = m_
        nv_ref[...] = v_

    spec = pl.BlockSpec((tr, cols), lambda i: (i, 0))
    shp = jax.ShapeDtypeStruct((rows, cols), F32)
    return pl.pallas_call(body, name=name, grid=(rows // tr,), in_specs=[spec] * 4, out_specs=[spec] * 3,
                          out_shape=[shp] * 3, compiler_params=_params(("parallel",)))(w, g, m, v)


WEIGHTS = ["norm_mix_g", "norm_mlp_g", "mlp_w_up", "mlp_w_down", "norm_final_g", "a_w_in", "a_conv_w", "a_conv_b",
           "a_gate_w", "a_gate_b", "a_lambda", "a_w_out", "b_w_in", "b_ln_g", "b_ln_b", "b_w_s", "b_b_s", "b_w_out",
           "c_w_in", "c_conv_w", "c_a_log", "c_dt_bias", "c_norm_g", "c_w_out"]
SHARD_AXIS = {"mlp_w_up": 2, "mlp_w_down": 1, "a_w_in": 2, "a_conv_w": 2, "a_conv_b": 1, "a_lambda": 2, "a_w_out": 1,
              "b_w_in": 2, "b_w_out": 1, "c_w_in": 2, "c_conv_w": 2, "c_w_out": 1}
MATMUL_WEIGHTS = ["mlp_w_up", "mlp_w_down", "a_w_in", "a_w_out", "b_w_in", "b_w_out", "c_w_out", "c_w_in"]
SMALL_SHARDED = ["a_conv_w", "a_conv_b", "a_lambda", "c_conv_w"]
REPLICATED = [n for n in WEIGHTS if n not in SHARD_AXIS]
FLAT_COLS = 1024


def _rows_of(shape):
    n = 1
    for s in shape:
        n *= s
    return -(-n // FLAT_COLS)


def _pack(arrays, total_rows, dtype):
    parts = []
    used = 0
    for a in arrays:
        r = _rows_of(a.shape)
        f = a.reshape(-1).astype(dtype)
        parts.append(jnp.pad(f, (0, r * FLAT_COLS - f.shape[0])).reshape(r, FLAT_COLS))
        used += r
    if total_rows > used:
        parts.append(jnp.zeros((total_rows - used, FLAT_COLS), dtype))
    return jnp.concatenate(parts, axis=0)


def _unpack(buf, shapes):
    out, r0 = [], 0
    for shp in shapes:
        r = _rows_of(shp)
        n = 1
        for s in shp:
            n *= s
        out.append(buf[r0:r0 + r].reshape(-1)[:n].reshape(shp))
        r0 += r
    return out


def _round_up(n, m):
    return -(-n // m) * m


def kernel(x, norm_mix_g, norm_mlp_g, mlp_w_up, mlp_w_down, norm_final_g, a_w_in, a_conv_w, a_conv_b, a_gate_w, a_gate_b, a_lambda, a_w_out, b_w_in, b_ln_g, b_ln_b, b_w_s, b_b_s, b_w_out, c_w_in, c_conv_w, c_a_log, c_dt_bias, c_norm_g, c_w_out, loss_target, m_norm_mix_g, m_norm_mlp_g, m_mlp_w_up, m_mlp_w_down, m_norm_final_g, m_a_w_in, m_a_conv_w, m_a_conv_b, m_a_gate_w, m_a_gate_b, m_a_lambda, m_a_w_out, m_b_w_in, m_b_ln_g, m_b_ln_b, m_b_w_s, m_b_b_s, m_b_w_out, m_c_w_in, m_c_conv_w, m_c_a_log, m_c_dt_bias, m_c_norm_g, m_c_w_out, v_norm_mix_g, v_norm_mlp_g, v_mlp_w_up, v_mlp_w_down, v_norm_final_g, v_a_w_in, v_a_conv_w, v_a_conv_b, v_a_gate_w, v_a_gate_b, v_a_lambda, v_a_w_out, v_b_w_in, v_b_ln_g, v_b_ln_b, v_b_w_s, v_b_b_s, v_b_w_out, v_c_w_in, v_c_conv_w, v_c_a_log, v_c_dt_bias, v_c_norm_g, v_c_w_out):
    given = dict(locals())
    w_loc = {n: given[n] for n in WEIGHTS}
    m_loc = {n: given["m_" + n] for n in WEIGHTS}
    v_loc = {n: given["v_" + n] for n in WEIGHTS}

    mat_rows = _round_up(sum(_rows_of(w_loc[n].shape) for n in MATMUL_WEIGHTS), 32)
    small_rows = _round_up(sum(_rows_of(w_loc[n].shape) for n in SMALL_SHARDED), 16)
    mat_all, small_all = _all_gather_xy(
        [_pack([w_loc[n] for n in MATMUL_WEIGHTS], mat_rows, BF),
         _pack([w_loc[n] for n in SMALL_SHARDED], small_rows, F32)], "gather_weights")
    W = {n: w_loc[n] for n in REPLICATED}
    for names, buf in ((MATMUL_WEIGHTS, mat_all), (SMALL_SHARDED, small_all)):
        per_chip = [_unpack(buf[q], [w_loc[n].shape for n in names]) for q in range(N_CHIPS)]
        for i, n in enumerate(names):
            W[n] = jnp.concatenate([per_chip[q][i] for q in range(N_CHIPS)], axis=SHARD_AXIS[n])

    loss, grad_x, G = _local_step(x, loss_target, W)
    loss = lax.psum(loss, ("x", "y", "c"))

    sharded = MATMUL_WEIGHTS + SMALL_SHARDED
    rep_rows = _round_up(sum(_rows_of(w_loc[n].shape) for n in REPLICATED), N_CHIPS * 16)
    rep_flat = _pack([G[n] for n in REPLICATED], rep_rows, F32).reshape(N_CHIPS, rep_rows // N_CHIPS, FLAT_COLS)
    shard_rows = sum(_rows_of(w_loc[n].shape) for n in sharded)
    total_rows = _round_up(shard_rows + rep_rows // N_CHIPS, 1024)
    slots = []
    for q in range(N_CHIPS):
        pieces = []
        for n in sharded:
            width = w_loc[n].shape[SHARD_AXIS[n]]
            pieces.append(lax.slice_in_dim(G[n], q * width, (q + 1) * width, axis=SHARD_AXIS[n]))
        body = _pack(pieces, shard_rows, F32)
        pad = jnp.zeros((total_rows - shard_rows - rep_rows // N_CHIPS, FLAT_COLS), F32)
        slots.append(jnp.concatenate([body, rep_flat[q], pad], axis=0))
    red = _reduce_scatter(jnp.stack(slots), "grads")
    g_loc = dict(zip(sharded, _unpack(red, [w_loc[n].shape for n in sharded])))
    rep_quarter = red[shard_rows:shard_rows + rep_rows // N_CHIPS]
    (rep_all,) = _all_gather_xy([rep_quarter], "gather_replicated_grads")
    g_loc.update(zip(REPLICATED, _unpack(rep_all.reshape(rep_rows, FLAT_COLS), [w_loc[n].shape for n in REPLICATED])))

    delta, new_m, new_v = {}, {}, {}
    big = [n for n in MATMUL_WEIGHTS if w_loc[n].size % FLAT_COLS == 0]
    for n in big:
        shp = w_loc[n].shape
        two_d = lambda a: a.reshape(-1, FLAT_COLS)
        d, nm, nv = _adamw(two_d(w_loc[n]), two_d(g_loc[n]), two_d(m_loc[n]), two_d(v_loc[n]), f"adamw_{n}")
        delta[n], new_m[n], new_v[n] = d.reshape(shp), nm.reshape(shp), nv.reshape(shp)
    small = [n for n in WEIGHTS if n not in big]
    small_shapes = [w_loc[n].shape for n in small]
    rows = _round_up(sum(_rows_of(s) for s in small_shapes), 8)
    packed = [_pack([src[n] for n in small], rows, F32) for src in (w_loc, g_loc, m_loc, v_loc)]
    for dst, buf in zip((delta, new_m, new_v), _adamw(*packed, "adamw_small")):
        dst.update(zip(small, _unpack(buf, small_shapes)))

    return (loss, grad_x, *[g_loc[n] for n in WEIGHTS], *[delta[n] for n in WEIGHTS],
            *[new_m[n] for n in WEIGHTS], *[new_v[n] for n in WEIGHTS])
```

```python
import functools

import jax
import jax.numpy as jnp
from jax import lax
from jax.experimental import pallas as pl
from jax.experimental.pallas import tpu as pltpu

F32 = jnp.float32
BF = jnp.bfloat16

LANES = 128
VMEM_LIMIT = 56 * 1024 * 1024
RG_C = 8.0
SGU_CHUNK = 128
GDN_CHUNK = 64
CONV_W = 4
N_MIXERS = 3

ADAM_LR = 0.001
ADAM_B1 = 0.9
ADAM_B2 = 0.999
ADAM_EPS = 1e-08
ADAM_WD = 0.01
ADAM_STEP = 10


def _params(sem=None):
    return pltpu.CompilerParams(dimension_semantics=sem, vmem_limit_bytes=VMEM_LIMIT)


def _shift_impl(x, s):
    if s == 0:
        return x
    n = x.shape[0]
    row = lax.broadcasted_iota(jnp.int32, x.shape, 0)
    if s > 0:
        return jnp.where(row >= s, pltpu.roll(x, s, 0), 0.0)
    return jnp.where(row < n + s, pltpu.roll(x, n + s, 0), 0.0)


@functools.partial(jax.custom_vjp, nondiff_argnums=(1,))
def _shift(x, s):
    return _shift_impl(x, s)


def _shift_fwd(x, s):
    return _shift_impl(x, s), None


def _shift_bwd(s, _, g):
    return (_shift_impl(g, -s),)


_shift.defvjp(_shift_fwd, _shift_bwd)


def _chunk_cumsum_impl(x, rev, chunk):
    n = x.shape[0]
    rc = lax.broadcasted_iota(jnp.int32, x.shape, 0) & (chunk - 1)
    sh = 1
    while sh < chunk:
        if rev:
            x = x + jnp.where(rc < chunk - sh, pltpu.roll(x, n - sh, 0), 0.0)
        else:
            x = x + jnp.where(rc >= sh, pltpu.roll(x, sh, 0), 0.0)
        sh *= 2
    return x


@functools.partial(jax.custom_vjp, nondiff_argnums=(1, 2))
def _chunk_cumsum(x, rev, chunk):
    return _chunk_cumsum_impl(x, rev, chunk)


def _chunk_cumsum_fwd(x, rev, chunk):
    return _chunk_cumsum_impl(x, rev, chunk), None


def _chunk_cumsum_bwd(rev, chunk, _, g):
    return (_chunk_cumsum_impl(g, not rev, chunk),)


_chunk_cumsum.defvjp(_chunk_cumsum_fwd, _chunk_cumsum_bwd)


def _rms(x, g, eps=1e-6):
    return x * lax.rsqrt(jnp.mean(x * x, axis=-1, keepdims=True) + eps) * g


def _neg_expm1(y, ey):
    series = -y * (1.0 + y * (1 / 2) * (1.0 + y * (1 / 3) * (1.0 + y * (1 / 4) * (1.0 + y * (1 / 5) * (1.0 + y * (1 / 6))))))
    return jnp.where(y > -0.125, series, 1.0 - ey)


def _bmm_raw(a, b, form):
    r = a.ndim - 2
    con = {"nn": ((r + 1,), (r,)), "nt": ((r + 1,), (r + 1,)), "tn": ((r,), (r,))}[form]
    batch = ((0,), (0,)) if r else ((), ())
    return lax.dot_general(a.astype(BF), b.astype(BF), (con, batch), preferred_element_type=F32)


@functools.partial(jax.custom_vjp, nondiff_argnums=(2,))
def _bmm(a, b, form):
    return _bmm_raw(a, b, form)


def _bmm_fwd(a, b, form):
    return _bmm_raw(a, b, form), (a, b)


def _bmm_bwd(form, res, g):
    a, b = res
    if form == "nn":
        da, db = _bmm_raw(g, b, "nt"), _bmm_raw(a, g, "tn")
    elif form == "nt":
        da, db = _bmm_raw(g, b, "nn"), _bmm_raw(g, a, "tn")
    else:
        da, db = _bmm_raw(b, g, "nt"), _bmm_raw(a, g, "nn")
    return da.astype(a.dtype), db.astype(b.dtype)


_bmm.defvjp(_bmm_fwd, _bmm_bwd)


def _conv4(z, rows):
    out = rows[0] * _shift(z, 2)
    for k in range(1, CONV_W):
        out = out + rows[k] * _shift(z, 2 - k)
    return out


_DIMS = {"nn": ((1,), (0,)), "nt": ((1,), (1,)), "tn": ((0,), (0,))}


def _mm(a, b, mode, name, *, out_dtype=F32, epi=None, extra=None, tm=1024, tn=1024, tk=1024):
    if mode == "tn":
        K, M = a.shape
    else:
        M, K = a.shape
    N = b.shape[0] if mode == "nt" else b.shape[1]
    tm, tn, tk = min(tm, M), min(tn, N), min(tk, K)
    assert M % tm == 0 and N % tn == 0 and K % tk == 0, (name, M, N, K)
    nk = K // tk
    a_spec = pl.BlockSpec((tk, tm), lambda i, j, k: (k, i)) if mode == "tn" else pl.BlockSpec((tm, tk), lambda i, j, k: (i, k))
    b_spec = pl.BlockSpec((tn, tk), lambda i, j, k: (j, k)) if mode == "nt" else pl.BlockSpec((tk, tn), lambda i, j, k: (k, j))
    o_spec = pl.BlockSpec((tm, tn), lambda i, j, k: (i, j))
    ins, specs = [a, b], [a_spec, b_spec]
    if epi in ("add", "relu2_bwd"):
        ins.append(extra)
        specs.append(o_spec)

    def body(*refs):
        a_ref, b_ref = refs[0], refs[1]
        e_ref = refs[2] if len(ins) == 3 else None
        o_ref = refs[len(ins)]

        def product():
            return lax.dot_general(a_ref[...].astype(BF), b_ref[...].astype(BF), (_DIMS[mode], ((), ())),
                                   preferred_element_type=F32)

        def finish(r):
            if epi == "relu2":
                r = jnp.square(jnp.maximum(r, 0.0))
            elif epi == "add":
                r = r + e_ref[...]
            elif epi == "relu2_bwd":
                r = r * (2.0 * jnp.sqrt(e_ref[...].astype(F32)))
            o_ref[...] = r.astype(out_dtype)

        if nk == 1:
            finish(product())
            return
        acc = refs[-1]
        k = pl.program_id(2)

        @pl.when(k == 0)
        def _():
            acc[...] = product()

        @pl.when(k > 0)
        def _():
            acc[...] += product()

        @pl.when(k == nk - 1)
        def _():
            finish(acc[...])

    return pl.pallas_call(
        body, name=name, grid=(M // tm, N // tn, nk), in_specs=specs, out_specs=o_spec,
        out_shape=jax.ShapeDtypeStruct((M, N), out_dtype),
        scratch_shapes=[pltpu.VMEM((tm, tn), F32)] if nk > 1 else [],
        compiler_params=_params(("parallel", "parallel", "arbitrary")))(*ins)


def _rows_tile(T):
    return min(512, T)


def _rms_fwd(x, g, name):
    T, D = x.shape
    tr = _rows_tile(T)

    def body(x_ref, g_ref, o_ref):
        o_ref[...] = _rms(x_ref[...], g_ref[...]).astype(BF)

    return pl.pallas_call(
        body, name=name, grid=(T // tr,),
        in_specs=[pl.BlockSpec((tr, D), lambda i: (i, 0)), pl.BlockSpec((1, D), lambda i: (0, 0))],
        out_specs=pl.BlockSpec((tr, D), lambda i: (i, 0)), out_shape=jax.ShapeDtypeStruct((T, D), BF),
        compiler_params=_params(("parallel",)))(x, g.reshape(1, D))


def _rms_bwd(x, g, dhn, dres, name):
    T, D = x.shape
    tr = _rows_tile(T)

    def body(x_ref, g_ref, dhn_ref, dres_ref, dx_ref, dg_ref):
        _, vjp = jax.vjp(_rms, x_ref[...], g_ref[...])
        dx, dg = vjp(dhn_ref[...])
        dx_ref[...] = dres_ref[...] + dx

        @pl.when(pl.program_id(0) == 0)
        def _():
            dg_ref[...] = jnp.zeros_like(dg_ref)

        dg_ref[...] += dg

    row = pl.BlockSpec((tr, D), lambda i: (i, 0))
    vec = pl.BlockSpec((1, D), lambda i: (0, 0))
    dx, dg = pl.pallas_call(
        body, name=name, grid=(T // tr,), in_specs=[row, vec, row, row], out_specs=[row, vec],
        out_shape=[jax.ShapeDtypeStruct((T, D), F32), jax.ShapeDtypeStruct((1, D), F32)],
        compiler_params=_params(("arbitrary",)))(x, g.reshape(1, D), dhn, dres)
    return dx, dg.reshape(D)


def _final_loss(x, g, tgt, name):
    T, D = x.shape
    tr = _rows_tile(T)

    def body(x_ref, g_ref, t_ref, l_ref, dx_ref, dg_ref):
        y, vjp = jax.vjp(_rms, x_ref[...], g_ref[...])
        err = y - t_ref[...]
        dx, dg = vjp(err * (1.0 / D))
        dx_ref[...] = dx

        @pl.when(pl.program_id(0) == 0)
        def _():
            dg_ref[...] = jnp.zeros_like(dg_ref)
            l_ref[...] = jnp.zeros_like(l_ref)

        dg_ref[...] += dg
        l_ref[...] += (0.5 / D) * jnp.sum(jnp.sum(err * err, axis=1, keepdims=True), axis=0, keepdims=True)

    row = pl.BlockSpec((tr, D), lambda i: (i, 0))
    vec = pl.BlockSpec((1, D), lambda i: (0, 0))
    loss, dx, dg = pl.pallas_call(
        body, name=name, grid=(T // tr,), in_specs=[row, vec, row],
        out_specs=[pl.BlockSpec((1, LANES), lambda i: (0, 0)), row, vec],
        out_shape=[jax.ShapeDtypeStruct((1, LANES), F32), jax.ShapeDtypeStruct((T, D), F32),
                   jax.ShapeDtypeStruct((1, D), F32)],
        compiler_params=_params(("arbitrary",)))(x, g.reshape(1, D), tgt)
    return loss[0, 0], dx, dg.reshape(D)


def _a_pre(zx, cws, cb, gws, gbs, lams):
    xr = _conv4(zx, cws) + cb
    out = []
    for d in range(2):
        r = jax.nn.sigmoid(_bmm(xr, gws[2 * d], "nn") + gbs[2 * d])
        ig = jax.nn.sigmoid(_bmm(xr, gws[2 * d + 1], "nn") + gbs[2 * d + 1])
        log_a = -RG_C * r * jax.nn.softplus(-lams[d])
        a = jnp.exp(log_a)
        out += [a, jnp.sqrt(_neg_expm1(2.0 * log_a, a * a)) * ig * xr]
    return tuple(out)


def _a_post(h0, h1, zg):
    return (h0 + h1) * jax.nn.gelu(zg)


SUBLANES = 8
SCAN_TILES = 8


def _scan_jobs(jobs):
    S, C = jobs[0][0].shape
    U = min(SCAN_TILES, S // SUBLANES)
    rows = U * SUBLANES
    row = lax.broadcasted_iota(jnp.int32, (SUBLANES, C), 0)

    def prefix(a, b, reverse):
        for sh in (1, 2, 4):
            if reverse:
                m, r = row < SUBLANES - sh, SUBLANES - sh
            else:
                m, r = row >= sh, sh
            a_s = jnp.where(m, pltpu.roll(a, r, 0), 1.0)
            b_s = jnp.where(m, pltpu.roll(b, r, 0), 0.0)
            b = a * b_s + b
            a = a * a_s
        return a, b

    def step(i, carries):
        out = []
        for (a_ref, b_ref, h_ref, reverse), c in zip(jobs, carries):
            blk = (S // rows - 1 - i) if reverse else i
            t0 = pl.multiple_of(blk * rows, rows)
            order = range(U - 1, -1, -1) if reverse else range(U)
            edge = slice(0, 1) if reverse else slice(SUBLANES - 1, SUBLANES)
            for j in order:
                sl = pl.ds(t0 + j * SUBLANES, SUBLANES)
                a, b = prefix(a_ref[sl, :], b_ref[sl, :], reverse)
                h_ref[sl, :] = a * jnp.broadcast_to(c, (SUBLANES, C)) + b
                c = a[edge, :] * c + b[edge, :]
            out.append(c)
        return tuple(out)

    lax.fori_loop(0, S // rows, step, tuple(jnp.zeros((1, C), F32) for _ in jobs))


def _a_load_params(cw_ref, cb_ref, gw_ref, gb_ref, lam_ref):
    cws = [cw_ref[k:k + 1, :] for k in range(CONV_W)]
    gws = [gw_ref[d, g] for d in range(2) for g in range(2)]
    gbs = [gb_ref[d, g] for d in range(2) for g in range(2)]
    lams = [lam_ref[d:d + 1, :] for d in range(2)]
    return cws, cb_ref[...], gws, gbs, lams


def _a_in_specs(S, H):
    zg = pl.BlockSpec((None, S, LANES), lambda h, b: (b, 0, h))
    zx = pl.BlockSpec((None, S, LANES), lambda h, b: (b, 0, H + h))
    cw = pl.BlockSpec((CONV_W, LANES), lambda h, b: (0, h))
    cb = pl.BlockSpec((1, LANES), lambda h, b: (0, h))
    gw = pl.BlockSpec((2, 2, None, LANES, LANES), lambda h, b: (0, 0, h, 0, 0))
    gb = pl.BlockSpec((2, 2, None, 1, LANES), lambda h, b: (0, 0, h, 0, 0))
    lam = pl.BlockSpec((2, LANES), lambda h, b: (0, h))
    return zg, zx, cw, cb, gw, gb, lam


def _a_core_fwd(z, cw, cb, gw, gb, lam, name):
    Bq, S, D2 = z.shape
    D = D2 // 2
    H = D // LANES

    def body(zg_ref, zx_ref, cw_ref, cb_ref, gw_ref, gb_ref, lam_ref, y_ref, a_s, b_s, h_s):
        ab = _a_pre(zx_ref[...], *_a_load_params(cw_ref, cb_ref, gw_ref, gb_ref, lam_ref))
        for d in range(2):
            a_s[d] = ab[2 * d]
            b_s[d] = ab[2 * d + 1]
        _scan_jobs([(a_s.at[d], b_s.at[d], h_s.at[d], d == 1) for d in range(2)])
        y_ref[...] = _a_post(h_s[0], h_s[1], zg_ref[...]).astype(BF)

    seq = pltpu.VMEM((2, S, LANES), F32)
    return pl.pallas_call(
        body, name=name, grid=(H, Bq), in_specs=list(_a_in_specs(S, H)),
        out_specs=pl.BlockSpec((None, S, LANES), lambda h, b: (b, 0, h)),
        out_shape=jax.ShapeDtypeStruct((Bq, S, D), BF), scratch_shapes=[seq, seq, seq],
        compiler_params=_params(("parallel", "arbitrary")))(z, z, cw, cb.reshape(1, D), gw, gb.reshape(2, 2, H, 1, LANES), lam)


def _a_core_bwd(z, dy, cw, cb, gw, gb, lam, name):
    Bq, S, D2 = z.shape
    D = D2 // 2
    H = D // LANES

    def body(zg_ref, zx_ref, dy_ref, cw_ref, cb_ref, gw_ref, gb_ref, lam_ref,
             dzg_ref, dzx_ref, dcw_ref, dcb_ref, dgw_ref, dgb_ref, dlam_ref, a_s, b_s, h_s, l_s):
        prm = _a_load_params(cw_ref, cb_ref, gw_ref, gb_ref, lam_ref)
        ab = _a_pre(zx_ref[...], *prm)
        for d in range(2):
            a_s[d] = ab[2 * d]
            b_s[d] = ab[2 * d + 1]
        _scan_jobs([(a_s.at[d], b_s.at[d], h_s.at[d], d == 1) for d in range(2)])
        _, post_vjp = jax.vjp(_a_post, h_s[0], h_s[1], zg_ref[...])
        dh0, dh1, dzg = post_vjp(dy_ref[...])
        dzg_ref[...] = dzg.astype(BF)
        for d, dh in ((0, dh0), (1, dh1)):
            b_s[d] = dh
            a_s[d] = _shift(a_s[d], -1 if d == 0 else 1)
        _scan_jobs([(a_s.at[d], b_s.at[d], l_s.at[d], d == 0) for d in range(2)])
        cot = []
        for d in range(2):
            cot += [l_s[d] * _shift(h_s[d], 1 if d == 0 else -1), l_s[d]]
        _, pre_vjp = jax.vjp(_a_pre, zx_ref[...], *prm)
        dzx, dcws, dcb, dgws, dgbs, dlams = pre_vjp(tuple(cot))
        dzx_ref[...] = dzx.astype(BF)

        @pl.when(pl.program_id(1) == 0)
        def _():
            for r in (dcw_ref, dcb_ref, dgw_ref, dgb_ref, dlam_ref):
                r[...] = jnp.zeros_like(r)

        for k in range(CONV_W):
            dcw_ref[k:k + 1, :] += dcws[k]
        dcb_ref[...] += dcb
        for d in range(2):
            dlam_ref[d:d + 1, :] += dlams[d]
            for g in range(2):
                dgw_ref[d, g] += dgws[2 * d + g]
                dgb_ref[d, g] += dgbs[2 * d + g]

    zg, zx, cws, cbs, gws, gbs, lams = _a_in_specs(S, H)
    dyspec = pl.BlockSpec((None, S, LANES), lambda h, b: (b, 0, h))
    seq = pltpu.VMEM((2, S, LANES), F32)
    dzg, dzx, dcw, dcb, dgw, dgb, dlam = pl.pallas_call(
        body, name=name, grid=(H, Bq), in_specs=[zg, zx, dyspec, cws, cbs, gws, gbs, lams],
        out_specs=[dyspec, dyspec, cws, cbs, gws, gbs, lams],
        out_shape=[jax.ShapeDtypeStruct((Bq, S, D), BF), jax.ShapeDtypeStruct((Bq, S, D), BF),
                   jax.ShapeDtypeStruct((CONV_W, D), F32), jax.ShapeDtypeStruct((1, D), F32),
                   jax.ShapeDtypeStruct((2, 2, H, LANES, LANES), F32), jax.ShapeDtypeStruct((2, 2, H, 1, LANES), F32),
                   jax.ShapeDtypeStruct((2, D), F32)],
        scratch_shapes=[seq, seq, seq, seq],
        compiler_params=_params(("parallel", "arbitrary")))(z, z, dy, cw, cb.reshape(1, D), gw, gb.reshape(2, 2, H, 1, LANES), lam)
    dz = jnp.concatenate([dzg, dzx], axis=-1)
    return dz, dcw, dcb.reshape(D), dgw, dgb.reshape(2, 2, H, LANES), dlam


def _b_fn(z, lng, lnb, wss, bsf):
    D = z.shape[1] // 2
    zz = jax.nn.gelu(z)
    u, v = zz[:, :D], zz[:, D:]
    mu = jnp.mean(v, axis=-1, keepdims=True)
    var = jnp.mean(jnp.square(v - mu), axis=-1, keepdims=True)
    vn = (v - mu) * lax.rsqrt(var + 1e-5) * lng + lnb
    vs = jnp.concatenate([_bmm(wss[g], vn[:, g * LANES:(g + 1) * LANES], "nn") for g in range(D // LANES)], axis=1)
    return u * (vs + bsf)


def _b_specs(D, GB):
    row = lambda w: pl.BlockSpec((SGU_CHUNK, w), lambda i: (i, 0))
    vec = pl.BlockSpec((1, D), lambda i: (0, 0))
    ws = pl.BlockSpec((GB, SGU_CHUNK, SGU_CHUNK), lambda i: (0, 0, 0))
    bsf = pl.BlockSpec((SGU_CHUNK, D), lambda i: (0, 0))
    return row, vec, ws, bsf


def _b_core_fwd(z, lng, lnb, ws, bsf, name):
    T, D2 = z.shape
    D = D2 // 2
    GB = D // LANES
    row, vec, wspec, bspec = _b_specs(D, GB)

    def body(z_ref, lng_ref, lnb_ref, ws_ref, bsf_ref, y_ref):
        wss = [ws_ref[g] for g in range(GB)]
        y_ref[...] = _b_fn(z_ref[...], lng_ref[...], lnb_ref[...], wss, bsf_ref[...]).astype(BF)

    return pl.pallas_call(
        body, name=name, grid=(T // SGU_CHUNK,), in_specs=[row(D2), vec, vec, wspec, bspec], out_specs=row(D),
        out_shape=jax.ShapeDtypeStruct((T, D), BF), compiler_params=_params(("parallel",)))(
            z, lng.reshape(1, D), lnb.reshape(1, D), ws, bsf)


def _b_core_bwd(z, dy, lng, lnb, ws, bsf, name):
    T, D2 = z.shape
    D = D2 // 2
    GB = D // LANES
    row, vec, wspec, bspec = _b_specs(D, GB)

    def body(z_ref, dy_ref, lng_ref, lnb_ref, ws_ref, bsf_ref, dz_ref, dlng_ref, dlnb_ref, dws_ref, dbsf_ref):
        wss = [ws_ref[g] for g in range(GB)]
        _, vjp = jax.vjp(_b_fn, z_ref[...], lng_ref[...], lnb_ref[...], wss, bsf_ref[...])
        dz, dlng, dlnb, dwss, dbsf = vjp(dy_ref[...])
        dz_ref[...] = dz.astype(BF)

        @pl.when(pl.program_id(0) == 0)
        def _():
            for r in (dlng_ref, dlnb_ref, dws_ref, dbsf_ref):
                r[...] = jnp.zeros_like(r)

        dlng_ref[...] += dlng
        dlnb_ref[...] += dlnb
        dbsf_ref[...] += dbsf
        for g in range(GB):
            dws_ref[g] += dwss[g]

    dz, dlng, dlnb, dws, dbsf = pl.pallas_call(
        body, name=name, grid=(T // SGU_CHUNK,), in_specs=[row(D2), row(D), vec, vec, wspec, bspec],
        out_specs=[row(D2), vec, vec, wspec, bspec],
        out_shape=[jax.ShapeDtypeStruct((T, D2), BF), jax.ShapeDtypeStruct((1, D), F32), jax.ShapeDtypeStruct((1, D), F32),
                   jax.ShapeDtypeStruct((GB, SGU_CHUNK, SGU_CHUNK), F32), jax.ShapeDtypeStruct((SGU_CHUNK, D), F32)],
        compiler_params=_params(("arbitrary",)))(z, dy, lng.reshape(1, D), lnb.reshape(1, D), ws, bsf)
    return dz, dlng.reshape(D), dlnb.reshape(D), dws, dbsf


def _lane_is(j):
    return lax.broadcasted_iota(jnp.int32, (1, LANES), 1) == j


def _lane_col(x, j):
    return jnp.sum(jnp.where(_lane_is(j), x, 0.0), axis=1, keepdims=True)


def _c_pre(zq, zk, zv, zs, cwq, cwk, cwv, pcs, head, HC):
    q = jax.nn.silu(_conv4(zq, cwq))
    k = jax.nn.silu(_conv4(zk, cwk))
    v = jax.nn.silu(_conv4(zv, cwv))
    q = q * lax.rsqrt(jnp.sum(q * q, axis=-1, keepdims=True) + 1e-6) * (LANES ** -0.5)
    k = k * lax.rsqrt(jnp.sum(k * k, axis=-1, keepdims=True) + 1e-6)
    gbp = jnp.zeros_like(zs)
    for d in range(2):
        a_logit = _lane_col(zs, d * HC + head)
        b_logit = _lane_col(zs, 2 * HC + d * HC + head)
        g = -jnp.exp(pcs[d]) * jax.nn.softplus(a_logit + pcs[2 + d])
        beta = jnp.broadcast_to(jax.nn.sigmoid(b_logit), g.shape)
        gbp = gbp + jnp.where(_lane_is(2 * d), g, 0.0) + jnp.where(_lane_is(2 * d + 1), beta, 0.0)
    return q, k, v, gbp


def _mm3(x, y):
    xh, yh = x.astype(BF), y.astype(BF)
    xl, yl = (x - xh.astype(F32)).astype(BF), (y - yh.astype(F32)).astype(BF)
    return _bmm_raw(xh, yh, "nn") + _bmm_raw(xh, yl, "nn") + _bmm_raw(xl, yh, "nn")


def _tri_inv_impl(a):
    C = a.shape[-1]
    eye = (lax.broadcasted_iota(jnp.int32, (1, C, C), 1) == lax.broadcasted_iota(jnp.int32, (1, C, C), 2)).astype(F32)
    r = eye - a
    p = a
    n = 2
    while n < C:
        p = _mm3(p, p)
        r = r + _mm3(r, p)
        n *= 2
    return r


@jax.custom_vjp
def _tri_inv(a):
    return _tri_inv_impl(a)


def _tri_inv_fwd(a):
    t = _tri_inv_impl(a)
    return t, t


def _tri_inv_bwd(t, g):
    tt = jnp.swapaxes(t, 1, 2)
    return (-_bmm_raw(_bmm_raw(tt, g, "nn"), tt, "nn"),)


_tri_inv.defvjp(_tri_inv_fwd, _tri_inv_bwd)


@jax.custom_vjp
def _pair_diff(gc3):
    m = gc3[:, :, :gc3.shape[1]]
    return m - jnp.swapaxes(m, 1, 2)


def _pair_diff_fwd(gc3):
    return _pair_diff(gc3), None


def _pair_diff_bwd(_, g):
    d = jnp.sum(g, axis=2, keepdims=True) - jnp.sum(jnp.swapaxes(g, 1, 2), axis=2, keepdims=True)
    return (jnp.broadcast_to(d * (1.0 / LANES), d.shape[:2] + (LANES,)),)


_pair_diff.defvjp(_pair_diff_fwd, _pair_diff_bwd)


def _c_phase1(q, k, v, gbp, rev):
    S = q.shape[0]
    C = GDN_CHUNK
    N = S // C
    col = 2 if rev else 0
    gB = jnp.broadcast_to(_lane_col(gbp, col), (S, LANES))
    bB = jnp.broadcast_to(_lane_col(gbp, col + 1), (S, LANES))
    r3 = lambda t: t.reshape(N, C, LANES)
    gc3 = r3(_chunk_cumsum(gB, rev, C))
    q3, k3, v3, b3, g3 = r3(q), r3(k), r3(v), r3(bB), r3(gB)
    ri = lax.broadcasted_iota(jnp.int32, (1, C, C), 1)
    ci = lax.broadcasted_iota(jnp.int32, (1, C, C), 2)
    incl = (ri <= ci) if rev else (ri >= ci)
    strict = (ri < ci) if rev else (ri > ci)
    decay = jnp.where(incl, jnp.exp(jnp.where(incl, _pair_diff(gc3), 0.0)), 0.0)
    kb = k3 * b3
    vb = v3 * b3
    A = jnp.where(strict, _bmm(kb, k3, "nt") * decay, 0.0)
    T = _tri_inv(A)
    egc = jnp.exp(gc3)
    u = _bmm(T, vb, "nn")
    w = _bmm(T, kb * egc, "nn")
    qk = _bmm(q3, k3, "nt") * decay
    glast = jnp.sum(g3, axis=1, keepdims=True)
    kd = k3 * jnp.exp(glast - gc3)
    k2 = _bmm(kd, w, "tn")
    z = _bmm(kd, u, "tn")
    qe2 = q3 * egc - _bmm(qk, w, "nn")
    o0 = _bmm(qk, u, "nn")
    return k2, z, jnp.exp(glast), qe2, o0


def _c_next_state(state, k2, z, eg):
    return state * eg - _bmm(k2, state, "nn") + z


def _c_out(state, qe2, o0):
    return _bmm(qe2, state, "nn") + o0


def _c_post(o, zg, ng):
    return _rms(o, ng) * jax.nn.silu(zg)


def _c_pre_specs(S, H):
    col = lambda c0: pl.BlockSpec((None, S, LANES), lambda h, b: (b, 0, c0 * H + h))
    zs = pl.BlockSpec((None, S, LANES), lambda h, b: (b, 0, 0))
    cw = lambda c0: pl.BlockSpec((None, CONV_W, LANES), lambda h, b: (c0, 0, h))
    pc = pl.BlockSpec((None, 4, LANES), lambda h, b: (h, 0, 0))
    return col, zs, cw, pc


def _c_pre_fwd(z, zs, cw3, pc, name):
    Bq, S, D4 = z.shape
    D = D4 // 4
    H = D // LANES
    col, zss, cw, pcs = _c_pre_specs(S, H)

    def body(zq_ref, zk_ref, zv_ref, zs_ref, cwq_ref, cwk_ref, cwv_ref, pc_ref, q_ref, k_ref, v_ref, gbp_ref):
        rows = lambda r: [r[i:i + 1, :] for i in range(r.shape[0])]
        q, k, v, gbp = _c_pre(zq_ref[...], zk_ref[...], zv_ref[...], zs_ref[...], rows(cwq_ref), rows(cwk_ref),
                              rows(cwv_ref), rows(pc_ref), pl.program_id(0), H)
        q_ref[...] = q
        k_ref[...] = k
        v_ref[...] = v
        gbp_ref[...] = gbp

    out = pl.BlockSpec((None, S, LANES), lambda h, b: (b, 0, h))
    shp = jax.ShapeDtypeStruct((Bq, S, D), F32)
    return pl.pallas_call(
        body, name=name, grid=(H, Bq), in_specs=[col(0), col(1), col(2), zss, cw(0), cw(1), cw(2), pcs],
        out_specs=[out] * 4, out_shape=[shp] * 4, compiler_params=_params(("parallel", "arbitrary")))(
            z, z, z, zs, cw3, cw3, cw3, pc)


def _c_pre_bwd(z, zs, cw3, pc, dq, dk, dv, dgbp, name):
    Bq, S, D4 = z.shape
    D = D4 // 4
    H = D // LANES
    col, zss, cw, pcs = _c_pre_specs(S, H)

    def body(zq_ref, zk_ref, zv_ref, zs_ref, cwq_ref, cwk_ref, cwv_ref, pc_ref, dq_ref, dk_ref, dv_ref, dgbp_ref,
             dzq_ref, dzk_ref, dzv_ref, dzs_ref, dcw_ref, dpc_ref):
        rows = lambda r: [r[i:i + 1, :] for i in range(r.shape[0])]
        fn = functools.partial(_c_pre, head=pl.program_id(0), HC=H)
        _, vjp = jax.vjp(fn, zq_ref[...], zk_ref[...], zv_ref[...], zs_ref[...], rows(cwq_ref), rows(cwk_ref),
                         rows(cwv_ref), rows(pc_ref))
        dzq, dzk, dzv, dzs, dcwq, dcwk, dcwv, dpcs = vjp((dq_ref[...], dk_ref[...], dv_ref[...], dgbp_ref[...]))
        dzq_ref[...] = dzq.astype(BF)
        dzk_ref[...] = dzk.astype(BF)
        dzv_ref[...] = dzv.astype(BF)
        dzs_ref[...] = dzs

        @pl.when(pl.program_id(1) == 0)
        def _():
            dcw_ref[...] = jnp.zeros_like(dcw_ref)
            dpc_ref[...] = jnp.zeros_like(dpc_ref)

        for c, dc in enumerate((dcwq, dcwk, dcwv)):
            for i in range(CONV_W):
                dcw_ref[c, i:i + 1, :] += dc[i]
        for i in range(4):
            dpc_ref[i:i + 1, :] += dpcs[i]

    out = pl.BlockSpec((None, S, LANES), lambda h, b: (b, 0, h))
    dzs_spec = pl.BlockSpec((None, None, S, LANES), lambda h, b: (h, b, 0, 0))
    dcw_spec = pl.BlockSpec((3, CONV_W, LANES), lambda h, b: (0, 0, h))
    bshape = jax.ShapeDtypeStruct((Bq, S, D), BF)
    dzq, dzk, dzv, dzs, dcw3, dpc = pl.pallas_call(
        body, name=name, grid=(H, Bq),
        in_specs=[col(0), col(1), col(2), zss, cw(0), cw(1), cw(2), pcs, out, out, out, out],
        out_specs=[out, out, out, dzs_spec, dcw_spec, pcs],
        out_shape=[bshape, bshape, bshape, jax.ShapeDtypeStruct((H, Bq, S, LANES), F32),
                   jax.ShapeDtypeStruct((3, CONV_W, D), F32), jax.ShapeDtypeStruct((H, 4, LANES), F32)],
        compiler_params=_params(("parallel", "arbitrary")))(z, z, z, zs, cw3, cw3, cw3, pc, dq, dk, dv, dgbp)
    return dzq, dzk, dzv, dzs, dcw3, dpc


def _c_mid_scratch(S, backward):
    N = S // GDN_CHUNK
    mat = pltpu.VMEM((N, LANES, LANES), F32)
    out = [mat, mat, pltpu.VMEM((N, 1, LANES), F32), pltpu.VMEM((N, GDN_CHUNK, LANES), F32), mat]
    return out + ([mat, mat] if backward else [])


PHASE1_CHUNKS = 8


def _c_blocks(S, fn):
    nb = min(PHASE1_CHUNKS, S // GDN_CHUNK)
    rows = nb * GDN_CHUNK

    def blk(i, carry):
        fn(pl.ds(pl.multiple_of(i * rows, rows), rows), pl.ds(pl.multiple_of(i * nb, nb), nb))
        return carry

    lax.fori_loop(0, S // rows, blk, 0)


def _c_phase1_blocks(in_refs, k2_ref, z_ref, eg_ref, qe2_ref, o_ref, rev):
    def fn(rows, chunks):
        k2, z, eg, qe2, o0 = _c_phase1(*[r[rows, :] for r in in_refs], rev)
        k2_ref[chunks] = k2
        z_ref[chunks] = z
        eg_ref[chunks] = eg
        qe2_ref[chunks] = qe2
        if o_ref is not None:
            o_ref[chunks] += o0

    _c_blocks(in_refs[0].shape[0], fn)


def _c_sweep(k2_ref, z_ref, eg_ref, st_ref, rev):
    N = st_ref.shape[0]

    def step(i, state):
        n = (N - 1 - i) if rev else i
        st_ref[n] = state
        return _c_next_state(state, k2_ref[n], z_ref[n], eg_ref[n])

    lax.fori_loop(0, N, step, jnp.zeros((LANES, LANES), F32))


def _c_sweep_adjoint(k2_ref, eg_ref, dso_ref, gs_ref, rev):
    N = gs_ref.shape[0]

    def step(i, g):
        n = i if rev else (N - 1 - i)
        gs_ref[n] = g
        return dso_ref[n] + g * eg_ref[n] - _bmm_raw(k2_ref[n], g, "tn")

    lax.fori_loop(0, N, step, jnp.zeros((LANES, LANES), F32))


def _c_mid_fwd(q, k, v, gbp, name):
    Bq, S, D = q.shape
    H = D // LANES
    N = S // GDN_CHUNK
    blk = pl.BlockSpec((None, S, LANES), lambda h, b: (b, 0, h))
    blk3 = pl.BlockSpec((None, N, GDN_CHUNK, LANES), lambda h, b: (b, 0, 0, h))

    def body(q_ref, k_ref, v_ref, gbp_ref, o3, k2_ref, z_ref, eg_ref, qe2_ref, st_ref):
        o3[...] = jnp.zeros_like(o3)
        for rev in (False, True):
            _c_phase1_blocks((q_ref, k_ref, v_ref, gbp_ref), k2_ref, z_ref, eg_ref, qe2_ref, o3, rev)
            _c_sweep(k2_ref, z_ref, eg_ref, st_ref, rev)

            def add_out(rows, chunks):
                o3[chunks] += _bmm_raw(qe2_ref[chunks], st_ref[chunks], "nn")

            _c_blocks(S, add_out)

    o = pl.pallas_call(
        body, name=name, grid=(H, Bq), in_specs=[blk] * 4, out_specs=blk3,
        out_shape=jax.ShapeDtypeStruct((Bq, N, GDN_CHUNK, D), F32),
        scratch_shapes=_c_mid_scratch(S, False), compiler_params=_params(("parallel", "parallel")))(q, k, v, gbp)
    return o.reshape(Bq, S, D)


def _c_mid_bwd(q, k, v, gbp, do, name):
    Bq, S, D = q.shape
    H = D // LANES
    N = S // GDN_CHUNK
    blk = pl.BlockSpec((None, S, LANES), lambda h, b: (b, 0, h))
    blk3 = pl.BlockSpec((None, N, GDN_CHUNK, LANES), lambda h, b: (b, 0, 0, h))

    def body(q_ref, k_ref, v_ref, gbp_ref, do3, dq_ref, dk_ref, dv_ref, dgbp_ref,
             k2_ref, z_ref, eg_ref, qe2_ref, st_ref, dso_ref, gs_ref):
        in_refs = (q_ref, k_ref, v_ref, gbp_ref)
        out_refs = (dq_ref, dk_ref, dv_ref, dgbp_ref)
        for rev in (False, True):
            _c_phase1_blocks(in_refs, k2_ref, z_ref, eg_ref, qe2_ref, None, rev)
            _c_sweep(k2_ref, z_ref, eg_ref, st_ref, rev)

            def out_to_state(rows, chunks):
                dso_ref[chunks] = _bmm_raw(qe2_ref[chunks], do3[chunks], "tn")

            _c_blocks(S, out_to_state)
            _c_sweep_adjoint(k2_ref, eg_ref, dso_ref, gs_ref, rev)

            def block_vjp(rows, chunks):
                states = st_ref[chunks]

                def chunk_fn(q_, k_, v_, gbp_):
                    k2, z, eg, qe2, o0 = _c_phase1(q_, k_, v_, gbp_, rev)
                    return _c_next_state(states, k2, z, eg), _c_out(states, qe2, o0)

                _, vjp = jax.vjp(chunk_fn, *[r[rows, :] for r in in_refs])
                for r, c in zip(out_refs, vjp((gs_ref[chunks], do3[chunks]))):
                    if rev:
                        r[rows, :] += c
                    else:
                        r[rows, :] = c

            _c_blocks(S, block_vjp)

    shp = jax.ShapeDtypeStruct((Bq, S, D), F32)
    return pl.pallas_call(
        body, name=name, grid=(H, Bq), in_specs=[blk] * 4 + [blk3], out_specs=[blk] * 4, out_shape=[shp] * 4,
        scratch_shapes=_c_mid_scratch(S, True), compiler_params=_params(("parallel", "parallel")))(
            q, k, v, gbp, do.reshape(Bq, N, GDN_CHUNK, D))


def _c_post_fwd(o, z, ng, name):
    Bq, S, D = o.shape
    H = D // LANES
    blk = pl.BlockSpec((None, S, LANES), lambda h, b: (b, 0, h))
    gate = pl.BlockSpec((None, S, LANES), lambda h, b: (b, 0, 3 * H + h))
    vec = pl.BlockSpec((1, LANES), lambda h, b: (0, 0))

    def body(o_ref, zg_ref, ng_ref, y_ref):
        y_ref[...] = _c_post(o_ref[...], zg_ref[...], ng_ref[...]).astype(BF)

    return pl.pallas_call(
        body, name=name, grid=(H, Bq), in_specs=[blk, gate, vec], out_specs=blk,
        out_shape=jax.ShapeDtypeStruct((Bq, S, D), BF), compiler_params=_params(("parallel", "parallel")))(
            o, z, ng.reshape(1, LANES))


def _c_post_bwd(o, z, ng, dy, name):
    Bq, S, D = o.shape
    H = D // LANES
    blk = pl.BlockSpec((None, S, LANES), lambda b, h: (b, 0, h))
    gate = pl.BlockSpec((None, S, LANES), lambda b, h: (b, 0, 3 * H + h))
    vec = pl.BlockSpec((1, LANES), lambda b, h: (0, 0))

    def body(o_ref, zg_ref, ng_ref, dy_ref, do_ref, dzg_ref, dng_ref):
        _, vjp = jax.vjp(_c_post, o_ref[...], zg_ref[...], ng_ref[...])
        do, dzg, dng = vjp(dy_ref[...])
        do_ref[...] = do
        dzg_ref[...] = dzg.astype(BF)

        @pl.when((pl.program_id(0) == 0) & (pl.program_id(1) == 0))
        def _():
            dng_ref[...] = jnp.zeros_like(dng_ref)

        dng_ref[...] += dng

    do, dzg, dng = pl.pallas_call(
        body, name=name, grid=(Bq, H), in_specs=[blk, gate, vec, blk], out_specs=[blk, blk, vec],
        out_shape=[jax.ShapeDtypeStruct((Bq, S, D), F32), jax.ShapeDtypeStruct((Bq, S, D), BF),
                   jax.ShapeDtypeStruct((1, LANES), F32)],
        compiler_params=_params(("arbitrary", "arbitrary")))(o, z, ng.reshape(1, LANES), dy)
    return do, dzg, dng.reshape(LANES)


def _c_param_rows(a_log, dt_bias):
    p = jnp.concatenate([a_log, dt_bias], axis=0).T
    return jnp.broadcast_to(p[:, :, None], p.shape + (LANES,)).astype(F32)


def _local_step(x, tgt, W):
    Bq, S, D = x.shape
    T = Bq * S
    H = D // LANES
    L = W["norm_mix_g"].shape[0]
    seq = lambda t: t.reshape(Bq, S, t.shape[-1])
    flat = lambda t: t.reshape(T, t.shape[-1])

    xs = flat(x)
    saved = []
    for i in range(L):
        kind, j = i % N_MIXERS, i // N_MIXERS
        tag = f"l{i}"
        sv = {"x": xs}
        hn = _rms_fwd(xs, W["norm_mix_g"][i], f"{tag}_mix_norm")
        sv["hn"] = hn
        if kind == 0:
            z = _mm(hn, W["a_w_in"][j], "nn", f"{tag}_a_in")
            y = _a_core_fwd(seq(z), W["a_conv_w"][j], W["a_conv_b"][j], W["a_gate_w"][j], W["a_gate_b"][j],
                            W["a_lambda"][j], f"{tag}_a_core")
            sv["z"] = z
            w_out = W["a_w_out"][j]
        elif kind == 1:
            z = _mm(hn, W["b_w_in"][j], "nn", f"{tag}_b_in")
            bsf = jnp.repeat(W["b_b_s"][j].T, LANES, axis=1)
            y = _b_core_fwd(z, W["b_ln_g"][j], W["b_ln_b"][j], W["b_w_s"][j], bsf, f"{tag}_b_core")
            sv["z"], sv["bsf"] = z, bsf
            w_out = W["b_w_out"][j]
        else:
            w_in = W["c_w_in"][j]
            w_small = jnp.pad(w_in[:, 4 * D:], ((0, 0), (0, LANES - 4 * H)))
            z = _mm(hn, w_in[:, :4 * D], "nn", f"{tag}_c_in")
            zs = _mm(hn, w_small, "nn", f"{tag}_c_in_small")
            cw3 = W["c_conv_w"][j].reshape(CONV_W, 3, D).transpose(1, 0, 2)
            pc = _c_param_rows(W["c_a_log"][j], W["c_dt_bias"][j])
            q, k, v, gbp = _c_pre_fwd(seq(z), seq(zs), cw3, pc, f"{tag}_c_pre")
            o = _c_mid_fwd(q, k, v, gbp, f"{tag}_c_mid")
            y = _c_post_fwd(o, seq(z), W["c_norm_g"][j], f"{tag}_c_post")
            sv.update(z=z, zs=zs, cw3=cw3, pc=pc, q=q, k=k, v=v, gbp=gbp, o=o, w_small=w_small)
            w_out = W["c_w_out"][j]
        y = flat(y)
        sv["y"] = y
        x1 = _mm(y, w_out, "nn", f"{tag}_mix_out", epi="add", extra=xs)
        sv["x1"] = x1
        hn2 = _rms_fwd(x1, W["norm_mlp_g"][i], f"{tag}_mlp_norm")
        act = _mm(hn2, W["mlp_w_up"][i], "nn", f"{tag}_mlp_up", out_dtype=BF, epi="relu2")
        xs = _mm(act, W["mlp_w_down"][i], "nn", f"{tag}_mlp_down", epi="add", extra=x1)
        sv["hn2"], sv["act"] = hn2, act
        saved.append(sv)

    loss, dx, dgf = _final_loss(xs, W["norm_final_g"], flat(tgt), "final_loss")

    G = {"norm_final_g": dgf}
    per_layer = {n: [None] * L for n in ("norm_mix_g", "norm_mlp_g", "mlp_w_up", "mlp_w_down")}
    mixer = {}
    for i in reversed(range(L)):
        kind, j = i % N_MIXERS, i // N_MIXERS
        tag = f"l{i}"
        sv = saved[i]
        dhid = _mm(dx, W["mlp_w_down"][i], "nt", f"{tag}_mlp_dhid", out_dtype=BF, epi="relu2_bwd", extra=sv["act"])
        per_layer["mlp_w_down"][i] = _mm(sv["act"], dx, "tn", f"{tag}_mlp_dwdown")
        per_layer["mlp_w_up"][i] = _mm(sv["hn2"], dhid, "tn", f"{tag}_mlp_dwup")
        dhn2 = _mm(dhid, W["mlp_w_up"][i], "nt", f"{tag}_mlp_dhn")
        dx, per_layer["norm_mlp_g"][i] = _rms_bwd(sv["x1"], W["norm_mlp_g"][i], dhn2, dx, f"{tag}_mlp_norm_bwd")
        g = {}
        if kind == 0:
            dy = _mm(dx, W["a_w_out"][j], "nt", f"{tag}_a_dy")
            g["a_w_out"] = _mm(sv["y"], dx, "tn", f"{tag}_a_dwout")
            dz, g["a_conv_w"], g["a_conv_b"], g["a_gate_w"], g["a_gate_b"], g["a_lambda"] = _a_core_bwd(
                seq(sv["z"]), seq(dy), W["a_conv_w"][j], W["a_conv_b"][j], W["a_gate_w"][j], W["a_gate_b"][j],
                W["a_lambda"][j], f"{tag}_a_core_bwd")
            dz = flat(dz)
            g["a_w_in"] = _mm(sv["hn"], dz, "tn", f"{tag}_a_dwin")
            dhn = _mm(dz, W["a_w_in"][j], "nt", f"{tag}_a_dhn")
        elif kind == 1:
            dy = _mm(dx, W["b_w_out"][j], "nt", f"{tag}_b_dy")
            g["b_w_out"] = _mm(sv["y"], dx, "tn", f"{tag}_b_dwout")
            dz, g["b_ln_g"], g["b_ln_b"], g["b_w_s"], dbsf = _b_core_bwd(
                sv["z"], dy, W["b_ln_g"][j], W["b_ln_b"][j], W["b_w_s"][j], sv["bsf"], f"{tag}_b_core_bwd")
            g["b_b_s"] = dbsf.reshape(SGU_CHUNK, H, LANES).sum(-1).T
            g["b_w_in"] = _mm(sv["hn"], dz, "tn", f"{tag}_b_dwin")
            dhn = _mm(dz, W["b_w_in"][j], "nt", f"{tag}_b_dhn")
        else:
            dy = _mm(dx, W["c_w_out"][j], "nt", f"{tag}_c_dy")
            g["c_w_out"] = _mm(sv["y"], dx, "tn", f"{tag}_c_dwout")
            do, dzg, g["c_norm_g"] = _c_post_bwd(sv["o"], seq(sv["z"]), W["c_norm_g"][j], seq(dy), f"{tag}_c_post_bwd")
            dq, dk, dv, dgbp = _c_mid_bwd(sv["q"], sv["k"], sv["v"], sv["gbp"], do, f"{tag}_c_mid_bwd")
            dzq, dzk, dzv, dzs_h, dcw3, dpc = _c_pre_bwd(seq(sv["z"]), seq(sv["zs"]), sv["cw3"], sv["pc"], dq, dk, dv, dgbp,
                                                         f"{tag}_c_pre_bwd")
            dz = flat(jnp.concatenate([dzq, dzk, dzv, dzg], axis=-1))
            dzs = flat(dzs_h.sum(0)).astype(BF)
            g["c_conv_w"] = dcw3.transpose(1, 0, 2).reshape(CONV_W, 3 * D)
            dpc = dpc.sum(-1)
            g["c_a_log"], g["c_dt_bias"] = dpc[:, :2].T, dpc[:, 2:].T
            dw_main = _mm(sv["hn"], dz, "tn", f"{tag}_c_dwin")
            dw_small = _mm(sv["hn"], dzs, "tn", f"{tag}_c_dwin_small")
            g["c_w_in"] = jnp.concatenate([dw_main, dw_small[:, :4 * H]], axis=1)
            dhn = _mm(dz, W["c_w_in"][j][:, :4 * D], "nt", f"{tag}_c_dhn")
            dhn = _mm(dzs, sv["w_small"], "nt", f"{tag}_c_dhn_small", epi="add", extra=dhn)
        dx, per_layer["norm_mix_g"][i] = _rms_bwd(sv["x"], W["norm_mix_g"][i], dhn, dx, f"{tag}_mix_norm_bwd")
        for n, val in g.items():
            mixer.setdefault(n, {})[j] = val

    for n, vals in per_layer.items():
        G[n] = jnp.stack(vals)
    for n, by_j in mixer.items():
        G[n] = jnp.stack([by_j[j] for j in sorted(by_j)])
    return loss, dx.reshape(Bq, S, D), G


MESH = pl.DeviceIdType.MESH
N_CHIPS = 4
HBM_SPEC = pl.BlockSpec(memory_space=pltpu.HBM)


def _place():
    x, y, c = lax.axis_index("x"), lax.axis_index("y"), lax.axis_index("c")
    others = [(1 - x, y), (x, 1 - y), (1 - x, 1 - y)]
    return x, y, c, others


def _all_gather_xy(bufs, name):
    n = len(bufs)

    def body(*refs):
        ins, outs = refs[:n], refs[n:2 * n]
        send, recv, fsend, frecv, lsem = refs[2 * n:]
        x, y, c, others = _place()
        p = 2 * x + y
        half = lambda b, cc: pl.ds(cc * (ins[b].shape[0] // 2), ins[b].shape[0] // 2)
        local = [pltpu.make_async_copy(ins[b], outs[b].at[p], lsem.at[b]) for b in range(n)]
        for cp in local:
            cp.start()

        def ici(b, j):
            qx, qy = others[j]
            return pltpu.make_async_remote_copy(
                src_ref=ins[b].at[half(b, c)], dst_ref=outs[b].at[p, half(b, c)], send_sem=send.at[b, j],
                recv_sem=recv.at[b, j], device_id=(qx, qy, c), device_id_type=MESH)

        def landed(b, j, cc):
            qx, qy = others[j]
            return outs[b].at[2 * qx + qy, half(b, cc)]

        def d2d(b, j):
            return pltpu.make_async_remote_copy(
                src_ref=landed(b, j, c), dst_ref=landed(b, j, c), send_sem=fsend.at[b, j], recv_sem=frecv.at[b, j],
                device_id=(x, y, 1 - c), device_id_type=MESH)

        pairs = [(b, j) for b in range(n) for j in range(3)]
        for b, j in pairs:
            ici(b, j).start()
        for b, j in pairs:
            pltpu.make_async_remote_copy(
                src_ref=ins[b].at[half(b, c)], dst_ref=landed(b, j, c), send_sem=send.at[b, j], recv_sem=recv.at[b, j],
                device_id=(x, y, c), device_id_type=MESH).wait_recv()
            d2d(b, j).start()
        for b, j in pairs:
            pltpu.make_async_remote_copy(
                src_ref=landed(b, j, 1 - c), dst_ref=landed(b, j, 1 - c), send_sem=fsend.at[b, j], recv_sem=frecv.at[b, j],
                device_id=(x, y, 1 - c), device_id_type=MESH).wait_recv()
        for b, j in pairs:
            ici(b, j).wait_send()
            d2d(b, j).wait_send()
        for cp in local:
            cp.wait()

    return pl.pallas_call(
        body, name=name, in_specs=[HBM_SPEC] * n, out_specs=[HBM_SPEC] * n,
        out_shape=[jax.ShapeDtypeStruct((N_CHIPS,) + b.shape, b.dtype) for b in bufs],
        scratch_shapes=[pltpu.SemaphoreType.DMA((n, 3))] * 4 + [pltpu.SemaphoreType.DMA((n,))],
        compiler_params=pltpu.CompilerParams(has_side_effects=True))(*bufs)


def _swap_halves(g, name):
    nq, _, h, cols = g.shape

    def body(g_ref, o_ref, send, recv):
        x, y, c, _ = _place()
        cp = pltpu.make_async_remote_copy(src_ref=g_ref.at[:, 1 - c], dst_ref=o_ref, send_sem=send, recv_sem=recv,
                                          device_id=(x, y, 1 - c), device_id_type=MESH)
        cp.start()
        cp.wait()

    return pl.pallas_call(
        body, name=name, in_specs=[HBM_SPEC], out_specs=HBM_SPEC, out_shape=jax.ShapeDtypeStruct((nq, h, cols), g.dtype),
        scratch_shapes=[pltpu.SemaphoreType.DMA, pltpu.SemaphoreType.DMA],
        compiler_params=pltpu.CompilerParams(has_side_effects=True))(g)


def _pair_sum(g, got, out_dtype, name):
    nq, _, h, cols = g.shape
    tr = min(512, h)

    def body(c_ref, g_ref, r_ref, o_ref):
        o_ref[...] = (g_ref[...] + r_ref[...]).astype(o_ref.dtype)

    spec = pl.BlockSpec((None, tr, cols), lambda q, i, c_ref: (q, i, 0))
    return pl.pallas_call(
        body, name=name,
        grid_spec=pltpu.PrefetchScalarGridSpec(
            num_scalar_prefetch=1, grid=(nq, h // tr),
            in_specs=[pl.BlockSpec((None, None, tr, cols), lambda q, i, c_ref: (q, c_ref[0], i, 0)), spec], out_specs=spec),
        out_shape=jax.ShapeDtypeStruct((nq, h, cols), out_dtype),
        compiler_params=_params(("parallel", "parallel")))(lax.axis_index("c").astype(jnp.int32).reshape(1), g, got)


def _scatter_xy(p_sum, name):
    nq, h, cols = p_sum.shape

    def body(p_ref, o_ref, send, recv, lsem):
        x, y, c, others = _place()
        me = 2 * x + y
        local = pltpu.make_async_copy(p_ref.at[me], o_ref.at[me], lsem)
        local.start()
        cps = []
        for j, (qx, qy) in enumerate(others):
            cps.append(pltpu.make_async_remote_copy(
                src_ref=p_ref.at[2 * qx + qy], dst_ref=o_ref.at[me], send_sem=send.at[j], recv_sem=recv.at[j],
                device_id=(qx, qy, c), device_id_type=MESH))
            cps[-1].start()
        for j, (qx, qy) in enumerate(others):
            pltpu.make_async_remote_copy(
                src_ref=p_ref.at[me], dst_ref=o_ref.at[2 * qx + qy], send_sem=send.at[j], recv_sem=recv.at[j],
                device_id=(qx, qy, c), device_id_type=MESH).wait_recv()
        for cp in cps:
            cp.wait_send()
        local.wait()

    return pl.pallas_call(
        body, name=name, in_specs=[HBM_SPEC], out_specs=HBM_SPEC, out_shape=jax.ShapeDtypeStruct((nq, h, cols), p_sum.dtype),
        scratch_shapes=[pltpu.SemaphoreType.DMA((3,)), pltpu.SemaphoreType.DMA((3,)), pltpu.SemaphoreType.DMA],
        compiler_params=pltpu.CompilerParams(has_side_effects=True))(p_sum)


def _chip_sum(r4, name):
    nq, h, cols = r4.shape
    tr = min(512, h)

    def body(r_ref, o_ref):
        f = lambda q: r_ref[q].astype(F32)
        o_ref[...] = ((f(0) + f(1)) + f(2)) + f(3)

    return pl.pallas_call(
        body, name=name, grid=(h // tr,), in_specs=[pl.BlockSpec((nq, tr, cols), lambda i: (0, i, 0))],
        out_specs=pl.BlockSpec((tr, cols), lambda i: (i, 0)), out_shape=jax.ShapeDtypeStruct((h, cols), F32),
        compiler_params=_params(("parallel",)))(r4)


def _join_halves(r, name):
    h, cols = r.shape

    def body(r_ref, o_ref, send, recv, lsem):
        x, y, c, _ = _place()
        local = pltpu.make_async_copy(r_ref, o_ref.at[c], lsem)
        local.start()
        cp = pltpu.make_async_remote_copy(src_ref=r_ref, dst_ref=o_ref.at[c], send_sem=send, recv_sem=recv,
                                          device_id=(x, y, 1 - c), device_id_type=MESH)
        cp.start()
        pltpu.make_async_remote_copy(src_ref=r_ref, dst_ref=o_ref.at[1 - c], send_sem=send, recv_sem=recv,
                                     device_id=(x, y, 1 - c), device_id_type=MESH).wait_recv()
        cp.wait_send()
        local.wait()

    return pl.pallas_call(
        body, name=name, in_specs=[HBM_SPEC], out_specs=HBM_SPEC, out_shape=jax.ShapeDtypeStruct((2, h, cols), r.dtype),
        scratch_shapes=[pltpu.SemaphoreType.DMA, pltpu.SemaphoreType.DMA, pltpu.SemaphoreType.DMA],
        compiler_params=pltpu.CompilerParams(has_side_effects=True))(r)


def _reduce_scatter(g, tag):
    nq, rows, cols = g.shape
    g = g.reshape(nq, 2, rows // 2, cols)
    got = _swap_halves(g, f"{tag}_swap")
    pair = _pair_sum(g, got, BF, f"{tag}_pair_sum")
    r4 = _scatter_xy(pair, f"{tag}_scatter")
    r = _chip_sum(r4, f"{tag}_chip_sum")
    return _join_halves(r, f"{tag}_join").reshape(rows, cols)


def _adamw(w, g, m, v, name):
    rows, cols = w.shape
    tr = rows
    for cand in (512, 344, 256, 128, 64, 32, 16, 8):
        if rows % cand == 0:
            tr = cand
            break

    def body(w_ref, g_ref, m_ref, v_ref, d_ref, nm_ref, nv_ref):
        g_ = g_ref[...]
        m_ = ADAM_B1 * m_ref[...] + (1.0 - ADAM_B1) * g_
        v_ = ADAM_B2 * v_ref[...] + (1.0 - ADAM_B2) * jnp.square(g_)
        m_hat = m_ / (1.0 - ADAM_B1 ** ADAM_STEP)
        v_hat = v_ / (1.0 - ADAM_B2 ** ADAM_STEP)
        d_ref[...] = -ADAM_LR * (m_hat / (jnp.sqrt(v_hat) + ADAM_EPS) + ADAM_WD * w_ref[...])
        nm_ref[...] = m_
        nv_ref[...] = v_

    spec = pl.BlockSpec((tr, cols), lambda i: (i, 0))
    shp = jax.ShapeDtypeStruct((rows, cols), F32)
    return pl.pallas_call(body, name=name, grid=(rows // tr,), in_specs=[spec] * 4, out_specs=[spec] * 3,
                          out_shape=[shp] * 3, compiler_params=_params(("parallel",)))(w, g, m, v)


WEIGHTS = ["norm_mix_g", "norm_mlp_g", "mlp_w_up", "mlp_w_down", "norm_final_g", "a_w_in", "a_conv_w", "a_conv_b",
           "a_gate_w", "a_gate_b", "a_lambda", "a_w_out", "b_w_in", "b_ln_g", "b_ln_b", "b_w_s", "b_b_s", "b_w_out",
           "c_w_in", "c_conv_w", "c_a_log", "c_dt_bias", "c_norm_g", "c_w_out"]
SHARD_AXIS = {"mlp_w_up": 2, "mlp_w_down": 1, "a_w_in": 2, "a_conv_w": 2, "a_conv_b": 1, "a_lambda": 2, "a_w_out": 1,
              "b_w_in": 2, "b_w_out": 1, "c_w_in": 2, "c_conv_w": 2, "c_w_out": 1}
MATMUL_WEIGHTS = ["mlp_w_up", "mlp_w_down", "a_w_in", "a_w_out", "b_w_in", "b_w_out", "c_w_out", "c_w_in"]
SMALL_SHARDED = ["a_conv_w", "a_conv_b", "a_lambda", "c_conv_w"]
REPLICATED = [n for n in WEIGHTS if n not in SHARD_AXIS]
FLAT_COLS = 1024


def _rows_of(shape):
    n = 1
    for s in shape:
        n *= s
    return -(-n // FLAT_COLS)


def _pack(arrays, total_rows, dtype):
    parts = []
    used = 0
    for a in arrays:
        r = _rows_of(a.shape)
        f = a.reshape(-1).astype(dtype)
        parts.append(jnp.pad(f, (0, r * FLAT_COLS - f.shape[0])).reshape(r, FLAT_COLS))
        used += r
    if total_rows > used:
        parts.append(jnp.zeros((total_rows - used, FLAT_COLS), dtype))
    return jnp.concatenate(parts, axis=0)


def _unpack(buf, shapes):
    out, r0 = [], 0
    for shp in shapes:
        r = _rows_of(shp)
        n = 1
        for s in shp:
            n *= s
        out.append(buf[r0:r0 + r].reshape(-1)[:n].reshape(shp))
        r0 += r
    return out


def _round_up(n, m):
    return -(-n // m) * m


def kernel(x, norm_mix_g, norm_mlp_g, mlp_w_up, mlp_w_down, norm_final_g, a_w_in, a_conv_w, a_conv_b, a_gate_w, a_gate_b, a_lambda, a_w_out, b_w_in, b_ln_g, b_ln_b, b_w_s, b_b_s, b_w_out, c_w_in, c_conv_w, c_a_log, c_dt_bias, c_norm_g, c_w_out, loss_target, m_norm_mix_g, m_norm_mlp_g, m_mlp_w_up, m_mlp_w_down, m_norm_final_g, m_a_w_in, m_a_conv_w, m_a_conv_b, m_a_gate_w, m_a_gate_b, m_a_lambda, m_a_w_out, m_b_w_in, m_b_ln_g, m_b_ln_b, m_b_w_s, m_b_b_s, m_b_w_out, m_c_w_in, m_c_conv_w, m_c_a_log, m_c_dt_bias, m_c_norm_g, m_c_w_out, v_norm_mix_g, v_norm_mlp_g, v_mlp_w_up, v_mlp_w_down, v_norm_final_g, v_a_w_in, v_a_conv_w, v_a_conv_b, v_a_gate_w, v_a_gate_b, v_a_lambda, v_a_w_out, v_b_w_in, v_b_ln_g, v_b_ln_b, v_b_w_s, v_b_b_s, v_b_w_out, v_c_w_in, v_c_conv_w, v_c_a_log, v_c_dt_bias, v_c_norm_g, v_c_w_out):
    given = dict(locals())
    w_loc = {n: given[n] for n in WEIGHTS}
    m_loc = {n: given["m_" + n] for n in WEIGHTS}
    v_loc = {n: given["v_" + n] for n in WEIGHTS}

    mat_rows = _round_up(sum(_rows_of(w_loc[n].shape) for n in MATMUL_WEIGHTS), 32)
    small_rows = _round_up(sum(_rows_of(w_loc[n].shape) for n in SMALL_SHARDED), 16)
    mat_all, small_all = _all_gather_xy(
        [_pack([w_loc[n] for n in MATMUL_WEIGHTS], mat_rows, BF),
         _pack([w_loc[n] for n in SMALL_SHARDED], small_rows, F32)], "gather_weights")
    W = {n: w_loc[n] for n in REPLICATED}
    for names, buf in ((MATMUL_WEIGHTS, mat_all), (SMALL_SHARDED, small_all)):
        per_chip = [_unpack(buf[q], [w_loc[n].shape for n in names]) for q in range(N_CHIPS)]
        for i, n in enumerate(names):
            W[n] = jnp.concatenate([per_chip[q][i] for q in range(N_CHIPS)], axis=SHARD_AXIS[n])

    loss, grad_x, G = _local_step(x, loss_target, W)
    loss = lax.psum(loss, ("x", "y", "c"))

    sharded = MATMUL_WEIGHTS + SMALL_SHARDED
    rep_rows = _round_up(sum(_rows_of(w_loc[n].shape) for n in REPLICATED), N_CHIPS * 16)
    rep_flat = _pack([G[n] for n in REPLICATED], rep_rows, F32).reshape(N_CHIPS, rep_rows // N_CHIPS, FLAT_COLS)
    shard_rows = sum(_rows_of(w_loc[n].shape) for n in sharded)
    total_rows = _round_up(shard_rows + rep_rows // N_CHIPS, 1024)
    slots = []
    for q in range(N_CHIPS):
        pieces = []
        for n in sharded:
            width = w_loc[n].shape[SHARD_AXIS[n]]
            pieces.append(lax.slice_in_dim(G[n], q * width, (q + 1) * width, axis=SHARD_AXIS[n]))
        body = _pack(pieces, shard_rows, F32)
        pad = jnp.zeros((total_rows - shard_rows - rep_rows // N_CHIPS, FLAT_COLS), F32)
        slots.append(jnp.concatenate([body, rep_flat[q], pad], axis=0))
    red = _reduce_scatter(jnp.stack(slots), "grads")
    g_loc = dict(zip(sharded, _unpack(red, [w_loc[n].shape for n in sharded])))
    rep_quarter = red[shard_rows:shard_rows + rep_rows // N_CHIPS]
    (rep_all,) = _all_gather_xy([rep_quarter], "gather_replicated_grads")
    g_loc.update(zip(REPLICATED, _unpack(rep_all.reshape(rep_rows, FLAT_COLS), [w_loc[n].shape for n in REPLICATED])))

    delta, new_m, new_v = {}, {}, {}
    big = [n for n in MATMUL_WEIGHTS if w_loc[n].size % FLAT_COLS == 0]
    for n in big:
        shp = w_loc[n].shape
        two_d = lambda a: a.reshape(-1, FLAT_COLS)
        d, nm, nv = _adamw(two_d(w_loc[n]), two_d(g_loc[n]), two_d(m_loc[n]), two_d(v_loc[n]), f"adamw_{n}")
        delta[n], new_m[n], new_v[n] = d.reshape(shp), nm.reshape(shp), nv.reshape(shp)
    small = [n for n in WEIGHTS if n not in big]
    small_shapes = [w_loc[n].shape for n in small]
    rows = _round_up(sum(_rows_of(s) for s in small_shapes), 8)
    packed = [_pack([src[n] for n in small], rows, F32) for src in (w_loc, g_loc, m_loc, v_loc)]
    for dst, buf in zip((delta, new_m, new_v), _adamw(*packed, "adamw_small")):
        dst.update(zip(small, _unpack(buf, small_shapes)))

    return (loss, grad_x, *[g_loc[n] for n in WEIGHTS], *[delta[n] for n in WEIGHTS],
            *[new_m[n] for n in WEIGHTS], *[new_v[n] for n in WEIGHTS])
```

```python
import functools

import jax
import jax.numpy as jnp
from jax import lax
from jax.experimental import pallas as pl
from jax.experimental.pallas import tpu as pltpu

F32 = jnp.float32
BF = jnp.bfloat16

LANES = 128
VMEM_LIMIT = 56 * 1024 * 1024
RG_C = 8.0
SGU_CHUNK = 128
GDN_CHUNK = 64
CONV_W = 4
N_MIXERS = 3

ADAM_LR = 0.001
ADAM_B1 = 0.9
ADAM_B2 = 0.999
ADAM_EPS = 1e-08
ADAM_WD = 0.01
ADAM_STEP = 10


def _params(sem=None):
    return pltpu.CompilerParams(dimension_semantics=sem, vmem_limit_bytes=VMEM_LIMIT)


def _shift_impl(x, s):
    if s == 0:
        return x
    n = x.shape[0]
    row = lax.broadcasted_iota(jnp.int32, x.shape, 0)
    if s > 0:
        return jnp.where(row >= s, pltpu.roll(x, s, 0), 0.0)
    return jnp.where(row < n + s, pltpu.roll(x, n + s, 0), 0.0)


@functools.partial(jax.custom_vjp, nondiff_argnums=(1,))
def _shift(x, s):
    return _shift_impl(x, s)


def _shift_fwd(x, s):
    return _shift_impl(x, s), None


def _shift_bwd(s, _, g):
    return (_shift_impl(g, -s),)


_shift.defvjp(_shift_fwd, _shift_bwd)


def _chunk_cumsum_impl(x, rev, chunk):
    n = x.shape[0]
    rc = lax.broadcasted_iota(jnp.int32, x.shape, 0) & (chunk - 1)
    sh = 1
    while sh < chunk:
        if rev:
            x = x + jnp.where(rc < chunk - sh, pltpu.roll(x, n - sh, 0), 0.0)
        else:
            x = x + jnp.where(rc >= sh, pltpu.roll(x, sh, 0), 0.0)
        sh *= 2
    return x


@functools.partial(jax.custom_vjp, nondiff_argnums=(1, 2))
def _chunk_cumsum(x, rev, chunk):
    return _chunk_cumsum_impl(x, rev, chunk)


def _chunk_cumsum_fwd(x, rev, chunk):
    return _chunk_cumsum_impl(x, rev, chunk), None


def _chunk_cumsum_bwd(rev, chunk, _, g):
    return (_chunk_cumsum_impl(g, not rev, chunk),)


_chunk_cumsum.defvjp(_chunk_cumsum_fwd, _chunk_cumsum_bwd)


def _rms(x, g, eps=1e-6):
    return x * lax.rsqrt(jnp.mean(x * x, axis=-1, keepdims=True) + eps) * g


def _sigmoid(x):
    return 0.5 * jnp.tanh(0.5 * x) + 0.5


def _silu(x):
    return x * _sigmoid(x)


def _softplus(x):
    return jnp.maximum(x, 0.0) + jnp.log1p(jnp.exp(-jnp.abs(x)))


def _sqrt_pos(x):
    return x * lax.rsqrt(jnp.maximum(x, 1e-30))


def _neg_expm1(y, ey):
    series = -y * (1.0 + y * (1 / 2) * (1.0 + y * (1 / 3) * (1.0 + y * (1 / 4) * (1.0 + y * (1 / 5) * (1.0 + y * (1 / 6))))))
    return jnp.where(y > -0.125, series, 1.0 - ey)


def _bmm_raw(a, b, form):
    r = a.ndim - 2
    con = {"nn": ((r + 1,), (r,)), "nt": ((r + 1,), (r + 1,)), "tn": ((r,), (r,))}[form]
    batch = ((0,), (0,)) if r else ((), ())
    return lax.dot_general(a.astype(BF), b.astype(BF), (con, batch), preferred_element_type=F32)


@functools.partial(jax.custom_vjp, nondiff_argnums=(2,))
def _bmm(a, b, form):
    return _bmm_raw(a, b, form)


def _bmm_fwd(a, b, form):
    return _bmm_raw(a, b, form), (a, b)


def _bmm_bwd(form, res, g):
    a, b = res
    if form == "nn":
        da, db = _bmm_raw(g, b, "nt"), _bmm_raw(a, g, "tn")
    elif form == "nt":
        da, db = _bmm_raw(g, b, "nn"), _bmm_raw(g, a, "tn")
    else:
        da, db = _bmm_raw(b, g, "nt"), _bmm_raw(a, g, "nn")
    return da.astype(a.dtype), db.astype(b.dtype)


_bmm.defvjp(_bmm_fwd, _bmm_bwd)


def _conv4(z, rows):
    out = rows[0] * _shift(z, 2)
    for k in range(1, CONV_W):
        out = out + rows[k] * _shift(z, 2 - k)
    return out


_DIMS = {"nn": ((1,), (0,)), "nt": ((1,), (1,)), "tn": ((0,), (0,))}


def _mm(a, b, mode, name, *, out_dtype=F32, epi=None, extra=None, tm=1024, tn=1024, tk=1024):
    if mode == "tn":
        K, M = a.shape
    else:
        M, K = a.shape
    N = b.shape[0] if mode == "nt" else b.shape[1]
    tm, tn, tk = min(tm, M), min(tn, N), min(tk, K)
    assert M % tm == 0 and N % tn == 0 and K % tk == 0, (name, M, N, K)
    nk = K // tk
    a_spec = pl.BlockSpec((tk, tm), lambda i, j, k: (k, i)) if mode == "tn" else pl.BlockSpec((tm, tk), lambda i, j, k: (i, k))
    b_spec = pl.BlockSpec((tn, tk), lambda i, j, k: (j, k)) if mode == "nt" else pl.BlockSpec((tk, tn), lambda i, j, k: (k, j))
    o_spec = pl.BlockSpec((tm, tn), lambda i, j, k: (i, j))
    ins, specs = [a, b], [a_spec, b_spec]
    if epi in ("add", "relu2_bwd"):
        ins.append(extra)
        specs.append(o_spec)

    def body(*refs):
        a_ref, b_ref = refs[0], refs[1]
        e_ref = refs[2] if len(ins) == 3 else None
        o_ref = refs[len(ins)]

        def product():
            return lax.dot_general(a_ref[...].astype(BF), b_ref[...].astype(BF), (_DIMS[mode], ((), ())),
                                   preferred_element_type=F32)

        def finish(r):
            if epi == "relu2":
                r = jnp.square(jnp.maximum(r, 0.0))
            elif epi == "add":
                r = r + e_ref[...]
            elif epi == "relu2_bwd":
                r = r * (2.0 * jnp.sqrt(e_ref[...].astype(F32)))
            o_ref[...] = r.astype(out_dtype)

        if nk == 1:
            finish(product())
            return
        acc = refs[-1]
        k = pl.program_id(2)

        @pl.when(k == 0)
        def _():
            acc[...] = product()

        @pl.when(k > 0)
        def _():
            acc[...] += product()

        @pl.when(k == nk - 1)
        def _():
            finish(acc[...])

    return pl.pallas_call(
        body, name=name, grid=(M // tm, N // tn, nk), in_specs=specs, out_specs=o_spec,
        out_shape=jax.ShapeDtypeStruct((M, N), out_dtype),
        scratch_shapes=[pltpu.VMEM((tm, tn), F32)] if nk > 1 else [],
        compiler_params=_params(("parallel", "parallel", "arbitrary")))(*ins)


def _rows_tile(T):
    return min(512, T)


def _rms_fwd(x, g, name):
    T, D = x.shape
    tr = _rows_tile(T)

    def body(x_ref, g_ref, o_ref):
        o_ref[...] = _rms(x_ref[...], g_ref[...]).astype(BF)

    return pl.pallas_call(
        body, name=name, grid=(T // tr,),
        in_specs=[pl.BlockSpec((tr, D), lambda i: (i, 0)), pl.BlockSpec((1, D), lambda i: (0, 0))],
        out_specs=pl.BlockSpec((tr, D), lambda i: (i, 0)), out_shape=jax.ShapeDtypeStruct((T, D), BF),
        compiler_params=_params(("parallel",)))(x, g.reshape(1, D))


def _rms_bwd(x, g, dhn, dres, name):
    T, D = x.shape
    tr = _rows_tile(T)

    def body(x_ref, g_ref, dhn_ref, dres_ref, dx_ref, dg_ref):
        _, vjp = jax.vjp(_rms, x_ref[...], g_ref[...])
        dx, dg = vjp(dhn_ref[...])
        dx_ref[...] = dres_ref[...] + dx

        @pl.when(pl.program_id(0) == 0)
        def _():
            dg_ref[...] = jnp.zeros_like(dg_ref)

        dg_ref[...] += dg

    row = pl.BlockSpec((tr, D), lambda i: (i, 0))
    vec = pl.BlockSpec((1, D), lambda i: (0, 0))
    dx, dg = pl.pallas_call(
        body, name=name, grid=(T // tr,), in_specs=[row, vec, row, row], out_specs=[row, vec],
        out_shape=[jax.ShapeDtypeStruct((T, D), F32), jax.ShapeDtypeStruct((1, D), F32)],
        compiler_params=_params(("arbitrary",)))(x, g.reshape(1, D), dhn, dres)
    return dx, dg.reshape(D)


def _final_loss(x, g, tgt, name):
    T, D = x.shape
    tr = _rows_tile(T)

    def body(x_ref, g_ref, t_ref, l_ref, dx_ref, dg_ref):
        y, vjp = jax.vjp(_rms, x_ref[...], g_ref[...])
        err = y - t_ref[...]
        dx, dg = vjp(err * (1.0 / D))
        dx_ref[...] = dx

        @pl.when(pl.program_id(0) == 0)
        def _():
            dg_ref[...] = jnp.zeros_like(dg_ref)
            l_ref[...] = jnp.zeros_like(l_ref)

        dg_ref[...] += dg
        l_ref[...] += (0.5 / D) * jnp.sum(jnp.sum(err * err, axis=1, keepdims=True), axis=0, keepdims=True)

    row = pl.BlockSpec((tr, D), lambda i: (i, 0))
    vec = pl.BlockSpec((1, D), lambda i: (0, 0))
    loss, dx, dg = pl.pallas_call(
        body, name=name, grid=(T // tr,), in_specs=[row, vec, row],
        out_specs=[pl.BlockSpec((1, LANES), lambda i: (0, 0)), row, vec],
        out_shape=[jax.ShapeDtypeStruct((1, LANES), F32), jax.ShapeDtypeStruct((T, D), F32),
                   jax.ShapeDtypeStruct((1, D), F32)],
        compiler_params=_params(("arbitrary",)))(x, g.reshape(1, D), tgt)
    return loss[0, 0], dx, dg.reshape(D)


def _a_pre(zx, cws, cb, gws, gbs, lams):
    xr = _conv4(zx, cws) + cb
    out = []
    for d in range(2):
        r = _sigmoid(_bmm(xr, gws[2 * d], "nn") + gbs[2 * d])
        ig = _sigmoid(_bmm(xr, gws[2 * d + 1], "nn") + gbs[2 * d + 1])
        log_a = -RG_C * r * _softplus(-lams[d])
        a = jnp.exp(log_a)
        out += [a, _sqrt_pos(_neg_expm1(2.0 * log_a, a * a)) * ig * xr]
    return tuple(out)


def _a_post(h0, h1, zg):
    return (h0 + h1) * jax.nn.gelu(zg)


SUBLANES = 8
SCAN_TILES = 8


def _scan_jobs(jobs):
    S, C = jobs[0][0].shape
    U = min(SCAN_TILES, S // SUBLANES)
    rows = U * SUBLANES
    row = lax.broadcasted_iota(jnp.int32, (SUBLANES, C), 0)

    def prefix(a, b, reverse):
        for sh in (1, 2, 4):
            if reverse:
                m, r = row < SUBLANES - sh, SUBLANES - sh
            else:
                m, r = row >= sh, sh
            a_s = jnp.where(m, pltpu.roll(a, r, 0), 1.0)
            b_s = jnp.where(m, pltpu.roll(b, r, 0), 0.0)
            b = a * b_s + b
            a = a * a_s
        return a, b

    def step(i, carries):
        out = []
        for (a_ref, b_ref, h_ref, reverse), c in zip(jobs, carries):
            blk = (S // rows - 1 - i) if reverse else i
            t0 = pl.multiple_of(blk * rows, rows)
            order = range(U - 1, -1, -1) if reverse else range(U)
            edge = slice(0, 1) if reverse else slice(SUBLANES - 1, SUBLANES)
            for j in order:
                sl = pl.ds(t0 + j * SUBLANES, SUBLANES)
                a, b = prefix(a_ref[sl, :], b_ref[sl, :], reverse)
                h_ref[sl, :] = a * jnp.broadcast_to(c, (SUBLANES, C)) + b
                c = a[edge, :] * c + b[edge, :]
            out.append(c)
        return tuple(out)

    lax.fori_loop(0, S // rows, step, tuple(jnp.zeros((1, C), F32) for _ in jobs))


def _a_load_params(cw_ref, cb_ref, gw_ref, gb_ref, lam_ref):
    cws = [cw_ref[k:k + 1, :] for k in range(CONV_W)]
    gws = [gw_ref[d, g] for d in range(2) for g in range(2)]
    gbs = [gb_ref[d, g] for d in range(2) for g in range(2)]
    lams = [lam_ref[d:d + 1, :] for d in range(2)]
    return cws, cb_ref[...], gws, gbs, lams


def _a_in_specs(S, H):
    zg = pl.BlockSpec((None, S, LANES), lambda h, b: (b, 0, h))
    zx = pl.BlockSpec((None, S, LANES), lambda h, b: (b, 0, H + h))
    cw = pl.BlockSpec((CONV_W, LANES), lambda h, b: (0, h))
    cb = pl.BlockSpec((1, LANES), lambda h, b: (0, h))
    gw = pl.BlockSpec((2, 2, None, LANES, LANES), lambda h, b: (0, 0, h, 0, 0))
    gb = pl.BlockSpec((2, 2, None, 1, LANES), lambda h, b: (0, 0, h, 0, 0))
    lam = pl.BlockSpec((2, LANES), lambda h, b: (0, h))
    return zg, zx, cw, cb, gw, gb, lam


def _a_core_fwd(z, cw, cb, gw, gb, lam, name):
    Bq, S, D2 = z.shape
    D = D2 // 2
    H = D // LANES

    def body(zg_ref, zx_ref, cw_ref, cb_ref, gw_ref, gb_ref, lam_ref, y_ref, a_s, b_s, h_s):
        ab = _a_pre(zx_ref[...], *_a_load_params(cw_ref, cb_ref, gw_ref, gb_ref, lam_ref))
        for d in range(2):
            a_s[d] = ab[2 * d]
            b_s[d] = ab[2 * d + 1]
        _scan_jobs([(a_s.at[d], b_s.at[d], h_s.at[d], d == 1) for d in range(2)])
        y_ref[...] = _a_post(h_s[0], h_s[1], zg_ref[...]).astype(BF)

    seq = pltpu.VMEM((2, S, LANES), F32)
    return pl.pallas_call(
        body, name=name, grid=(H, Bq), in_specs=list(_a_in_specs(S, H)),
        out_specs=pl.BlockSpec((None, S, LANES), lambda h, b: (b, 0, h)),
        out_shape=jax.ShapeDtypeStruct((Bq, S, D), BF), scratch_shapes=[seq, seq, seq],
        compiler_params=_params(("parallel", "arbitrary")))(z, z, cw, cb.reshape(1, D), gw, gb.reshape(2, 2, H, 1, LANES), lam)


def _a_core_bwd(z, dy, cw, cb, gw, gb, lam, name):
    Bq, S, D2 = z.shape
    D = D2 // 2
    H = D // LANES

    def body(zg_ref, zx_ref, dy_ref, cw_ref, cb_ref, gw_ref, gb_ref, lam_ref,
             dzg_ref, dzx_ref, dcw_ref, dcb_ref, dgw_ref, dgb_ref, dlam_ref, a_s, b_s, h_s, l_s):
        prm = _a_load_params(cw_ref, cb_ref, gw_ref, gb_ref, lam_ref)
        ab = _a_pre(zx_ref[...], *prm)
        for d in range(2):
            a_s[d] = ab[2 * d]
            b_s[d] = ab[2 * d + 1]
        _scan_jobs([(a_s.at[d], b_s.at[d], h_s.at[d], d == 1) for d in range(2)])
        _, post_vjp = jax.vjp(_a_post, h_s[0], h_s[1], zg_ref[...])
        dh0, dh1, dzg = post_vjp(dy_ref[...])
        dzg_ref[...] = dzg.astype(BF)
        for d, dh in ((0, dh0), (1, dh1)):
            b_s[d] = dh
            a_s[d] = _shift(a_s[d], -1 if d == 0 else 1)
        _scan_jobs([(a_s.at[d], b_s.at[d], l_s.at[d], d == 0) for d in range(2)])
        cot = []
        for d in range(2):
            cot += [l_s[d] * _shift(h_s[d], 1 if d == 0 else -1), l_s[d]]
        _, pre_vjp = jax.vjp(_a_pre, zx_ref[...], *prm)
        dzx, dcws, dcb, dgws, dgbs, dlams = pre_vjp(tuple(cot))
        dzx_ref[...] = dzx.astype(BF)

        @pl.when(pl.program_id(1) == 0)
        def _():
            for r in (dcw_ref, dcb_ref, dgw_ref, dgb_ref, dlam_ref):
                r[...] = jnp.zeros_like(r)

        for k in range(CONV_W):
            dcw_ref[k:k + 1, :] += dcws[k]
        dcb_ref[...] += dcb
        for d in range(2):
            dlam_ref[d:d + 1, :] += dlams[d]
            for g in range(2):
                dgw_ref[d, g] += dgws[2 * d + g]
                dgb_ref[d, g] += dgbs[2 * d + g]

    zg, zx, cws, cbs, gws, gbs, lams = _a_in_specs(S, H)
    dyspec = pl.BlockSpec((None, S, LANES), lambda h, b: (b, 0, h))
    seq = pltpu.VMEM((2, S, LANES), F32)
    dzg, dzx, dcw, dcb, dgw, dgb, dlam = pl.pallas_call(
        body, name=name, grid=(H, Bq), in_specs=[zg, zx, dyspec, cws, cbs, gws, gbs, lams],
        out_specs=[dyspec, dyspec, cws, cbs, gws, gbs, lams],
        out_shape=[jax.ShapeDtypeStruct((Bq, S, D), BF), jax.ShapeDtypeStruct((Bq, S, D), BF),
                   jax.ShapeDtypeStruct((CONV_W, D), F32), jax.ShapeDtypeStruct((1, D), F32),
                   jax.ShapeDtypeStruct((2, 2, H, LANES, LANES), F32), jax.ShapeDtypeStruct((2, 2, H, 1, LANES), F32),
                   jax.ShapeDtypeStruct((2, D), F32)],
        scratch_shapes=[seq, seq, seq, seq],
        compiler_params=_params(("parallel", "arbitrary")))(z, z, dy, cw, cb.reshape(1, D), gw, gb.reshape(2, 2, H, 1, LANES), lam)
    dz = jnp.concatenate([dzg, dzx], axis=-1)
    return dz, dcw, dcb.reshape(D), dgw, dgb.reshape(2, 2, H, LANES), dlam


def _b_fn(z, lng, lnb, wss, bsf):
    D = z.shape[1] // 2
    zz = jax.nn.gelu(z)
    u, v = zz[:, :D], zz[:, D:]
    mu = jnp.mean(v, axis=-1, keepdims=True)
    var = jnp.mean(jnp.square(v - mu), axis=-1, keepdims=True)
    vn = (v - mu) * lax.rsqrt(var + 1e-5) * lng + lnb
    vs = jnp.concatenate([_bmm(wss[g], vn[:, g * LANES:(g + 1) * LANES], "nn") for g in range(D // LANES)], axis=1)
    return u * (vs + bsf)


def _b_specs(D, GB):
    row = lambda w: pl.BlockSpec((SGU_CHUNK, w), lambda i: (i, 0))
    vec = pl.BlockSpec((1, D), lambda i: (0, 0))
    ws = pl.BlockSpec((GB, SGU_CHUNK, SGU_CHUNK), lambda i: (0, 0, 0))
    bsf = pl.BlockSpec((SGU_CHUNK, D), lambda i: (0, 0))
    return row, vec, ws, bsf


def _b_core_fwd(z, lng, lnb, ws, bsf, name):
    T, D2 = z.shape
    D = D2 // 2
    GB = D // LANES
    row, vec, wspec, bspec = _b_specs(D, GB)

    def body(z_ref, lng_ref, lnb_ref, ws_ref, bsf_ref, y_ref):
        wss = [ws_ref[g] for g in range(GB)]
        y_ref[...] = _b_fn(z_ref[...], lng_ref[...], lnb_ref[...], wss, bsf_ref[...]).astype(BF)

    return pl.pallas_call(
        body, name=name, grid=(T // SGU_CHUNK,), in_specs=[row(D2), vec, vec, wspec, bspec], out_specs=row(D),
        out_shape=jax.ShapeDtypeStruct((T, D), BF), compiler_params=_params(("parallel",)))(
            z, lng.reshape(1, D), lnb.reshape(1, D), ws, bsf)


def _b_core_bwd(z, dy, lng, lnb, ws, bsf, name):
    T, D2 = z.shape
    D = D2 // 2
    GB = D // LANES
    row, vec, wspec, bspec = _b_specs(D, GB)

    def body(z_ref, dy_ref, lng_ref, lnb_ref, ws_ref, bsf_ref, dz_ref, dlng_ref, dlnb_ref, dws_ref, dbsf_ref):
        wss = [ws_ref[g] for g in range(GB)]
        _, vjp = jax.vjp(_b_fn, z_ref[...], lng_ref[...], lnb_ref[...], wss, bsf_ref[...])
        dz, dlng, dlnb, dwss, dbsf = vjp(dy_ref[...])
        dz_ref[...] = dz.astype(BF)

        @pl.when(pl.program_id(0) == 0)
        def _():
            for r in (dlng_ref, dlnb_ref, dws_ref, dbsf_ref):
                r[...] = jnp.zeros_like(r)

        dlng_ref[...] += dlng
        dlnb_ref[...] += dlnb
        dbsf_ref[...] += dbsf
        for g in range(GB):
            dws_ref[g] += dwss[g]

    dz, dlng, dlnb, dws, dbsf = pl.pallas_call(
        body, name=name, grid=(T // SGU_CHUNK,), in_specs=[row(D2), row(D), vec, vec, wspec, bspec],
        out_specs=[row(D2), vec, vec, wspec, bspec],
        out_shape=[jax.ShapeDtypeStruct((T, D2), BF), jax.ShapeDtypeStruct((1, D), F32), jax.ShapeDtypeStruct((1, D), F32),
                   jax.ShapeDtypeStruct((GB, SGU_CHUNK, SGU_CHUNK), F32), jax.ShapeDtypeStruct((SGU_CHUNK, D), F32)],
        compiler_params=_params(("arbitrary",)))(z, dy, lng.reshape(1, D), lnb.reshape(1, D), ws, bsf)
    return dz, dlng.reshape(D), dlnb.reshape(D), dws, dbsf


def _lane_is(j):
    return lax.broadcasted_iota(jnp.int32, (1, LANES), 1) == j


def _lane_col(x, j):
    return jnp.sum(jnp.where(_lane_is(j), x, 0.0), axis=1, keepdims=True)


def _c_pre(zq, zk, zv, zs, cwq, cwk, cwv, pcs, head, HC):
    q = _silu(_conv4(zq, cwq))
    k = _silu(_conv4(zk, cwk))
    v = _silu(_conv4(zv, cwv))
    q = q * lax.rsqrt(jnp.sum(q * q, axis=-1, keepdims=True) + 1e-6) * (LANES ** -0.5)
    k = k * lax.rsqrt(jnp.sum(k * k, axis=-1, keepdims=True) + 1e-6)
    gbp = jnp.zeros_like(zs)
    for d in range(2):
        a_logit = _lane_col(zs, d * HC + head)
        b_logit = _lane_col(zs, 2 * HC + d * HC + head)
        g = -jnp.exp(pcs[d]) * _softplus(a_logit + pcs[2 + d])
        beta = jnp.broadcast_to(_sigmoid(b_logit), g.shape)
        gbp = gbp + jnp.where(_lane_is(2 * d), g, 0.0) + jnp.where(_lane_is(2 * d + 1), beta, 0.0)
    return q, k, v, gbp


def _mm3(x, y):
    xh, yh = x.astype(BF), y.astype(BF)
    xl, yl = (x - xh.astype(F32)).astype(BF), (y - yh.astype(F32)).astype(BF)
    return _bmm_raw(xh, yh, "nn") + _bmm_raw(xh, yl, "nn") + _bmm_raw(xl, yh, "nn")


def _tri_inv_impl(a):
    C = a.shape[-1]
    eye = (lax.broadcasted_iota(jnp.int32, (1, C, C), 1) == lax.broadcasted_iota(jnp.int32, (1, C, C), 2)).astype(F32)
    r = eye - a
    p = a
    n = 2
    while n < C:
        p = _mm3(p, p)
        r = r + _mm3(r, p)
        n *= 2
    return r


@jax.custom_vjp
def _tri_inv(a):
    return _tri_inv_impl(a)


def _tri_inv_fwd(a):
    t = _tri_inv_impl(a)
    return t, t


def _tri_inv_bwd(t, g):
    tt = jnp.swapaxes(t, 1, 2)
    return (-_bmm_raw(_bmm_raw(tt, g, "nn"), tt, "nn"),)


_tri_inv.defvjp(_tri_inv_fwd, _tri_inv_bwd)


@jax.custom_vjp
def _pair_diff(gc3):
    m = gc3[:, :, :gc3.shape[1]]
    return m - jnp.swapaxes(m, 1, 2)


def _pair_diff_fwd(gc3):
    return _pair_diff(gc3), None


def _pair_diff_bwd(_, g):
    d = jnp.sum(g, axis=2, keepdims=True) - jnp.sum(jnp.swapaxes(g, 1, 2), axis=2, keepdims=True)
    return (jnp.broadcast_to(d * (1.0 / LANES), d.shape[:2] + (LANES,)),)


_pair_diff.defvjp(_pair_diff_fwd, _pair_diff_bwd)


@jax.custom_vjp
def _tri_inv_saved(a, t):
    return t


def _tri_inv_saved_fwd(a, t):
    return t, t


def _tri_inv_saved_bwd(t, g):
    return _tri_inv_bwd(t, g)[0], jnp.zeros_like(t)


_tri_inv_saved.defvjp(_tri_inv_saved_fwd, _tri_inv_saved_bwd)


def _c_phase1(q, k, v, gbp, rev, t_saved=None):
    S = q.shape[0]
    C = GDN_CHUNK
    N = S // C
    col = 2 if rev else 0
    gB = jnp.broadcast_to(_lane_col(gbp, col), (S, LANES))
    bB = jnp.broadcast_to(_lane_col(gbp, col + 1), (S, LANES))
    r3 = lambda t: t.reshape(N, C, LANES)
    gc3 = r3(_chunk_cumsum(gB, rev, C))
    q3, k3, v3, b3, g3 = r3(q), r3(k), r3(v), r3(bB), r3(gB)
    ri = lax.broadcasted_iota(jnp.int32, (1, C, C), 1)
    ci = lax.broadcasted_iota(jnp.int32, (1, C, C), 2)
    incl = (ri <= ci) if rev else (ri >= ci)
    strict = (ri < ci) if rev else (ri > ci)
    decay = jnp.where(incl, jnp.exp(jnp.where(incl, _pair_diff(gc3), 0.0)), 0.0)
    kb = k3 * b3
    vb = v3 * b3
    A = jnp.where(strict, _bmm(kb, k3, "nt") * decay, 0.0)
    T = _tri_inv(A) if t_saved is None else _tri_inv_saved(A, t_saved)
    egc = jnp.exp(gc3)
    u = _bmm(T, vb, "nn")
    w = _bmm(T, kb * egc, "nn")
    qk = _bmm(q3, k3, "nt") * decay
    glast = jnp.sum(g3, axis=1, keepdims=True)
    kd = k3 * jnp.exp(glast - gc3)
    k2 = _bmm(kd, w, "tn")
    z = _bmm(kd, u, "tn")
    qe2 = q3 * egc - _bmm(qk, w, "nn")
    o0 = _bmm(qk, u, "nn")
    return k2, z, jnp.exp(glast), qe2, o0, T


def _c_next_state(state, k2, z, eg):
    return state * eg - _bmm(k2, state, "nn") + z


def _c_out(state, qe2, o0):
    return _bmm(qe2, state, "nn") + o0


def _c_post(o, zg, ng):
    return _rms(o, ng) * _silu(zg)


def _c_pre_specs(S, H):
    col = lambda c0: pl.BlockSpec((None, S, LANES), lambda h, b: (b, 0, c0 * H + h))
    zs = pl.BlockSpec((None, S, LANES), lambda h, b: (b, 0, 0))
    cw = lambda c0: pl.BlockSpec((None, CONV_W, LANES), lambda h, b: (c0, 0, h))
    pc = pl.BlockSpec((None, 4, LANES), lambda h, b: (h, 0, 0))
    return col, zs, cw, pc


def _c_pre_fwd(z, zs, cw3, pc, name):
    Bq, S, D4 = z.shape
    D = D4 // 4
    H = D // LANES
    col, zss, cw, pcs = _c_pre_specs(S, H)

    def body(zq_ref, zk_ref, zv_ref, zs_ref, cwq_ref, cwk_ref, cwv_ref, pc_ref, q_ref, k_ref, v_ref, gbp_ref):
        rows = lambda r: [r[i:i + 1, :] for i in range(r.shape[0])]
        q, k, v, gbp = _c_pre(zq_ref[...], zk_ref[...], zv_ref[...], zs_ref[...], rows(cwq_ref), rows(cwk_ref),
                              rows(cwv_ref), rows(pc_ref), pl.program_id(0), H)
        q_ref[...] = q
        k_ref[...] = k
        v_ref[...] = v
        gbp_ref[...] = gbp

    out = pl.BlockSpec((None, S, LANES), lambda h, b: (b, 0, h))
    shp = jax.ShapeDtypeStruct((Bq, S, D), F32)
    return pl.pallas_call(
        body, name=name, grid=(H, Bq), in_specs=[col(0), col(1), col(2), zss, cw(0), cw(1), cw(2), pcs],
        out_specs=[out] * 4, out_shape=[shp] * 4, compiler_params=_params(("parallel", "arbitrary")))(
            z, z, z, zs, cw3, cw3, cw3, pc)


def _c_pre_bwd(z, zs, cw3, pc, dq, dk, dv, dgbp, name):
    Bq, S, D4 = z.shape
    D = D4 // 4
    H = D // LANES
    col, zss, cw, pcs = _c_pre_specs(S, H)

    def body(zq_ref, zk_ref, zv_ref, zs_ref, cwq_ref, cwk_ref, cwv_ref, pc_ref, dq_ref, dk_ref, dv_ref, dgbp_ref,
             dzq_ref, dzk_ref, dzv_ref, dzs_ref, dcw_ref, dpc_ref):
        rows = lambda r: [r[i:i + 1, :] for i in range(r.shape[0])]
        fn = functools.partial(_c_pre, head=pl.program_id(0), HC=H)
        _, vjp = jax.vjp(fn, zq_ref[...], zk_ref[...], zv_ref[...], zs_ref[...], rows(cwq_ref), rows(cwk_ref),
                         rows(cwv_ref), rows(pc_ref))
        dzq, dzk, dzv, dzs, dcwq, dcwk, dcwv, dpcs = vjp((dq_ref[...], dk_ref[...], dv_ref[...], dgbp_ref[...]))
        dzq_ref[...] = dzq.astype(BF)
        dzk_ref[...] = dzk.astype(BF)
        dzv_ref[...] = dzv.astype(BF)
        dzs_ref[...] = dzs

        @pl.when(pl.program_id(1) == 0)
        def _():
            dcw_ref[...] = jnp.zeros_like(dcw_ref)
            dpc_ref[...] = jnp.zeros_like(dpc_ref)

        for c, dc in enumerate((dcwq, dcwk, dcwv)):
            for i in range(CONV_W):
                dcw_ref[c, i:i + 1, :] += dc[i]
        for i in range(4):
            dpc_ref[i:i + 1, :] += dpcs[i]

    out = pl.BlockSpec((None, S, LANES), lambda h, b: (b, 0, h))
    dzs_spec = pl.BlockSpec((None, None, S, LANES), lambda h, b: (h, b, 0, 0))
    dcw_spec = pl.BlockSpec((3, CONV_W, LANES), lambda h, b: (0, 0, h))
    bshape = jax.ShapeDtypeStruct((Bq, S, D), BF)
    dzq, dzk, dzv, dzs, dcw3, dpc = pl.pallas_call(
        body, name=name, grid=(H, Bq),
        in_specs=[col(0), col(1), col(2), zss, cw(0), cw(1), cw(2), pcs, out, out, out, out],
        out_specs=[out, out, out, dzs_spec, dcw_spec, pcs],
        out_shape=[bshape, bshape, bshape, jax.ShapeDtypeStruct((H, Bq, S, LANES), F32),
                   jax.ShapeDtypeStruct((3, CONV_W, D), F32), jax.ShapeDtypeStruct((H, 4, LANES), F32)],
        compiler_params=_params(("parallel", "arbitrary")))(z, z, z, zs, cw3, cw3, cw3, pc, dq, dk, dv, dgbp)
    return dzq, dzk, dzv, dzs, dcw3, dpc


def _c_saved_shapes(Bq, H, S):
    N, C = S // GDN_CHUNK, GDN_CHUNK
    return [(Bq, H, 2, N, LANES, LANES), (Bq, H, 2, N, LANES, LANES), (Bq, H, 2, N, 1, LANES), (Bq, H, 2, N, C, LANES),
            (Bq, H, 2, N, C, C)]


def _c_saved_scratch(S):
    return [pltpu.VMEM(shp[3:], F32) for shp in _c_saved_shapes(1, 1, S)]


PHASE1_CHUNKS = 8


def _c_blocks(S, fn):
    nb = min(PHASE1_CHUNKS, S // GDN_CHUNK)
    rows = nb * GDN_CHUNK

    def blk(i, carry):
        fn(pl.ds(pl.multiple_of(i * rows, rows), rows), pl.ds(pl.multiple_of(i * nb, nb), nb))
        return carry

    lax.fori_loop(0, S // rows, blk, 0)


def _c_phase1_blocks(in_refs, k2_ref, z_ref, eg_ref, qe2_ref, t_ref, o_ref, rev):
    def fn(rows, chunks):
        k2, z, eg, qe2, o0, t = _c_phase1(*[r[rows, :] for r in in_refs], rev)
        k2_ref[chunks] = k2
        z_ref[chunks] = z
        eg_ref[chunks] = eg
        qe2_ref[chunks] = qe2
        t_ref[chunks] = t
        o_ref[chunks] += o0

    _c_blocks(in_refs[0].shape[0], fn)


def _c_sweep(k2_ref, z_ref, eg_ref, st_ref, rev):
    N = st_ref.shape[0]

    def step(i, state):
        n = (N - 1 - i) if rev else i
        st_ref[n] = state
        return _c_next_state(state, k2_ref[n], z_ref[n], eg_ref[n])

    lax.fori_loop(0, N, step, jnp.zeros((LANES, LANES), F32))


def _c_sweep_adjoint(k2_ref, eg_ref, dso_ref, gs_ref, rev):
    N = gs_ref.shape[0]

    def step(i, g):
        n = i if rev else (N - 1 - i)
        gs_ref[n] = g
        return dso_ref[n] + g * eg_ref[n] - _bmm_raw(k2_ref[n], g, "tn")

    lax.fori_loop(0, N, step, jnp.zeros((LANES, LANES), F32))


def _c_mid_fwd(q, k, v, gbp, name):
    Bq, S, D = q.shape
    H = D // LANES
    N = S // GDN_CHUNK
    blk = pl.BlockSpec((None, S, LANES), lambda h, b: (b, 0, h))
    blk3 = pl.BlockSpec((None, N, GDN_CHUNK, LANES), lambda h, b: (b, 0, 0, h))
    n_saved = len(_c_saved_shapes(Bq, H, S))

    def body(q_ref, k_ref, v_ref, gbp_ref, o3, *rest):
        saved_hbm, scr = rest[:n_saved], rest[n_saved:]
        per_dir = n_saved + 1
        sems = scr[2 * per_dir]
        o3[...] = jnp.zeros_like(o3)
        copies = []
        for d, rev in enumerate((False, True)):
            k2_ref, st_ref, eg_ref, qe2_ref, t_ref, z_ref = scr[d * per_dir:(d + 1) * per_dir]
            _c_phase1_blocks((q_ref, k_ref, v_ref, gbp_ref), k2_ref, z_ref, eg_ref, qe2_ref, t_ref, o3, rev)
            _c_sweep(k2_ref, z_ref, eg_ref, st_ref, rev)

            def add_out(rows, chunks):
                o3[chunks] += _bmm_raw(qe2_ref[chunks], st_ref[chunks], "nn")

            _c_blocks(S, add_out)
            for i, (src, dst) in enumerate(zip((k2_ref, st_ref, eg_ref, qe2_ref, t_ref), saved_hbm)):
                copies.append(pltpu.make_async_copy(src, dst.at[pl.program_id(1), pl.program_id(0), d], sems.at[d, i]))
                copies[-1].start()
        for cp in copies:
            cp.wait()

    one_dir = _c_saved_scratch(S) + [pltpu.VMEM((N, LANES, LANES), F32)]
    outs = pl.pallas_call(
        body, name=name, grid=(H, Bq), in_specs=[blk] * 4,
        out_specs=[blk3] + [pl.BlockSpec(memory_space=pltpu.HBM)] * n_saved,
        out_shape=[jax.ShapeDtypeStruct((Bq, N, GDN_CHUNK, D), F32)]
        + [jax.ShapeDtypeStruct(shp, F32) for shp in _c_saved_shapes(Bq, H, S)],
        scratch_shapes=one_dir + one_dir + [pltpu.SemaphoreType.DMA((2, n_saved))],
        compiler_params=_params(("parallel", "parallel")))(q, k, v, gbp)
    return outs[0].reshape(Bq, S, D), tuple(outs[1:])


def _c_mid_bwd(q, k, v, gbp, do, saved, name):
    Bq, S, D = q.shape
    H = D // LANES
    N = S // GDN_CHUNK
    blk = pl.BlockSpec((None, S, LANES), lambda h, b: (b, 0, h))
    blk3 = pl.BlockSpec((None, N, GDN_CHUNK, LANES), lambda h, b: (b, 0, 0, h))
    n_saved = len(saved)

    def body(q_ref, k_ref, v_ref, gbp_ref, do3, *rest):
        saved_hbm = rest[:n_saved]
        dq_ref, dk_ref, dv_ref, dgbp_ref = rest[n_saved:n_saved + 4]
        scr = rest[n_saved + 4:]
        sets = (scr[:n_saved], scr[n_saved:2 * n_saved])
        dso_ref, gs_ref, sems = scr[2 * n_saved:]
        in_refs = (q_ref, k_ref, v_ref, gbp_ref)
        out_refs = (dq_ref, dk_ref, dv_ref, dgbp_ref)
        copies = [[pltpu.make_async_copy(src.at[pl.program_id(1), pl.program_id(0), d], dst, sems.at[d, i])
                   for i, (src, dst) in enumerate(zip(saved_hbm, sets[d]))] for d in range(2)]
        for cp in copies[0] + copies[1]:
            cp.start()
        for d, rev in enumerate((False, True)):
            k2_ref, st_ref, eg_ref, qe2_ref, t_ref = sets[d]
            for cp in copies[d]:
                cp.wait()

            def out_to_state(rows, chunks):
                dso_ref[chunks] = _bmm_raw(qe2_ref[chunks], do3[chunks], "tn")

            _c_blocks(S, out_to_state)
            _c_sweep_adjoint(k2_ref, eg_ref, dso_ref, gs_ref, rev)

            def block_vjp(rows, chunks):
                states, t_saved = st_ref[chunks], t_ref[chunks]

                def chunk_fn(q_, k_, v_, gbp_):
                    k2, z, eg, qe2, o0, _ = _c_phase1(q_, k_, v_, gbp_, rev, t_saved)
                    return _c_next_state(states, k2, z, eg), _c_out(states, qe2, o0)

                _, vjp = jax.vjp(chunk_fn, *[r[rows, :] for r in in_refs])
                for r, c in zip(out_refs, vjp((gs_ref[chunks], do3[chunks]))):
                    if rev:
                        r[rows, :] += c
                    else:
                        r[rows, :] = c

            _c_blocks(S, block_vjp)

    shp = jax.ShapeDtypeStruct((Bq, S, D), F32)
    mat = pltpu.VMEM((N, LANES, LANES), F32)
    return pl.pallas_call(
        body, name=name, grid=(H, Bq), in_specs=[blk] * 4 + [blk3] + [pl.BlockSpec(memory_space=pltpu.HBM)] * n_saved,
        out_specs=[blk] * 4, out_shape=[shp] * 4,
        scratch_shapes=_c_saved_scratch(S) + _c_saved_scratch(S) + [mat, mat, pltpu.SemaphoreType.DMA((2, n_saved))],
        compiler_params=_params(("parallel", "parallel")))(q, k, v, gbp, do.reshape(Bq, N, GDN_CHUNK, D), *saved)


def _c_post_fwd(o, z, ng, name):
    Bq, S, D = o.shape
    H = D // LANES
    blk = pl.BlockSpec((None, S, LANES), lambda h, b: (b, 0, h))
    gate = pl.BlockSpec((None, S, LANES), lambda h, b: (b, 0, 3 * H + h))
    vec = pl.BlockSpec((1, LANES), lambda h, b: (0, 0))

    def body(o_ref, zg_ref, ng_ref, y_ref):
        y_ref[...] = _c_post(o_ref[...], zg_ref[...], ng_ref[...]).astype(BF)

    return pl.pallas_call(
        body, name=name, grid=(H, Bq), in_specs=[blk, gate, vec], out_specs=blk,
        out_shape=jax.ShapeDtypeStruct((Bq, S, D), BF), compiler_params=_params(("parallel", "parallel")))(
            o, z, ng.reshape(1, LANES))


def _c_post_bwd(o, z, ng, dy, name):
    Bq, S, D = o.shape
    H = D // LANES
    blk = pl.BlockSpec((None, S, LANES), lambda b, h: (b, 0, h))
    gate = pl.BlockSpec((None, S, LANES), lambda b, h: (b, 0, 3 * H + h))
    vec = pl.BlockSpec((1, LANES), lambda b, h: (0, 0))

    def body(o_ref, zg_ref, ng_ref, dy_ref, do_ref, dzg_ref, dng_ref):
        _, vjp = jax.vjp(_c_post, o_ref[...], zg_ref[...], ng_ref[...])
        do, dzg, dng = vjp(dy_ref[...])
        do_ref[...] = do
        dzg_ref[...] = dzg.astype(BF)

        @pl.when((pl.program_id(0) == 0) & (pl.program_id(1) == 0))
        def _():
            dng_ref[...] = jnp.zeros_like(dng_ref)

        dng_ref[...] += dng

    do, dzg, dng = pl.pallas_call(
        body, name=name, grid=(Bq, H), in_specs=[blk, gate, vec, blk], out_specs=[blk, blk, vec],
        out_shape=[jax.ShapeDtypeStruct((Bq, S, D), F32), jax.ShapeDtypeStruct((Bq, S, D), BF),
                   jax.ShapeDtypeStruct((1, LANES), F32)],
        compiler_params=_params(("arbitrary", "arbitrary")))(o, z, ng.reshape(1, LANES), dy)
    return do, dzg, dng.reshape(LANES)


def _c_param_rows(a_log, dt_bias):
    p = jnp.concatenate([a_log, dt_bias], axis=0).T
    return jnp.broadcast_to(p[:, :, None], p.shape + (LANES,)).astype(F32)


def _local_step(x, tgt, W):
    Bq, S, D = x.shape
    T = Bq * S
    H = D // LANES
    L = W["norm_mix_g"].shape[0]
    seq = lambda t: t.reshape(Bq, S, t.shape[-1])
    flat = lambda t: t.reshape(T, t.shape[-1])

    xs = flat(x)
    saved = []
    for i in range(L):
        kind, j = i % N_MIXERS, i // N_MIXERS
        tag = f"l{i}"
        sv = {"x": xs}
        hn = _rms_fwd(xs, W["norm_mix_g"][i], f"{tag}_mix_norm")
        sv["hn"] = hn
        if kind == 0:
            z = _mm(hn, W["a_w_in"][j], "nn", f"{tag}_a_in")
            y = _a_core_fwd(seq(z), W["a_conv_w"][j], W["a_conv_b"][j], W["a_gate_w"][j], W["a_gate_b"][j],
                            W["a_lambda"][j], f"{tag}_a_core")
            sv["z"] = z
            w_out = W["a_w_out"][j]
        elif kind == 1:
            z = _mm(hn, W["b_w_in"][j], "nn", f"{tag}_b_in")
            bsf = jnp.repeat(W["b_b_s"][j].T, LANES, axis=1)
            y = _b_core_fwd(z, W["b_ln_g"][j], W["b_ln_b"][j], W["b_w_s"][j], bsf, f"{tag}_b_core")
            sv["z"], sv["bsf"] = z, bsf
            w_out = W["b_w_out"][j]
        else:
            w_in = W["c_w_in"][j]
            w_small = jnp.pad(w_in[:, 4 * D:], ((0, 0), (0, LANES - 4 * H)))
            z = _mm(hn, w_in[:, :4 * D], "nn", f"{tag}_c_in")
            zs = _mm(hn, w_small, "nn", f"{tag}_c_in_small")
            cw3 = W["c_conv_w"][j].reshape(CONV_W, 3, D).transpose(1, 0, 2)
            pc = _c_param_rows(W["c_a_log"][j], W["c_dt_bias"][j])
            q, k, v, gbp = _c_pre_fwd(seq(z), seq(zs), cw3, pc, f"{tag}_c_pre")
            o, sv["mid"] = _c_mid_fwd(q, k, v, gbp, f"{tag}_c_mid")
            y = _c_post_fwd(o, seq(z), W["c_norm_g"][j], f"{tag}_c_post")
            sv.update(z=z, zs=zs, cw3=cw3, pc=pc, q=q, k=k, v=v, gbp=gbp, o=o, w_small=w_small)
            w_out = W["c_w_out"][j]
        y = flat(y)
        sv["y"] = y
        x1 = _mm(y, w_out, "nn", f"{tag}_mix_out", epi="add", extra=xs)
        sv["x1"] = x1
        hn2 = _rms_fwd(x1, W["norm_mlp_g"][i], f"{tag}_mlp_norm")
        act = _mm(hn2, W["mlp_w_up"][i], "nn", f"{tag}_mlp_up", out_dtype=BF, epi="relu2")
        xs = _mm(act, W["mlp_w_down"][i], "nn", f"{tag}_mlp_down", epi="add", extra=x1)
        sv["hn2"], sv["act"] = hn2, act
        saved.append(sv)

    loss, dx, dgf = _final_loss(xs, W["norm_final_g"], flat(tgt), "final_loss")

    G = {"norm_final_g": dgf}
    per_layer = {n: [None] * L for n in ("norm_mix_g", "norm_mlp_g", "mlp_w_up", "mlp_w_down")}
    mixer = {}
    for i in reversed(range(L)):
        kind, j = i % N_MIXERS, i // N_MIXERS
        tag = f"l{i}"
        sv = saved[i]
        dhid = _mm(dx, W["mlp_w_down"][i], "nt", f"{tag}_mlp_dhid", out_dtype=BF, epi="relu2_bwd", extra=sv["act"])
        per_layer["mlp_w_down"][i] = _mm(sv["act"], dx, "tn", f"{tag}_mlp_dwdown")
        per_layer["mlp_w_up"][i] = _mm(sv["hn2"], dhid, "tn", f"{tag}_mlp_dwup")
        dhn2 = _mm(dhid, W["mlp_w_up"][i], "nt", f"{tag}_mlp_dhn")
        dx, per_layer["norm_mlp_g"][i] = _rms_bwd(sv["x1"], W["norm_mlp_g"][i], dhn2, dx, f"{tag}_mlp_norm_bwd")
        g = {}
        if kind == 0:
            dy = _mm(dx, W["a_w_out"][j], "nt", f"{tag}_a_dy")
            g["a_w_out"] = _mm(sv["y"], dx, "tn", f"{tag}_a_dwout")
            dz, g["a_conv_w"], g["a_conv_b"], g["a_gate_w"], g["a_gate_b"], g["a_lambda"] = _a_core_bwd(
                seq(sv["z"]), seq(dy), W["a_conv_w"][j], W["a_conv_b"][j], W["a_gate_w"][j], W["a_gate_b"][j],
                W["a_lambda"][j], f"{tag}_a_core_bwd")
            dz = flat(dz)
            g["a_w_in"] = _mm(sv["hn"], dz, "tn", f"{tag}_a_dwin")
            dhn = _mm(dz, W["a_w_in"][j], "nt", f"{tag}_a_dhn")
        elif kind == 1:
            dy = _mm(dx, W["b_w_out"][j], "nt", f"{tag}_b_dy")
            g["b_w_out"] = _mm(sv["y"], dx, "tn", f"{tag}_b_dwout")
            dz, g["b_ln_g"], g["b_ln_b"], g["b_w_s"], dbsf = _b_core_bwd(
                sv["z"], dy, W["b_ln_g"][j], W["b_ln_b"][j], W["b_w_s"][j], sv["bsf"], f"{tag}_b_core_bwd")
            g["b_b_s"] = dbsf.reshape(SGU_CHUNK, H, LANES).sum(-1).T
            g["b_w_in"] = _mm(sv["hn"], dz, "tn", f"{tag}_b_dwin")
            dhn = _mm(dz, W["b_w_in"][j], "nt", f"{tag}_b_dhn")
        else:
            dy = _mm(dx, W["c_w_out"][j], "nt", f"{tag}_c_dy")
            g["c_w_out"] = _mm(sv["y"], dx, "tn", f"{tag}_c_dwout")
            do, dzg, g["c_norm_g"] = _c_post_bwd(sv["o"], seq(sv["z"]), W["c_norm_g"][j], seq(dy), f"{tag}_c_post_bwd")
            dq, dk, dv, dgbp = _c_mid_bwd(sv["q"], sv["k"], sv["v"], sv["gbp"], do, sv["mid"], f"{tag}_c_mid_bwd")
            dzq, dzk, dzv, dzs_h, dcw3, dpc = _c_pre_bwd(seq(sv["z"]), seq(sv["zs"]), sv["cw3"], sv["pc"], dq, dk, dv, dgbp,
                                                         f"{tag}_c_pre_bwd")
            dz = flat(jnp.concatenate([dzq, dzk, dzv, dzg], axis=-1))
            dzs = flat(dzs_h.sum(0)).astype(BF)
            g["c_conv_w"] = dcw3.transpose(1, 0, 2).reshape(CONV_W, 3 * D)
            dpc = dpc.sum(-1)
            g["c_a_log"], g["c_dt_bias"] = dpc[:, :2].T, dpc[:, 2:].T
            dw_main = _mm(sv["hn"], dz, "tn", f"{tag}_c_dwin")
            dw_small = _mm(sv["hn"], dzs, "tn", f"{tag}_c_dwin_small")
            g["c_w_in"] = jnp.concatenate([dw_main, dw_small[:, :4 * H]], axis=1)
            dhn = _mm(dz, W["c_w_in"][j][:, :4 * D], "nt", f"{tag}_c_dhn")
            dhn = _mm(dzs, sv["w_small"], "nt", f"{tag}_c_dhn_small", epi="add", extra=dhn)
        dx, per_layer["norm_mix_g"][i] = _rms_bwd(sv["x"], W["norm_mix_g"][i], dhn, dx, f"{tag}_mix_norm_bwd")
        for n, val in g.items():
            mixer.setdefault(n, {})[j] = val

    for n, vals in per_layer.items():
        G[n] = jnp.stack(vals)
    for n, by_j in mixer.items():
        G[n] = jnp.stack([by_j[j] for j in sorted(by_j)])
    return loss, dx.reshape(Bq, S, D), G


MESH = pl.DeviceIdType.MESH
N_CHIPS = 4
HBM_SPEC = pl.BlockSpec(memory_space=pltpu.HBM)


def _place():
    x, y, c = lax.axis_index("x"), lax.axis_index("y"), lax.axis_index("c")
    others = [(1 - x, y), (x, 1 - y), (1 - x, 1 - y)]
    return x, y, c, others


def _all_gather_xy(bufs, name):
    n = len(bufs)

    def body(*refs):
        ins, outs = refs[:n], refs[n:2 * n]
        send, recv, fsend, frecv = refs[2 * n:]
        x, y, c, others = _place()
        p = 2 * x + y
        half = lambda b, cc: pl.ds(cc * (ins[b].shape[0] // 2), ins[b].shape[0] // 2)

        def ici(b, j):
            qx, qy = others[j]
            return pltpu.make_async_remote_copy(
                src_ref=ins[b].at[half(b, c)], dst_ref=outs[b].at[p, half(b, c)], send_sem=send.at[b, j],
                recv_sem=recv.at[b, j], device_id=(qx, qy, c), device_id_type=MESH)

        def landed(b, j, cc):
            qx, qy = others[j]
            return outs[b].at[2 * qx + qy, half(b, cc)]

        def d2d(b, j):
            return pltpu.make_async_remote_copy(
                src_ref=landed(b, j, c), dst_ref=landed(b, j, c), send_sem=fsend.at[b, j], recv_sem=frecv.at[b, j],
                device_id=(x, y, 1 - c), device_id_type=MESH)

        pairs = [(b, j) for b in range(n) for j in range(3)]
        for b, j in pairs:
            ici(b, j).start()
        for b, j in pairs:
            pltpu.make_async_remote_copy(
                src_ref=ins[b].at[half(b, c)], dst_ref=landed(b, j, c), send_sem=send.at[b, j], recv_sem=recv.at[b, j],
                device_id=(x, y, c), device_id_type=MESH).wait_recv()
            d2d(b, j).start()
        for b, j in pairs:
            pltpu.make_async_remote_copy(
                src_ref=landed(b, j, 1 - c), dst_ref=landed(b, j, 1 - c), send_sem=fsend.at[b, j], recv_sem=frecv.at[b, j],
                device_id=(x, y, 1 - c), device_id_type=MESH).wait_recv()
        for b, j in pairs:
            ici(b, j).wait_send()
            d2d(b, j).wait_send()

    return pl.pallas_call(
        body, name=name, in_specs=[HBM_SPEC] * n, out_specs=[HBM_SPEC] * n,
        out_shape=[jax.ShapeDtypeStruct((N_CHIPS,) + b.shape, b.dtype) for b in bufs],
        scratch_shapes=[pltpu.SemaphoreType.DMA((n, 3))] * 4,
        compiler_params=pltpu.CompilerParams(has_side_effects=True))(*bufs)


def _own_or(gathered, own):
    p = 2 * lax.axis_index("x") + lax.axis_index("y")
    return [jnp.where(p == q, own, gathered[q]) for q in range(N_CHIPS)]


def _swap_halves(g, name):
    nq, _, h, cols = g.shape

    def body(g_ref, o_ref, send, recv):
        x, y, c, _ = _place()
        cp = pltpu.make_async_remote_copy(src_ref=g_ref.at[:, 1 - c], dst_ref=o_ref, send_sem=send, recv_sem=recv,
                                          device_id=(x, y, 1 - c), device_id_type=MESH)
        cp.start()
        cp.wait()

    return pl.pallas_call(
        body, name=name, in_specs=[HBM_SPEC], out_specs=HBM_SPEC, out_shape=jax.ShapeDtypeStruct((nq, h, cols), g.dtype),
        scratch_shapes=[pltpu.SemaphoreType.DMA, pltpu.SemaphoreType.DMA],
        compiler_params=pltpu.CompilerParams(has_side_effects=True))(g)


def _pair_sum(g, got, out_dtype, name):
    nq, _, h, cols = g.shape
    tr = min(512, h)

    def body(c_ref, g_ref, r_ref, o_ref):
        o_ref[...] = (g_ref[...] + r_ref[...]).astype(o_ref.dtype)

    spec = pl.BlockSpec((None, tr, cols), lambda q, i, c_ref: (q, i, 0))
    return pl.pallas_call(
        body, name=name,
        grid_spec=pltpu.PrefetchScalarGridSpec(
            num_scalar_prefetch=1, grid=(nq, h // tr),
            in_specs=[pl.BlockSpec((None, None, tr, cols), lambda q, i, c_ref: (q, c_ref[0], i, 0)), spec], out_specs=spec),
        out_shape=jax.ShapeDtypeStruct((nq, h, cols), out_dtype),
        compiler_params=_params(("parallel", "parallel")))(lax.axis_index("c").astype(jnp.int32).reshape(1), g, got)


def _scatter_xy(p_sum, name):
    nq, h, cols = p_sum.shape

    def body(p_ref, o_ref, send, recv):
        x, y, c, others = _place()
        me = 2 * x + y
        cps = []
        for j, (qx, qy) in enumerate(others):
            cps.append(pltpu.make_async_remote_copy(
                src_ref=p_ref.at[2 * qx + qy], dst_ref=o_ref.at[me], send_sem=send.at[j], recv_sem=recv.at[j],
                device_id=(qx, qy, c), device_id_type=MESH))
            cps[-1].start()
        for j, (qx, qy) in enumerate(others):
            pltpu.make_async_remote_copy(
                src_ref=p_ref.at[me], dst_ref=o_ref.at[2 * qx + qy], send_sem=send.at[j], recv_sem=recv.at[j],
                device_id=(qx, qy, c), device_id_type=MESH).wait_recv()
        for cp in cps:
            cp.wait_send()

    return pl.pallas_call(
        body, name=name, in_specs=[HBM_SPEC], out_specs=HBM_SPEC, out_shape=jax.ShapeDtypeStruct((nq, h, cols), p_sum.dtype),
        scratch_shapes=[pltpu.SemaphoreType.DMA((3,)), pltpu.SemaphoreType.DMA((3,))],
        compiler_params=pltpu.CompilerParams(has_side_effects=True))(p_sum)


def _chip_sum(r4, p_sum, name):
    nq, h, cols = r4.shape
    tr = min(512, h)

    def body(r_ref, p_ref, o_ref):
        me = 2 * lax.axis_index("x") + lax.axis_index("y")
        f = lambda q: jnp.where(me == q, p_ref[q], r_ref[q]).astype(F32)
        o_ref[...] = ((f(0) + f(1)) + f(2)) + f(3)

    spec = pl.BlockSpec((nq, tr, cols), lambda i: (0, i, 0))
    return pl.pallas_call(
        body, name=name, grid=(h // tr,), in_specs=[spec, spec],
        out_specs=pl.BlockSpec((tr, cols), lambda i: (i, 0)), out_shape=jax.ShapeDtypeStruct((h, cols), F32),
        compiler_params=_params(("parallel",)))(r4, p_sum)


def _join_halves(r, name):
    h, cols = r.shape

    def body(r_ref, o_ref, send, recv):
        x, y, c, _ = _place()
        cp = pltpu.make_async_remote_copy(src_ref=r_ref, dst_ref=o_ref.at[c], send_sem=send, recv_sem=recv,
                                          device_id=(x, y, 1 - c), device_id_type=MESH)
        cp.start()
        pltpu.make_async_remote_copy(src_ref=r_ref, dst_ref=o_ref.at[1 - c], send_sem=send, recv_sem=recv,
                                     device_id=(x, y, 1 - c), device_id_type=MESH).wait_recv()
        cp.wait_send()

    got = pl.pallas_call(
        body, name=name, in_specs=[HBM_SPEC], out_specs=HBM_SPEC, out_shape=jax.ShapeDtypeStruct((2, h, cols), r.dtype),
        scratch_shapes=[pltpu.SemaphoreType.DMA, pltpu.SemaphoreType.DMA],
        compiler_params=pltpu.CompilerParams(has_side_effects=True))(r)
    c = lax.axis_index("c")
    return jnp.stack([jnp.where(c == s, r, got[s]) for s in range(2)])


def _reduce_scatter(g, tag):
    nq, rows, cols = g.shape
    g = g.reshape(nq, 2, rows // 2, cols)
    got = _swap_halves(g, f"{tag}_swap")
    pair = _pair_sum(g, got, BF, f"{tag}_pair_sum")
    r4 = _scatter_xy(pair, f"{tag}_scatter")
    r = _chip_sum(r4, pair, f"{tag}_chip_sum")
    return _join_halves(r, f"{tag}_join").reshape(rows, cols)


def _adamw(w, g, m, v, name):
    rows, cols = w.shape
    tr = rows
    for cand in (512, 344, 256, 128, 64, 32, 16, 8):
        if rows % cand == 0:
            tr = cand
            break

    def body(w_ref, g_ref, m_ref, v_ref, d_ref, nm_ref, nv_ref):
        g_ = g_ref[...]
        m_ = ADAM_B1 * m_ref[...] + (1.0 - ADAM_B1) * g_
        v_ = ADAM_B2 * v_ref[...] + (1.0 - ADAM_B2) * jnp.square(g_)
        m_hat = m_ / (1.0 - ADAM_B1 ** ADAM_STEP)
        v_hat = v_ / (1.0 - ADAM_B2 ** ADAM_STEP)
        d_ref[...] = -ADAM_LR * (m_hat / (jnp.sqrt(v_hat) + ADAM_EPS) + ADAM_WD * w_ref[...])
        nm_ref[...] = m_
        nv_ref[...] = v_

    spec = pl.BlockSpec((tr, cols), lambda i: (i, 0))
    shp = jax.ShapeDtypeStruct((rows, cols), F32)
    return pl.pallas_call(body, name=name, grid=(rows // tr,), in_specs=[spec] * 4, out_specs=[spec] * 3,
                          out_shape=[shp] * 3, compiler_params=_params(("parallel",)))(w, g, m, v)


WEIGHTS = ["norm_mix_g", "norm_mlp_g", "mlp_w_up", "mlp_w_down", "norm_final_g", "a_w_in", "a_conv_w", "a_conv_b",
           "a_gate_w", "a_gate_b", "a_lambda", "a_w_out", "b_w_in", "b_ln_g", "b_ln_b", "b_w_s", "b_b_s", "b_w_out",
           "c_w_in", "c_conv_w", "c_a_log", "c_dt_bias", "c_norm_g", "c_w_out"]
SHARD_AXIS = {"mlp_w_up": 2, "mlp_w_down": 1, "a_w_in": 2, "a_conv_w": 2, "a_conv_b": 1, "a_lambda": 2, "a_w_out": 1,
              "b_w_in": 2, "b_w_out": 1, "c_w_in": 2, "c_conv_w": 2, "c_w_out": 1}
MATMUL_WEIGHTS = ["mlp_w_up", "mlp_w_down", "a_w_in", "a_w_out", "b_w_in", "b_w_out", "c_w_out", "c_w_in"]
SMALL_SHARDED = ["a_conv_w", "a_conv_b", "a_lambda", "c_conv_w"]
REPLICATED = [n for n in WEIGHTS if n not in SHARD_AXIS]
FLAT_COLS = 1024


def _rows_of(shape):
    n = 1
    for s in shape:
        n *= s
    return -(-n // FLAT_COLS)


def _pack(arrays, total_rows, dtype):
    parts = []
    used = 0
    for a in arrays:
        r = _rows_of(a.shape)
        f = a.reshape(-1).astype(dtype)
        parts.append(jnp.pad(f, (0, r * FLAT_COLS - f.shape[0])).reshape(r, FLAT_COLS))
        used += r
    if total_rows > used:
        parts.append(jnp.zeros((total_rows - used, FLAT_COLS), dtype))
    return jnp.concatenate(parts, axis=0)


def _unpack(buf, shapes):
    out, r0 = [], 0
    for shp in shapes:
        r = _rows_of(shp)
        n = 1
        for s in shp:
            n *= s
        out.append(buf[r0:r0 + r].reshape(-1)[:n].reshape(shp))
        r0 += r
    return out


def _round_up(n, m):
    return -(-n // m) * m


def kernel(x, norm_mix_g, norm_mlp_g, mlp_w_up, mlp_w_down, norm_final_g, a_w_in, a_conv_w, a_conv_b, a_gate_w, a_gate_b, a_lambda, a_w_out, b_w_in, b_ln_g, b_ln_b, b_w_s, b_b_s, b_w_out, c_w_in, c_conv_w, c_a_log, c_dt_bias, c_norm_g, c_w_out, loss_target, m_norm_mix_g, m_norm_mlp_g, m_mlp_w_up, m_mlp_w_down, m_norm_final_g, m_a_w_in, m_a_conv_w, m_a_conv_b, m_a_gate_w, m_a_gate_b, m_a_lambda, m_a_w_out, m_b_w_in, m_b_ln_g, m_b_ln_b, m_b_w_s, m_b_b_s, m_b_w_out, m_c_w_in, m_c_conv_w, m_c_a_log, m_c_dt_bias, m_c_norm_g, m_c_w_out, v_norm_mix_g, v_norm_mlp_g, v_mlp_w_up, v_mlp_w_down, v_norm_final_g, v_a_w_in, v_a_conv_w, v_a_conv_b, v_a_gate_w, v_a_gate_b, v_a_lambda, v_a_w_out, v_b_w_in, v_b_ln_g, v_b_ln_b, v_b_w_s, v_b_b_s, v_b_w_out, v_c_w_in, v_c_conv_w, v_c_a_log, v_c_dt_bias, v_c_norm_g, v_c_w_out):
    given = dict(locals())
    w_loc = {n: given[n] for n in WEIGHTS}
    m_loc = {n: given["m_" + n] for n in WEIGHTS}
    v_loc = {n: given["v_" + n] for n in WEIGHTS}

    mat_rows = _round_up(sum(_rows_of(w_loc[n].shape) for n in MATMUL_WEIGHTS), 32)
    small_rows = _round_up(sum(_rows_of(w_loc[n].shape) for n in SMALL_SHARDED), 16)
    mat_all, small_all = _all_gather_xy(
        [_pack([w_loc[n] for n in MATMUL_WEIGHTS], mat_rows, BF),
         _pack([w_loc[n] for n in SMALL_SHARDED], small_rows, F32)], "gather_weights")
    W = {n: w_loc[n] for n in REPLICATED}
    for names, buf in ((MATMUL_WEIGHTS, mat_all), (SMALL_SHARDED, small_all)):
        per_chip = [_unpack(buf[q], [w_loc[n].shape for n in names]) for q in range(N_CHIPS)]
        for i, n in enumerate(names):
            parts = _own_or([per_chip[q][i] for q in range(N_CHIPS)], w_loc[n].astype(buf.dtype))
            W[n] = jnp.concatenate(parts, axis=SHARD_AXIS[n])

    loss, grad_x, G = _local_step(x, loss_target, W)
    loss = lax.psum(loss, ("x", "y", "c"))

    sharded = MATMUL_WEIGHTS + SMALL_SHARDED
    rep_rows = _round_up(sum(_rows_of(w_loc[n].shape) for n in REPLICATED), N_CHIPS * 16)
    rep_flat = _pack([G[n] for n in REPLICATED], rep_rows, F32).reshape(N_CHIPS, rep_rows // N_CHIPS, FLAT_COLS)
    shard_rows = sum(_rows_of(w_loc[n].shape) for n in sharded)
    total_rows = _round_up(shard_rows + rep_rows // N_CHIPS, 1024)
    slots = []
    for q in range(N_CHIPS):
        pieces = []
        for n in sharded:
            width = w_loc[n].shape[SHARD_AXIS[n]]
            pieces.append(lax.slice_in_dim(G[n], q * width, (q + 1) * width, axis=SHARD_AXIS[n]))
        body = _pack(pieces, shard_rows, F32)
        pad = jnp.zeros((total_rows - shard_rows - rep_rows // N_CHIPS, FLAT_COLS), F32)
        slots.append(jnp.concatenate([body, rep_flat[q], pad], axis=0))
    red = _reduce_scatter(jnp.stack(slots), "grads")
    g_loc = dict(zip(sharded, _unpack(red, [w_loc[n].shape for n in sharded])))
    rep_quarter = red[shard_rows:shard_rows + rep_rows // N_CHIPS]
    (rep_all,) = _all_gather_xy([rep_quarter], "gather_replicated_grads")
    rep_all = jnp.stack(_own_or(rep_all, rep_quarter))
    g_loc.update(zip(REPLICATED, _unpack(rep_all.reshape(rep_rows, FLAT_COLS), [w_loc[n].shape for n in REPLICATED])))

    delta, new_m, new_v = {}, {}, {}
    big = [n for n in MATMUL_WEIGHTS if w_loc[n].size % FLAT_COLS == 0]
    for n in big:
        shp = w_loc[n].shape
        two_d = lambda a: a.reshape(-1, FLAT_COLS)
        d, nm, nv = _adamw(two_d(w_loc[n]), two_d(g_loc[n]), two_d(m_loc[n]), two_d(v_loc[n]), f"adamw_{n}")
        delta[n], new_m[n], new_v[n] = d.reshape(shp), nm.reshape(shp), nv.reshape(shp)
    small = [n for n in WEIGHTS if n not in big]
    small_shapes = [w_loc[n].shape for n in small]
    rows = _round_up(sum(_rows_of(s) for s in small_shapes), 8)
    packed = [_pack([src[n] for n in small], rows, F32) for src in (w_loc, g_loc, m_loc, v_loc)]
    for dst, buf in zip((delta, new_m, new_v), _adamw(*packed, "adamw_small")):
        dst.update(zip(small, _unpack(buf, small_shapes)))

    return (loss, grad_x, *[g_loc[n] for n in WEIGHTS], *[delta[n] for n in WEIGHTS],
            *[new_m[n] for n in WEIGHTS], *[new_v[n] for n in WEIGHTS])
```

```python
import functools

import jax
import jax.numpy as jnp
from jax import lax
from jax.experimental import pallas as pl
from jax.experimental.pallas import tpu as pltpu

F32 = jnp.float32
BF = jnp.bfloat16

LANES = 128
VMEM_LIMIT = 56 * 1024 * 1024
RG_C = 8.0
SGU_CHUNK = 128
GDN_CHUNK = 64
CONV_W = 4
N_MIXERS = 3

ADAM_LR = 0.001
ADAM_B1 = 0.9
ADAM_B2 = 0.999
ADAM_EPS = 1e-08
ADAM_WD = 0.01
ADAM_STEP = 10


def _params(sem=None):
    return pltpu.CompilerParams(dimension_semantics=sem, vmem_limit_bytes=VMEM_LIMIT)


def _shift_impl(x, s):
    if s == 0:
        return x
    n = x.shape[0]
    row = lax.broadcasted_iota(jnp.int32, x.shape, 0)
    if s > 0:
        return jnp.where(row >= s, pltpu.roll(x, s, 0), 0.0)
    return jnp.where(row < n + s, pltpu.roll(x, n + s, 0), 0.0)


@functools.partial(jax.custom_vjp, nondiff_argnums=(1,))
def _shift(x, s):
    return _shift_impl(x, s)


def _shift_fwd(x, s):
    return _shift_impl(x, s), None


def _shift_bwd(s, _, g):
    return (_shift_impl(g, -s),)


_shift.defvjp(_shift_fwd, _shift_bwd)


def _chunk_cumsum_impl(x, rev, chunk):
    n = x.shape[0]
    rc = lax.broadcasted_iota(jnp.int32, x.shape, 0) & (chunk - 1)
    sh = 1
    while sh < chunk:
        if rev:
            x = x + jnp.where(rc < chunk - sh, pltpu.roll(x, n - sh, 0), 0.0)
        else:
            x = x + jnp.where(rc >= sh, pltpu.roll(x, sh, 0), 0.0)
        sh *= 2
    return x


@functools.partial(jax.custom_vjp, nondiff_argnums=(1, 2))
def _chunk_cumsum(x, rev, chunk):
    return _chunk_cumsum_impl(x, rev, chunk)


def _chunk_cumsum_fwd(x, rev, chunk):
    return _chunk_cumsum_impl(x, rev, chunk), None


def _chunk_cumsum_bwd(rev, chunk, _, g):
    return (_chunk_cumsum_impl(g, not rev, chunk),)


_chunk_cumsum.defvjp(_chunk_cumsum_fwd, _chunk_cumsum_bwd)


def _rms(x, g, eps=1e-6):
    return x * lax.rsqrt(jnp.mean(x * x, axis=-1, keepdims=True) + eps) * g


def _sigmoid(x):
    return 0.5 * jnp.tanh(0.5 * x) + 0.5


def _silu(x):
    return x * _sigmoid(x)


def _softplus(x):
    return jnp.maximum(x, 0.0) + jnp.log1p(jnp.exp(-jnp.abs(x)))


def _neg_expm1(y, ey):
    series = -y * (1.0 + y * (1 / 2) * (1.0 + y * (1 / 3) * (1.0 + y * (1 / 4))))
    return jnp.where(y > -1 / 32, series, 1.0 - ey)


@jax.custom_vjp
def _sqrt_one_minus_sq(log_a, a):
    t = _neg_expm1(2.0 * log_a, a * a)
    return t * lax.rsqrt(jnp.maximum(t, 1e-30))


def _sqrt_one_minus_sq_fwd(log_a, a):
    t = _neg_expm1(2.0 * log_a, a * a)
    rs = lax.rsqrt(jnp.maximum(t, 1e-30))
    return t * rs, (a, rs)


def _sqrt_one_minus_sq_bwd(res, g):
    a, rs = res
    return -g * (a * a) * rs, jnp.zeros_like(a)


_sqrt_one_minus_sq.defvjp(_sqrt_one_minus_sq_fwd, _sqrt_one_minus_sq_bwd)


def _bmm_raw(a, b, form):
    r = a.ndim - 2
    con = {"nn": ((r + 1,), (r,)), "nt": ((r + 1,), (r + 1,)), "tn": ((r,), (r,))}[form]
    batch = ((0,), (0,)) if r else ((), ())
    return lax.dot_general(a.astype(BF), b.astype(BF), (con, batch), preferred_element_type=F32)


@functools.partial(jax.custom_vjp, nondiff_argnums=(2,))
def _bmm(a, b, form):
    return _bmm_raw(a, b, form)


def _bmm_fwd(a, b, form):
    return _bmm_raw(a, b, form), (a, b)


def _bmm_bwd(form, res, g):
    a, b = res
    if form == "nn":
        da, db = _bmm_raw(g, b, "nt"), _bmm_raw(a, g, "tn")
    elif form == "nt":
        da, db = _bmm_raw(g, b, "nn"), _bmm_raw(g, a, "tn")
    else:
        da, db = _bmm_raw(b, g, "nt"), _bmm_raw(a, g, "nn")
    return da.astype(a.dtype), db.astype(b.dtype)


_bmm.defvjp(_bmm_fwd, _bmm_bwd)


def _conv4(z, rows):
    out = rows[0] * _shift(z, 2)
    for k in range(1, CONV_W):
        out = out + rows[k] * _shift(z, 2 - k)
    return out


_DIMS = {"nn": ((1,), (0,)), "nt": ((1,), (1,)), "tn": ((0,), (0,))}


def _mm(a, b, mode, name, *, out_dtype=F32, epi=None, extra=None, tm=1024, tn=1024, tk=1024, b_index=None, n_cols=None,
        by_chip=None):
    if mode == "tn":
        K, M = a.shape
    else:
        M, K = a.shape
    N = n_cols if b_index is not None else (b.shape[0] if mode == "nt" else b.shape[1])
    tm, tn, tk = min(tm, M), min(tn, N), min(tk, K)
    assert M % tm == 0 and N % tn == 0 and K % tk == 0, (name, M, N, K)
    nk = K // tk
    a_spec = pl.BlockSpec((tk, tm), lambda i, j, k: (k, i)) if mode == "tn" else pl.BlockSpec((tm, tk), lambda i, j, k: (i, k))
    b_block = (tn, tk) if mode == "nt" else (tk, tn)
    if b_index is not None:
        b_spec = pl.BlockSpec((None,) * (b.ndim - 2) + b_block, lambda i, j, k: b_index(j, k))
    elif mode == "nt":
        b_spec = pl.BlockSpec(b_block, lambda i, j, k: (j, k))
    else:
        b_spec = pl.BlockSpec(b_block, lambda i, j, k: (k, j))
    o_spec = pl.BlockSpec((tm, tn), lambda i, j, k: (i, j))
    ins, specs = [a, b], [a_spec, b_spec]
    if epi in ("add", "relu2_bwd"):
        ins.append(extra)
        specs.append(o_spec)
    o_shape = (M, N)
    if by_chip == "cols":
        assert tn == N // N_CHIPS and epi is None
        o_shape, o_spec = (N_CHIPS, M, tn), pl.BlockSpec((None, tm, tn), lambda i, j, k: (j, i, 0))

    def body(*refs):
        a_ref, b_ref = refs[0], refs[1]
        e_ref = refs[2] if len(ins) == 3 else None
        o_ref = refs[len(ins)]

        def product():
            return lax.dot_general(a_ref[...].astype(BF), b_ref[...].astype(BF), (_DIMS[mode], ((), ())),
                                   preferred_element_type=F32)

        def finish(r):
            if epi == "relu2":
                r = jnp.square(jnp.maximum(r, 0.0))
            elif epi == "add":
                r = r + e_ref[...]
            elif epi == "relu2_bwd":
                r = r * (2.0 * jnp.sqrt(e_ref[...].astype(F32)))
            o_ref[...] = r.astype(out_dtype)

        if nk == 1:
            finish(product())
            return
        acc = refs[-1]
        k = pl.program_id(2)

        @pl.when(k == 0)
        def _():
            acc[...] = product()

        @pl.when(k > 0)
        def _():
            acc[...] += product()

        @pl.when(k == nk - 1)
        def _():
            finish(acc[...])

    return pl.pallas_call(
        body, name=name, grid=(M // tm, N // tn, nk), in_specs=specs, out_specs=o_spec,
        out_shape=jax.ShapeDtypeStruct(o_shape, out_dtype),
        scratch_shapes=[pltpu.VMEM((tm, tn), F32)] if nk > 1 else [],
        compiler_params=_params(("parallel", "parallel", "arbitrary")))(*ins)


def _rows_tile(T):
    return min(512, T)


def _rms_fwd(x, g, name):
    T, D = x.shape
    tr = _rows_tile(T)

    def body(x_ref, g_ref, o_ref):
        o_ref[...] = _rms(x_ref[...], g_ref[...]).astype(BF)

    return pl.pallas_call(
        body, name=name, grid=(T // tr,),
        in_specs=[pl.BlockSpec((tr, D), lambda i: (i, 0)), pl.BlockSpec((1, D), lambda i: (0, 0))],
        out_specs=pl.BlockSpec((tr, D), lambda i: (i, 0)), out_shape=jax.ShapeDtypeStruct((T, D), BF),
        compiler_params=_params(("parallel",)))(x, g.reshape(1, D))


def _rms_bwd(x, g, dhn, dres, name):
    T, D = x.shape
    tr = _rows_tile(T)

    def body(x_ref, g_ref, dhn_ref, dres_ref, dx_ref, dg_ref):
        _, vjp = jax.vjp(_rms, x_ref[...], g_ref[...])
        dx, dg = vjp(dhn_ref[...])
        dx_ref[...] = dres_ref[...] + dx

        @pl.when(pl.program_id(0) == 0)
        def _():
            dg_ref[...] = jnp.zeros_like(dg_ref)

        dg_ref[...] += dg

    row = pl.BlockSpec((tr, D), lambda i: (i, 0))
    vec = pl.BlockSpec((1, D), lambda i: (0, 0))
    dx, dg = pl.pallas_call(
        body, name=name, grid=(T // tr,), in_specs=[row, vec, row, row], out_specs=[row, vec],
        out_shape=[jax.ShapeDtypeStruct((T, D), F32), jax.ShapeDtypeStruct((1, D), F32)],
        compiler_params=_params(("arbitrary",)))(x, g.reshape(1, D), dhn, dres)
    return dx, dg.reshape(D)


def _final_loss(x, g, tgt, name):
    T, D = x.shape
    tr = _rows_tile(T)

    def body(x_ref, g_ref, t_ref, l_ref, dx_ref, dg_ref):
        y, vjp = jax.vjp(_rms, x_ref[...], g_ref[...])
        err = y - t_ref[...]
        dx, dg = vjp(err * (1.0 / D))
        dx_ref[...] = dx

        @pl.when(pl.program_id(0) == 0)
        def _():
            dg_ref[...] = jnp.zeros_like(dg_ref)
            l_ref[...] = jnp.zeros_like(l_ref)

        dg_ref[...] += dg
        l_ref[...] += (0.5 / D) * jnp.sum(jnp.sum(err * err, axis=1, keepdims=True), axis=0, keepdims=True)

    row = pl.BlockSpec((tr, D), lambda i: (i, 0))
    vec = pl.BlockSpec((1, D), lambda i: (0, 0))
    loss, dx, dg = pl.pallas_call(
        body, name=name, grid=(T // tr,), in_specs=[row, vec, row],
        out_specs=[pl.BlockSpec((1, LANES), lambda i: (0, 0)), row, vec],
        out_shape=[jax.ShapeDtypeStruct((1, LANES), F32), jax.ShapeDtypeStruct((T, D), F32),
                   jax.ShapeDtypeStruct((1, D), F32)],
        compiler_params=_params(("arbitrary",)))(x, g.reshape(1, D), tgt)
    return loss[0, 0], dx, dg.reshape(D)


def _a_pre(zx, cws, cb, gws, gbs, lams):
    xr = _conv4(zx, cws) + cb
    out = []
    for d in range(2):
        r = _sigmoid(_bmm(xr, gws[2 * d], "nn") + gbs[2 * d])
        ig = _sigmoid(_bmm(xr, gws[2 * d + 1], "nn") + gbs[2 * d + 1])
        log_a = -RG_C * r * _softplus(-lams[d])
        a = jnp.exp(log_a)
        out += [a, _sqrt_one_minus_sq(log_a, a) * ig * xr]
    return tuple(out)


def _a_post(h0, h1, zg):
    return (h0 + h1) * jax.nn.gelu(zg)


SUBLANES = 8
SCAN_TILES = 8


def _scan_jobs(jobs):
    S, C = jobs[0][0].shape
    U = min(SCAN_TILES, S // SUBLANES)
    rows = U * SUBLANES
    row = lax.broadcasted_iota(jnp.int32, (SUBLANES, C), 0)

    def prefix(a, b, reverse):
        for sh in (1, 2, 4):
            if reverse:
                m, r = row < SUBLANES - sh, SUBLANES - sh
            else:
                m, r = row >= sh, sh
            a_s = jnp.where(m, pltpu.roll(a, r, 0), 1.0)
            b_s = jnp.where(m, pltpu.roll(b, r, 0), 0.0)
            b = a * b_s + b
            a = a * a_s
        return a, b

    def step(i, carries):
        out = []
        for (a_ref, b_ref, h_ref, reverse), c in zip(jobs, carries):
            blk = (S // rows - 1 - i) if reverse else i
            t0 = pl.multiple_of(blk * rows, rows)
            order = range(U - 1, -1, -1) if reverse else range(U)
            edge = slice(0, 1) if reverse else slice(SUBLANES - 1, SUBLANES)
            for j in order:
                sl = pl.ds(t0 + j * SUBLANES, SUBLANES)
                a, b = prefix(a_ref[sl, :], b_ref[sl, :], reverse)
                h_ref[sl, :] = a * jnp.broadcast_to(c, (SUBLANES, C)) + b
                c = a[edge, :] * c + b[edge, :]
            out.append(c)
        return tuple(out)

    lax.fori_loop(0, S // rows, step, tuple(jnp.zeros((1, C), F32) for _ in jobs))


def _a_load_params(cw_ref, cb_ref, gw_ref, gb_ref, lam_ref):
    cws = [cw_ref[k:k + 1, :] for k in range(CONV_W)]
    gws = [gw_ref[d, g] for d in range(2) for g in range(2)]
    gbs = [gb_ref[d, g] for d in range(2) for g in range(2)]
    lams = [lam_ref[d:d + 1, :] for d in range(2)]
    return cws, cb_ref[...], gws, gbs, lams


def _a_in_specs(S, H):
    zg = pl.BlockSpec((None, S, LANES), lambda h, b: (b, 0, h))
    zx = pl.BlockSpec((None, S, LANES), lambda h, b: (b, 0, H + h))
    cw = pl.BlockSpec((CONV_W, LANES), lambda h, b: (0, h))
    cb = pl.BlockSpec((1, LANES), lambda h, b: (0, h))
    gw = pl.BlockSpec((2, 2, None, LANES, LANES), lambda h, b: (0, 0, h, 0, 0))
    gb = pl.BlockSpec((2, 2, None, 1, LANES), lambda h, b: (0, 0, h, 0, 0))
    lam = pl.BlockSpec((2, LANES), lambda h, b: (0, h))
    return zg, zx, cw, cb, gw, gb, lam


def _a_core_fwd(z, cw, cb, gw, gb, lam, name):
    Bq, S, D2 = z.shape
    D = D2 // 2
    H = D // LANES

    def body(zg_ref, zx_ref, cw_ref, cb_ref, gw_ref, gb_ref, lam_ref, y_ref, a_s, b_s, h_s):
        ab = _a_pre(zx_ref[...], *_a_load_params(cw_ref, cb_ref, gw_ref, gb_ref, lam_ref))
        for d in range(2):
            a_s[d] = ab[2 * d]
            b_s[d] = ab[2 * d + 1]
        _scan_jobs([(a_s.at[d], b_s.at[d], h_s.at[d], d == 1) for d in range(2)])
        y_ref[...] = _a_post(h_s[0], h_s[1], zg_ref[...]).astype(BF)

    seq = pltpu.VMEM((2, S, LANES), F32)
    return pl.pallas_call(
        body, name=name, grid=(H, Bq), in_specs=list(_a_in_specs(S, H)),
        out_specs=pl.BlockSpec((None, S, LANES), lambda h, b: (b, 0, h)),
        out_shape=jax.ShapeDtypeStruct((Bq, S, D), BF), scratch_shapes=[seq, seq, seq],
        compiler_params=_params(("parallel", "arbitrary")))(z, z, cw, cb.reshape(1, D), gw, gb.reshape(2, 2, H, 1, LANES), lam)


def _a_core_bwd(z, dy, cw, cb, gw, gb, lam, name):
    Bq, S, D2 = z.shape
    D = D2 // 2
    H = D // LANES

    def body(zg_ref, zx_ref, dy_ref, cw_ref, cb_ref, gw_ref, gb_ref, lam_ref,
             dzg_ref, dzx_ref, dcw_ref, dcb_ref, dgw_ref, dgb_ref, dlam_ref, a_s, b_s, h_s, l_s):
        prm = _a_load_params(cw_ref, cb_ref, gw_ref, gb_ref, lam_ref)
        ab = _a_pre(zx_ref[...], *prm)
        for d in range(2):
            a_s[d] = ab[2 * d]
            b_s[d] = ab[2 * d + 1]
        _scan_jobs([(a_s.at[d], b_s.at[d], h_s.at[d], d == 1) for d in range(2)])
        _, post_vjp = jax.vjp(_a_post, h_s[0], h_s[1], zg_ref[...])
        dh0, dh1, dzg = post_vjp(dy_ref[...])
        dzg_ref[...] = dzg.astype(BF)
        for d, dh in ((0, dh0), (1, dh1)):
            b_s[d] = dh
            a_s[d] = _shift(a_s[d], -1 if d == 0 else 1)
        _scan_jobs([(a_s.at[d], b_s.at[d], l_s.at[d], d == 0) for d in range(2)])
        cot = []
        for d in range(2):
            cot += [l_s[d] * _shift(h_s[d], 1 if d == 0 else -1), l_s[d]]
        _, pre_vjp = jax.vjp(_a_pre, zx_ref[...], *prm)
        dzx, dcws, dcb, dgws, dgbs, dlams = pre_vjp(tuple(cot))
        dzx_ref[...] = dzx.astype(BF)

        @pl.when(pl.program_id(1) == 0)
        def _():
            for r in (dcw_ref, dcb_ref, dgw_ref, dgb_ref, dlam_ref):
                r[...] = jnp.zeros_like(r)

        for k in range(CONV_W):
            dcw_ref[k:k + 1, :] += dcws[k]
        dcb_ref[...] += dcb
        for d in range(2):
            dlam_ref[d:d + 1, :] += dlams[d]
            for g in range(2):
                dgw_ref[d, g] += dgws[2 * d + g]
                dgb_ref[d, g] += dgbs[2 * d + g]

    zg, zx, cws, cbs, gws, gbs, lams = _a_in_specs(S, H)
    dyspec = pl.BlockSpec((None, S, LANES), lambda h, b: (b, 0, h))
    seq = pltpu.VMEM((2, S, LANES), F32)
    dzg, dzx, dcw, dcb, dgw, dgb, dlam = pl.pallas_call(
        body, name=name, grid=(H, Bq), in_specs=[zg, zx, dyspec, cws, cbs, gws, gbs, lams],
        out_specs=[dyspec, dyspec, cws, cbs, gws, gbs, lams],
        out_shape=[jax.ShapeDtypeStruct((Bq, S, D), BF), jax.ShapeDtypeStruct((Bq, S, D), BF),
                   jax.ShapeDtypeStruct((CONV_W, D), F32), jax.ShapeDtypeStruct((1, D), F32),
                   jax.ShapeDtypeStruct((2, 2, H, LANES, LANES), F32), jax.ShapeDtypeStruct((2, 2, H, 1, LANES), F32),
                   jax.ShapeDtypeStruct((2, D), F32)],
        scratch_shapes=[seq, seq, seq, seq],
        compiler_params=_params(("parallel", "arbitrary")))(z, z, dy, cw, cb.reshape(1, D), gw, gb.reshape(2, 2, H, 1, LANES), lam)
    dz = jnp.concatenate([dzg, dzx], axis=-1)
    return dz, dcw, dcb.reshape(D), dgw, dgb.reshape(2, 2, H, LANES), dlam


def _b_fn(z, lng, lnb, wss, bsf):
    D = z.shape[1] // 2
    zz = jax.nn.gelu(z)
    u, v = zz[:, :D], zz[:, D:]
    mu = jnp.mean(v, axis=-1, keepdims=True)
    var = jnp.mean(jnp.square(v - mu), axis=-1, keepdims=True)
    vn = (v - mu) * lax.rsqrt(var + 1e-5) * lng + lnb
    vs = jnp.concatenate([_bmm(wss[g], vn[:, g * LANES:(g + 1) * LANES], "nn") for g in range(D // LANES)], axis=1)
    return u * (vs + bsf)


def _b_specs(D, GB):
    row = lambda w: pl.BlockSpec((SGU_CHUNK, w), lambda i: (i, 0))
    vec = pl.BlockSpec((1, D), lambda i: (0, 0))
    ws = pl.BlockSpec((GB, SGU_CHUNK, SGU_CHUNK), lambda i: (0, 0, 0))
    bsf = pl.BlockSpec((SGU_CHUNK, D), lambda i: (0, 0))
    return row, vec, ws, bsf


def _b_core_fwd(z, lng, lnb, ws, bsf, name):
    T, D2 = z.shape
    D = D2 // 2
    GB = D // LANES
    row, vec, wspec, bspec = _b_specs(D, GB)

    def body(z_ref, lng_ref, lnb_ref, ws_ref, bsf_ref, y_ref):
        wss = [ws_ref[g] for g in range(GB)]
        y_ref[...] = _b_fn(z_ref[...], lng_ref[...], lnb_ref[...], wss, bsf_ref[...]).astype(BF)

    return pl.pallas_call(
        body, name=name, grid=(T // SGU_CHUNK,), in_specs=[row(D2), vec, vec, wspec, bspec], out_specs=row(D),
        out_shape=jax.ShapeDtypeStruct((T, D), BF), compiler_params=_params(("parallel",)))(
            z, lng.reshape(1, D), lnb.reshape(1, D), ws, bsf)


def _b_core_bwd(z, dy, lng, lnb, ws, bsf, name):
    T, D2 = z.shape
    D = D2 // 2
    GB = D // LANES
    row, vec, wspec, bspec = _b_specs(D, GB)

    def body(z_ref, dy_ref, lng_ref, lnb_ref, ws_ref, bsf_ref, dz_ref, dlng_ref, dlnb_ref, dws_ref, dbsf_ref):
        wss = [ws_ref[g] for g in range(GB)]
        _, vjp = jax.vjp(_b_fn, z_ref[...], lng_ref[...], lnb_ref[...], wss, bsf_ref[...])
        dz, dlng, dlnb, dwss, dbsf = vjp(dy_ref[...])
        dz_ref[...] = dz.astype(BF)

        @pl.when(pl.program_id(0) == 0)
        def _():
            for r in (dlng_ref, dlnb_ref, dws_ref, dbsf_ref):
                r[...] = jnp.zeros_like(r)

        dlng_ref[...] += dlng
        dlnb_ref[...] += dlnb
        dbsf_ref[...] += dbsf
        for g in range(GB):
            dws_ref[g] += dwss[g]

    dz, dlng, dlnb, dws, dbsf = pl.pallas_call(
        body, name=name, grid=(T // SGU_CHUNK,), in_specs=[row(D2), row(D), vec, vec, wspec, bspec],
        out_specs=[row(D2), vec, vec, wspec, bspec],
        out_shape=[jax.ShapeDtypeStruct((T, D2), BF), jax.ShapeDtypeStruct((1, D), F32), jax.ShapeDtypeStruct((1, D), F32),
                   jax.ShapeDtypeStruct((GB, SGU_CHUNK, SGU_CHUNK), F32), jax.ShapeDtypeStruct((SGU_CHUNK, D), F32)],
        compiler_params=_params(("arbitrary",)))(z, dy, lng.reshape(1, D), lnb.reshape(1, D), ws, bsf)
    return dz, dlng.reshape(D), dlnb.reshape(D), dws, dbsf


def _lane_is(j):
    return lax.broadcasted_iota(jnp.int32, (1, LANES), 1) == j


def _lane_col(x, j):
    return jnp.sum(jnp.where(_lane_is(j), x, 0.0), axis=1, keepdims=True)


def _c_pre(zq, zk, zv, zs, cwq, cwk, cwv, pcs, head, HC):
    q = _silu(_conv4(zq, cwq))
    k = _silu(_conv4(zk, cwk))
    v = _silu(_conv4(zv, cwv))
    q = q * lax.rsqrt(jnp.sum(q * q, axis=-1, keepdims=True) + 1e-6) * (LANES ** -0.5)
    k = k * lax.rsqrt(jnp.sum(k * k, axis=-1, keepdims=True) + 1e-6)
    gbp = jnp.zeros_like(zs)
    for d in range(2):
        a_logit = _lane_col(zs, d * HC + head)
        b_logit = _lane_col(zs, 2 * HC + d * HC + head)
        g = -jnp.exp(pcs[d]) * _softplus(a_logit + pcs[2 + d])
        beta = jnp.broadcast_to(_sigmoid(b_logit), g.shape)
        gbp = gbp + jnp.where(_lane_is(2 * d), g, 0.0) + jnp.where(_lane_is(2 * d + 1), beta, 0.0)
    return q, k, v, gbp


def _mm3(x, y):
    xh, yh = x.astype(BF), y.astype(BF)
    xl, yl = (x - xh.astype(F32)).astype(BF), (y - yh.astype(F32)).astype(BF)
    return _bmm_raw(xh, yh, "nn") + _bmm_raw(xh, yl, "nn") + _bmm_raw(xl, yh, "nn")


def _tri_inv_impl(a):
    C = a.shape[-1]
    eye = (lax.broadcasted_iota(jnp.int32, (1, C, C), 1) == lax.broadcasted_iota(jnp.int32, (1, C, C), 2)).astype(F32)
    r = eye - a
    p = a
    n = 2
    while n < C:
        p = _mm3(p, p)
        r = r + _mm3(r, p)
        n *= 2
    return r


@jax.custom_vjp
def _tri_inv(a):
    return _tri_inv_impl(a)


def _tri_inv_fwd(a):
    t = _tri_inv_impl(a)
    return t, t


def _tri_inv_bwd(t, g):
    tt = jnp.swapaxes(t, 1, 2)
    return (-_bmm_raw(_bmm_raw(tt, g, "nn"), tt, "nn"),)


_tri_inv.defvjp(_tri_inv_fwd, _tri_inv_bwd)


@jax.custom_vjp
def _pair_diff(gc3):
    m = gc3[:, :, :gc3.shape[1]]
    return m - jnp.swapaxes(m, 1, 2)


def _pair_diff_fwd(gc3):
    return _pair_diff(gc3), None


def _pair_diff_bwd(_, g):
    d = jnp.sum(g, axis=2, keepdims=True) - jnp.sum(jnp.swapaxes(g, 1, 2), axis=2, keepdims=True)
    return (jnp.broadcast_to(d * (1.0 / LANES), d.shape[:2] + (LANES,)),)


_pair_diff.defvjp(_pair_diff_fwd, _pair_diff_bwd)


@jax.custom_vjp
def _tri_inv_saved(a, t):
    return t


def _tri_inv_saved_fwd(a, t):
    return t, t


def _tri_inv_saved_bwd(t, g):
    return _tri_inv_bwd(t, g)[0], jnp.zeros_like(t)


_tri_inv_saved.defvjp(_tri_inv_saved_fwd, _tri_inv_saved_bwd)


def _c_phase1(q, k, v, gbp, rev, t_saved=None):
    S = q.shape[0]
    C = GDN_CHUNK
    N = S // C
    col = 2 if rev else 0
    gB = jnp.broadcast_to(_lane_col(gbp, col), (S, LANES))
    bB = jnp.broadcast_to(_lane_col(gbp, col + 1), (S, LANES))
    r3 = lambda t: t.reshape(N, C, LANES)
    gc3 = r3(_chunk_cumsum(gB, rev, C))
    q3, k3, v3, b3, g3 = r3(q), r3(k), r3(v), r3(bB), r3(gB)
    ri = lax.broadcasted_iota(jnp.int32, (1, C, C), 1)
    ci = lax.broadcasted_iota(jnp.int32, (1, C, C), 2)
    incl = (ri <= ci) if rev else (ri >= ci)
    strict = (ri < ci) if rev else (ri > ci)
    decay = jnp.where(incl, jnp.exp(jnp.where(incl, _pair_diff(gc3), 0.0)), 0.0)
    kb = k3 * b3
    vb = v3 * b3
    A = jnp.where(strict, _bmm(kb, k3, "nt") * decay, 0.0)
    T = _tri_inv(A) if t_saved is None else _tri_inv_saved(A, t_saved)
    egc = jnp.exp(gc3)
    u = _bmm(T, vb, "nn")
    w = _bmm(T, kb * egc, "nn")
    qk = _bmm(q3, k3, "nt") * decay
    glast = jnp.sum(g3, axis=1, keepdims=True)
    kd = k3 * jnp.exp(glast - gc3)
    k2 = _bmm(kd, w, "tn")
    z = _bmm(kd, u, "tn")
    qe2 = q3 * egc - _bmm(qk, w, "nn")
    o0 = _bmm(qk, u, "nn")
    return k2, z, jnp.exp(glast), qe2, o0, T


def _c_next_state(state, k2, z, eg):
    return state * eg - _bmm(k2, state, "nn") + z


def _c_out(state, qe2, o0):
    return _bmm(qe2, state, "nn") + o0


def _c_post(o, zg, ng):
    return _rms(o, ng) * _silu(zg)


def _c_pre_specs(S, H):
    col = lambda c0: pl.BlockSpec((None, S, LANES), lambda h, b: (b, 0, c0 * H + h))
    zs = pl.BlockSpec((None, S, LANES), lambda h, b: (b, 0, 0))
    cw = lambda c0: pl.BlockSpec((None, CONV_W, LANES), lambda h, b: (c0, 0, h))
    pc = pl.BlockSpec((None, 4, LANES), lambda h, b: (h, 0, 0))
    return col, zs, cw, pc


def _c_pre_fwd(z, zs, cw3, pc, name):
    Bq, S, D4 = z.shape
    D = D4 // 4
    H = D // LANES
    col, zss, cw, pcs = _c_pre_specs(S, H)

    def body(zq_ref, zk_ref, zv_ref, zs_ref, cwq_ref, cwk_ref, cwv_ref, pc_ref, q_ref, k_ref, v_ref, gbp_ref):
        rows = lambda r: [r[i:i + 1, :] for i in range(r.shape[0])]
        q, k, v, gbp = _c_pre(zq_ref[...], zk_ref[...], zv_ref[...], zs_ref[...], rows(cwq_ref), rows(cwk_ref),
                              rows(cwv_ref), rows(pc_ref), pl.program_id(0), H)
        q_ref[...] = q
        k_ref[...] = k
        v_ref[...] = v
        gbp_ref[...] = gbp

    out = pl.BlockSpec((None, S, LANES), lambda h, b: (b, 0, h))
    shp = jax.ShapeDtypeStruct((Bq, S, D), F32)
    return pl.pallas_call(
        body, name=name, grid=(H, Bq), in_specs=[col(0), col(1), col(2), zss, cw(0), cw(1), cw(2), pcs],
        out_specs=[out] * 4, out_shape=[shp] * 4, compiler_params=_params(("parallel", "arbitrary")))(
            z, z, z, zs, cw3, cw3, cw3, pc)


def _c_pre_bwd(z, zs, cw3, pc, dq, dk, dv, dgbp, name):
    Bq, S, D4 = z.shape
    D = D4 // 4
    H = D // LANES
    col, zss, cw, pcs = _c_pre_specs(S, H)

    def body(zq_ref, zk_ref, zv_ref, zs_ref, cwq_ref, cwk_ref, cwv_ref, pc_ref, dq_ref, dk_ref, dv_ref, dgbp_ref,
             dzq_ref, dzk_ref, dzv_ref, dzs_ref, dcw_ref, dpc_ref):
        rows = lambda r: [r[i:i + 1, :] for i in range(r.shape[0])]
        fn = functools.partial(_c_pre, head=pl.program_id(0), HC=H)
        _, vjp = jax.vjp(fn, zq_ref[...], zk_ref[...], zv_ref[...], zs_ref[...], rows(cwq_ref), rows(cwk_ref),
                         rows(cwv_ref), rows(pc_ref))
        dzq, dzk, dzv, dzs, dcwq, dcwk, dcwv, dpcs = vjp((dq_ref[...], dk_ref[...], dv_ref[...], dgbp_ref[...]))
        dzq_ref[...] = dzq.astype(BF)
        dzk_ref[...] = dzk.astype(BF)
        dzv_ref[...] = dzv.astype(BF)
        dzs_ref[...] = dzs

        @pl.when(pl.program_id(1) == 0)
        def _():
            dcw_ref[...] = jnp.zeros_like(dcw_ref)
            dpc_ref[...] = jnp.zeros_like(dpc_ref)

        for c, dc in enumerate((dcwq, dcwk, dcwv)):
            for i in range(CONV_W):
                dcw_ref[c, i:i + 1, :] += dc[i]
        for i in range(4):
            dpc_ref[i:i + 1, :] += dpcs[i]

    out = pl.BlockSpec((None, S, LANES), lambda h, b: (b, 0, h))
    dzs_spec = pl.BlockSpec((None, None, S, LANES), lambda h, b: (h, b, 0, 0))
    dcw_spec = pl.BlockSpec((3, CONV_W, LANES), lambda h, b: (0, 0, h))
    bshape = jax.ShapeDtypeStruct((Bq, S, D), BF)
    dzq, dzk, dzv, dzs, dcw3, dpc = pl.pallas_call(
        body, name=name, grid=(H, Bq),
        in_specs=[col(0), col(1), col(2), zss, cw(0), cw(1), cw(2), pcs, out, out, out, out],
        out_specs=[out, out, out, dzs_spec, dcw_spec, pcs],
        out_shape=[bshape, bshape, bshape, jax.ShapeDtypeStruct((H, Bq, S, LANES), F32),
                   jax.ShapeDtypeStruct((3, CONV_W, D), F32), jax.ShapeDtypeStruct((H, 4, LANES), F32)],
        compiler_params=_params(("parallel", "arbitrary")))(z, z, z, zs, cw3, cw3, cw3, pc, dq, dk, dv, dgbp)
    return dzq, dzk, dzv, dzs, dcw3, dpc


def _c_saved_shapes(Bq, H, S):
    N, C = S // GDN_CHUNK, GDN_CHUNK
    return [(Bq, H, 2, N, LANES, LANES), (Bq, H, 2, N, LANES, LANES), (Bq, H, 2, N, 1, LANES), (Bq, H, 2, N, C, LANES),
            (Bq, H, 2, N, C, C)]


def _c_saved_scratch(S):
    return [pltpu.VMEM(shp[3:], F32) for shp in _c_saved_shapes(1, 1, S)]


PHASE1_CHUNKS = 8


def _c_blocks(S, fn):
    nb = min(PHASE1_CHUNKS, S // GDN_CHUNK)
    rows = nb * GDN_CHUNK

    def blk(i, carry):
        fn(pl.ds(pl.multiple_of(i * rows, rows), rows), pl.ds(pl.multiple_of(i * nb, nb), nb))
        return carry

    lax.fori_loop(0, S // rows, blk, 0)


def _c_phase1_blocks(in_refs, k2_ref, z_ref, eg_ref, qe2_ref, t_ref, o_ref, rev):
    def fn(rows, chunks):
        k2, z, eg, qe2, o0, t = _c_phase1(*[r[rows, :] for r in in_refs], rev)
        k2_ref[chunks] = k2
        z_ref[chunks] = z
        eg_ref[chunks] = eg
        qe2_ref[chunks] = qe2
        t_ref[chunks] = t
        o_ref[chunks] += o0

    _c_blocks(in_refs[0].shape[0], fn)


def _c_sweep(k2_ref, z_ref, eg_ref, st_ref, rev):
    N = st_ref.shape[0]

    def step(i, state):
        n = (N - 1 - i) if rev else i
        st_ref[n] = state
        return _c_next_state(state, k2_ref[n], z_ref[n], eg_ref[n])

    lax.fori_loop(0, N, step, jnp.zeros((LANES, LANES), F32))


def _c_sweep_adjoint(k2_ref, eg_ref, dso_ref, gs_ref, rev):
    N = gs_ref.shape[0]

    def step(i, g):
        n = i if rev else (N - 1 - i)
        gs_ref[n] = g
        return dso_ref[n] + g * eg_ref[n] - _bmm_raw(k2_ref[n], g, "tn")

    lax.fori_loop(0, N, step, jnp.zeros((LANES, LANES), F32))


def _c_mid_fwd(q, k, v, gbp, name):
    Bq, S, D = q.shape
    H = D // LANES
    N = S // GDN_CHUNK
    blk = pl.BlockSpec((None, S, LANES), lambda h, b: (b, 0, h))
    blk3 = pl.BlockSpec((None, N, GDN_CHUNK, LANES), lambda h, b: (b, 0, 0, h))
    n_saved = len(_c_saved_shapes(Bq, H, S))

    def body(q_ref, k_ref, v_ref, gbp_ref, o3, *rest):
        saved_hbm, scr = rest[:n_saved], rest[n_saved:]
        per_dir = n_saved + 1
        sems = scr[2 * per_dir]
        o3[...] = jnp.zeros_like(o3)
        copies = []
        for d, rev in enumerate((False, True)):
            k2_ref, st_ref, eg_ref, qe2_ref, t_ref, z_ref = scr[d * per_dir:(d + 1) * per_dir]
            _c_phase1_blocks((q_ref, k_ref, v_ref, gbp_ref), k2_ref, z_ref, eg_ref, qe2_ref, t_ref, o3, rev)
            _c_sweep(k2_ref, z_ref, eg_ref, st_ref, rev)

            def add_out(rows, chunks):
                o3[chunks] += _bmm_raw(qe2_ref[chunks], st_ref[chunks], "nn")

            _c_blocks(S, add_out)
            for i, (src, dst) in enumerate(zip((k2_ref, st_ref, eg_ref, qe2_ref, t_ref), saved_hbm)):
                copies.append(pltpu.make_async_copy(src, dst.at[pl.program_id(1), pl.program_id(0), d], sems.at[d, i]))
                copies[-1].start()
        for cp in copies:
            cp.wait()

    one_dir = _c_saved_scratch(S) + [pltpu.VMEM((N, LANES, LANES), F32)]
    outs = pl.pallas_call(
        body, name=name, grid=(H, Bq), in_specs=[blk] * 4,
        out_specs=[blk3] + [pl.BlockSpec(memory_space=pltpu.HBM)] * n_saved,
        out_shape=[jax.ShapeDtypeStruct((Bq, N, GDN_CHUNK, D), F32)]
        + [jax.ShapeDtypeStruct(shp, F32) for shp in _c_saved_shapes(Bq, H, S)],
        scratch_shapes=one_dir + one_dir + [pltpu.SemaphoreType.DMA((2, n_saved))],
        compiler_params=_params(("parallel", "parallel")))(q, k, v, gbp)
    return outs[0].reshape(Bq, S, D), tuple(outs[1:])


def _c_mid_bwd(q, k, v, gbp, do, saved, name):
    Bq, S, D = q.shape
    H = D // LANES
    N = S // GDN_CHUNK
    blk = pl.BlockSpec((None, S, LANES), lambda h, b: (b, 0, h))
    blk3 = pl.BlockSpec((None, N, GDN_CHUNK, LANES), lambda h, b: (b, 0, 0, h))
    n_saved = len(saved)

    def body(q_ref, k_ref, v_ref, gbp_ref, do3, *rest):
        saved_hbm = rest[:n_saved]
        dq_ref, dk_ref, dv_ref, dgbp_ref = rest[n_saved:n_saved + 4]
        scr = rest[n_saved + 4:]
        sets = (scr[:n_saved], scr[n_saved:2 * n_saved])
        dso_ref, gs_ref, sems = scr[2 * n_saved:]
        in_refs = (q_ref, k_ref, v_ref, gbp_ref)
        out_refs = (dq_ref, dk_ref, dv_ref, dgbp_ref)
        copies = [[pltpu.make_async_copy(src.at[pl.program_id(1), pl.program_id(0), d], dst, sems.at[d, i])
                   for i, (src, dst) in enumerate(zip(saved_hbm, sets[d]))] for d in range(2)]
        for cp in copies[0] + copies[1]:
            cp.start()
        for d, rev in enumerate((False, True)):
            k2_ref, st_ref, eg_ref, qe2_ref, t_ref = sets[d]
            for cp in copies[d]:
                cp.wait()

            def out_to_state(rows, chunks):
                dso_ref[chunks] = _bmm_raw(qe2_ref[chunks], do3[chunks], "tn")

            _c_blocks(S, out_to_state)
            _c_sweep_adjoint(k2_ref, eg_ref, dso_ref, gs_ref, rev)

            def block_vjp(rows, chunks):
                states, t_saved = st_ref[chunks], t_ref[chunks]

                def chunk_fn(q_, k_, v_, gbp_):
                    k2, z, eg, qe2, o0, _ = _c_phase1(q_, k_, v_, gbp_, rev, t_saved)
                    return _c_next_state(states, k2, z, eg), _c_out(states, qe2, o0)

                _, vjp = jax.vjp(chunk_fn, *[r[rows, :] for r in in_refs])
                for r, c in zip(out_refs, vjp((gs_ref[chunks], do3[chunks]))):
                    if rev:
                        r[rows, :] += c
                    else:
                        r[rows, :] = c

            _c_blocks(S, block_vjp)

    shp = jax.ShapeDtypeStruct((Bq, S, D), F32)
    mat = pltpu.VMEM((N, LANES, LANES), F32)
    return pl.pallas_call(
        body, name=name, grid=(H, Bq), in_specs=[blk] * 4 + [blk3] + [pl.BlockSpec(memory_space=pltpu.HBM)] * n_saved,
        out_specs=[blk] * 4, out_shape=[shp] * 4,
        scratch_shapes=_c_saved_scratch(S) + _c_saved_scratch(S) + [mat, mat, pltpu.SemaphoreType.DMA((2, n_saved))],
        compiler_params=_params(("parallel", "parallel")))(q, k, v, gbp, do.reshape(Bq, N, GDN_CHUNK, D), *saved)


def _c_post_fwd(o, z, ng, name):
    Bq, S, D = o.shape
    H = D // LANES
    blk = pl.BlockSpec((None, S, LANES), lambda h, b: (b, 0, h))
    gate = pl.BlockSpec((None, S, LANES), lambda h, b: (b, 0, 3 * H + h))
    vec = pl.BlockSpec((1, LANES), lambda h, b: (0, 0))

    def body(o_ref, zg_ref, ng_ref, y_ref):
        y_ref[...] = _c_post(o_ref[...], zg_ref[...], ng_ref[...]).astype(BF)

    return pl.pallas_call(
        body, name=name, grid=(H, Bq), in_specs=[blk, gate, vec], out_specs=blk,
        out_shape=jax.ShapeDtypeStruct((Bq, S, D), BF), compiler_params=_params(("parallel", "parallel")))(
            o, z, ng.reshape(1, LANES))


def _c_post_bwd(o, z, ng, dy, name):
    Bq, S, D = o.shape
    H = D // LANES
    blk = pl.BlockSpec((None, S, LANES), lambda b, h: (b, 0, h))
    gate = pl.BlockSpec((None, S, LANES), lambda b, h: (b, 0, 3 * H + h))
    vec = pl.BlockSpec((1, LANES), lambda b, h: (0, 0))

    def body(o_ref, zg_ref, ng_ref, dy_ref, do_ref, dzg_ref, dng_ref):
        _, vjp = jax.vjp(_c_post, o_ref[...], zg_ref[...], ng_ref[...])
        do, dzg, dng = vjp(dy_ref[...])
        do_ref[...] = do
        dzg_ref[...] = dzg.astype(BF)

        @pl.when((pl.program_id(0) == 0) & (pl.program_id(1) == 0))
        def _():
            dng_ref[...] = jnp.zeros_like(dng_ref)

        dng_ref[...] += dng

    do, dzg, dng = pl.pallas_call(
        body, name=name, grid=(Bq, H), in_specs=[blk, gate, vec, blk], out_specs=[blk, blk, vec],
        out_shape=[jax.ShapeDtypeStruct((Bq, S, D), F32), jax.ShapeDtypeStruct((Bq, S, D), BF),
                   jax.ShapeDtypeStruct((1, LANES), F32)],
        compiler_params=_params(("arbitrary", "arbitrary")))(o, z, ng.reshape(1, LANES), dy)
    return do, dzg, dng.reshape(LANES)


def _c_param_rows(a_log, dt_bias):
    p = jnp.concatenate([a_log, dt_bias], axis=0).T
    return jnp.broadcast_to(p[:, :, None], p.shape + (LANES,)).astype(F32)


def _local_step(x, tgt, W, by_chip=False):
    Bq, S, D = x.shape
    T = Bq * S
    H = D // LANES
    L = W["norm_mix_g"].shape[0]
    seq = lambda t: t.reshape(Bq, S, t.shape[-1])
    flat = lambda t: t.reshape(T, t.shape[-1])
    if "mlp_up_slots" in W:
        up4, down4 = W["mlp_up_slots"], W["mlp_down_slots"]
    else:
        up4 = W["mlp_w_up"].reshape(L, D, N_CHIPS, -1).transpose(2, 0, 1, 3)
        down4 = W["mlp_w_down"].reshape(L, N_CHIPS, -1, D).transpose(1, 0, 2, 3)
    sw = up4.shape[-1]
    F = N_CHIPS * sw

    xs = flat(x)
    saved = []
    for i in range(L):
        kind, j = i % N_MIXERS, i // N_MIXERS
        tag = f"l{i}"
        sv = {"x": xs}
        hn = _rms_fwd(xs, W["norm_mix_g"][i], f"{tag}_mix_norm")
        sv["hn"] = hn
        if kind == 0:
            z = _mm(hn, W["a_w_in"][j], "nn", f"{tag}_a_in")
            y = _a_core_fwd(seq(z), W["a_conv_w"][j], W["a_conv_b"][j], W["a_gate_w"][j], W["a_gate_b"][j],
                            W["a_lambda"][j], f"{tag}_a_core")
            sv["z"] = z
            w_out = W["a_w_out"][j]
        elif kind == 1:
            z = _mm(hn, W["b_w_in"][j], "nn", f"{tag}_b_in")
            bsf = jnp.repeat(W["b_b_s"][j].T, LANES, axis=1)
            y = _b_core_fwd(z, W["b_ln_g"][j], W["b_ln_b"][j], W["b_w_s"][j], bsf, f"{tag}_b_core")
            sv["z"], sv["bsf"] = z, bsf
            w_out = W["b_w_out"][j]
        else:
            w_in = W["c_w_in"][j]
            w_small = jnp.pad(w_in[:, 4 * D:], ((0, 0), (0, LANES - 4 * H)))
            z = _mm(hn, w_in[:, :4 * D], "nn", f"{tag}_c_in")
            zs = _mm(hn, w_small, "nn", f"{tag}_c_in_small")
            cw3 = W["c_conv_w"][j].reshape(CONV_W, 3, D).transpose(1, 0, 2)
            pc = _c_param_rows(W["c_a_log"][j], W["c_dt_bias"][j])
            q, k, v, gbp = _c_pre_fwd(seq(z), seq(zs), cw3, pc, f"{tag}_c_pre")
            o, sv["mid"] = _c_mid_fwd(q, k, v, gbp, f"{tag}_c_mid")
            y = _c_post_fwd(o, seq(z), W["c_norm_g"][j], f"{tag}_c_post")
            sv.update(z=z, zs=zs, cw3=cw3, pc=pc, q=q, k=k, v=v, gbp=gbp, o=o, w_small=w_small)
            w_out = W["c_w_out"][j]
        y = flat(y)
        sv["y"] = y
        x1 = _mm(y, w_out, "nn", f"{tag}_mix_out", epi="add", extra=xs)
        sv["x1"] = x1
        hn2 = _rms_fwd(x1, W["norm_mlp_g"][i], f"{tag}_mlp_norm")
        act = _mm(hn2, up4, "nn", f"{tag}_mlp_up", out_dtype=BF, epi="relu2", tn=sw, tk=D, n_cols=F,
                  b_index=lambda j, k, i=i: (j, i, 0, 0))
        xs = _mm(act, down4, "nn", f"{tag}_mlp_down", epi="add", extra=x1, tn=D, tk=sw, n_cols=D,
                 b_index=lambda j, k, i=i: (k, i, 0, 0))
        sv["hn2"], sv["act"] = hn2, act
        saved.append(sv)

    loss, dx, dgf = _final_loss(xs, W["norm_final_g"], flat(tgt), "final_loss")

    G = {"norm_final_g": dgf}
    per_layer = {n: [None] * L for n in ("norm_mix_g", "norm_mlp_g", "mlp_w_up", "mlp_w_down")}
    mixer = {}
    for i in reversed(range(L)):
        kind, j = i % N_MIXERS, i // N_MIXERS
        tag = f"l{i}"
        sv = saved[i]
        dhid = _mm(dx, down4, "nt", f"{tag}_mlp_dhid", out_dtype=BF, epi="relu2_bwd", extra=sv["act"], tn=sw, tk=D,
                   n_cols=F, b_index=lambda j, k, i=i: (j, i, 0, 0))
        per_layer["mlp_w_down"][i] = _mm(sv["act"], dx, "tn", f"{tag}_mlp_dwdown").reshape(N_CHIPS, sw, D)
        per_layer["mlp_w_up"][i] = _mm(sv["hn2"], dhid, "tn", f"{tag}_mlp_dwup", tn=sw, by_chip="cols")
        dhn2 = _mm(dhid, up4, "nt", f"{tag}_mlp_dhn", tn=D, tk=sw, n_cols=D, b_index=lambda j, k, i=i: (k, i, 0, 0))
        dx, per_layer["norm_mlp_g"][i] = _rms_bwd(sv["x1"], W["norm_mlp_g"][i], dhn2, dx, f"{tag}_mlp_norm_bwd")
        g = {}
        if kind == 0:
            dy = _mm(dx, W["a_w_out"][j], "nt", f"{tag}_a_dy")
            g["a_w_out"] = _mm(sv["y"], dx, "tn", f"{tag}_a_dwout")
            dz, g["a_conv_w"], g["a_conv_b"], g["a_gate_w"], g["a_gate_b"], g["a_lambda"] = _a_core_bwd(
                seq(sv["z"]), seq(dy), W["a_conv_w"][j], W["a_conv_b"][j], W["a_gate_w"][j], W["a_gate_b"][j],
                W["a_lambda"][j], f"{tag}_a_core_bwd")
            dz = flat(dz)
            g["a_w_in"] = _mm(sv["hn"], dz, "tn", f"{tag}_a_dwin")
            dhn = _mm(dz, W["a_w_in"][j], "nt", f"{tag}_a_dhn")
        elif kind == 1:
            dy = _mm(dx, W["b_w_out"][j], "nt", f"{tag}_b_dy")
            g["b_w_out"] = _mm(sv["y"], dx, "tn", f"{tag}_b_dwout")
            dz, g["b_ln_g"], g["b_ln_b"], g["b_w_s"], dbsf = _b_core_bwd(
                sv["z"], dy, W["b_ln_g"][j], W["b_ln_b"][j], W["b_w_s"][j], sv["bsf"], f"{tag}_b_core_bwd")
            g["b_b_s"] = dbsf.reshape(SGU_CHUNK, H, LANES).sum(-1).T
            g["b_w_in"] = _mm(sv["hn"], dz, "tn", f"{tag}_b_dwin")
            dhn = _mm(dz, W["b_w_in"][j], "nt", f"{tag}_b_dhn")
        else:
            dy = _mm(dx, W["c_w_out"][j], "nt", f"{tag}_c_dy")
            g["c_w_out"] = _mm(sv["y"], dx, "tn", f"{tag}_c_dwout")
            do, dzg, g["c_norm_g"] = _c_post_bwd(sv["o"], seq(sv["z"]), W["c_norm_g"][j], seq(dy), f"{tag}_c_post_bwd")
            dq, dk, dv, dgbp = _c_mid_bwd(sv["q"], sv["k"], sv["v"], sv["gbp"], do, sv["mid"], f"{tag}_c_mid_bwd")
            dzq, dzk, dzv, dzs_h, dcw3, dpc = _c_pre_bwd(seq(sv["z"]), seq(sv["zs"]), sv["cw3"], sv["pc"], dq, dk, dv, dgbp,
                                                         f"{tag}_c_pre_bwd")
            dz = flat(jnp.concatenate([dzq, dzk, dzv, dzg], axis=-1))
            dzs = flat(dzs_h.sum(0)).astype(BF)
            g["c_conv_w"] = dcw3.transpose(1, 0, 2).reshape(CONV_W, 3 * D)
            dpc = dpc.sum(-1)
            g["c_a_log"], g["c_dt_bias"] = dpc[:, :2].T, dpc[:, 2:].T
            dw_main = _mm(sv["hn"], dz, "tn", f"{tag}_c_dwin")
            dw_small = _mm(sv["hn"], dzs, "tn", f"{tag}_c_dwin_small")
            g["c_w_in"] = jnp.concatenate([dw_main, dw_small[:, :4 * H]], axis=1)
            dhn = _mm(dz, W["c_w_in"][j][:, :4 * D], "nt", f"{tag}_c_dhn")
            dhn = _mm(dzs, sv["w_small"], "nt", f"{tag}_c_dhn_small", epi="add", extra=dhn)
        dx, per_layer["norm_mix_g"][i] = _rms_bwd(sv["x"], W["norm_mix_g"][i], dhn, dx, f"{tag}_mix_norm_bwd")
        for n, val in g.items():
            mixer.setdefault(n, {})[j] = val

    for n, vals in per_layer.items():
        if n not in ("mlp_w_up", "mlp_w_down"):
            G[n] = jnp.stack(vals)
        elif by_chip:
            G[n] = vals
        elif n == "mlp_w_up":
            G[n] = jnp.stack(vals).transpose(0, 2, 1, 3).reshape(L, D, F)
        else:
            G[n] = jnp.stack(vals).reshape(L, F, D)
    for n, by_j in mixer.items():
        G[n] = jnp.stack([by_j[j] for j in sorted(by_j)])
    return loss, dx.reshape(Bq, S, D), G


MESH = pl.DeviceIdType.MESH
N_CHIPS = 4
HBM_SPEC = pl.BlockSpec(memory_space=pltpu.HBM)


def _place():
    x, y, c = lax.axis_index("x"), lax.axis_index("y"), lax.axis_index("c")
    others = [(1 - x, y), (x, 1 - y), (1 - x, 1 - y)]
    return x, y, c, others


def _all_gather_xy(bufs, name):
    n = len(bufs)
    pieces = [_stage_rows(b.shape[0], b.shape[1] * b.dtype.itemsize) for b in bufs]

    def body(*refs):
        ins, outs = refs[:n], refs[n:2 * n]
        send, recv, fsend, frecv = refs[2 * n:2 * n + 4]
        stages = refs[2 * n + 4:]
        x, y, c, others = _place()
        p = 2 * x + y
        half = lambda b, cc: pl.ds(cc * (ins[b].shape[0] // 2), ins[b].shape[0] // 2)

        def ici(b, j):
            qx, qy = others[j]
            return pltpu.make_async_remote_copy(
                src_ref=ins[b].at[half(b, c)], dst_ref=outs[b].at[p, half(b, c)], send_sem=send.at[b, j],
                recv_sem=recv.at[b, j], device_id=(qx, qy, c), device_id_type=MESH)

        def landed(b, j, cc):
            qx, qy = others[j]
            return outs[b].at[2 * qx + qy, half(b, cc)]

        def d2d(b, j):
            return pltpu.make_async_remote_copy(
                src_ref=landed(b, j, c), dst_ref=landed(b, j, c), send_sem=fsend.at[b, j], recv_sem=frecv.at[b, j],
                device_id=(x, y, 1 - c), device_id_type=MESH)

        pairs = [(b, j) for b in range(n) for j in range(3)]
        for b, j in pairs:
            ici(b, j).start()
        for b in range(n):
            def own_piece(i, carry, b=b):
                rows = pl.ds(pl.multiple_of(i * pieces[b], pieces[b]), pieces[b])
                pltpu.sync_copy(ins[b].at[rows], stages[b])
                pltpu.sync_copy(stages[b], outs[b].at[p, rows])
                return carry

            lax.fori_loop(0, ins[b].shape[0] // pieces[b], own_piece, 0)
        for b, j in pairs:
            pltpu.make_async_remote_copy(
                src_ref=ins[b].at[half(b, c)], dst_ref=landed(b, j, c), send_sem=send.at[b, j], recv_sem=recv.at[b, j],
                device_id=(x, y, c), device_id_type=MESH).wait_recv()
            d2d(b, j).start()
        for b, j in pairs:
            pltpu.make_async_remote_copy(
                src_ref=landed(b, j, 1 - c), dst_ref=landed(b, j, 1 - c), send_sem=fsend.at[b, j], recv_sem=frecv.at[b, j],
                device_id=(x, y, 1 - c), device_id_type=MESH).wait_recv()
        for b, j in pairs:
            ici(b, j).wait_send()
            d2d(b, j).wait_send()

    return pl.pallas_call(
        body, name=name, in_specs=[HBM_SPEC] * n, out_specs=[HBM_SPEC] * n,
        out_shape=[jax.ShapeDtypeStruct((N_CHIPS,) + b.shape, b.dtype) for b in bufs],
        scratch_shapes=[pltpu.SemaphoreType.DMA((n, 3))] * 4
        + [pltpu.VMEM((r, b.shape[1]), b.dtype) for r, b in zip(pieces, bufs)],
        compiler_params=pltpu.CompilerParams(has_side_effects=True, vmem_limit_bytes=VMEM_LIMIT))(*bufs)


STAGE_BYTES = 2 * 1024 * 1024


def _stage_rows(rows, row_bytes):
    for d in range(min(rows, max(1, STAGE_BYTES // row_bytes)), 0, -1):
        if rows % d == 0 and (d % 16 == 0 or d == rows):
            return d
    return rows


def _swap_halves(g, name):
    nq, _, h, cols = g.shape

    def body(g_ref, o_ref, send, recv):
        x, y, c, _ = _place()
        cp = pltpu.make_async_remote_copy(src_ref=g_ref.at[:, 1 - c], dst_ref=o_ref, send_sem=send, recv_sem=recv,
                                          device_id=(x, y, 1 - c), device_id_type=MESH)
        cp.start()
        cp.wait()

    return pl.pallas_call(
        body, name=name, in_specs=[HBM_SPEC], out_specs=HBM_SPEC, out_shape=jax.ShapeDtypeStruct((nq, h, cols), g.dtype),
        scratch_shapes=[pltpu.SemaphoreType.DMA, pltpu.SemaphoreType.DMA],
        compiler_params=pltpu.CompilerParams(has_side_effects=True))(g)


def _pair_sum(g, got, out_dtype, name):
    nq, _, h, cols = g.shape
    tr = min(512, h)

    def body(c_ref, g_ref, r_ref, o_ref):
        o_ref[...] = (g_ref[...] + r_ref[...]).astype(o_ref.dtype)

    spec = pl.BlockSpec((None, tr, cols), lambda q, i, c_ref: (q, i, 0))
    return pl.pallas_call(
        body, name=name,
        grid_spec=pltpu.PrefetchScalarGridSpec(
            num_scalar_prefetch=1, grid=(nq, h // tr),
            in_specs=[pl.BlockSpec((None, None, tr, cols), lambda q, i, c_ref: (q, c_ref[0], i, 0)), spec], out_specs=spec),
        out_shape=jax.ShapeDtypeStruct((nq, h, cols), out_dtype),
        compiler_params=_params(("parallel", "parallel")))(lax.axis_index("c").astype(jnp.int32).reshape(1), g, got)


def _scatter_xy(p_sum, name):
    nq, h, cols = p_sum.shape

    def body(p_ref, o_ref, send, recv):
        x, y, c, others = _place()
        me = 2 * x + y
        cps = []
        for j, (qx, qy) in enumerate(others):
            cps.append(pltpu.make_async_remote_copy(
                src_ref=p_ref.at[2 * qx + qy], dst_ref=o_ref.at[me], send_sem=send.at[j], recv_sem=recv.at[j],
                device_id=(qx, qy, c), device_id_type=MESH))
            cps[-1].start()
        for j, (qx, qy) in enumerate(others):
            pltpu.make_async_remote_copy(
                src_ref=p_ref.at[me], dst_ref=o_ref.at[2 * qx + qy], send_sem=send.at[j], recv_sem=recv.at[j],
                device_id=(qx, qy, c), device_id_type=MESH).wait_recv()
        for cp in cps:
            cp.wait_send()

    return pl.pallas_call(
        body, name=name, in_specs=[HBM_SPEC], out_specs=HBM_SPEC, out_shape=jax.ShapeDtypeStruct((nq, h, cols), p_sum.dtype),
        scratch_shapes=[pltpu.SemaphoreType.DMA((3,)), pltpu.SemaphoreType.DMA((3,))],
        compiler_params=pltpu.CompilerParams(has_side_effects=True))(p_sum)


def _chip_sum(r4, p_sum, name):
    nq, h, cols = r4.shape
    tr = min(512, h)

    def body(r_ref, p_ref, o_ref):
        me = 2 * lax.axis_index("x") + lax.axis_index("y")
        f = lambda q: jnp.where(me == q, p_ref[q], r_ref[q]).astype(F32)
        o_ref[...] = ((f(0) + f(1)) + f(2)) + f(3)

    spec = pl.BlockSpec((nq, tr, cols), lambda i: (0, i, 0))
    return pl.pallas_call(
        body, name=name, grid=(h // tr,), in_specs=[spec, spec],
        out_specs=pl.BlockSpec((tr, cols), lambda i: (i, 0)), out_shape=jax.ShapeDtypeStruct((h, cols), F32),
        compiler_params=_params(("parallel",)))(r4, p_sum)


def _join_halves(r, name):
    h, cols = r.shape

    def body(r_ref, o_ref, send, recv):
        x, y, c, _ = _place()
        cp = pltpu.make_async_remote_copy(src_ref=r_ref, dst_ref=o_ref.at[c], send_sem=send, recv_sem=recv,
                                          device_id=(x, y, 1 - c), device_id_type=MESH)
        cp.start()
        pltpu.make_async_remote_copy(src_ref=r_ref, dst_ref=o_ref.at[1 - c], send_sem=send, recv_sem=recv,
                                     device_id=(x, y, 1 - c), device_id_type=MESH).wait_recv()
        cp.wait_send()

    got = pl.pallas_call(
        body, name=name, in_specs=[HBM_SPEC], out_specs=HBM_SPEC, out_shape=jax.ShapeDtypeStruct((2, h, cols), r.dtype),
        scratch_shapes=[pltpu.SemaphoreType.DMA, pltpu.SemaphoreType.DMA],
        compiler_params=pltpu.CompilerParams(has_side_effects=True))(r)
    c = lax.axis_index("c")
    return jnp.stack([jnp.where(c == s, r, got[s]) for s in range(2)])


def _reduce_scatter(g, tag):
    nq, rows, cols = g.shape
    g = g.reshape(nq, 2, rows // 2, cols)
    got = _swap_halves(g, f"{tag}_swap")
    pair = _pair_sum(g, got, BF, f"{tag}_pair_sum")
    r4 = _scatter_xy(pair, f"{tag}_scatter")
    r = _chip_sum(r4, pair, f"{tag}_chip_sum")
    return _join_halves(r, f"{tag}_join").reshape(rows, cols)


def _adamw(w, g, m, v, name):
    rows, cols = w.shape
    tr = rows
    for cand in (512, 344, 256, 128, 64, 32, 16, 8):
        if rows % cand == 0:
            tr = cand
            break

    def body(w_ref, g_ref, m_ref, v_ref, d_ref, nm_ref, nv_ref):
        g_ = g_ref[...]
        m_ = ADAM_B1 * m_ref[...] + (1.0 - ADAM_B1) * g_
        v_ = ADAM_B2 * v_ref[...] + (1.0 - ADAM_B2) * jnp.square(g_)
        m_hat = m_ / (1.0 - ADAM_B1 ** ADAM_STEP)
        v_hat = v_ / (1.0 - ADAM_B2 ** ADAM_STEP)
        d_ref[...] = -ADAM_LR * (m_hat / (jnp.sqrt(v_hat) + ADAM_EPS) + ADAM_WD * w_ref[...])
        nm_ref[...] = m_
        nv_ref[...] = v_

    spec = pl.BlockSpec((tr, cols), lambda i: (i, 0))
    shp = jax.ShapeDtypeStruct((rows, cols), F32)
    return pl.pallas_call(body, name=name, grid=(rows // tr,), in_specs=[spec] * 4, out_specs=[spec] * 3,
                          out_shape=[shp] * 3, compiler_params=_params(("parallel",)))(w, g, m, v)


WEIGHTS = ["norm_mix_g", "norm_mlp_g", "mlp_w_up", "mlp_w_down", "norm_final_g", "a_w_in", "a_conv_w", "a_conv_b",
           "a_gate_w", "a_gate_b", "a_lambda", "a_w_out", "b_w_in", "b_ln_g", "b_ln_b", "b_w_s", "b_b_s", "b_w_out",
           "c_w_in", "c_conv_w", "c_a_log", "c_dt_bias", "c_norm_g", "c_w_out"]
SHARD_AXIS = {"mlp_w_up": 2, "mlp_w_down": 1, "a_w_in": 2, "a_conv_w": 2, "a_conv_b": 1, "a_lambda": 2, "a_w_out": 1,
              "b_w_in": 2, "b_w_out": 1, "c_w_in": 2, "c_conv_w": 2, "c_w_out": 1}
MATMUL_WEIGHTS = ["mlp_w_up", "mlp_w_down", "a_w_in", "a_w_out", "b_w_in", "b_w_out", "c_w_out", "c_w_in"]
SMALL_SHARDED = ["a_conv_w", "a_conv_b", "a_lambda", "c_conv_w"]
REPLICATED = [n for n in WEIGHTS if n not in SHARD_AXIS]
FLAT_COLS = 1024


def _rows_of(shape):
    n = 1
    for s in shape:
        n *= s
    return -(-n // FLAT_COLS)


def _pack(arrays, total_rows, dtype):
    parts = []
    used = 0
    for a in arrays:
        r = _rows_of(a.shape)
        f = a.reshape(-1).astype(dtype)
        parts.append(jnp.pad(f, (0, r * FLAT_COLS - f.shape[0])).reshape(r, FLAT_COLS))
        used += r
    if total_rows > used:
        parts.append(jnp.zeros((total_rows - used, FLAT_COLS), dtype))
    return jnp.concatenate(parts, axis=0)


def _slots_of(g, axis):
    w = g.shape[axis] // N_CHIPS
    flat = jnp.stack([lax.slice_in_dim(g, q * w, (q + 1) * w, axis=axis).reshape(-1) for q in range(N_CHIPS)])
    rows = _rows_of(flat.shape[1:])
    return jnp.pad(flat, ((0, 0), (0, rows * FLAT_COLS - flat.shape[1]))).reshape(N_CHIPS, rows, FLAT_COLS)


def _unpack(buf, shapes):
    out, r0 = [], 0
    for shp in shapes:
        r = _rows_of(shp)
        n = 1
        for s in shp:
            n *= s
        out.append(buf[r0:r0 + r].reshape(-1)[:n].reshape(shp))
        r0 += r
    return out


def _round_up(n, m):
    return -(-n // m) * m


def kernel(x, norm_mix_g, norm_mlp_g, mlp_w_up, mlp_w_down, norm_final_g, a_w_in, a_conv_w, a_conv_b, a_gate_w, a_gate_b, a_lambda, a_w_out, b_w_in, b_ln_g, b_ln_b, b_w_s, b_b_s, b_w_out, c_w_in, c_conv_w, c_a_log, c_dt_bias, c_norm_g, c_w_out, loss_target, m_norm_mix_g, m_norm_mlp_g, m_mlp_w_up, m_mlp_w_down, m_norm_final_g, m_a_w_in, m_a_conv_w, m_a_conv_b, m_a_gate_w, m_a_gate_b, m_a_lambda, m_a_w_out, m_b_w_in, m_b_ln_g, m_b_ln_b, m_b_w_s, m_b_b_s, m_b_w_out, m_c_w_in, m_c_conv_w, m_c_a_log, m_c_dt_bias, m_c_norm_g, m_c_w_out, v_norm_mix_g, v_norm_mlp_g, v_mlp_w_up, v_mlp_w_down, v_norm_final_g, v_a_w_in, v_a_conv_w, v_a_conv_b, v_a_gate_w, v_a_gate_b, v_a_lambda, v_a_w_out, v_b_w_in, v_b_ln_g, v_b_ln_b, v_b_w_s, v_b_b_s, v_b_w_out, v_c_w_in, v_c_conv_w, v_c_a_log, v_c_dt_bias, v_c_norm_g, v_c_w_out):
    given = dict(locals())
    w_loc = {n: given[n] for n in WEIGHTS}
    m_loc = {n: given["m_" + n] for n in WEIGHTS}
    v_loc = {n: given["v_" + n] for n in WEIGHTS}

    mlp = ["mlp_w_up", "mlp_w_down"]
    rest = [n for n in MATMUL_WEIGHTS if n not in mlp]
    rest_rows = _round_up(sum(_rows_of(w_loc[n].shape) for n in rest), 512)
    small_rows = _round_up(sum(_rows_of(w_loc[n].shape) for n in SMALL_SHARDED), 16)
    as_rows = lambda w: w.astype(BF).reshape(-1, w.shape[-1])
    up_all, down_all, rest_all, small_all = _all_gather_xy(
        [as_rows(mlp_w_up), as_rows(mlp_w_down), _pack([w_loc[n] for n in rest], rest_rows, BF),
         _pack([w_loc[n] for n in SMALL_SHARDED], small_rows, F32)], "gather_weights")
    W = {n: w_loc[n] for n in REPLICATED}
    W["mlp_up_slots"] = up_all.reshape((N_CHIPS,) + mlp_w_up.shape)
    W["mlp_down_slots"] = down_all.reshape((N_CHIPS,) + mlp_w_down.shape)
    for names, buf in ((rest, rest_all), (SMALL_SHARDED, small_all)):
        per_chip = [_unpack(buf[q], [w_loc[n].shape for n in names]) for q in range(N_CHIPS)]
        for i, n in enumerate(names):
            W[n] = jnp.concatenate([per_chip[q][i] for q in range(N_CHIPS)], axis=SHARD_AXIS[n])

    loss, grad_x, G = _local_step(x, loss_target, W, by_chip=True)
    loss = lax.psum(loss, ("x", "y", "c"))

    sharded = MATMUL_WEIGHTS + SMALL_SHARDED
    rep_rows = _round_up(sum(_rows_of(w_loc[n].shape) for n in REPLICATED), N_CHIPS * 16)
    rep_flat = _pack([G[n] for n in REPLICATED], rep_rows, F32).reshape(N_CHIPS, rep_rows // N_CHIPS, FLAT_COLS)
    shard_rows = sum(_rows_of(w_loc[n].shape) for n in sharded)
    total_rows = _round_up(shard_rows + rep_rows // N_CHIPS, 1024)
    parts = []
    for n in sharded:
        if n in mlp:
            parts += [g.reshape(N_CHIPS, -1, FLAT_COLS) for g in G[n]]
        else:
            parts.append(_slots_of(G[n], SHARD_AXIS[n]))
    parts.append(rep_flat)
    parts.append(jnp.zeros((N_CHIPS, total_rows - shard_rows - rep_rows // N_CHIPS, FLAT_COLS), F32))
    red = _reduce_scatter(jnp.concatenate(parts, axis=1), "grads")
    g_loc = dict(zip(sharded, _unpack(red, [w_loc[n].shape for n in sharded])))
    rep_quarter = red[shard_rows:shard_rows + rep_rows // N_CHIPS]
    (rep_all,) = _all_gather_xy([rep_quarter], "gather_replicated_grads")
    g_loc.update(zip(REPLICATED, _unpack(rep_all.reshape(rep_rows, FLAT_COLS), [w_loc[n].shape for n in REPLICATED])))

    delta, new_m, new_v = {}, {}, {}
    big = [n for n in MATMUL_WEIGHTS if w_loc[n].size % FLAT_COLS == 0]
    for n in big:
        shp = w_loc[n].shape
        two_d = lambda a: a.reshape(-1, FLAT_COLS)
        d, nm, nv = _adamw(two_d(w_loc[n]), two_d(g_loc[n]), two_d(m_loc[n]), two_d(v_loc[n]), f"adamw_{n}")
        delta[n], new_m[n], new_v[n] = d.reshape(shp), nm.reshape(shp), nv.reshape(shp)
    small = [n for n in WEIGHTS if n not in big]
    small_shapes = [w_loc[n].shape for n in small]
    rows = _round_up(sum(_rows_of(s) for s in small_shapes), 8)
    packed = [_pack([src[n] for n in small], rows, F32) for src in (w_loc, g_loc, m_loc, v_loc)]
    for dst, buf in zip((delta, new_m, new_v), _adamw(*packed, "adamw_small")):
        dst.update(zip(small, _unpack(buf, small_shapes)))

    return (loss, grad_x, *[g_loc[n] for n in WEIGHTS], *[delta[n] for n in WEIGHTS],
            *[new_m[n] for n in WEIGHTS], *[new_v[n] for n in WEIGHTS])
```

```python
import functools

import jax
import jax.numpy as jnp
from jax import lax
from jax.experimental import pallas as pl
from jax.experimental.pallas import tpu as pltpu

F32 = jnp.float32
BF = jnp.bfloat16

LANES = 128
VMEM_LIMIT = 56 * 1024 * 1024
RG_C = 8.0
SGU_CHUNK = 128
GDN_CHUNK = 64
CONV_W = 4
N_MIXERS = 3

ADAM_LR = 0.001
ADAM_B1 = 0.9
ADAM_B2 = 0.999
ADAM_EPS = 1e-08
ADAM_WD = 0.01
ADAM_STEP = 10


def _params(sem=None):
    return pltpu.CompilerParams(dimension_semantics=sem, vmem_limit_bytes=VMEM_LIMIT)


def _shift_impl(x, s):
    if s == 0:
        return x
    n = x.shape[0]
    row = lax.broadcasted_iota(jnp.int32, x.shape, 0)
    if s > 0:
        return jnp.where(row >= s, pltpu.roll(x, s, 0), 0.0)
    return jnp.where(row < n + s, pltpu.roll(x, n + s, 0), 0.0)


@functools.partial(jax.custom_vjp, nondiff_argnums=(1,))
def _shift(x, s):
    return _shift_impl(x, s)


def _shift_fwd(x, s):
    return _shift_impl(x, s), None


def _shift_bwd(s, _, g):
    return (_shift_impl(g, -s),)


_shift.defvjp(_shift_fwd, _shift_bwd)


def _chunk_cumsum_impl(x, rev, chunk):
    n = x.shape[0]
    rc = lax.broadcasted_iota(jnp.int32, x.shape, 0) & (chunk - 1)
    sh = 1
    while sh < chunk:
        if rev:
            x = x + jnp.where(rc < chunk - sh, pltpu.roll(x, n - sh, 0), 0.0)
        else:
            x = x + jnp.where(rc >= sh, pltpu.roll(x, sh, 0), 0.0)
        sh *= 2
    return x


@functools.partial(jax.custom_vjp, nondiff_argnums=(1, 2))
def _chunk_cumsum(x, rev, chunk):
    return _chunk_cumsum_impl(x, rev, chunk)


def _chunk_cumsum_fwd(x, rev, chunk):
    return _chunk_cumsum_impl(x, rev, chunk), None


def _chunk_cumsum_bwd(rev, chunk, _, g):
    return (_chunk_cumsum_impl(g, not rev, chunk),)


_chunk_cumsum.defvjp(_chunk_cumsum_fwd, _chunk_cumsum_bwd)


def _rms(x, g, eps=1e-6):
    return x * lax.rsqrt(jnp.mean(x * x, axis=-1, keepdims=True) + eps) * g


def _sigmoid(x):
    return 0.5 * jnp.tanh(0.5 * x) + 0.5


def _silu(x):
    return x * _sigmoid(x)


def _softplus(x):
    return jnp.maximum(x, 0.0) + jnp.log1p(jnp.exp(-jnp.abs(x)))


def _neg_expm1(y, ey):
    series = -y * (1.0 + y * (1 / 2) * (1.0 + y * (1 / 3) * (1.0 + y * (1 / 4))))
    return jnp.where(y > -1 / 32, series, 1.0 - ey)


@jax.custom_vjp
def _sqrt_one_minus_sq(log_a, a):
    t = _neg_expm1(2.0 * log_a, a * a)
    return t * lax.rsqrt(jnp.maximum(t, 1e-30))


def _sqrt_one_minus_sq_fwd(log_a, a):
    t = _neg_expm1(2.0 * log_a, a * a)
    rs = lax.rsqrt(jnp.maximum(t, 1e-30))
    return t * rs, (a, rs)


def _sqrt_one_minus_sq_bwd(res, g):
    a, rs = res
    return -g * (a * a) * rs, jnp.zeros_like(a)


_sqrt_one_minus_sq.defvjp(_sqrt_one_minus_sq_fwd, _sqrt_one_minus_sq_bwd)


def _bmm_raw(a, b, form):
    r = a.ndim - 2
    con = {"nn": ((r + 1,), (r,)), "nt": ((r + 1,), (r + 1,)), "tn": ((r,), (r,))}[form]
    batch = ((0,), (0,)) if r else ((), ())
    return lax.dot_general(a.astype(BF), b.astype(BF), (con, batch), preferred_element_type=F32)


@functools.partial(jax.custom_vjp, nondiff_argnums=(2,))
def _bmm(a, b, form):
    return _bmm_raw(a, b, form)


def _bmm_fwd(a, b, form):
    return _bmm_raw(a, b, form), (a, b)


def _bmm_bwd(form, res, g):
    a, b = res
    if form == "nn":
        da, db = _bmm_raw(g, b, "nt"), _bmm_raw(a, g, "tn")
    elif form == "nt":
        da, db = _bmm_raw(g, b, "nn"), _bmm_raw(g, a, "tn")
    else:
        da, db = _bmm_raw(b, g, "nt"), _bmm_raw(a, g, "nn")
    return da.astype(a.dtype), db.astype(b.dtype)


_bmm.defvjp(_bmm_fwd, _bmm_bwd)


def _conv4(z, rows):
    out = rows[0] * _shift(z, 2)
    for k in range(1, CONV_W):
        out = out + rows[k] * _shift(z, 2 - k)
    return out


_DIMS = {"nn": ((1,), (0,)), "nt": ((1,), (1,)), "tn": ((0,), (0,))}


def _mm(a, b, mode, name, *, out_dtype=F32, epi=None, extra=None, tm=1024, tn=1024, tk=1024, b_index=None, n_cols=None,
        by_chip=None):
    if mode == "tn":
        K, M = a.shape
    else:
        M, K = a.shape
    N = n_cols if b_index is not None else (b.shape[0] if mode == "nt" else b.shape[1])
    tm, tn, tk = min(tm, M), min(tn, N), min(tk, K)
    assert M % tm == 0 and N % tn == 0 and K % tk == 0, (name, M, N, K)
    nk = K // tk
    a_spec = pl.BlockSpec((tk, tm), lambda i, j, k: (k, i)) if mode == "tn" else pl.BlockSpec((tm, tk), lambda i, j, k: (i, k))
    b_block = (tn, tk) if mode == "nt" else (tk, tn)
    if b_index is not None:
        b_spec = pl.BlockSpec((None,) * (b.ndim - 2) + b_block, lambda i, j, k: b_index(j, k))
    elif mode == "nt":
        b_spec = pl.BlockSpec(b_block, lambda i, j, k: (j, k))
    else:
        b_spec = pl.BlockSpec(b_block, lambda i, j, k: (k, j))
    o_spec = pl.BlockSpec((tm, tn), lambda i, j, k: (i, j))
    ins, specs = [a, b], [a_spec, b_spec]
    if epi in ("add", "relu2_bwd"):
        ins.append(extra)
        specs.append(o_spec)
    o_shape = (M, N)
    if by_chip == "cols":
        assert tn == N // N_CHIPS and epi is None
        o_shape, o_spec = (N_CHIPS, M, tn), pl.BlockSpec((None, tm, tn), lambda i, j, k: (j, i, 0))

    def body(*refs):
        a_ref, b_ref = refs[0], refs[1]
        e_ref = refs[2] if len(ins) == 3 else None
        o_ref = refs[len(ins)]

        def product():
            return lax.dot_general(a_ref[...].astype(BF), b_ref[...].astype(BF), (_DIMS[mode], ((), ())),
                                   preferred_element_type=F32)

        def finish(r):
            if epi == "relu2":
                r = jnp.square(jnp.maximum(r, 0.0))
            elif epi == "add":
                r = r + e_ref[...]
            elif epi == "relu2_bwd":
                r = r * (2.0 * jnp.sqrt(e_ref[...].astype(F32)))
            o_ref[...] = r.astype(out_dtype)

        if nk == 1:
            finish(product())
            return
        acc = refs[-1]
        k = pl.program_id(2)

        @pl.when(k == 0)
        def _():
            acc[...] = product()

        @pl.when(k > 0)
        def _():
            acc[...] += product()

        @pl.when(k == nk - 1)
        def _():
            finish(acc[...])

    return pl.pallas_call(
        body, name=name, grid=(M // tm, N // tn, nk), in_specs=specs, out_specs=o_spec,
        out_shape=jax.ShapeDtypeStruct(o_shape, out_dtype),
        scratch_shapes=[pltpu.VMEM((tm, tn), F32)] if nk > 1 else [],
        compiler_params=_params(("parallel", "parallel", "arbitrary")))(*ins)


def _rows_tile(T):
    return min(512, T)


def _rms_fwd(x, g, name):
    T, D = x.shape
    tr = _rows_tile(T)

    def body(x_ref, g_ref, o_ref):
        o_ref[...] = _rms(x_ref[...], g_ref[...]).astype(BF)

    return pl.pallas_call(
        body, name=name, grid=(T // tr,),
        in_specs=[pl.BlockSpec((tr, D), lambda i: (i, 0)), pl.BlockSpec((1, D), lambda i: (0, 0))],
        out_specs=pl.BlockSpec((tr, D), lambda i: (i, 0)), out_shape=jax.ShapeDtypeStruct((T, D), BF),
        compiler_params=_params(("parallel",)))(x, g.reshape(1, D))


def _rms_bwd(x, g, dhn, dres, name):
    T, D = x.shape
    tr = _rows_tile(T)

    def body(x_ref, g_ref, dhn_ref, dres_ref, dx_ref, dg_ref):
        _, vjp = jax.vjp(_rms, x_ref[...], g_ref[...])
        dx, dg = vjp(dhn_ref[...])
        dx_ref[...] = dres_ref[...] + dx

        @pl.when(pl.program_id(0) == 0)
        def _():
            dg_ref[...] = jnp.zeros_like(dg_ref)

        dg_ref[...] += dg

    row = pl.BlockSpec((tr, D), lambda i: (i, 0))
    vec = pl.BlockSpec((1, D), lambda i: (0, 0))
    dx, dg = pl.pallas_call(
        body, name=name, grid=(T // tr,), in_specs=[row, vec, row, row], out_specs=[row, vec],
        out_shape=[jax.ShapeDtypeStruct((T, D), F32), jax.ShapeDtypeStruct((1, D), F32)],
        compiler_params=_params(("arbitrary",)))(x, g.reshape(1, D), dhn, dres)
    return dx, dg.reshape(D)


def _final_loss(x, g, tgt, name):
    T, D = x.shape
    tr = _rows_tile(T)

    def body(x_ref, g_ref, t_ref, l_ref, dx_ref, dg_ref):
        y, vjp = jax.vjp(_rms, x_ref[...], g_ref[...])
        err = y - t_ref[...]
        dx, dg = vjp(err * (1.0 / D))
        dx_ref[...] = dx

        @pl.when(pl.program_id(0) == 0)
        def _():
            dg_ref[...] = jnp.zeros_like(dg_ref)
            l_ref[...] = jnp.zeros_like(l_ref)

        dg_ref[...] += dg
        l_ref[...] += (0.5 / D) * jnp.sum(jnp.sum(err * err, axis=1, keepdims=True), axis=0, keepdims=True)

    row = pl.BlockSpec((tr, D), lambda i: (i, 0))
    vec = pl.BlockSpec((1, D), lambda i: (0, 0))
    loss, dx, dg = pl.pallas_call(
        body, name=name, grid=(T // tr,), in_specs=[row, vec, row],
        out_specs=[pl.BlockSpec((1, LANES), lambda i: (0, 0)), row, vec],
        out_shape=[jax.ShapeDtypeStruct((1, LANES), F32), jax.ShapeDtypeStruct((T, D), F32),
                   jax.ShapeDtypeStruct((1, D), F32)],
        compiler_params=_params(("arbitrary",)))(x, g.reshape(1, D), tgt)
    return loss[0, 0], dx, dg.reshape(D)


def _a_pre(zx, cws, cb, gws, gbs, lams):
    xr = _conv4(zx, cws) + cb
    out = []
    for d in range(2):
        r = _sigmoid(_bmm(xr, gws[2 * d], "nn") + gbs[2 * d])
        ig = _sigmoid(_bmm(xr, gws[2 * d + 1], "nn") + gbs[2 * d + 1])
        log_a = -RG_C * r * _softplus(-lams[d])
        a = jnp.exp(log_a)
        out += [a, _sqrt_one_minus_sq(log_a, a) * ig * xr]
    return tuple(out)


def _a_post(h0, h1, zg):
    return (h0 + h1) * jax.nn.gelu(zg)


SUBLANES = 8
SCAN_TILES = 8


def _scan_jobs(jobs):
    S, C = jobs[0][0].shape
    U = min(SCAN_TILES, S // SUBLANES)
    rows = U * SUBLANES
    row = lax.broadcasted_iota(jnp.int32, (SUBLANES, C), 0)

    def prefix(a, b, reverse):
        for sh in (1, 2, 4):
            if reverse:
                m, r = row < SUBLANES - sh, SUBLANES - sh
            else:
                m, r = row >= sh, sh
            a_s = jnp.where(m, pltpu.roll(a, r, 0), 1.0)
            b_s = jnp.where(m, pltpu.roll(b, r, 0), 0.0)
            b = a * b_s + b
            a = a * a_s
        return a, b

    def step(i, carries):
        out = []
        for (a_ref, b_ref, h_ref, reverse), c in zip(jobs, carries):
            blk = (S // rows - 1 - i) if reverse else i
            t0 = pl.multiple_of(blk * rows, rows)
            order = range(U - 1, -1, -1) if reverse else range(U)
            edge = slice(0, 1) if reverse else slice(SUBLANES - 1, SUBLANES)
            for j in order:
                sl = pl.ds(t0 + j * SUBLANES, SUBLANES)
                a, b = prefix(a_ref[sl, :], b_ref[sl, :], reverse)
                h_ref[sl, :] = a * jnp.broadcast_to(c, (SUBLANES, C)) + b
                c = a[edge, :] * c + b[edge, :]
            out.append(c)
        return tuple(out)

    lax.fori_loop(0, S // rows, step, tuple(jnp.zeros((1, C), F32) for _ in jobs))


def _a_load_params(cw_ref, cb_ref, gw_ref, gb_ref, lam_ref):
    cws = [cw_ref[k:k + 1, :] for k in range(CONV_W)]
    gws = [gw_ref[d, g] for d in range(2) for g in range(2)]
    gbs = [gb_ref[d, g] for d in range(2) for g in range(2)]
    lams = [lam_ref[d:d + 1, :] for d in range(2)]
    return cws, cb_ref[...], gws, gbs, lams


def _a_in_specs(S, H):
    zg = pl.BlockSpec((None, S, LANES), lambda h, b: (b, 0, h))
    zx = pl.BlockSpec((None, S, LANES), lambda h, b: (b, 0, H + h))
    cw = pl.BlockSpec((CONV_W, LANES), lambda h, b: (0, h))
    cb = pl.BlockSpec((1, LANES), lambda h, b: (0, h))
    gw = pl.BlockSpec((2, 2, None, LANES, LANES), lambda h, b: (0, 0, h, 0, 0))
    gb = pl.BlockSpec((2, 2, None, 1, LANES), lambda h, b: (0, 0, h, 0, 0))
    lam = pl.BlockSpec((2, LANES), lambda h, b: (0, h))
    return zg, zx, cw, cb, gw, gb, lam


def _a_core_fwd(z, cw, cb, gw, gb, lam, name):
    Bq, S, D2 = z.shape
    D = D2 // 2
    H = D // LANES

    def body(zg_ref, zx_ref, cw_ref, cb_ref, gw_ref, gb_ref, lam_ref, y_ref, a_s, b_s, h_s):
        ab = _a_pre(zx_ref[...], *_a_load_params(cw_ref, cb_ref, gw_ref, gb_ref, lam_ref))
        for d in range(2):
            a_s[d] = ab[2 * d]
            b_s[d] = ab[2 * d + 1]
        _scan_jobs([(a_s.at[d], b_s.at[d], h_s.at[d], d == 1) for d in range(2)])
        y_ref[...] = _a_post(h_s[0], h_s[1], zg_ref[...]).astype(BF)

    seq = pltpu.VMEM((2, S, LANES), F32)
    return pl.pallas_call(
        body, name=name, grid=(H, Bq), in_specs=list(_a_in_specs(S, H)),
        out_specs=pl.BlockSpec((None, S, LANES), lambda h, b: (b, 0, h)),
        out_shape=jax.ShapeDtypeStruct((Bq, S, D), BF), scratch_shapes=[seq, seq, seq],
        compiler_params=_params(("parallel", "arbitrary")))(z, z, cw, cb.reshape(1, D), gw, gb.reshape(2, 2, H, 1, LANES), lam)


def _a_core_bwd(z, dy, cw, cb, gw, gb, lam, name):
    Bq, S, D2 = z.shape
    D = D2 // 2
    H = D // LANES

    def body(zg_ref, zx_ref, dy_ref, cw_ref, cb_ref, gw_ref, gb_ref, lam_ref,
             dzg_ref, dzx_ref, dcw_ref, dcb_ref, dgw_ref, dgb_ref, dlam_ref, a_s, b_s, h_s, l_s):
        prm = _a_load_params(cw_ref, cb_ref, gw_ref, gb_ref, lam_ref)
        ab = _a_pre(zx_ref[...], *prm)
        for d in range(2):
            a_s[d] = ab[2 * d]
            b_s[d] = ab[2 * d + 1]
        _scan_jobs([(a_s.at[d], b_s.at[d], h_s.at[d], d == 1) for d in range(2)])
        _, post_vjp = jax.vjp(_a_post, h_s[0], h_s[1], zg_ref[...])
        dh0, dh1, dzg = post_vjp(dy_ref[...])
        dzg_ref[...] = dzg.astype(BF)
        for d, dh in ((0, dh0), (1, dh1)):
            b_s[d] = dh
            a_s[d] = _shift(a_s[d], -1 if d == 0 else 1)
        _scan_jobs([(a_s.at[d], b_s.at[d], l_s.at[d], d == 0) for d in range(2)])
        cot = []
        for d in range(2):
            cot += [l_s[d] * _shift(h_s[d], 1 if d == 0 else -1), l_s[d]]
        _, pre_vjp = jax.vjp(_a_pre, zx_ref[...], *prm)
        dzx, dcws, dcb, dgws, dgbs, dlams = pre_vjp(tuple(cot))
        dzx_ref[...] = dzx.astype(BF)

        @pl.when(pl.program_id(1) == 0)
        def _():
            for r in (dcw_ref, dcb_ref, dgw_ref, dgb_ref, dlam_ref):
                r[...] = jnp.zeros_like(r)

        for k in range(CONV_W):
            dcw_ref[k:k + 1, :] += dcws[k]
        dcb_ref[...] += dcb
        for d in range(2):
            dlam_ref[d:d + 1, :] += dlams[d]
            for g in range(2):
                dgw_ref[d, g] += dgws[2 * d + g]
                dgb_ref[d, g] += dgbs[2 * d + g]

    zg, zx, cws, cbs, gws, gbs, lams = _a_in_specs(S, H)
    dyspec = pl.BlockSpec((None, S, LANES), lambda h, b: (b, 0, h))
    seq = pltpu.VMEM((2, S, LANES), F32)
    dzg, dzx, dcw, dcb, dgw, dgb, dlam = pl.pallas_call(
        body, name=name, grid=(H, Bq), in_specs=[zg, zx, dyspec, cws, cbs, gws, gbs, lams],
        out_specs=[dyspec, dyspec, cws, cbs, gws, gbs, lams],
        out_shape=[jax.ShapeDtypeStruct((Bq, S, D), BF), jax.ShapeDtypeStruct((Bq, S, D), BF),
                   jax.ShapeDtypeStruct((CONV_W, D), F32), jax.ShapeDtypeStruct((1, D), F32),
                   jax.ShapeDtypeStruct((2, 2, H, LANES, LANES), F32), jax.ShapeDtypeStruct((2, 2, H, 1, LANES), F32),
                   jax.ShapeDtypeStruct((2, D), F32)],
        scratch_shapes=[seq, seq, seq, seq],
        compiler_params=_params(("parallel", "arbitrary")))(z, z, dy, cw, cb.reshape(1, D), gw, gb.reshape(2, 2, H, 1, LANES), lam)
    dz = jnp.concatenate([dzg, dzx], axis=-1)
    return dz, dcw, dcb.reshape(D), dgw, dgb.reshape(2, 2, H, LANES), dlam


def _b_fn(z, lng, lnb, wss, bsf):
    D = z.shape[1] // 2
    zz = jax.nn.gelu(z)
    u, v = zz[:, :D], zz[:, D:]
    mu = jnp.mean(v, axis=-1, keepdims=True)
    var = jnp.mean(jnp.square(v - mu), axis=-1, keepdims=True)
    vn = (v - mu) * lax.rsqrt(var + 1e-5) * lng + lnb
    vs = jnp.concatenate([_bmm(wss[g], vn[:, g * LANES:(g + 1) * LANES], "nn") for g in range(D // LANES)], axis=1)
    return u * (vs + bsf)


def _b_specs(D, GB):
    row = lambda w: pl.BlockSpec((SGU_CHUNK, w), lambda i: (i, 0))
    vec = pl.BlockSpec((1, D), lambda i: (0, 0))
    ws = pl.BlockSpec((GB, SGU_CHUNK, SGU_CHUNK), lambda i: (0, 0, 0))
    bsf = pl.BlockSpec((SGU_CHUNK, D), lambda i: (0, 0))
    return row, vec, ws, bsf


def _b_core_fwd(z, lng, lnb, ws, bsf, name):
    T, D2 = z.shape
    D = D2 // 2
    GB = D // LANES
    row, vec, wspec, bspec = _b_specs(D, GB)

    def body(z_ref, lng_ref, lnb_ref, ws_ref, bsf_ref, y_ref):
        wss = [ws_ref[g] for g in range(GB)]
        y_ref[...] = _b_fn(z_ref[...], lng_ref[...], lnb_ref[...], wss, bsf_ref[...]).astype(BF)

    return pl.pallas_call(
        body, name=name, grid=(T // SGU_CHUNK,), in_specs=[row(D2), vec, vec, wspec, bspec], out_specs=row(D),
        out_shape=jax.ShapeDtypeStruct((T, D), BF), compiler_params=_params(("parallel",)))(
            z, lng.reshape(1, D), lnb.reshape(1, D), ws, bsf)


def _b_core_bwd(z, dy, lng, lnb, ws, bsf, name):
    T, D2 = z.shape
    D = D2 // 2
    GB = D // LANES
    row, vec, wspec, bspec = _b_specs(D, GB)

    def body(z_ref, dy_ref, lng_ref, lnb_ref, ws_ref, bsf_ref, dz_ref, dlng_ref, dlnb_ref, dws_ref, dbsf_ref):
        wss = [ws_ref[g] for g in range(GB)]
        _, vjp = jax.vjp(_b_fn, z_ref[...], lng_ref[...], lnb_ref[...], wss, bsf_ref[...])
        dz, dlng, dlnb, dwss, dbsf = vjp(dy_ref[...])
        dz_ref[...] = dz.astype(BF)

        @pl.when(pl.program_id(0) == 0)
        def _():
            for r in (dlng_ref, dlnb_ref, dws_ref, dbsf_ref):
                r[...] = jnp.zeros_like(r)

        dlng_ref[...] += dlng
        dlnb_ref[...] += dlnb
        dbsf_ref[...] += dbsf
        for g in range(GB):
            dws_ref[g] += dwss[g]

    dz, dlng, dlnb, dws, dbsf = pl.pallas_call(
        body, name=name, grid=(T // SGU_CHUNK,), in_specs=[row(D2), row(D), vec, vec, wspec, bspec],
        out_specs=[row(D2), vec, vec, wspec, bspec],
        out_shape=[jax.ShapeDtypeStruct((T, D2), BF), jax.ShapeDtypeStruct((1, D), F32), jax.ShapeDtypeStruct((1, D), F32),
                   jax.ShapeDtypeStruct((GB, SGU_CHUNK, SGU_CHUNK), F32), jax.ShapeDtypeStruct((SGU_CHUNK, D), F32)],
        compiler_params=_params(("arbitrary",)))(z, dy, lng.reshape(1, D), lnb.reshape(1, D), ws, bsf)
    return dz, dlng.reshape(D), dlnb.reshape(D), dws, dbsf


def _lane_is(j):
    return lax.broadcasted_iota(jnp.int32, (1, LANES), 1) == j


def _lane_col(x, j):
    return jnp.sum(jnp.where(_lane_is(j), x, 0.0), axis=1, keepdims=True)


def _c_pre(zq, zk, zv, zs, cwq, cwk, cwv, pcs, head, HC):
    q = _silu(_conv4(zq, cwq))
    k = _silu(_conv4(zk, cwk))
    v = _silu(_conv4(zv, cwv))
    q = q * lax.rsqrt(jnp.sum(q * q, axis=-1, keepdims=True) + 1e-6) * (LANES ** -0.5)
    k = k * lax.rsqrt(jnp.sum(k * k, axis=-1, keepdims=True) + 1e-6)
    gbp = jnp.zeros_like(zs)
    for d in range(2):
        a_logit = _lane_col(zs, d * HC + head)
        b_logit = _lane_col(zs, 2 * HC + d * HC + head)
        g = -jnp.exp(pcs[d]) * _softplus(a_logit + pcs[2 + d])
        beta = jnp.broadcast_to(_sigmoid(b_logit), g.shape)
        gbp = gbp + jnp.where(_lane_is(2 * d), g, 0.0) + jnp.where(_lane_is(2 * d + 1), beta, 0.0)
    return q, k, v, gbp


def _mm3(x, y):
    xh, yh = x.astype(BF), y.astype(BF)
    xl, yl = (x - xh.astype(F32)).astype(BF), (y - yh.astype(F32)).astype(BF)
    return _bmm_raw(xh, yh, "nn") + _bmm_raw(xh, yl, "nn") + _bmm_raw(xl, yh, "nn")


def _tri_inv_impl(a):
    C = a.shape[-1]
    eye = (lax.broadcasted_iota(jnp.int32, (1, C, C), 1) == lax.broadcasted_iota(jnp.int32, (1, C, C), 2)).astype(F32)
    r = eye - a
    p = a
    n = 2
    while n < C:
        p = _mm3(p, p)
        r = r + _mm3(r, p)
        n *= 2
    return r


@jax.custom_vjp
def _tri_inv(a):
    return _tri_inv_impl(a)


def _tri_inv_fwd(a):
    t = _tri_inv_impl(a)
    return t, t


def _tri_inv_bwd(t, g):
    tt = jnp.swapaxes(t, 1, 2)
    return (-_bmm_raw(_bmm_raw(tt, g, "nn"), tt, "nn"),)


_tri_inv.defvjp(_tri_inv_fwd, _tri_inv_bwd)


@jax.custom_vjp
def _pair_diff(gc3):
    m = gc3[:, :, :gc3.shape[1]]
    return m - jnp.swapaxes(m, 1, 2)


def _pair_diff_fwd(gc3):
    return _pair_diff(gc3), None


def _pair_diff_bwd(_, g):
    d = jnp.sum(g, axis=2, keepdims=True) - jnp.sum(jnp.swapaxes(g, 1, 2), axis=2, keepdims=True)
    return (jnp.broadcast_to(d * (1.0 / LANES), d.shape[:2] + (LANES,)),)


_pair_diff.defvjp(_pair_diff_fwd, _pair_diff_bwd)


@jax.custom_vjp
def _tri_inv_saved(a, t):
    return t


def _tri_inv_saved_fwd(a, t):
    return t, t


def _tri_inv_saved_bwd(t, g):
    return _tri_inv_bwd(t, g)[0], jnp.zeros_like(t)


_tri_inv_saved.defvjp(_tri_inv_saved_fwd, _tri_inv_saved_bwd)


def _c_phase1(q, k, v, gbp, rev, t_saved=None):
    S = q.shape[0]
    C = GDN_CHUNK
    N = S // C
    col = 2 if rev else 0
    gB = jnp.broadcast_to(_lane_col(gbp, col), (S, LANES))
    bB = jnp.broadcast_to(_lane_col(gbp, col + 1), (S, LANES))
    r3 = lambda t: t.reshape(N, C, LANES)
    gc3 = r3(_chunk_cumsum(gB, rev, C))
    q3, k3, v3, b3, g3 = r3(q), r3(k), r3(v), r3(bB), r3(gB)
    ri = lax.broadcasted_iota(jnp.int32, (1, C, C), 1)
    ci = lax.broadcasted_iota(jnp.int32, (1, C, C), 2)
    incl = (ri <= ci) if rev else (ri >= ci)
    strict = (ri < ci) if rev else (ri > ci)
    decay = jnp.where(incl, jnp.exp(jnp.where(incl, _pair_diff(gc3), 0.0)), 0.0)
    kb = k3 * b3
    vb = v3 * b3
    A = jnp.where(strict, _bmm(kb, k3, "nt") * decay, 0.0)
    T = _tri_inv(A) if t_saved is None else _tri_inv_saved(A, t_saved)
    egc = jnp.exp(gc3)
    u = _bmm(T, vb, "nn")
    w = _bmm(T, kb * egc, "nn")
    qk = _bmm(q3, k3, "nt") * decay
    glast = jnp.sum(g3, axis=1, keepdims=True)
    kd = k3 * jnp.exp(glast - gc3)
    k2 = _bmm(kd, w, "tn")
    z = _bmm(kd, u, "tn")
    qe2 = q3 * egc - _bmm(qk, w, "nn")
    o0 = _bmm(qk, u, "nn")
    return k2, z, jnp.exp(glast), qe2, o0, T


def _c_next_state(state, k2, z, eg):
    return state * eg - _bmm(k2, state, "nn") + z


def _c_out(state, qe2, o0):
    return _bmm(qe2, state, "nn") + o0


def _c_post(o, zg, ng):
    return _rms(o, ng) * _silu(zg)


def _c_pre_specs(S, H):
    col = lambda c0: pl.BlockSpec((None, S, LANES), lambda h, b: (b, 0, c0 * H + h))
    zs = pl.BlockSpec((None, S, LANES), lambda h, b: (b, 0, 0))
    cw = lambda c0: pl.BlockSpec((None, CONV_W, LANES), lambda h, b: (c0, 0, h))
    pc = pl.BlockSpec((None, 4, LANES), lambda h, b: (h, 0, 0))
    return col, zs, cw, pc


def _c_pre_fwd(z, zs, cw3, pc, name):
    Bq, S, D4 = z.shape
    D = D4 // 4
    H = D // LANES
    col, zss, cw, pcs = _c_pre_specs(S, H)

    def body(zq_ref, zk_ref, zv_ref, zs_ref, cwq_ref, cwk_ref, cwv_ref, pc_ref, q_ref, k_ref, v_ref, gbp_ref):
        rows = lambda r: [r[i:i + 1, :] for i in range(r.shape[0])]
        q, k, v, gbp = _c_pre(zq_ref[...], zk_ref[...], zv_ref[...], zs_ref[...], rows(cwq_ref), rows(cwk_ref),
                              rows(cwv_ref), rows(pc_ref), pl.program_id(0), H)
        q_ref[...] = q
        k_ref[...] = k
        v_ref[...] = v
        gbp_ref[...] = gbp

    out = pl.BlockSpec((None, S, LANES), lambda h, b: (b, 0, h))
    shp = jax.ShapeDtypeStruct((Bq, S, D), F32)
    return pl.pallas_call(
        body, name=name, grid=(H, Bq), in_specs=[col(0), col(1), col(2), zss, cw(0), cw(1), cw(2), pcs],
        out_specs=[out] * 4, out_shape=[shp] * 4, compiler_params=_params(("parallel", "arbitrary")))(
            z, z, z, zs, cw3, cw3, cw3, pc)


def _c_pre_bwd(z, zs, cw3, pc, dq, dk, dv, dgbp, name):
    Bq, S, D4 = z.shape
    D = D4 // 4
    H = D // LANES
    col, zss, cw, pcs = _c_pre_specs(S, H)

    def body(zq_ref, zk_ref, zv_ref, zs_ref, cwq_ref, cwk_ref, cwv_ref, pc_ref, dq_ref, dk_ref, dv_ref, dgbp_ref,
             dzq_ref, dzk_ref, dzv_ref, dzs_ref, dcw_ref, dpc_ref):
        rows = lambda r: [r[i:i + 1, :] for i in range(r.shape[0])]
        fn = functools.partial(_c_pre, head=pl.program_id(0), HC=H)
        _, vjp = jax.vjp(fn, zq_ref[...], zk_ref[...], zv_ref[...], zs_ref[...], rows(cwq_ref), rows(cwk_ref),
                         rows(cwv_ref), rows(pc_ref))
        dzq, dzk, dzv, dzs, dcwq, dcwk, dcwv, dpcs = vjp((dq_ref[...], dk_ref[...], dv_ref[...], dgbp_ref[...]))
        dzq_ref[...] = dzq.astype(BF)
        dzk_ref[...] = dzk.astype(BF)
        dzv_ref[...] = dzv.astype(BF)
        dzs_ref[...] = dzs

        @pl.when(pl.program_id(1) == 0)
        def _():
            dcw_ref[...] = jnp.zeros_like(dcw_ref)
            dpc_ref[...] = jnp.zeros_like(dpc_ref)

        for c, dc in enumerate((dcwq, dcwk, dcwv)):
            for i in range(CONV_W):
                dcw_ref[c, i:i + 1, :] += dc[i]
        for i in range(4):
            dpc_ref[i:i + 1, :] += dpcs[i]

    out = pl.BlockSpec((None, S, LANES), lambda h, b: (b, 0, h))
    dzs_spec = pl.BlockSpec((None, None, S, LANES), lambda h, b: (h, b, 0, 0))
    dcw_spec = pl.BlockSpec((3, CONV_W, LANES), lambda h, b: (0, 0, h))
    bshape = jax.ShapeDtypeStruct((Bq, S, D), BF)
    dzq, dzk, dzv, dzs, dcw3, dpc = pl.pallas_call(
        body, name=name, grid=(H, Bq),
        in_specs=[col(0), col(1), col(2), zss, cw(0), cw(1), cw(2), pcs, out, out, out, out],
        out_specs=[out, out, out, dzs_spec, dcw_spec, pcs],
        out_shape=[bshape, bshape, bshape, jax.ShapeDtypeStruct((H, Bq, S, LANES), F32),
                   jax.ShapeDtypeStruct((3, CONV_W, D), F32), jax.ShapeDtypeStruct((H, 4, LANES), F32)],
        compiler_params=_params(("parallel", "arbitrary")))(z, z, z, zs, cw3, cw3, cw3, pc, dq, dk, dv, dgbp)
    return dzq, dzk, dzv, dzs, dcw3, dpc


def _c_saved_shapes(Bq, H, S):
    N, C = S // GDN_CHUNK, GDN_CHUNK
    return [(Bq, H, 2, N, LANES, LANES), (Bq, H, 2, N, LANES, LANES), (Bq, H, 2, N, 1, LANES), (Bq, H, 2, N, C, LANES),
            (Bq, H, 2, N, C, C)]


def _c_saved_scratch(S):
    return [pltpu.VMEM(shp[3:], F32) for shp in _c_saved_shapes(1, 1, S)]


PHASE1_CHUNKS = 8


def _c_blocks(S, fn):
    nb = min(PHASE1_CHUNKS, S // GDN_CHUNK)
    rows = nb * GDN_CHUNK

    def blk(i, carry):
        fn(pl.ds(pl.multiple_of(i * rows, rows), rows), pl.ds(pl.multiple_of(i * nb, nb), nb))
        return carry

    lax.fori_loop(0, S // rows, blk, 0)


def _c_phase1_blocks(in_refs, k2_ref, z_ref, eg_ref, qe2_ref, t_ref, o_ref, rev):
    def fn(rows, chunks):
        k2, z, eg, qe2, o0, t = _c_phase1(*[r[rows, :] for r in in_refs], rev)
        k2_ref[chunks] = k2
        z_ref[chunks] = z
        eg_ref[chunks] = eg
        qe2_ref[chunks] = qe2
        t_ref[chunks] = t
        o_ref[chunks] += o0

    _c_blocks(in_refs[0].shape[0], fn)


def _c_sweep(k2_ref, z_ref, eg_ref, st_ref, rev):
    N = st_ref.shape[0]

    def step(i, state):
        n = (N - 1 - i) if rev else i
        st_ref[n] = state
        return _c_next_state(state, k2_ref[n], z_ref[n], eg_ref[n])

    lax.fori_loop(0, N, step, jnp.zeros((LANES, LANES), F32))


def _c_sweep_adjoint(k2_ref, eg_ref, dso_ref, gs_ref, rev):
    N = gs_ref.shape[0]

    def step(i, g):
        n = i if rev else (N - 1 - i)
        gs_ref[n] = g
        return dso_ref[n] + g * eg_ref[n] - _bmm_raw(k2_ref[n], g, "tn")

    lax.fori_loop(0, N, step, jnp.zeros((LANES, LANES), F32))


def _c_mid_fwd(q, k, v, gbp, name):
    Bq, S, D = q.shape
    H = D // LANES
    N = S // GDN_CHUNK
    blk = pl.BlockSpec((None, S, LANES), lambda h, b: (b, 0, h))
    blk3 = pl.BlockSpec((None, N, GDN_CHUNK, LANES), lambda h, b: (b, 0, 0, h))
    n_saved = len(_c_saved_shapes(Bq, H, S))

    def body(q_ref, k_ref, v_ref, gbp_ref, o3, *rest):
        saved_hbm, scr = rest[:n_saved], rest[n_saved:]
        per_dir = n_saved + 1
        sems = scr[2 * per_dir]
        o3[...] = jnp.zeros_like(o3)
        copies = []
        for d, rev in enumerate((False, True)):
            k2_ref, st_ref, eg_ref, qe2_ref, t_ref, z_ref = scr[d * per_dir:(d + 1) * per_dir]
            _c_phase1_blocks((q_ref, k_ref, v_ref, gbp_ref), k2_ref, z_ref, eg_ref, qe2_ref, t_ref, o3, rev)
            _c_sweep(k2_ref, z_ref, eg_ref, st_ref, rev)

            def add_out(rows, chunks):
                o3[chunks] += _bmm_raw(qe2_ref[chunks], st_ref[chunks], "nn")

            _c_blocks(S, add_out)
            for i, (src, dst) in enumerate(zip((k2_ref, st_ref, eg_ref, qe2_ref, t_ref), saved_hbm)):
                copies.append(pltpu.make_async_copy(src, dst.at[pl.program_id(1), pl.program_id(0), d], sems.at[d, i]))
                copies[-1].start()
        for cp in copies:
            cp.wait()

    one_dir = _c_saved_scratch(S) + [pltpu.VMEM((N, LANES, LANES), F32)]
    outs = pl.pallas_call(
        body, name=name, grid=(H, Bq), in_specs=[blk] * 4,
        out_specs=[blk3] + [pl.BlockSpec(memory_space=pltpu.HBM)] * n_saved,
        out_shape=[jax.ShapeDtypeStruct((Bq, N, GDN_CHUNK, D), F32)]
        + [jax.ShapeDtypeStruct(shp, F32) for shp in _c_saved_shapes(Bq, H, S)],
        scratch_shapes=one_dir + one_dir + [pltpu.SemaphoreType.DMA((2, n_saved))],
        compiler_params=_params(("parallel", "parallel")))(q, k, v, gbp)
    return outs[0].reshape(Bq, S, D), tuple(outs[1:])


def _c_mid_bwd(q, k, v, gbp, do, saved, name):
    Bq, S, D = q.shape
    H = D // LANES
    N = S // GDN_CHUNK
    blk = pl.BlockSpec((None, S, LANES), lambda h, b: (b, 0, h))
    blk3 = pl.BlockSpec((None, N, GDN_CHUNK, LANES), lambda h, b: (b, 0, 0, h))
    n_saved = len(saved)

    def body(q_ref, k_ref, v_ref, gbp_ref, do3, *rest):
        saved_hbm = rest[:n_saved]
        dq_ref, dk_ref, dv_ref, dgbp_ref = rest[n_saved:n_saved + 4]
        scr = rest[n_saved + 4:]
        sets = (scr[:n_saved], scr[n_saved:2 * n_saved])
        dso_ref, gs_ref, sems = scr[2 * n_saved:]
        in_refs = (q_ref, k_ref, v_ref, gbp_ref)
        out_refs = (dq_ref, dk_ref, dv_ref, dgbp_ref)
        copies = [[pltpu.make_async_copy(src.at[pl.program_id(1), pl.program_id(0), d], dst, sems.at[d, i])
                   for i, (src, dst) in enumerate(zip(saved_hbm, sets[d]))] for d in range(2)]
        for cp in copies[0] + copies[1]:
            cp.start()
        for d, rev in enumerate((False, True)):
            k2_ref, st_ref, eg_ref, qe2_ref, t_ref = sets[d]
            for cp in copies[d]:
                cp.wait()

            def out_to_state(rows, chunks):
                dso_ref[chunks] = _bmm_raw(qe2_ref[chunks], do3[chunks], "tn")

            _c_blocks(S, out_to_state)
            _c_sweep_adjoint(k2_ref, eg_ref, dso_ref, gs_ref, rev)

            def block_vjp(rows, chunks):
                states, t_saved = st_ref[chunks], t_ref[chunks]

                def chunk_fn(q_, k_, v_, gbp_):
                    k2, z, eg, qe2, o0, _ = _c_phase1(q_, k_, v_, gbp_, rev, t_saved)
                    return _c_next_state(states, k2, z, eg), _c_out(states, qe2, o0)

                _, vjp = jax.vjp(chunk_fn, *[r[rows, :] for r in in_refs])
                for r, c in zip(out_refs, vjp((gs_ref[chunks], do3[chunks]))):
                    if rev:
                        r[rows, :] += c
                    else:
                        r[rows, :] = c

            _c_blocks(S, block_vjp)

    shp = jax.ShapeDtypeStruct((Bq, S, D), F32)
    mat = pltpu.VMEM((N, LANES, LANES), F32)
    return pl.pallas_call(
        body, name=name, grid=(H, Bq), in_specs=[blk] * 4 + [blk3] + [pl.BlockSpec(memory_space=pltpu.HBM)] * n_saved,
        out_specs=[blk] * 4, out_shape=[shp] * 4,
        scratch_shapes=_c_saved_scratch(S) + _c_saved_scratch(S) + [mat, mat, pltpu.SemaphoreType.DMA((2, n_saved))],
        compiler_params=_params(("parallel", "parallel")))(q, k, v, gbp, do.reshape(Bq, N, GDN_CHUNK, D), *saved)


def _c_post_fwd(o, z, ng, name):
    Bq, S, D = o.shape
    H = D // LANES
    blk = pl.BlockSpec((None, S, LANES), lambda h, b: (b, 0, h))
    gate = pl.BlockSpec((None, S, LANES), lambda h, b: (b, 0, 3 * H + h))
    vec = pl.BlockSpec((1, LANES), lambda h, b: (0, 0))

    def body(o_ref, zg_ref, ng_ref, y_ref):
        y_ref[...] = _c_post(o_ref[...], zg_ref[...], ng_ref[...]).astype(BF)

    return pl.pallas_call(
        body, name=name, grid=(H, Bq), in_specs=[blk, gate, vec], out_specs=blk,
        out_shape=jax.ShapeDtypeStruct((Bq, S, D), BF), compiler_params=_params(("parallel", "parallel")))(
            o, z, ng.reshape(1, LANES))


def _c_post_bwd(o, z, ng, dy, name):
    Bq, S, D = o.shape
    H = D // LANES
    blk = pl.BlockSpec((None, S, LANES), lambda b, h: (b, 0, h))
    gate = pl.BlockSpec((None, S, LANES), lambda b, h: (b, 0, 3 * H + h))
    vec = pl.BlockSpec((1, LANES), lambda b, h: (0, 0))

    def body(o_ref, zg_ref, ng_ref, dy_ref, do_ref, dzg_ref, dng_ref):
        _, vjp = jax.vjp(_c_post, o_ref[...], zg_ref[...], ng_ref[...])
        do, dzg, dng = vjp(dy_ref[...])
        do_ref[...] = do
        dzg_ref[...] = dzg.astype(BF)

        @pl.when((pl.program_id(0) == 0) & (pl.program_id(1) == 0))
        def _():
            dng_ref[...] = jnp.zeros_like(dng_ref)

        dng_ref[...] += dng

    do, dzg, dng = pl.pallas_call(
        body, name=name, grid=(Bq, H), in_specs=[blk, gate, vec, blk], out_specs=[blk, blk, vec],
        out_shape=[jax.ShapeDtypeStruct((Bq, S, D), F32), jax.ShapeDtypeStruct((Bq, S, D), BF),
                   jax.ShapeDtypeStruct((1, LANES), F32)],
        compiler_params=_params(("arbitrary", "arbitrary")))(o, z, ng.reshape(1, LANES), dy)
    return do, dzg, dng.reshape(LANES)


def _c_param_rows(a_log, dt_bias):
    p = jnp.concatenate([a_log, dt_bias], axis=0).T
    return jnp.broadcast_to(p[:, :, None], p.shape + (LANES,)).astype(F32)


def _local_step(x, tgt, W, by_chip=False):
    Bq, S, D = x.shape
    T = Bq * S
    H = D // LANES
    L = W["norm_mix_g"].shape[0]
    seq = lambda t: t.reshape(Bq, S, t.shape[-1])
    flat = lambda t: t.reshape(T, t.shape[-1])
    if "mlp_up_slots" in W:
        up4, down4 = W["mlp_up_slots"], W["mlp_down_slots"]
    else:
        up4 = W["mlp_w_up"].reshape(L, D, N_CHIPS, -1).transpose(2, 0, 1, 3)
        down4 = W["mlp_w_down"].reshape(L, N_CHIPS, -1, D).transpose(1, 0, 2, 3)
    sw = up4.shape[-1]
    F = N_CHIPS * sw

    xs = flat(x)
    saved = []
    for i in range(L):
        kind, j = i % N_MIXERS, i // N_MIXERS
        tag = f"l{i}"
        sv = {"x": xs}
        hn = _rms_fwd(xs, W["norm_mix_g"][i], f"{tag}_mix_norm")
        sv["hn"] = hn
        if kind == 0:
            z = _mm(hn, W["a_w_in"][j], "nn", f"{tag}_a_in")
            y = _a_core_fwd(seq(z), W["a_conv_w"][j], W["a_conv_b"][j], W["a_gate_w"][j], W["a_gate_b"][j],
                            W["a_lambda"][j], f"{tag}_a_core")
            sv["z"] = z
            w_out = W["a_w_out"][j]
        elif kind == 1:
            z = _mm(hn, W["b_w_in"][j], "nn", f"{tag}_b_in")
            bsf = jnp.repeat(W["b_b_s"][j].T, LANES, axis=1)
            y = _b_core_fwd(z, W["b_ln_g"][j], W["b_ln_b"][j], W["b_w_s"][j], bsf, f"{tag}_b_core")
            sv["z"], sv["bsf"] = z, bsf
            w_out = W["b_w_out"][j]
        else:
            w_in = W["c_w_in"][j]
            w_small = jnp.pad(w_in[:, 4 * D:], ((0, 0), (0, LANES - 4 * H)))
            z = _mm(hn, w_in[:, :4 * D], "nn", f"{tag}_c_in")
            zs = _mm(hn, w_small, "nn", f"{tag}_c_in_small")
            cw3 = W["c_conv_w"][j].reshape(CONV_W, 3, D).transpose(1, 0, 2)
            pc = _c_param_rows(W["c_a_log"][j], W["c_dt_bias"][j])
            q, k, v, gbp = _c_pre_fwd(seq(z), seq(zs), cw3, pc, f"{tag}_c_pre")
            o, sv["mid"] = _c_mid_fwd(q, k, v, gbp, f"{tag}_c_mid")
            y = _c_post_fwd(o, seq(z), W["c_norm_g"][j], f"{tag}_c_post")
            sv.update(z=z, zs=zs, cw3=cw3, pc=pc, q=q, k=k, v=v, gbp=gbp, o=o, w_small=w_small)
            w_out = W["c_w_out"][j]
        y = flat(y)
        sv["y"] = y
        x1 = _mm(y, w_out, "nn", f"{tag}_mix_out", epi="add", extra=xs)
        sv["x1"] = x1
        hn2 = _rms_fwd(x1, W["norm_mlp_g"][i], f"{tag}_mlp_norm")
        act = _mm(hn2, up4, "nn", f"{tag}_mlp_up", out_dtype=BF, epi="relu2", tm=2048, tn=sw, tk=D, n_cols=F,
                  b_index=lambda j, k, i=i: (j, i, 0, 0))
        xs = _mm(act, down4, "nn", f"{tag}_mlp_down", epi="add", extra=x1, tn=D, tk=sw, n_cols=D,
                 b_index=lambda j, k, i=i: (k, i, 0, 0))
        sv["hn2"], sv["act"] = hn2, act
        saved.append(sv)

    loss, dx, dgf = _final_loss(xs, W["norm_final_g"], flat(tgt), "final_loss")

    G = {"norm_final_g": dgf}
    per_layer = {n: [None] * L for n in ("norm_mix_g", "norm_mlp_g", "mlp_w_up", "mlp_w_down")}
    mixer = {}
    for i in reversed(range(L)):
        kind, j = i % N_MIXERS, i // N_MIXERS
        tag = f"l{i}"
        sv = saved[i]
        dhid = _mm(dx, down4, "nt", f"{tag}_mlp_dhid", out_dtype=BF, epi="relu2_bwd", extra=sv["act"], tm=2048, tn=sw, tk=D,
                   n_cols=F, b_index=lambda j, k, i=i: (j, i, 0, 0))
        per_layer["mlp_w_down"][i] = _mm(sv["act"], dx, "tn", f"{tag}_mlp_dwdown").reshape(N_CHIPS, sw, D)
        per_layer["mlp_w_up"][i] = _mm(sv["hn2"], dhid, "tn", f"{tag}_mlp_dwup", tn=sw, by_chip="cols")
        dhn2 = _mm(dhid, up4, "nt", f"{tag}_mlp_dhn", tn=D, tk=sw, n_cols=D, b_index=lambda j, k, i=i: (k, i, 0, 0))
        dx, per_layer["norm_mlp_g"][i] = _rms_bwd(sv["x1"], W["norm_mlp_g"][i], dhn2, dx, f"{tag}_mlp_norm_bwd")
        g = {}
        if kind == 0:
            dy = _mm(dx, W["a_w_out"][j], "nt", f"{tag}_a_dy")
            g["a_w_out"] = _mm(sv["y"], dx, "tn", f"{tag}_a_dwout")
            dz, g["a_conv_w"], g["a_conv_b"], g["a_gate_w"], g["a_gate_b"], g["a_lambda"] = _a_core_bwd(
                seq(sv["z"]), seq(dy), W["a_conv_w"][j], W["a_conv_b"][j], W["a_gate_w"][j], W["a_gate_b"][j],
                W["a_lambda"][j], f"{tag}_a_core_bwd")
            dz = flat(dz)
            g["a_w_in"] = _mm(sv["hn"], dz, "tn", f"{tag}_a_dwin")
            dhn = _mm(dz, W["a_w_in"][j], "nt", f"{tag}_a_dhn")
        elif kind == 1:
            dy = _mm(dx, W["b_w_out"][j], "nt", f"{tag}_b_dy")
            g["b_w_out"] = _mm(sv["y"], dx, "tn", f"{tag}_b_dwout")
            dz, g["b_ln_g"], g["b_ln_b"], g["b_w_s"], dbsf = _b_core_bwd(
                sv["z"], dy, W["b_ln_g"][j], W["b_ln_b"][j], W["b_w_s"][j], sv["bsf"], f"{tag}_b_core_bwd")
            g["b_b_s"] = dbsf.reshape(SGU_CHUNK, H, LANES).sum(-1).T
            g["b_w_in"] = _mm(sv["hn"], dz, "tn", f"{tag}_b_dwin")
            dhn = _mm(dz, W["b_w_in"][j], "nt", f"{tag}_b_dhn")
        else:
            dy = _mm(dx, W["c_w_out"][j], "nt", f"{tag}_c_dy")
            g["c_w_out"] = _mm(sv["y"], dx, "tn", f"{tag}_c_dwout")
            do, dzg, g["c_norm_g"] = _c_post_bwd(sv["o"], seq(sv["z"]), W["c_norm_g"][j], seq(dy), f"{tag}_c_post_bwd")
            dq, dk, dv, dgbp = _c_mid_bwd(sv["q"], sv["k"], sv["v"], sv["gbp"], do, sv["mid"], f"{tag}_c_mid_bwd")
            dzq, dzk, dzv, dzs_h, dcw3, dpc = _c_pre_bwd(seq(sv["z"]), seq(sv["zs"]), sv["cw3"], sv["pc"], dq, dk, dv, dgbp,
                                                         f"{tag}_c_pre_bwd")
            dz = flat(jnp.concatenate([dzq, dzk, dzv, dzg], axis=-1))
            dzs = flat(dzs_h.sum(0)).astype(BF)
            g["c_conv_w"] = dcw3.transpose(1, 0, 2).reshape(CONV_W, 3 * D)
            dpc = dpc.sum(-1)
            g["c_a_log"], g["c_dt_bias"] = dpc[:, :2].T, dpc[:, 2:].T
            dw_main = _mm(sv["hn"], dz, "tn", f"{tag}_c_dwin")
            dw_small = _mm(sv["hn"], dzs, "tn", f"{tag}_c_dwin_small")
            g["c_w_in"] = jnp.concatenate([dw_main, dw_small[:, :4 * H]], axis=1)
            dhn = _mm(dz, W["c_w_in"][j][:, :4 * D], "nt", f"{tag}_c_dhn")
            dhn = _mm(dzs, sv["w_small"], "nt", f"{tag}_c_dhn_small", epi="add", extra=dhn)
        dx, per_layer["norm_mix_g"][i] = _rms_bwd(sv["x"], W["norm_mix_g"][i], dhn, dx, f"{tag}_mix_norm_bwd")
        for n, val in g.items():
            mixer.setdefault(n, {})[j] = val

    for n, vals in per_layer.items():
        if n not in ("mlp_w_up", "mlp_w_down"):
            G[n] = jnp.stack(vals)
        elif by_chip:
            G[n] = vals
        elif n == "mlp_w_up":
            G[n] = jnp.stack(vals).transpose(0, 2, 1, 3).reshape(L, D, F)
        else:
            G[n] = jnp.stack(vals).reshape(L, F, D)
    for n, by_j in mixer.items():
        G[n] = jnp.stack([by_j[j] for j in sorted(by_j)])
    return loss, dx.reshape(Bq, S, D), G


MESH = pl.DeviceIdType.MESH
N_CHIPS = 4
HBM_SPEC = pl.BlockSpec(memory_space=pltpu.HBM)


def _place():
    x, y, c = lax.axis_index("x"), lax.axis_index("y"), lax.axis_index("c")
    others = [(1 - x, y), (x, 1 - y), (1 - x, 1 - y)]
    return x, y, c, others


def _all_gather_xy(bufs, name):
    n = len(bufs)
    pieces = [_stage_rows(b.shape[0], b.shape[1] * b.dtype.itemsize) for b in bufs]

    def body(*refs):
        ins, outs = refs[:n], refs[n:2 * n]
        send, recv, fsend, frecv = refs[2 * n:2 * n + 4]
        stages = refs[2 * n + 4:]
        x, y, c, others = _place()
        p = 2 * x + y
        half = lambda b, cc: pl.ds(cc * (ins[b].shape[0] // 2), ins[b].shape[0] // 2)

        def ici(b, j):
            qx, qy = others[j]
            return pltpu.make_async_remote_copy(
                src_ref=ins[b].at[half(b, c)], dst_ref=outs[b].at[p, half(b, c)], send_sem=send.at[b, j],
                recv_sem=recv.at[b, j], device_id=(qx, qy, c), device_id_type=MESH)

        def landed(b, j, cc):
            qx, qy = others[j]
            return outs[b].at[2 * qx + qy, half(b, cc)]

        def d2d(b, j):
            return pltpu.make_async_remote_copy(
                src_ref=landed(b, j, c), dst_ref=landed(b, j, c), send_sem=fsend.at[b, j], recv_sem=frecv.at[b, j],
                device_id=(x, y, 1 - c), device_id_type=MESH)

        pairs = [(b, j) for b in range(n) for j in range(3)]
        for b, j in pairs:
            ici(b, j).start()
        for b in range(n):
            def own_piece(i, carry, b=b):
                rows = pl.ds(pl.multiple_of(i * pieces[b], pieces[b]), pieces[b])
                pltpu.sync_copy(ins[b].at[rows], stages[b])
                pltpu.sync_copy(stages[b], outs[b].at[p, rows])
                return carry

            lax.fori_loop(0, ins[b].shape[0] // pieces[b], own_piece, 0)
        for b, j in pairs:
            pltpu.make_async_remote_copy(
                src_ref=ins[b].at[half(b, c)], dst_ref=landed(b, j, c), send_sem=send.at[b, j], recv_sem=recv.at[b, j],
                device_id=(x, y, c), device_id_type=MESH).wait_recv()
            d2d(b, j).start()
        for b, j in pairs:
            pltpu.make_async_remote_copy(
                src_ref=landed(b, j, 1 - c), dst_ref=landed(b, j, 1 - c), send_sem=fsend.at[b, j], recv_sem=frecv.at[b, j],
                device_id=(x, y, 1 - c), device_id_type=MESH).wait_recv()
        for b, j in pairs:
            ici(b, j).wait_send()
            d2d(b, j).wait_send()

    return pl.pallas_call(
        body, name=name, in_specs=[HBM_SPEC] * n, out_specs=[HBM_SPEC] * n,
        out_shape=[jax.ShapeDtypeStruct((N_CHIPS,) + b.shape, b.dtype) for b in bufs],
        scratch_shapes=[pltpu.SemaphoreType.DMA((n, 3))] * 4
        + [pltpu.VMEM((r, b.shape[1]), b.dtype) for r, b in zip(pieces, bufs)],
        compiler_params=pltpu.CompilerParams(has_side_effects=True, vmem_limit_bytes=VMEM_LIMIT))(*bufs)


STAGE_BYTES = 2 * 1024 * 1024


def _stage_rows(rows, row_bytes):
    for d in range(min(rows, max(1, STAGE_BYTES // row_bytes)), 0, -1):
        if rows % d == 0 and (d % 16 == 0 or d == rows):
            return d
    return rows


def _swap_halves(gs, name):
    n = len(gs)

    def body(*refs):
        g_refs, o_refs, send, recv = refs[:n], refs[n:2 * n], refs[2 * n], refs[2 * n + 1]
        x, y, c, _ = _place()
        cps = [pltpu.make_async_remote_copy(src_ref=g_refs[b].at[:, 1 - c], dst_ref=o_refs[b], send_sem=send.at[b],
                                            recv_sem=recv.at[b], device_id=(x, y, 1 - c), device_id_type=MESH)
               for b in range(n)]
        for cp in cps:
            cp.start()
        for cp in cps:
            cp.wait()

    return pl.pallas_call(
        body, name=name, in_specs=[HBM_SPEC] * n, out_specs=[HBM_SPEC] * n,
        out_shape=[jax.ShapeDtypeStruct((g.shape[0],) + g.shape[2:], g.dtype) for g in gs],
        scratch_shapes=[pltpu.SemaphoreType.DMA((n,)), pltpu.SemaphoreType.DMA((n,))],
        compiler_params=pltpu.CompilerParams(has_side_effects=True))(*gs)


def _pair_sum(g, got, out_dtype, name):
    nq, _, h, cols = g.shape
    tr = min(512, h)

    def body(c_ref, g_ref, r_ref, o_ref):
        o_ref[...] = (g_ref[...] + r_ref[...]).astype(o_ref.dtype)

    spec = pl.BlockSpec((None, tr, cols), lambda q, i, c_ref: (q, i, 0))
    return pl.pallas_call(
        body, name=name,
        grid_spec=pltpu.PrefetchScalarGridSpec(
            num_scalar_prefetch=1, grid=(nq, h // tr),
            in_specs=[pl.BlockSpec((None, None, tr, cols), lambda q, i, c_ref: (q, c_ref[0], i, 0)), spec], out_specs=spec),
        out_shape=jax.ShapeDtypeStruct((nq, h, cols), out_dtype),
        compiler_params=_params(("parallel", "parallel")))(lax.axis_index("c").astype(jnp.int32).reshape(1), g, got)


def _scatter_xy(p_sums, name):
    n = len(p_sums)

    def body(*refs):
        p_refs, o_refs, send, recv = refs[:n], refs[n:2 * n], refs[2 * n], refs[2 * n + 1]
        x, y, c, others = _place()
        me = 2 * x + y
        cps = []
        for b in range(n):
            for j, (qx, qy) in enumerate(others):
                cps.append(pltpu.make_async_remote_copy(
                    src_ref=p_refs[b].at[2 * qx + qy], dst_ref=o_refs[b].at[me], send_sem=send.at[b, j],
                    recv_sem=recv.at[b, j], device_id=(qx, qy, c), device_id_type=MESH))
                cps[-1].start()
        for b in range(n):
            for j, (qx, qy) in enumerate(others):
                pltpu.make_async_remote_copy(
                    src_ref=p_refs[b].at[me], dst_ref=o_refs[b].at[2 * qx + qy], send_sem=send.at[b, j],
                    recv_sem=recv.at[b, j], device_id=(qx, qy, c), device_id_type=MESH).wait_recv()
        for cp in cps:
            cp.wait_send()

    return pl.pallas_call(
        body, name=name, in_specs=[HBM_SPEC] * n, out_specs=[HBM_SPEC] * n,
        out_shape=[jax.ShapeDtypeStruct(p.shape, p.dtype) for p in p_sums],
        scratch_shapes=[pltpu.SemaphoreType.DMA((n, 3)), pltpu.SemaphoreType.DMA((n, 3))],
        compiler_params=pltpu.CompilerParams(has_side_effects=True))(*p_sums)


def _chip_sum(r4, p_sum, name):
    nq, h, cols = r4.shape
    tr = min(512, h)

    def body(r_ref, p_ref, o_ref):
        me = 2 * lax.axis_index("x") + lax.axis_index("y")
        f = lambda q: jnp.where(me == q, p_ref[q], r_ref[q]).astype(F32)
        o_ref[...] = ((f(0) + f(1)) + f(2)) + f(3)

    spec = pl.BlockSpec((nq, tr, cols), lambda i: (0, i, 0))
    return pl.pallas_call(
        body, name=name, grid=(h // tr,), in_specs=[spec, spec],
        out_specs=pl.BlockSpec((tr, cols), lambda i: (i, 0)), out_shape=jax.ShapeDtypeStruct((h, cols), F32),
        compiler_params=_params(("parallel",)))(r4, p_sum)


def _join_halves(rs, name):
    n = len(rs)

    def body(*refs):
        r_refs, o_refs, send, recv = refs[:n], refs[n:2 * n], refs[2 * n], refs[2 * n + 1]
        x, y, c, _ = _place()
        cps = [pltpu.make_async_remote_copy(src_ref=r_refs[b], dst_ref=o_refs[b].at[c], send_sem=send.at[b],
                                            recv_sem=recv.at[b], device_id=(x, y, 1 - c), device_id_type=MESH)
               for b in range(n)]
        for cp in cps:
            cp.start()
        for b in range(n):
            pltpu.make_async_remote_copy(src_ref=r_refs[b], dst_ref=o_refs[b].at[1 - c], send_sem=send.at[b],
                                         recv_sem=recv.at[b], device_id=(x, y, 1 - c), device_id_type=MESH).wait_recv()
        for cp in cps:
            cp.wait_send()

    gots = pl.pallas_call(
        body, name=name, in_specs=[HBM_SPEC] * n, out_specs=[HBM_SPEC] * n,
        out_shape=[jax.ShapeDtypeStruct((2,) + r.shape, r.dtype) for r in rs],
        scratch_shapes=[pltpu.SemaphoreType.DMA((n,)), pltpu.SemaphoreType.DMA((n,))],
        compiler_params=pltpu.CompilerParams(has_side_effects=True))(*rs)
    c = lax.axis_index("c")
    return [jnp.stack([jnp.where(c == s, r, got[s]) for s in range(2)]) for r, got in zip(rs, gots)]


def _reduce_scatter(gs, tag):
    gs = [g.reshape(g.shape[0], 2, g.shape[1] // 2, g.shape[2]) for g in gs]
    gots = _swap_halves(gs, f"{tag}_swap")
    pairs = [_pair_sum(g, got, BF, f"{tag}_pair_sum{i}") for i, (g, got) in enumerate(zip(gs, gots))]
    r4s = _scatter_xy(pairs, f"{tag}_scatter")
    rs = [_chip_sum(r4, pair, f"{tag}_chip_sum{i}") for i, (r4, pair) in enumerate(zip(r4s, pairs))]
    return [j.reshape(-1, j.shape[-1]) for j in _join_halves(rs, f"{tag}_join")]


def _adamw(w, g, m, v, name):
    rows, cols = w.shape
    tr = rows
    for cand in (512, 344, 256, 128, 64, 32, 16, 8):
        if rows % cand == 0:
            tr = cand
            break

    def body(w_ref, g_ref, m_ref, v_ref, d_ref, nm_ref, nv_ref):
        g_ = g_ref[...]
        m_ = ADAM_B1 * m_ref[...] + (1.0 - ADAM_B1) * g_
        v_ = ADAM_B2 * v_ref[...] + (1.0 - ADAM_B2) * jnp.square(g_)
        m_hat = m_ / (1.0 - ADAM_B1 ** ADAM_STEP)
        v_hat = v_ / (1.0 - ADAM_B2 ** ADAM_STEP)
        d_ref[...] = -ADAM_LR * (m_hat / (jnp.sqrt(v_hat) + ADAM_EPS) + ADAM_WD * w_ref[...])
        nm_ref[...] = m_
        nv_ref[...] = v_

    spec = pl.BlockSpec((tr, cols), lambda i: (i, 0))
    shp = jax.ShapeDtypeStruct((rows, cols), F32)
    return pl.pallas_call(body, name=name, grid=(rows // tr,), in_specs=[spec] * 4, out_specs=[spec] * 3,
                          out_shape=[shp] * 3, compiler_params=_params(("parallel",)))(w, g, m, v)


WEIGHTS = ["norm_mix_g", "norm_mlp_g", "mlp_w_up", "mlp_w_down", "norm_final_g", "a_w_in", "a_conv_w", "a_conv_b",
           "a_gate_w", "a_gate_b", "a_lambda", "a_w_out", "b_w_in", "b_ln_g", "b_ln_b", "b_w_s", "b_b_s", "b_w_out",
           "c_w_in", "c_conv_w", "c_a_log", "c_dt_bias", "c_norm_g", "c_w_out"]
SHARD_AXIS = {"mlp_w_up": 2, "mlp_w_down": 1, "a_w_in": 2, "a_conv_w": 2, "a_conv_b": 1, "a_lambda": 2, "a_w_out": 1,
              "b_w_in": 2, "b_w_out": 1, "c_w_in": 2, "c_conv_w": 2, "c_w_out": 1}
MATMUL_WEIGHTS = ["mlp_w_up", "mlp_w_down", "a_w_in", "a_w_out", "b_w_in", "b_w_out", "c_w_out", "c_w_in"]
SMALL_SHARDED = ["a_conv_w", "a_conv_b", "a_lambda", "c_conv_w"]
REPLICATED = [n for n in WEIGHTS if n not in SHARD_AXIS]
FLAT_COLS = 1024


def _rows_of(shape):
    n = 1
    for s in shape:
        n *= s
    return -(-n // FLAT_COLS)


def _pack(arrays, total_rows, dtype):
    parts = []
    used = 0
    for a in arrays:
        r = _rows_of(a.shape)
        f = a.reshape(-1).astype(dtype)
        parts.append(jnp.pad(f, (0, r * FLAT_COLS - f.shape[0])).reshape(r, FLAT_COLS))
        used += r
    if total_rows > used:
        parts.append(jnp.zeros((total_rows - used, FLAT_COLS), dtype))
    return jnp.concatenate(parts, axis=0)


def _slots_of(g, axis):
    w = g.shape[axis] // N_CHIPS
    flat = jnp.stack([lax.slice_in_dim(g, q * w, (q + 1) * w, axis=axis).reshape(-1) for q in range(N_CHIPS)])
    rows = _rows_of(flat.shape[1:])
    return jnp.pad(flat, ((0, 0), (0, rows * FLAT_COLS - flat.shape[1]))).reshape(N_CHIPS, rows, FLAT_COLS)


def _unpack(buf, shapes):
    out, r0 = [], 0
    for shp in shapes:
        r = _rows_of(shp)
        n = 1
        for s in shp:
            n *= s
        out.append(buf[r0:r0 + r].reshape(-1)[:n].reshape(shp))
        r0 += r
    return out


def _round_up(n, m):
    return -(-n // m) * m


def kernel(x, norm_mix_g, norm_mlp_g, mlp_w_up, mlp_w_down, norm_final_g, a_w_in, a_conv_w, a_conv_b, a_gate_w, a_gate_b, a_lambda, a_w_out, b_w_in, b_ln_g, b_ln_b, b_w_s, b_b_s, b_w_out, c_w_in, c_conv_w, c_a_log, c_dt_bias, c_norm_g, c_w_out, loss_target, m_norm_mix_g, m_norm_mlp_g, m_mlp_w_up, m_mlp_w_down, m_norm_final_g, m_a_w_in, m_a_conv_w, m_a_conv_b, m_a_gate_w, m_a_gate_b, m_a_lambda, m_a_w_out, m_b_w_in, m_b_ln_g, m_b_ln_b, m_b_w_s, m_b_b_s, m_b_w_out, m_c_w_in, m_c_conv_w, m_c_a_log, m_c_dt_bias, m_c_norm_g, m_c_w_out, v_norm_mix_g, v_norm_mlp_g, v_mlp_w_up, v_mlp_w_down, v_norm_final_g, v_a_w_in, v_a_conv_w, v_a_conv_b, v_a_gate_w, v_a_gate_b, v_a_lambda, v_a_w_out, v_b_w_in, v_b_ln_g, v_b_ln_b, v_b_w_s, v_b_b_s, v_b_w_out, v_c_w_in, v_c_conv_w, v_c_a_log, v_c_dt_bias, v_c_norm_g, v_c_w_out):
    given = dict(locals())
    w_loc = {n: given[n] for n in WEIGHTS}
    m_loc = {n: given["m_" + n] for n in WEIGHTS}
    v_loc = {n: given["v_" + n] for n in WEIGHTS}

    mlp = ["mlp_w_up", "mlp_w_down"]
    rest = [n for n in MATMUL_WEIGHTS if n not in mlp]
    rest_rows = _round_up(sum(_rows_of(w_loc[n].shape) for n in rest), 512)
    small_rows = _round_up(sum(_rows_of(w_loc[n].shape) for n in SMALL_SHARDED), 16)
    as_rows = lambda w: w.astype(BF).reshape(-1, w.shape[-1])
    up_all, down_all, rest_all, small_all = _all_gather_xy(
        [as_rows(mlp_w_up), as_rows(mlp_w_down), _pack([w_loc[n] for n in rest], rest_rows, BF),
         _pack([w_loc[n] for n in SMALL_SHARDED], small_rows, F32)], "gather_weights")
    W = {n: w_loc[n] for n in REPLICATED}
    W["mlp_up_slots"] = up_all.reshape((N_CHIPS,) + mlp_w_up.shape)
    W["mlp_down_slots"] = down_all.reshape((N_CHIPS,) + mlp_w_down.shape)
    for names, buf in ((rest, rest_all), (SMALL_SHARDED, small_all)):
        per_chip = [_unpack(buf[q], [w_loc[n].shape for n in names]) for q in range(N_CHIPS)]
        for i, n in enumerate(names):
            W[n] = jnp.concatenate([per_chip[q][i] for q in range(N_CHIPS)], axis=SHARD_AXIS[n])

    loss, grad_x, G = _local_step(x, loss_target, W, by_chip=True)
    loss = lax.psum(loss, ("x", "y", "c"))

    sharded = rest + SMALL_SHARDED
    rep_rows = _round_up(sum(_rows_of(w_loc[n].shape) for n in REPLICATED), N_CHIPS * 16)
    rep_flat = _pack([G[n] for n in REPLICATED], rep_rows, F32).reshape(N_CHIPS, rep_rows // N_CHIPS, FLAT_COLS)
    shard_rows = sum(_rows_of(w_loc[n].shape) for n in sharded)
    total_rows = _round_up(shard_rows + rep_rows // N_CHIPS, 1024)
    parts = [_slots_of(G[n], SHARD_AXIS[n]) for n in sharded] + [rep_flat]
    parts.append(jnp.zeros((N_CHIPS, total_rows - shard_rows - rep_rows // N_CHIPS, FLAT_COLS), F32))
    mlp_bufs = [g.reshape(N_CHIPS, -1, g.shape[-1]) for n in mlp for g in G[n]]
    *mlp_red, red = _reduce_scatter(mlp_bufs + [jnp.concatenate(parts, axis=1)], "grads")
    n_layers = mlp_w_up.shape[0]
    g_loc = {"mlp_w_up": jnp.stack(mlp_red[:n_layers]).reshape(mlp_w_up.shape),
             "mlp_w_down": jnp.stack(mlp_red[n_layers:]).reshape(mlp_w_down.shape)}
    g_loc.update(zip(sharded, _unpack(red, [w_loc[n].shape for n in sharded])))
    rep_quarter = red[shard_rows:shard_rows + rep_rows // N_CHIPS]
    (rep_all,) = _all_gather_xy([rep_quarter], "gather_replicated_grads")
    g_loc.update(zip(REPLICATED, _unpack(rep_all.reshape(rep_rows, FLAT_COLS), [w_loc[n].shape for n in REPLICATED])))

    delta, new_m, new_v = {}, {}, {}
    big = [n for n in MATMUL_WEIGHTS if w_loc[n].size % FLAT_COLS == 0]
    for n in big:
        shp = w_loc[n].shape
        two_d = lambda a: a.reshape(-1, FLAT_COLS)
        d, nm, nv = _adamw(two_d(w_loc[n]), two_d(g_loc[n]), two_d(m_loc[n]), two_d(v_loc[n]), f"adamw_{n}")
        delta[n], new_m[n], new_v[n] = d.reshape(shp), nm.reshape(shp), nv.reshape(shp)
    small = [n for n in WEIGHTS if n not in big]
    small_shapes = [w_loc[n].shape for n in small]
    rows = _round_up(sum(_rows_of(s) for s in small_shapes), 8)
    packed = [_pack([src[n] for n in small], rows, F32) for src in (w_loc, g_loc, m_loc, v_loc)]
    for dst, buf in zip((delta, new_m, new_v), _adamw(*packed, "adamw_small")):
        dst.update(zip(small, _unpack(buf, small_shapes)))

    return (loss, grad_x, *[g_loc[n] for n in WEIGHTS], *[delta[n] for n in WEIGHTS],
            *[new_m[n] for n in WEIGHTS], *[new_v[n] for n in WEIGHTS])
```

```python
import functools

import jax
import jax.numpy as jnp
from jax import lax
from jax.experimental import pallas as pl
from jax.experimental.pallas import tpu as pltpu

F32 = jnp.float32
BF = jnp.bfloat16

LANES = 128
VMEM_LIMIT = 56 * 1024 * 1024
RG_C = 8.0
SGU_CHUNK = 128
GDN_CHUNK = 64
CONV_W = 4
N_MIXERS = 3

ADAM_LR = 0.001
ADAM_B1 = 0.9
ADAM_B2 = 0.999
ADAM_EPS = 1e-08
ADAM_WD = 0.01
ADAM_STEP = 10


def _params(sem=None):
    return pltpu.CompilerParams(dimension_semantics=sem, vmem_limit_bytes=VMEM_LIMIT)


def _shift_impl(x, s):
    if s == 0:
        return x
    n = x.shape[0]
    row = lax.broadcasted_iota(jnp.int32, x.shape, 0)
    if s > 0:
        return jnp.where(row >= s, pltpu.roll(x, s, 0), 0.0)
    return jnp.where(row < n + s, pltpu.roll(x, n + s, 0), 0.0)


@functools.partial(jax.custom_vjp, nondiff_argnums=(1,))
def _shift(x, s):
    return _shift_impl(x, s)


def _shift_fwd(x, s):
    return _shift_impl(x, s), None


def _shift_bwd(s, _, g):
    return (_shift_impl(g, -s),)


_shift.defvjp(_shift_fwd, _shift_bwd)


def _chunk_cumsum_impl(x, rev, chunk):
    n = x.shape[0]
    rc = lax.broadcasted_iota(jnp.int32, x.shape, 0) & (chunk - 1)
    sh = 1
    while sh < chunk:
        if rev:
            x = x + jnp.where(rc < chunk - sh, pltpu.roll(x, n - sh, 0), 0.0)
        else:
            x = x + jnp.where(rc >= sh, pltpu.roll(x, sh, 0), 0.0)
        sh *= 2
    return x


@functools.partial(jax.custom_vjp, nondiff_argnums=(1, 2))
def _chunk_cumsum(x, rev, chunk):
    return _chunk_cumsum_impl(x, rev, chunk)


def _chunk_cumsum_fwd(x, rev, chunk):
    return _chunk_cumsum_impl(x, rev, chunk), None


def _chunk_cumsum_bwd(rev, chunk, _, g):
    return (_chunk_cumsum_impl(g, not rev, chunk),)


_chunk_cumsum.defvjp(_chunk_cumsum_fwd, _chunk_cumsum_bwd)


def _rms(x, g, eps=1e-6):
    return x * lax.rsqrt(jnp.mean(x * x, axis=-1, keepdims=True) + eps) * g


def _sigmoid(x):
    return 0.5 * jnp.tanh(0.5 * x) + 0.5


def _silu(x):
    return x * _sigmoid(x)


def _softplus(x):
    return jnp.maximum(x, 0.0) + jnp.log1p(jnp.exp(-jnp.abs(x)))


def _neg_expm1(y, ey):
    series = -y * (1.0 + y * (1 / 2) * (1.0 + y * (1 / 3) * (1.0 + y * (1 / 4))))
    return jnp.where(y > -1 / 32, series, 1.0 - ey)


@jax.custom_vjp
def _sqrt_one_minus_sq(log_a, a):
    t = _neg_expm1(2.0 * log_a, a * a)
    return t * lax.rsqrt(jnp.maximum(t, 1e-30))


def _sqrt_one_minus_sq_fwd(log_a, a):
    t = _neg_expm1(2.0 * log_a, a * a)
    rs = lax.rsqrt(jnp.maximum(t, 1e-30))
    return t * rs, (a, rs)


def _sqrt_one_minus_sq_bwd(res, g):
    a, rs = res
    return -g * (a * a) * rs, jnp.zeros_like(a)


_sqrt_one_minus_sq.defvjp(_sqrt_one_minus_sq_fwd, _sqrt_one_minus_sq_bwd)


def _bmm_raw(a, b, form):
    r = a.ndim - 2
    con = {"nn": ((r + 1,), (r,)), "nt": ((r + 1,), (r + 1,)), "tn": ((r,), (r,))}[form]
    batch = ((0,), (0,)) if r else ((), ())
    return lax.dot_general(a.astype(BF), b.astype(BF), (con, batch), preferred_element_type=F32)


@functools.partial(jax.custom_vjp, nondiff_argnums=(2,))
def _bmm(a, b, form):
    return _bmm_raw(a, b, form)


def _bmm_fwd(a, b, form):
    return _bmm_raw(a, b, form), (a, b)


def _bmm_bwd(form, res, g):
    a, b = res
    if form == "nn":
        da, db = _bmm_raw(g, b, "nt"), _bmm_raw(a, g, "tn")
    elif form == "nt":
        da, db = _bmm_raw(g, b, "nn"), _bmm_raw(g, a, "tn")
    else:
        da, db = _bmm_raw(b, g, "nt"), _bmm_raw(a, g, "nn")
    return da.astype(a.dtype), db.astype(b.dtype)


_bmm.defvjp(_bmm_fwd, _bmm_bwd)


def _conv4(z, rows):
    out = rows[0] * _shift(z, 2)
    for k in range(1, CONV_W):
        out = out + rows[k] * _shift(z, 2 - k)
    return out


_DIMS = {"nn": ((1,), (0,)), "nt": ((1,), (1,)), "tn": ((0,), (0,))}


def _mm(a, b, mode, name, *, out_dtype=F32, epi=None, extra=None, tm=2048, tn=1024, tk=1024, b_index=None, n_cols=None,
        by_chip=None):
    if mode == "tn":
        K, M = a.shape
    else:
        M, K = a.shape
    N = n_cols if b_index is not None else (b.shape[0] if mode == "nt" else b.shape[1])
    if epi == "add":
        tn = min(tn, 512)
    tm, tn, tk = min(tm, M), min(tn, N), min(tk, K)
    assert M % tm == 0 and N % tn == 0 and K % tk == 0, (name, M, N, K)
    nk = K // tk
    a_spec = pl.BlockSpec((tk, tm), lambda i, j, k: (k, i)) if mode == "tn" else pl.BlockSpec((tm, tk), lambda i, j, k: (i, k))
    b_block = (tn, tk) if mode == "nt" else (tk, tn)
    if b_index is not None:
        b_spec = pl.BlockSpec((None,) * (b.ndim - 2) + b_block, lambda i, j, k: b_index(j, k))
    elif mode == "nt":
        b_spec = pl.BlockSpec(b_block, lambda i, j, k: (j, k))
    else:
        b_spec = pl.BlockSpec(b_block, lambda i, j, k: (k, j))
    o_spec = pl.BlockSpec((tm, tn), lambda i, j, k: (i, j))
    ins, specs = [a, b], [a_spec, b_spec]
    if epi in ("add", "relu2_bwd"):
        ins.append(extra)
        specs.append(o_spec)
    o_shape = (M, N)
    if by_chip == "cols":
        assert tn == N // N_CHIPS and epi is None
        o_shape, o_spec = (N_CHIPS, M, tn), pl.BlockSpec((None, tm, tn), lambda i, j, k: (j, i, 0))

    def body(*refs):
        a_ref, b_ref = refs[0], refs[1]
        e_ref = refs[2] if len(ins) == 3 else None
        o_ref = refs[len(ins)]

        def product():
            return lax.dot_general(a_ref[...].astype(BF), b_ref[...].astype(BF), (_DIMS[mode], ((), ())),
                                   preferred_element_type=F32)

        def finish(r):
            if epi == "relu2":
                r = jnp.square(jnp.maximum(r, 0.0))
            elif epi == "add":
                r = r + e_ref[...]
            elif epi == "relu2_bwd":
                e = e_ref[...].astype(F32)
                r = r * (2.0 * e * lax.rsqrt(jnp.maximum(e, 1e-30)))
            o_ref[...] = r.astype(out_dtype)

        if nk == 1:
            finish(product())
            return
        acc = refs[-1]
        k = pl.program_id(2)

        @pl.when(k == 0)
        def _():
            acc[...] = product()

        @pl.when(k > 0)
        def _():
            acc[...] += product()

        @pl.when(k == nk - 1)
        def _():
            finish(acc[...])

    return pl.pallas_call(
        body, name=name, grid=(M // tm, N // tn, nk), in_specs=specs, out_specs=o_spec,
        out_shape=jax.ShapeDtypeStruct(o_shape, out_dtype),
        scratch_shapes=[pltpu.VMEM((tm, tn), F32)] if nk > 1 else [],
        compiler_params=_params(("parallel", "parallel", "arbitrary")))(*ins)


def _rows_tile(T):
    return min(512, T)


def _rms_fwd(x, g, name):
    T, D = x.shape
    tr = _rows_tile(T)

    def body(x_ref, g_ref, o_ref):
        o_ref[...] = _rms(x_ref[...], g_ref[...]).astype(BF)

    return pl.pallas_call(
        body, name=name, grid=(T // tr,),
        in_specs=[pl.BlockSpec((tr, D), lambda i: (i, 0)), pl.BlockSpec((1, D), lambda i: (0, 0))],
        out_specs=pl.BlockSpec((tr, D), lambda i: (i, 0)), out_shape=jax.ShapeDtypeStruct((T, D), BF),
        compiler_params=_params(("parallel",)))(x, g.reshape(1, D))


def _rms_bwd(x, g, dhn, dres, name):
    T, D = x.shape
    tr = _rows_tile(T)

    def body(x_ref, g_ref, dhn_ref, dres_ref, dx_ref, dg_ref):
        _, vjp = jax.vjp(_rms, x_ref[...], g_ref[...])
        dx, dg = vjp(dhn_ref[...])
        dx_ref[...] = dres_ref[...] + dx

        @pl.when(pl.program_id(0) == 0)
        def _():
            dg_ref[...] = jnp.zeros_like(dg_ref)

        dg_ref[...] += dg

    row = pl.BlockSpec((tr, D), lambda i: (i, 0))
    vec = pl.BlockSpec((1, D), lambda i: (0, 0))
    dx, dg = pl.pallas_call(
        body, name=name, grid=(T // tr,), in_specs=[row, vec, row, row], out_specs=[row, vec],
        out_shape=[jax.ShapeDtypeStruct((T, D), F32), jax.ShapeDtypeStruct((1, D), F32)],
        compiler_params=_params(("arbitrary",)))(x, g.reshape(1, D), dhn, dres)
    return dx, dg.reshape(D)


def _final_loss(x, g, tgt, name):
    T, D = x.shape
    tr = _rows_tile(T)

    def body(x_ref, g_ref, t_ref, l_ref, dx_ref, dg_ref):
        y, vjp = jax.vjp(_rms, x_ref[...], g_ref[...])
        err = y - t_ref[...]
        dx, dg = vjp(err * (1.0 / D))
        dx_ref[...] = dx

        @pl.when(pl.program_id(0) == 0)
        def _():
            dg_ref[...] = jnp.zeros_like(dg_ref)
            l_ref[...] = jnp.zeros_like(l_ref)

        dg_ref[...] += dg
        l_ref[...] += (0.5 / D) * jnp.sum(jnp.sum(err * err, axis=1, keepdims=True), axis=0, keepdims=True)

    row = pl.BlockSpec((tr, D), lambda i: (i, 0))
    vec = pl.BlockSpec((1, D), lambda i: (0, 0))
    loss, dx, dg = pl.pallas_call(
        body, name=name, grid=(T // tr,), in_specs=[row, vec, row],
        out_specs=[pl.BlockSpec((1, LANES), lambda i: (0, 0)), row, vec],
        out_shape=[jax.ShapeDtypeStruct((1, LANES), F32), jax.ShapeDtypeStruct((T, D), F32),
                   jax.ShapeDtypeStruct((1, D), F32)],
        compiler_params=_params(("arbitrary",)))(x, g.reshape(1, D), tgt)
    return loss[0, 0], dx, dg.reshape(D)


def _a_pre(zx, cws, cb, gws, gbs, lams):
    xr = _conv4(zx, cws) + cb
    out = []
    for d in range(2):
        r = _sigmoid(_bmm(xr, gws[2 * d], "nn") + gbs[2 * d])
        ig = _sigmoid(_bmm(xr, gws[2 * d + 1], "nn") + gbs[2 * d + 1])
        log_a = -RG_C * r * _softplus(-lams[d])
        a = jnp.exp(log_a)
        out += [a, _sqrt_one_minus_sq(log_a, a) * ig * xr]
    return tuple(out)


def _a_post(h0, h1, zg):
    return (h0 + h1) * jax.nn.gelu(zg)


SUBLANES = 8
SCAN_TILES = 8


def _scan_jobs(jobs):
    S, C = jobs[0][0].shape
    U = min(SCAN_TILES, S // SUBLANES)
    rows = U * SUBLANES
    row = lax.broadcasted_iota(jnp.int32, (SUBLANES, C), 0)

    def prefix(a, b, reverse):
        for sh in (1, 2, 4):
            if reverse:
                m, r = row < SUBLANES - sh, SUBLANES - sh
            else:
                m, r = row >= sh, sh
            a_s = jnp.where(m, pltpu.roll(a, r, 0), 1.0)
            b_s = jnp.where(m, pltpu.roll(b, r, 0), 0.0)
            b = a * b_s + b
            a = a * a_s
        return a, b

    def step(i, carries):
        out = []
        for (a_ref, b_ref, h_ref, reverse), c in zip(jobs, carries):
            blk = (S // rows - 1 - i) if reverse else i
            t0 = pl.multiple_of(blk * rows, rows)
            order = range(U - 1, -1, -1) if reverse else range(U)
            edge = slice(0, 1) if reverse else slice(SUBLANES - 1, SUBLANES)
            for j in order:
                sl = pl.ds(t0 + j * SUBLANES, SUBLANES)
                a, b = prefix(a_ref[sl, :], b_ref[sl, :], reverse)
                h_ref[sl, :] = a * jnp.broadcast_to(c, (SUBLANES, C)) + b
                c = a[edge, :] * c + b[edge, :]
            out.append(c)
        return tuple(out)

    lax.fori_loop(0, S // rows, step, tuple(jnp.zeros((1, C), F32) for _ in jobs))


def _a_load_params(cw_ref, cb_ref, gw_ref, gb_ref, lam_ref):
    cws = [cw_ref[k:k + 1, :] for k in range(CONV_W)]
    gws = [gw_ref[d, g] for d in range(2) for g in range(2)]
    gbs = [gb_ref[d, g] for d in range(2) for g in range(2)]
    lams = [lam_ref[d:d + 1, :] for d in range(2)]
    return cws, cb_ref[...], gws, gbs, lams


def _a_in_specs(S, H):
    zg = pl.BlockSpec((None, S, LANES), lambda h, b: (b, 0, h))
    zx = pl.BlockSpec((None, S, LANES), lambda h, b: (b, 0, H + h))
    cw = pl.BlockSpec((CONV_W, LANES), lambda h, b: (0, h))
    cb = pl.BlockSpec((1, LANES), lambda h, b: (0, h))
    gw = pl.BlockSpec((2, 2, None, LANES, LANES), lambda h, b: (0, 0, h, 0, 0))
    gb = pl.BlockSpec((2, 2, None, 1, LANES), lambda h, b: (0, 0, h, 0, 0))
    lam = pl.BlockSpec((2, LANES), lambda h, b: (0, h))
    return zg, zx, cw, cb, gw, gb, lam


def _a_core_fwd(z, cw, cb, gw, gb, lam, name):
    Bq, S, D2 = z.shape
    D = D2 // 2
    H = D // LANES

    def body(zg_ref, zx_ref, cw_ref, cb_ref, gw_ref, gb_ref, lam_ref, y_ref, a_s, b_s, h_s):
        ab = _a_pre(zx_ref[...], *_a_load_params(cw_ref, cb_ref, gw_ref, gb_ref, lam_ref))
        for d in range(2):
            a_s[d] = ab[2 * d]
            b_s[d] = ab[2 * d + 1]
        _scan_jobs([(a_s.at[d], b_s.at[d], h_s.at[d], d == 1) for d in range(2)])
        y_ref[...] = _a_post(h_s[0], h_s[1], zg_ref[...]).astype(BF)

    seq = pltpu.VMEM((2, S, LANES), F32)
    return pl.pallas_call(
        body, name=name, grid=(H, Bq), in_specs=list(_a_in_specs(S, H)),
        out_specs=pl.BlockSpec((None, S, LANES), lambda h, b: (b, 0, h)),
        out_shape=jax.ShapeDtypeStruct((Bq, S, D), BF), scratch_shapes=[seq, seq, seq],
        compiler_params=_params(("parallel", "arbitrary")))(z, z, cw, cb.reshape(1, D), gw, gb.reshape(2, 2, H, 1, LANES), lam)


def _a_core_bwd(z, dy, cw, cb, gw, gb, lam, name):
    Bq, S, D2 = z.shape
    D = D2 // 2
    H = D // LANES

    def body(zg_ref, zx_ref, dy_ref, cw_ref, cb_ref, gw_ref, gb_ref, lam_ref,
             dzg_ref, dzx_ref, dcw_ref, dcb_ref, dgw_ref, dgb_ref, dlam_ref, a_s, b_s, h_s, l_s):
        prm = _a_load_params(cw_ref, cb_ref, gw_ref, gb_ref, lam_ref)
        ab = _a_pre(zx_ref[...], *prm)
        for d in range(2):
            a_s[d] = ab[2 * d]
            b_s[d] = ab[2 * d + 1]
        _scan_jobs([(a_s.at[d], b_s.at[d], h_s.at[d], d == 1) for d in range(2)])
        _, post_vjp = jax.vjp(_a_post, h_s[0], h_s[1], zg_ref[...])
        dh0, dh1, dzg = post_vjp(dy_ref[...])
        dzg_ref[...] = dzg.astype(BF)
        for d, dh in ((0, dh0), (1, dh1)):
            b_s[d] = dh
            a_s[d] = _shift(a_s[d], -1 if d == 0 else 1)
        _scan_jobs([(a_s.at[d], b_s.at[d], l_s.at[d], d == 0) for d in range(2)])
        cot = []
        for d in range(2):
            cot += [l_s[d] * _shift(h_s[d], 1 if d == 0 else -1), l_s[d]]
        _, pre_vjp = jax.vjp(_a_pre, zx_ref[...], *prm)
        dzx, dcws, dcb, dgws, dgbs, dlams = pre_vjp(tuple(cot))
        dzx_ref[...] = dzx.astype(BF)

        @pl.when(pl.program_id(1) == 0)
        def _():
            for r in (dcw_ref, dcb_ref, dgw_ref, dgb_ref, dlam_ref):
                r[...] = jnp.zeros_like(r)

        for k in range(CONV_W):
            dcw_ref[k:k + 1, :] += dcws[k]
        dcb_ref[...] += dcb
        for d in range(2):
            dlam_ref[d:d + 1, :] += dlams[d]
            for g in range(2):
                dgw_ref[d, g] += dgws[2 * d + g]
                dgb_ref[d, g] += dgbs[2 * d + g]

    zg, zx, cws, cbs, gws, gbs, lams = _a_in_specs(S, H)
    dyspec = pl.BlockSpec((None, S, LANES), lambda h, b: (b, 0, h))
    seq = pltpu.VMEM((2, S, LANES), F32)
    dzg, dzx, dcw, dcb, dgw, dgb, dlam = pl.pallas_call(
        body, name=name, grid=(H, Bq), in_specs=[zg, zx, dyspec, cws, cbs, gws, gbs, lams],
        out_specs=[dyspec, dyspec, cws, cbs, gws, gbs, lams],
        out_shape=[jax.ShapeDtypeStruct((Bq, S, D), BF), jax.ShapeDtypeStruct((Bq, S, D), BF),
                   jax.ShapeDtypeStruct((CONV_W, D), F32), jax.ShapeDtypeStruct((1, D), F32),
                   jax.ShapeDtypeStruct((2, 2, H, LANES, LANES), F32), jax.ShapeDtypeStruct((2, 2, H, 1, LANES), F32),
                   jax.ShapeDtypeStruct((2, D), F32)],
        scratch_shapes=[seq, seq, seq, seq],
        compiler_params=_params(("parallel", "arbitrary")))(z, z, dy, cw, cb.reshape(1, D), gw, gb.reshape(2, 2, H, 1, LANES), lam)
    dz = jnp.concatenate([dzg, dzx], axis=-1)
    return dz, dcw, dcb.reshape(D), dgw, dgb.reshape(2, 2, H, LANES), dlam


def _b_fn(z, lng, lnb, wss, bsf):
    D = z.shape[1] // 2
    zz = jax.nn.gelu(z)
    u, v = zz[:, :D], zz[:, D:]
    mu = jnp.mean(v, axis=-1, keepdims=True)
    var = jnp.mean(jnp.square(v - mu), axis=-1, keepdims=True)
    vn = (v - mu) * lax.rsqrt(var + 1e-5) * lng + lnb
    vs = jnp.concatenate([_bmm(wss[g], vn[:, g * LANES:(g + 1) * LANES], "nn") for g in range(D // LANES)], axis=1)
    return u * (vs + bsf)


def _b_specs(D, GB):
    row = lambda w: pl.BlockSpec((SGU_CHUNK, w), lambda i: (i, 0))
    vec = pl.BlockSpec((1, D), lambda i: (0, 0))
    ws = pl.BlockSpec((GB, SGU_CHUNK, SGU_CHUNK), lambda i: (0, 0, 0))
    bsf = pl.BlockSpec((SGU_CHUNK, D), lambda i: (0, 0))
    return row, vec, ws, bsf


def _b_core_fwd(z, lng, lnb, ws, bsf, name):
    T, D2 = z.shape
    D = D2 // 2
    GB = D // LANES
    row, vec, wspec, bspec = _b_specs(D, GB)

    def body(z_ref, lng_ref, lnb_ref, ws_ref, bsf_ref, y_ref):
        wss = [ws_ref[g] for g in range(GB)]
        y_ref[...] = _b_fn(z_ref[...], lng_ref[...], lnb_ref[...], wss, bsf_ref[...]).astype(BF)

    return pl.pallas_call(
        body, name=name, grid=(T // SGU_CHUNK,), in_specs=[row(D2), vec, vec, wspec, bspec], out_specs=row(D),
        out_shape=jax.ShapeDtypeStruct((T, D), BF), compiler_params=_params(("parallel",)))(
            z, lng.reshape(1, D), lnb.reshape(1, D), ws, bsf)


def _b_core_bwd(z, dy, lng, lnb, ws, bsf, name):
    T, D2 = z.shape
    D = D2 // 2
    GB = D // LANES
    row, vec, wspec, bspec = _b_specs(D, GB)

    def body(z_ref, dy_ref, lng_ref, lnb_ref, ws_ref, bsf_ref, dz_ref, dlng_ref, dlnb_ref, dws_ref, dbsf_ref):
        wss = [ws_ref[g] for g in range(GB)]
        _, vjp = jax.vjp(_b_fn, z_ref[...], lng_ref[...], lnb_ref[...], wss, bsf_ref[...])
        dz, dlng, dlnb, dwss, dbsf = vjp(dy_ref[...])
        dz_ref[...] = dz.astype(BF)

        @pl.when(pl.program_id(0) == 0)
        def _():
            for r in (dlng_ref, dlnb_ref, dws_ref, dbsf_ref):
                r[...] = jnp.zeros_like(r)

        dlng_ref[...] += dlng
        dlnb_ref[...] += dlnb
        dbsf_ref[...] += dbsf
        for g in range(GB):
            dws_ref[g] += dwss[g]

    dz, dlng, dlnb, dws, dbsf = pl.pallas_call(
        body, name=name, grid=(T // SGU_CHUNK,), in_specs=[row(D2), row(D), vec, vec, wspec, bspec],
        out_specs=[row(D2), vec, vec, wspec, bspec],
        out_shape=[jax.ShapeDtypeStruct((T, D2), BF), jax.ShapeDtypeStruct((1, D), F32), jax.ShapeDtypeStruct((1, D), F32),
                   jax.ShapeDtypeStruct((GB, SGU_CHUNK, SGU_CHUNK), F32), jax.ShapeDtypeStruct((SGU_CHUNK, D), F32)],
        compiler_params=_params(("arbitrary",)))(z, dy, lng.reshape(1, D), lnb.reshape(1, D), ws, bsf)
    return dz, dlng.reshape(D), dlnb.reshape(D), dws, dbsf


def _lane_is(j):
    return lax.broadcasted_iota(jnp.int32, (1, LANES), 1) == j


def _lane_col(x, j):
    return jnp.sum(jnp.where(_lane_is(j), x, 0.0), axis=1, keepdims=True)


def _c_pre(zq, zk, zv, zs, cwq, cwk, cwv, pcs, head, HC):
    q = _silu(_conv4(zq, cwq))
    k = _silu(_conv4(zk, cwk))
    v = _silu(_conv4(zv, cwv))
    q = q * lax.rsqrt(jnp.sum(q * q, axis=-1, keepdims=True) + 1e-6) * (LANES ** -0.5)
    k = k * lax.rsqrt(jnp.sum(k * k, axis=-1, keepdims=True) + 1e-6)
    gbp = jnp.zeros_like(zs)
    for d in range(2):
        a_logit = _lane_col(zs, d * HC + head)
        b_logit = _lane_col(zs, 2 * HC + d * HC + head)
        g = -jnp.exp(pcs[d]) * _softplus(a_logit + pcs[2 + d])
        beta = jnp.broadcast_to(_sigmoid(b_logit), g.shape)
        gbp = gbp + jnp.where(_lane_is(2 * d), g, 0.0) + jnp.where(_lane_is(2 * d + 1), beta, 0.0)
    return q, k, v, gbp


def _mm3(x, y):
    xh, yh = x.astype(BF), y.astype(BF)
    xl, yl = (x - xh.astype(F32)).astype(BF), (y - yh.astype(F32)).astype(BF)
    return _bmm_raw(xh, yh, "nn") + _bmm_raw(xh, yl, "nn") + _bmm_raw(xl, yh, "nn")


def _tri_inv_impl(a):
    C = a.shape[-1]
    eye = (lax.broadcasted_iota(jnp.int32, (1, C, C), 1) == lax.broadcasted_iota(jnp.int32, (1, C, C), 2)).astype(F32)
    r = eye - a
    p = a
    n = 2
    while n < C:
        p = _mm3(p, p)
        r = r + _mm3(r, p)
        n *= 2
    return r


@jax.custom_vjp
def _tri_inv(a):
    return _tri_inv_impl(a)


def _tri_inv_fwd(a):
    t = _tri_inv_impl(a)
    return t, t


def _tri_inv_bwd(t, g):
    tt = jnp.swapaxes(t, 1, 2)
    return (-_bmm_raw(_bmm_raw(tt, g, "nn"), tt, "nn"),)


_tri_inv.defvjp(_tri_inv_fwd, _tri_inv_bwd)


@jax.custom_vjp
def _pair_diff(gc3):
    m = gc3[:, :, :gc3.shape[1]]
    return m - jnp.swapaxes(m, 1, 2)


def _pair_diff_fwd(gc3):
    return _pair_diff(gc3), None


def _pair_diff_bwd(_, g):
    d = jnp.sum(g, axis=2, keepdims=True) - jnp.sum(jnp.swapaxes(g, 1, 2), axis=2, keepdims=True)
    return (jnp.broadcast_to(d * (1.0 / LANES), d.shape[:2] + (LANES,)),)


_pair_diff.defvjp(_pair_diff_fwd, _pair_diff_bwd)


@jax.custom_vjp
def _tri_inv_saved(a, t):
    return t


def _tri_inv_saved_fwd(a, t):
    return t, t


def _tri_inv_saved_bwd(t, g):
    return _tri_inv_bwd(t, g)[0], jnp.zeros_like(t)


_tri_inv_saved.defvjp(_tri_inv_saved_fwd, _tri_inv_saved_bwd)


def _c_phase1(q, k, v, gbp, rev, t_saved=None):
    S = q.shape[0]
    C = GDN_CHUNK
    N = S // C
    col = 2 if rev else 0
    gB = jnp.broadcast_to(_lane_col(gbp, col), (S, LANES))
    bB = jnp.broadcast_to(_lane_col(gbp, col + 1), (S, LANES))
    r3 = lambda t: t.reshape(N, C, LANES)
    gc3 = r3(_chunk_cumsum(gB, rev, C))
    q3, k3, v3, b3, g3 = r3(q), r3(k), r3(v), r3(bB), r3(gB)
    ri = lax.broadcasted_iota(jnp.int32, (1, C, C), 1)
    ci = lax.broadcasted_iota(jnp.int32, (1, C, C), 2)
    incl = (ri <= ci) if rev else (ri >= ci)
    strict = (ri < ci) if rev else (ri > ci)
    decay = jnp.where(incl, jnp.exp(jnp.where(incl, _pair_diff(gc3), 0.0)), 0.0)
    kb = k3 * b3
    vb = v3 * b3
    A = jnp.where(strict, _bmm(kb, k3, "nt") * decay, 0.0)
    T = _tri_inv(A) if t_saved is None else _tri_inv_saved(A, t_saved)
    egc = jnp.exp(gc3)
    u = _bmm(T, vb, "nn")
    w = _bmm(T, kb * egc, "nn")
    qk = _bmm(q3, k3, "nt") * decay
    glast = jnp.sum(g3, axis=1, keepdims=True)
    kd = k3 * jnp.exp(glast - gc3)
    k2 = _bmm(kd, w, "tn")
    z = _bmm(kd, u, "tn")
    qe2 = q3 * egc - _bmm(qk, w, "nn")
    o0 = _bmm(qk, u, "nn")
    return k2, z, jnp.exp(glast), qe2, o0, T


def _c_next_state(state, k2, z, eg):
    return state * eg - _bmm(k2, state, "nn") + z


def _c_out(state, qe2, o0):
    return _bmm(qe2, state, "nn") + o0


def _c_post(o, zg, ng):
    return _rms(o, ng) * _silu(zg)


def _c_pre_specs(S, H):
    col = lambda c0: pl.BlockSpec((None, S, LANES), lambda h, b: (b, 0, c0 * H + h))
    zs = pl.BlockSpec((None, S, LANES), lambda h, b: (b, 0, 0))
    cw = lambda c0: pl.BlockSpec((None, CONV_W, LANES), lambda h, b: (c0, 0, h))
    pc = pl.BlockSpec((None, 4, LANES), lambda h, b: (h, 0, 0))
    return col, zs, cw, pc


def _c_pre_fwd(z, zs, cw3, pc, name):
    Bq, S, D4 = z.shape
    D = D4 // 4
    H = D // LANES
    col, zss, cw, pcs = _c_pre_specs(S, H)

    def body(zq_ref, zk_ref, zv_ref, zs_ref, cwq_ref, cwk_ref, cwv_ref, pc_ref, q_ref, k_ref, v_ref, gbp_ref):
        rows = lambda r: [r[i:i + 1, :] for i in range(r.shape[0])]
        q, k, v, gbp = _c_pre(zq_ref[...], zk_ref[...], zv_ref[...], zs_ref[...], rows(cwq_ref), rows(cwk_ref),
                              rows(cwv_ref), rows(pc_ref), pl.program_id(0), H)
        q_ref[...] = q
        k_ref[...] = k
        v_ref[...] = v
        gbp_ref[...] = gbp

    out = pl.BlockSpec((None, S, LANES), lambda h, b: (b, 0, h))
    shp = jax.ShapeDtypeStruct((Bq, S, D), F32)
    return pl.pallas_call(
        body, name=name, grid=(H, Bq), in_specs=[col(0), col(1), col(2), zss, cw(0), cw(1), cw(2), pcs],
        out_specs=[out] * 4, out_shape=[shp] * 4, compiler_params=_params(("parallel", "arbitrary")))(
            z, z, z, zs, cw3, cw3, cw3, pc)


def _c_pre_bwd(z, zs, cw3, pc, dq, dk, dv, dgbp, name):
    Bq, S, D4 = z.shape
    D = D4 // 4
    H = D // LANES
    col, zss, cw, pcs = _c_pre_specs(S, H)

    def body(zq_ref, zk_ref, zv_ref, zs_ref, cwq_ref, cwk_ref, cwv_ref, pc_ref, dq_ref, dk_ref, dv_ref, dgbp_ref,
             dzq_ref, dzk_ref, dzv_ref, dzs_ref, dcw_ref, dpc_ref):
        rows = lambda r: [r[i:i + 1, :] for i in range(r.shape[0])]
        fn = functools.partial(_c_pre, head=pl.program_id(0), HC=H)
        _, vjp = jax.vjp(fn, zq_ref[...], zk_ref[...], zv_ref[...], zs_ref[...], rows(cwq_ref), rows(cwk_ref),
                         rows(cwv_ref), rows(pc_ref))
        dzq, dzk, dzv, dzs, dcwq, dcwk, dcwv, dpcs = vjp((dq_ref[...], dk_ref[...], dv_ref[...], dgbp_ref[...]))
        dzq_ref[...] = dzq.astype(BF)
        dzk_ref[...] = dzk.astype(BF)
        dzv_ref[...] = dzv.astype(BF)
        dzs_ref[...] = dzs

        @pl.when(pl.program_id(1) == 0)
        def _():
            dcw_ref[...] = jnp.zeros_like(dcw_ref)
            dpc_ref[...] = jnp.zeros_like(dpc_ref)

        for c, dc in enumerate((dcwq, dcwk, dcwv)):
            for i in range(CONV_W):
                dcw_ref[c, i:i + 1, :] += dc[i]
        for i in range(4):
            dpc_ref[i:i + 1, :] += dpcs[i]

    out = pl.BlockSpec((None, S, LANES), lambda h, b: (b, 0, h))
    dzs_spec = pl.BlockSpec((None, None, S, LANES), lambda h, b: (h, b, 0, 0))
    dcw_spec = pl.BlockSpec((3, CONV_W, LANES), lambda h, b: (0, 0, h))
    bshape = jax.ShapeDtypeStruct((Bq, S, D), BF)
    dzq, dzk, dzv, dzs, dcw3, dpc = pl.pallas_call(
        body, name=name, grid=(H, Bq),
        in_specs=[col(0), col(1), col(2), zss, cw(0), cw(1), cw(2), pcs, out, out, out, out],
        out_specs=[out, out, out, dzs_spec, dcw_spec, pcs],
        out_shape=[bshape, bshape, bshape, jax.ShapeDtypeStruct((H, Bq, S, LANES), F32),
                   jax.ShapeDtypeStruct((3, CONV_W, D), F32), jax.ShapeDtypeStruct((H, 4, LANES), F32)],
        compiler_params=_params(("parallel", "arbitrary")))(z, z, z, zs, cw3, cw3, cw3, pc, dq, dk, dv, dgbp)
    return dzq, dzk, dzv, dzs, dcw3, dpc


def _c_saved_shapes(Bq, H, S):
    N, C = S // GDN_CHUNK, GDN_CHUNK
    return [(Bq, H, 2, N, LANES, LANES), (Bq, H, 2, N, LANES, LANES), (Bq, H, 2, N, 1, LANES), (Bq, H, 2, N, C, LANES),
            (Bq, H, 2, N, C, C)]


def _c_saved_scratch(S):
    return [pltpu.VMEM(shp[3:], F32) for shp in _c_saved_shapes(1, 1, S)]


PHASE1_CHUNKS = 8


def _c_blocks(S, fn):
    nb = min(PHASE1_CHUNKS, S // GDN_CHUNK)
    rows = nb * GDN_CHUNK

    def blk(i, carry):
        fn(pl.ds(pl.multiple_of(i * rows, rows), rows), pl.ds(pl.multiple_of(i * nb, nb), nb))
        return carry

    lax.fori_loop(0, S // rows, blk, 0)


def _c_phase1_blocks(in_refs, k2_ref, z_ref, eg_ref, qe2_ref, t_ref, o_ref, rev):
    def fn(rows, chunks):
        k2, z, eg, qe2, o0, t = _c_phase1(*[r[rows, :] for r in in_refs], rev)
        k2_ref[chunks] = k2
        z_ref[chunks] = z
        eg_ref[chunks] = eg
        qe2_ref[chunks] = qe2
        t_ref[chunks] = t
        o_ref[chunks] += o0

    _c_blocks(in_refs[0].shape[0], fn)


def _c_sweep(k2_ref, z_ref, eg_ref, st_ref, rev):
    N = st_ref.shape[0]

    def step(i, state):
        n = (N - 1 - i) if rev else i
        st_ref[n] = state
        return _c_next_state(state, k2_ref[n], z_ref[n], eg_ref[n])

    lax.fori_loop(0, N, step, jnp.zeros((LANES, LANES), F32))


def _c_sweep_adjoint(k2_ref, eg_ref, dso_ref, gs_ref, rev):
    N = gs_ref.shape[0]

    def step(i, g):
        n = i if rev else (N - 1 - i)
        gs_ref[n] = g
        return dso_ref[n] + g * eg_ref[n] - _bmm_raw(k2_ref[n], g, "tn")

    lax.fori_loop(0, N, step, jnp.zeros((LANES, LANES), F32))


def _c_mid_fwd(q, k, v, gbp, name):
    Bq, S, D = q.shape
    H = D // LANES
    N = S // GDN_CHUNK
    blk = pl.BlockSpec((None, S, LANES), lambda h, b: (b, 0, h))
    blk3 = pl.BlockSpec((None, N, GDN_CHUNK, LANES), lambda h, b: (b, 0, 0, h))
    n_saved = len(_c_saved_shapes(Bq, H, S))

    def body(q_ref, k_ref, v_ref, gbp_ref, o3, *rest):
        saved_hbm, scr = rest[:n_saved], rest[n_saved:]
        per_dir = n_saved + 1
        sems = scr[2 * per_dir]
        o3[...] = jnp.zeros_like(o3)
        copies = []
        for d, rev in enumerate((False, True)):
            k2_ref, st_ref, eg_ref, qe2_ref, t_ref, z_ref = scr[d * per_dir:(d + 1) * per_dir]
            _c_phase1_blocks((q_ref, k_ref, v_ref, gbp_ref), k2_ref, z_ref, eg_ref, qe2_ref, t_ref, o3, rev)
            _c_sweep(k2_ref, z_ref, eg_ref, st_ref, rev)

            def add_out(rows, chunks):
                o3[chunks] += _bmm_raw(qe2_ref[chunks], st_ref[chunks], "nn")

            _c_blocks(S, add_out)
            for i, (src, dst) in enumerate(zip((k2_ref, st_ref, eg_ref, qe2_ref, t_ref), saved_hbm)):
                copies.append(pltpu.make_async_copy(src, dst.at[pl.program_id(1), pl.program_id(0), d], sems.at[d, i]))
                copies[-1].start()
        for cp in copies:
            cp.wait()

    one_dir = _c_saved_scratch(S) + [pltpu.VMEM((N, LANES, LANES), F32)]
    outs = pl.pallas_call(
        body, name=name, grid=(H, Bq), in_specs=[blk] * 4,
        out_specs=[blk3] + [pl.BlockSpec(memory_space=pltpu.HBM)] * n_saved,
        out_shape=[jax.ShapeDtypeStruct((Bq, N, GDN_CHUNK, D), F32)]
        + [jax.ShapeDtypeStruct(shp, F32) for shp in _c_saved_shapes(Bq, H, S)],
        scratch_shapes=one_dir + one_dir + [pltpu.SemaphoreType.DMA((2, n_saved))],
        compiler_params=_params(("parallel", "parallel")))(q, k, v, gbp)
    return outs[0].reshape(Bq, S, D), tuple(outs[1:])


def _c_mid_bwd(q, k, v, gbp, do, saved, name):
    Bq, S, D = q.shape
    H = D // LANES
    N = S // GDN_CHUNK
    blk = pl.BlockSpec((None, S, LANES), lambda h, b: (b, 0, h))
    blk3 = pl.BlockSpec((None, N, GDN_CHUNK, LANES), lambda h, b: (b, 0, 0, h))
    n_saved = len(saved)

    def body(q_ref, k_ref, v_ref, gbp_ref, do3, *rest):
        saved_hbm = rest[:n_saved]
        dq_ref, dk_ref, dv_ref, dgbp_ref = rest[n_saved:n_saved + 4]
        scr = rest[n_saved + 4:]
        sets = (scr[:n_saved], scr[n_saved:2 * n_saved])
        dso_ref, gs_ref, sems = scr[2 * n_saved:]
        in_refs = (q_ref, k_ref, v_ref, gbp_ref)
        out_refs = (dq_ref, dk_ref, dv_ref, dgbp_ref)
        copies = [[pltpu.make_async_copy(src.at[pl.program_id(1), pl.program_id(0), d], dst, sems.at[d, i])
                   for i, (src, dst) in enumerate(zip(saved_hbm, sets[d]))] for d in range(2)]
        for cp in copies[0] + copies[1]:
            cp.start()
        for d, rev in enumerate((False, True)):
            k2_ref, st_ref, eg_ref, qe2_ref, t_ref = sets[d]
            for cp in copies[d]:
                cp.wait()

            def out_to_state(rows, chunks):
                dso_ref[chunks] = _bmm_raw(qe2_ref[chunks], do3[chunks], "tn")

            _c_blocks(S, out_to_state)
            _c_sweep_adjoint(k2_ref, eg_ref, dso_ref, gs_ref, rev)

            def block_vjp(rows, chunks):
                states, t_saved = st_ref[chunks], t_ref[chunks]

                def chunk_fn(q_, k_, v_, gbp_):
                    k2, z, eg, qe2, o0, _ = _c_phase1(q_, k_, v_, gbp_, rev, t_saved)
                    return _c_next_state(states, k2, z, eg), _c_out(states, qe2, o0)

                _, vjp = jax.vjp(chunk_fn, *[r[rows, :] for r in in_refs])
                for r, c in zip(out_refs, vjp((gs_ref[chunks], do3[chunks]))):
                    if rev:
                        r[rows, :] += c
                    else:
                        r[rows, :] = c

            _c_blocks(S, block_vjp)

    shp = jax.ShapeDtypeStruct((Bq, S, D), F32)
    mat = pltpu.VMEM((N, LANES, LANES), F32)
    return pl.pallas_call(
        body, name=name, grid=(H, Bq), in_specs=[blk] * 4 + [blk3] + [pl.BlockSpec(memory_space=pltpu.HBM)] * n_saved,
        out_specs=[blk] * 4, out_shape=[shp] * 4,
        scratch_shapes=_c_saved_scratch(S) + _c_saved_scratch(S) + [mat, mat, pltpu.SemaphoreType.DMA((2, n_saved))],
        compiler_params=_params(("parallel", "parallel")))(q, k, v, gbp, do.reshape(Bq, N, GDN_CHUNK, D), *saved)


def _c_post_fwd(o, z, ng, name):
    Bq, S, D = o.shape
    H = D // LANES
    blk = pl.BlockSpec((None, S, LANES), lambda h, b: (b, 0, h))
    gate = pl.BlockSpec((None, S, LANES), lambda h, b: (b, 0, 3 * H + h))
    vec = pl.BlockSpec((1, LANES), lambda h, b: (0, 0))

    def body(o_ref, zg_ref, ng_ref, y_ref):
        y_ref[...] = _c_post(o_ref[...], zg_ref[...], ng_ref[...]).astype(BF)

    return pl.pallas_call(
        body, name=name, grid=(H, Bq), in_specs=[blk, gate, vec], out_specs=blk,
        out_shape=jax.ShapeDtypeStruct((Bq, S, D), BF), compiler_params=_params(("parallel", "parallel")))(
            o, z, ng.reshape(1, LANES))


def _c_post_bwd(o, z, ng, dy, name):
    Bq, S, D = o.shape
    H = D // LANES
    blk = pl.BlockSpec((None, S, LANES), lambda b, h: (b, 0, h))
    gate = pl.BlockSpec((None, S, LANES), lambda b, h: (b, 0, 3 * H + h))
    vec = pl.BlockSpec((1, LANES), lambda b, h: (0, 0))

    def body(o_ref, zg_ref, ng_ref, dy_ref, do_ref, dzg_ref, dng_ref):
        _, vjp = jax.vjp(_c_post, o_ref[...], zg_ref[...], ng_ref[...])
        do, dzg, dng = vjp(dy_ref[...])
        do_ref[...] = do
        dzg_ref[...] = dzg.astype(BF)

        @pl.when((pl.program_id(0) == 0) & (pl.program_id(1) == 0))
        def _():
            dng_ref[...] = jnp.zeros_like(dng_ref)

        dng_ref[...] += dng

    do, dzg, dng = pl.pallas_call(
        body, name=name, grid=(Bq, H), in_specs=[blk, gate, vec, blk], out_specs=[blk, blk, vec],
        out_shape=[jax.ShapeDtypeStruct((Bq, S, D), F32), jax.ShapeDtypeStruct((Bq, S, D), BF),
                   jax.ShapeDtypeStruct((1, LANES), F32)],
        compiler_params=_params(("arbitrary", "arbitrary")))(o, z, ng.reshape(1, LANES), dy)
    return do, dzg, dng.reshape(LANES)


def _c_param_rows(a_log, dt_bias):
    p = jnp.concatenate([a_log, dt_bias], axis=0).T
    return jnp.broadcast_to(p[:, :, None], p.shape + (LANES,)).astype(F32)


def _local_step(x, tgt, W, by_chip=False):
    Bq, S, D = x.shape
    T = Bq * S
    H = D // LANES
    L = W["norm_mix_g"].shape[0]
    seq = lambda t: t.reshape(Bq, S, t.shape[-1])
    flat = lambda t: t.reshape(T, t.shape[-1])
    if "mlp_up_slots" in W:
        up4, down4 = W["mlp_up_slots"], W["mlp_down_slots"]
    else:
        up4 = W["mlp_w_up"].reshape(L, D, N_CHIPS, -1).transpose(2, 0, 1, 3)
        down4 = W["mlp_w_down"].reshape(L, N_CHIPS, -1, D).transpose(1, 0, 2, 3)
    sw = up4.shape[-1]
    F = N_CHIPS * sw

    xs = flat(x)
    saved = []
    for i in range(L):
        kind, j = i % N_MIXERS, i // N_MIXERS
        tag = f"l{i}"
        sv = {"x": xs}
        hn = _rms_fwd(xs, W["norm_mix_g"][i], f"{tag}_mix_norm")
        sv["hn"] = hn
        if kind == 0:
            z = _mm(hn, W["a_w_in"][j], "nn", f"{tag}_a_in")
            y = _a_core_fwd(seq(z), W["a_conv_w"][j], W["a_conv_b"][j], W["a_gate_w"][j], W["a_gate_b"][j],
                            W["a_lambda"][j], f"{tag}_a_core")
            sv["z"] = z
            w_out = W["a_w_out"][j]
        elif kind == 1:
            z = _mm(hn, W["b_w_in"][j], "nn", f"{tag}_b_in")
            bsf = jnp.repeat(W["b_b_s"][j].T, LANES, axis=1)
            y = _b_core_fwd(z, W["b_ln_g"][j], W["b_ln_b"][j], W["b_w_s"][j], bsf, f"{tag}_b_core")
            sv["z"], sv["bsf"] = z, bsf
            w_out = W["b_w_out"][j]
        else:
            w_in = W["c_w_in"][j]
            w_small = jnp.pad(w_in[:, 4 * D:], ((0, 0), (0, LANES - 4 * H)))
            z = _mm(hn, w_in[:, :4 * D], "nn", f"{tag}_c_in")
            zs = _mm(hn, w_small, "nn", f"{tag}_c_in_small")
            cw3 = W["c_conv_w"][j].reshape(CONV_W, 3, D).transpose(1, 0, 2)
            pc = _c_param_rows(W["c_a_log"][j], W["c_dt_bias"][j])
            q, k, v, gbp = _c_pre_fwd(seq(z), seq(zs), cw3, pc, f"{tag}_c_pre")
            o, sv["mid"] = _c_mid_fwd(q, k, v, gbp, f"{tag}_c_mid")
            y = _c_post_fwd(o, seq(z), W["c_norm_g"][j], f"{tag}_c_post")
            sv.update(z=z, zs=zs, cw3=cw3, pc=pc, q=q, k=k, v=v, gbp=gbp, o=o, w_small=w_small)
            w_out = W["c_w_out"][j]
        y = flat(y)
        sv["y"] = y
        x1 = _mm(y, w_out, "nn", f"{tag}_mix_out", epi="add", extra=xs)
        sv["x1"] = x1
        hn2 = _rms_fwd(x1, W["norm_mlp_g"][i], f"{tag}_mlp_norm")
        act = _mm(hn2, up4, "nn", f"{tag}_mlp_up", out_dtype=BF, epi="relu2", tm=2048, tn=sw, tk=D, n_cols=F,
                  b_index=lambda j, k, i=i: (j, i, 0, 0))
        xs = _mm(act, down4, "nn", f"{tag}_mlp_down", epi="add", extra=x1, tn=D, tk=sw, n_cols=D,
                 b_index=lambda j, k, i=i: (k, i, 0, j))
        sv["hn2"], sv["act"] = hn2, act
        saved.append(sv)

    loss, dx, dgf = _final_loss(xs, W["norm_final_g"], flat(tgt), "final_loss")

    G = {"norm_final_g": dgf}
    per_layer = {n: [None] * L for n in ("norm_mix_g", "norm_mlp_g", "mlp_w_up", "mlp_w_down")}
    mixer = {}
    for i in reversed(range(L)):
        kind, j = i % N_MIXERS, i // N_MIXERS
        tag = f"l{i}"
        sv = saved[i]
        dhid = _mm(dx, down4, "nt", f"{tag}_mlp_dhid", out_dtype=BF, epi="relu2_bwd", extra=sv["act"], tm=2048, tn=sw, tk=D,
                   n_cols=F, b_index=lambda j, k, i=i: (j, i, 0, 0))
        per_layer["mlp_w_down"][i] = _mm(sv["act"], dx, "tn", f"{tag}_mlp_dwdown").reshape(N_CHIPS, sw, D)
        per_layer["mlp_w_up"][i] = _mm(sv["hn2"], dhid, "tn", f"{tag}_mlp_dwup", tn=sw, by_chip="cols")
        dhn2 = _mm(dhid, up4, "nt", f"{tag}_mlp_dhn", tn=D, tk=sw, n_cols=D, b_index=lambda j, k, i=i: (k, i, j, 0))
        dx, per_layer["norm_mlp_g"][i] = _rms_bwd(sv["x1"], W["norm_mlp_g"][i], dhn2, dx, f"{tag}_mlp_norm_bwd")
        g = {}
        if kind == 0:
            dy = _mm(dx, W["a_w_out"][j], "nt", f"{tag}_a_dy")
            g["a_w_out"] = _mm(sv["y"], dx, "tn", f"{tag}_a_dwout")
            dz, g["a_conv_w"], g["a_conv_b"], g["a_gate_w"], g["a_gate_b"], g["a_lambda"] = _a_core_bwd(
                seq(sv["z"]), seq(dy), W["a_conv_w"][j], W["a_conv_b"][j], W["a_gate_w"][j], W["a_gate_b"][j],
                W["a_lambda"][j], f"{tag}_a_core_bwd")
            dz = flat(dz)
            g["a_w_in"] = _mm(sv["hn"], dz, "tn", f"{tag}_a_dwin")
            dhn = _mm(dz, W["a_w_in"][j], "nt", f"{tag}_a_dhn")
        elif kind == 1:
            dy = _mm(dx, W["b_w_out"][j], "nt", f"{tag}_b_dy")
            g["b_w_out"] = _mm(sv["y"], dx, "tn", f"{tag}_b_dwout")
            dz, g["b_ln_g"], g["b_ln_b"], g["b_w_s"], dbsf = _b_core_bwd(
                sv["z"], dy, W["b_ln_g"][j], W["b_ln_b"][j], W["b_w_s"][j], sv["bsf"], f"{tag}_b_core_bwd")
            g["b_b_s"] = dbsf.reshape(SGU_CHUNK, H, LANES).sum(-1).T
            g["b_w_in"] = _mm(sv["hn"], dz, "tn", f"{tag}_b_dwin")
            dhn = _mm(dz, W["b_w_in"][j], "nt", f"{tag}_b_dhn")
        else:
            dy = _mm(dx, W["c_w_out"][j], "nt", f"{tag}_c_dy")
            g["c_w_out"] = _mm(sv["y"], dx, "tn", f"{tag}_c_dwout")
            do, dzg, g["c_norm_g"] = _c_post_bwd(sv["o"], seq(sv["z"]), W["c_norm_g"][j], seq(dy), f"{tag}_c_post_bwd")
            dq, dk, dv, dgbp = _c_mid_bwd(sv["q"], sv["k"], sv["v"], sv["gbp"], do, sv["mid"], f"{tag}_c_mid_bwd")
            dzq, dzk, dzv, dzs_h, dcw3, dpc = _c_pre_bwd(seq(sv["z"]), seq(sv["zs"]), sv["cw3"], sv["pc"], dq, dk, dv, dgbp,
                                                         f"{tag}_c_pre_bwd")
            dz = flat(jnp.concatenate([dzq, dzk, dzv, dzg], axis=-1))
            dzs = flat(dzs_h.sum(0)).astype(BF)
            g["c_conv_w"] = dcw3.transpose(1, 0, 2).reshape(CONV_W, 3 * D)
            dpc = dpc.sum(-1)
            g["c_a_log"], g["c_dt_bias"] = dpc[:, :2].T, dpc[:, 2:].T
            dw_main = _mm(sv["hn"], dz, "tn", f"{tag}_c_dwin")
            dw_small = _mm(sv["hn"], dzs, "tn", f"{tag}_c_dwin_small")
            g["c_w_in"] = jnp.concatenate([dw_main, dw_small[:, :4 * H]], axis=1)
            dhn = _mm(dz, W["c_w_in"][j][:, :4 * D], "nt", f"{tag}_c_dhn")
            dhn = _mm(dzs, sv["w_small"], "nt", f"{tag}_c_dhn_small", epi="add", extra=dhn)
        dx, per_layer["norm_mix_g"][i] = _rms_bwd(sv["x"], W["norm_mix_g"][i], dhn, dx, f"{tag}_mix_norm_bwd")
        for n, val in g.items():
            mixer.setdefault(n, {})[j] = val

    for n, vals in per_layer.items():
        if n not in ("mlp_w_up", "mlp_w_down"):
            G[n] = jnp.stack(vals)
        elif by_chip:
            G[n] = vals
        elif n == "mlp_w_up":
            G[n] = jnp.stack(vals).transpose(0, 2, 1, 3).reshape(L, D, F)
        else:
            G[n] = jnp.stack(vals).reshape(L, F, D)
    for n, by_j in mixer.items():
        G[n] = jnp.stack([by_j[j] for j in sorted(by_j)])
    return loss, dx.reshape(Bq, S, D), G


MESH = pl.DeviceIdType.MESH
N_CHIPS = 4
HBM_SPEC = pl.BlockSpec(memory_space=pltpu.HBM)


def _place():
    x, y, c = lax.axis_index("x"), lax.axis_index("y"), lax.axis_index("c")
    others = [(1 - x, y), (x, 1 - y), (1 - x, 1 - y)]
    return x, y, c, others


def _all_gather_xy(bufs, name):
    n = len(bufs)
    pieces = [_stage_rows(b.shape[0], b.shape[1] * b.dtype.itemsize) for b in bufs]

    def body(*refs):
        ins, outs = refs[:n], refs[n:2 * n]
        send, recv, fsend, frecv = refs[2 * n:2 * n + 4]
        stages = refs[2 * n + 4:]
        x, y, c, others = _place()
        p = 2 * x + y
        half = lambda b, cc: pl.ds(cc * (ins[b].shape[0] // 2), ins[b].shape[0] // 2)

        def ici(b, j):
            qx, qy = others[j]
            return pltpu.make_async_remote_copy(
                src_ref=ins[b].at[half(b, c)], dst_ref=outs[b].at[p, half(b, c)], send_sem=send.at[b, j],
                recv_sem=recv.at[b, j], device_id=(qx, qy, c), device_id_type=MESH)

        def landed(b, j, cc):
            qx, qy = others[j]
            return outs[b].at[2 * qx + qy, half(b, cc)]

        def d2d(b, j):
            return pltpu.make_async_remote_copy(
                src_ref=landed(b, j, c), dst_ref=landed(b, j, c), send_sem=fsend.at[b, j], recv_sem=frecv.at[b, j],
                device_id=(x, y, 1 - c), device_id_type=MESH)

        pairs = [(b, j) for b in range(n) for j in range(3)]
        for b, j in pairs:
            ici(b, j).start()
        for b in range(n):
            def own_piece(i, carry, b=b):
                rows = pl.ds(pl.multiple_of(i * pieces[b], pieces[b]), pieces[b])
                pltpu.sync_copy(ins[b].at[rows], stages[b])
                pltpu.sync_copy(stages[b], outs[b].at[p, rows])
                return carry

            lax.fori_loop(0, ins[b].shape[0] // pieces[b], own_piece, 0)
        for b, j in pairs:
            pltpu.make_async_remote_copy(
                src_ref=ins[b].at[half(b, c)], dst_ref=landed(b, j, c), send_sem=send.at[b, j], recv_sem=recv.at[b, j],
                device_id=(x, y, c), device_id_type=MESH).wait_recv()
            d2d(b, j).start()
        for b, j in pairs:
            pltpu.make_async_remote_copy(
                src_ref=landed(b, j, 1 - c), dst_ref=landed(b, j, 1 - c), send_sem=fsend.at[b, j], recv_sem=frecv.at[b, j],
                device_id=(x, y, 1 - c), device_id_type=MESH).wait_recv()
        for b, j in pairs:
            ici(b, j).wait_send()
            d2d(b, j).wait_send()

    return pl.pallas_call(
        body, name=name, in_specs=[HBM_SPEC] * n, out_specs=[HBM_SPEC] * n,
        out_shape=[jax.ShapeDtypeStruct((N_CHIPS,) + b.shape, b.dtype) for b in bufs],
        scratch_shapes=[pltpu.SemaphoreType.DMA((n, 3))] * 4
        + [pltpu.VMEM((r, b.shape[1]), b.dtype) for r, b in zip(pieces, bufs)],
        compiler_params=pltpu.CompilerParams(has_side_effects=True, vmem_limit_bytes=VMEM_LIMIT))(*bufs)


STAGE_BYTES = 2 * 1024 * 1024


def _stage_rows(rows, row_bytes):
    for d in range(min(rows, max(1, STAGE_BYTES // row_bytes)), 0, -1):
        if rows % d == 0 and (d % 16 == 0 or d == rows):
            return d
    return rows


def _swap_halves(gs, name):
    n = len(gs)

    def body(*refs):
        g_refs, o_refs, send, recv = refs[:n], refs[n:2 * n], refs[2 * n], refs[2 * n + 1]
        x, y, c, _ = _place()
        cps = [pltpu.make_async_remote_copy(src_ref=g_refs[b].at[:, 1 - c], dst_ref=o_refs[b], send_sem=send.at[b],
                                            recv_sem=recv.at[b], device_id=(x, y, 1 - c), device_id_type=MESH)
               for b in range(n)]
        for cp in cps:
            cp.start()
        for cp in cps:
            cp.wait()

    return pl.pallas_call(
        body, name=name, in_specs=[HBM_SPEC] * n, out_specs=[HBM_SPEC] * n,
        out_shape=[jax.ShapeDtypeStruct((g.shape[0],) + g.shape[2:], g.dtype) for g in gs],
        scratch_shapes=[pltpu.SemaphoreType.DMA((n,)), pltpu.SemaphoreType.DMA((n,))],
        compiler_params=pltpu.CompilerParams(has_side_effects=True))(*gs)


def _pair_sum(g, got, out_dtype, name):
    nq, _, h, cols = g.shape
    tr = min(512, h)

    def body(c_ref, g_ref, r_ref, o_ref):
        o_ref[...] = (g_ref[...] + r_ref[...]).astype(o_ref.dtype)

    spec = pl.BlockSpec((None, tr, cols), lambda q, i, c_ref: (q, i, 0))
    return pl.pallas_call(
        body, name=name,
        grid_spec=pltpu.PrefetchScalarGridSpec(
            num_scalar_prefetch=1, grid=(nq, h // tr),
            in_specs=[pl.BlockSpec((None, None, tr, cols), lambda q, i, c_ref: (q, c_ref[0], i, 0)), spec], out_specs=spec),
        out_shape=jax.ShapeDtypeStruct((nq, h, cols), out_dtype),
        compiler_params=_params(("parallel", "parallel")))(lax.axis_index("c").astype(jnp.int32).reshape(1), g, got)


def _scatter_xy(p_sums, name):
    n = len(p_sums)

    def body(*refs):
        p_refs, o_refs, send, recv = refs[:n], refs[n:2 * n], refs[2 * n], refs[2 * n + 1]
        x, y, c, others = _place()
        me = 2 * x + y
        cps = []
        for b in range(n):
            for j, (qx, qy) in enumerate(others):
                cps.append(pltpu.make_async_remote_copy(
                    src_ref=p_refs[b].at[2 * qx + qy], dst_ref=o_refs[b].at[me], send_sem=send.at[b, j],
                    recv_sem=recv.at[b, j], device_id=(qx, qy, c), device_id_type=MESH))
                cps[-1].start()
        for b in range(n):
            for j, (qx, qy) in enumerate(others):
                pltpu.make_async_remote_copy(
                    src_ref=p_refs[b].at[me], dst_ref=o_refs[b].at[2 * qx + qy], send_sem=send.at[b, j],
                    recv_sem=recv.at[b, j], device_id=(qx, qy, c), device_id_type=MESH).wait_recv()
        for cp in cps:
            cp.wait_send()

    return pl.pallas_call(
        body, name=name, in_specs=[HBM_SPEC] * n, out_specs=[HBM_SPEC] * n,
        out_shape=[jax.ShapeDtypeStruct(p.shape, p.dtype) for p in p_sums],
        scratch_shapes=[pltpu.SemaphoreType.DMA((n, 3)), pltpu.SemaphoreType.DMA((n, 3))],
        compiler_params=pltpu.CompilerParams(has_side_effects=True))(*p_sums)


def _chip_sum(r4, p_sum, name):
    nq, h, cols = r4.shape
    tr = min(512, h)

    def body(r_ref, p_ref, o_ref):
        me = 2 * lax.axis_index("x") + lax.axis_index("y")
        f = lambda q: jnp.where(me == q, p_ref[q], r_ref[q]).astype(F32)
        o_ref[...] = ((f(0) + f(1)) + f(2)) + f(3)

    spec = pl.BlockSpec((nq, tr, cols), lambda i: (0, i, 0))
    return pl.pallas_call(
        body, name=name, grid=(h // tr,), in_specs=[spec, spec],
        out_specs=pl.BlockSpec((tr, cols), lambda i: (i, 0)), out_shape=jax.ShapeDtypeStruct((h, cols), F32),
        compiler_params=_params(("parallel",)))(r4, p_sum)


def _join_halves(rs, name):
    n = len(rs)

    def body(*refs):
        r_refs, o_refs, send, recv = refs[:n], refs[n:2 * n], refs[2 * n], refs[2 * n + 1]
        x, y, c, _ = _place()
        cps = [pltpu.make_async_remote_copy(src_ref=r_refs[b], dst_ref=o_refs[b].at[c], send_sem=send.at[b],
                                            recv_sem=recv.at[b], device_id=(x, y, 1 - c), device_id_type=MESH)
               for b in range(n)]
        for cp in cps:
            cp.start()
        for b in range(n):
            pltpu.make_async_remote_copy(src_ref=r_refs[b], dst_ref=o_refs[b].at[1 - c], send_sem=send.at[b],
                                         recv_sem=recv.at[b], device_id=(x, y, 1 - c), device_id_type=MESH).wait_recv()
        for cp in cps:
            cp.wait_send()

    gots = pl.pallas_call(
        body, name=name, in_specs=[HBM_SPEC] * n, out_specs=[HBM_SPEC] * n,
        out_shape=[jax.ShapeDtypeStruct((2,) + r.shape, r.dtype) for r in rs],
        scratch_shapes=[pltpu.SemaphoreType.DMA((n,)), pltpu.SemaphoreType.DMA((n,))],
        compiler_params=pltpu.CompilerParams(has_side_effects=True))(*rs)
    c = lax.axis_index("c")
    return [jnp.stack([jnp.where(c == s, r, got[s]) for s in range(2)]) for r, got in zip(rs, gots)]


def _reduce_scatter(gs, tag):
    gs = [g.reshape(g.shape[0], 2, g.shape[1] // 2, g.shape[2]) for g in gs]
    gots = _swap_halves(gs, f"{tag}_swap")
    pairs = [_pair_sum(g, got, BF, f"{tag}_pair_sum{i}") for i, (g, got) in enumerate(zip(gs, gots))]
    r4s = _scatter_xy(pairs, f"{tag}_scatter")
    rs = [_chip_sum(r4, pair, f"{tag}_chip_sum{i}") for i, (r4, pair) in enumerate(zip(r4s, pairs))]
    return [j.reshape(-1, j.shape[-1]) for j in _join_halves(rs, f"{tag}_join")]


def _adamw(w, g, m, v, name):
    rows, cols = w.shape
    tr = rows
    for cand in (512, 344, 256, 128, 64, 32, 16, 8):
        if rows % cand == 0:
            tr = cand
            break

    def body(w_ref, g_ref, m_ref, v_ref, d_ref, nm_ref, nv_ref):
        g_ = g_ref[...]
        m_ = ADAM_B1 * m_ref[...] + (1.0 - ADAM_B1) * g_
        v_ = ADAM_B2 * v_ref[...] + (1.0 - ADAM_B2) * jnp.square(g_)
        m_hat = m_ / (1.0 - ADAM_B1 ** ADAM_STEP)
        v_hat = v_ / (1.0 - ADAM_B2 ** ADAM_STEP)
        d_ref[...] = -ADAM_LR * (m_hat / (jnp.sqrt(v_hat) + ADAM_EPS) + ADAM_WD * w_ref[...])
        nm_ref[...] = m_
        nv_ref[...] = v_

    spec = pl.BlockSpec((tr, cols), lambda i: (i, 0))
    shp = jax.ShapeDtypeStruct((rows, cols), F32)
    return pl.pallas_call(body, name=name, grid=(rows // tr,), in_specs=[spec] * 4, out_specs=[spec] * 3,
                          out_shape=[shp] * 3, compiler_params=_params(("parallel",)))(w, g, m, v)


WEIGHTS = ["norm_mix_g", "norm_mlp_g", "mlp_w_up", "mlp_w_down", "norm_final_g", "a_w_in", "a_conv_w", "a_conv_b",
           "a_gate_w", "a_gate_b", "a_lambda", "a_w_out", "b_w_in", "b_ln_g", "b_ln_b", "b_w_s", "b_b_s", "b_w_out",
           "c_w_in", "c_conv_w", "c_a_log", "c_dt_bias", "c_norm_g", "c_w_out"]
SHARD_AXIS = {"mlp_w_up": 2, "mlp_w_down": 1, "a_w_in": 2, "a_conv_w": 2, "a_conv_b": 1, "a_lambda": 2, "a_w_out": 1,
              "b_w_in": 2, "b_w_out": 1, "c_w_in": 2, "c_conv_w": 2, "c_w_out": 1}
MATMUL_WEIGHTS = ["mlp_w_up", "mlp_w_down", "a_w_in", "a_w_out", "b_w_in", "b_w_out", "c_w_out", "c_w_in"]
SMALL_SHARDED = ["a_conv_w", "a_conv_b", "a_lambda", "c_conv_w"]
REPLICATED = [n for n in WEIGHTS if n not in SHARD_AXIS]
FLAT_COLS = 1024


def _rows_of(shape):
    n = 1
    for s in shape:
        n *= s
    return -(-n // FLAT_COLS)


def _pack(arrays, total_rows, dtype):
    parts = []
    used = 0
    for a in arrays:
        r = _rows_of(a.shape)
        f = a.reshape(-1).astype(dtype)
        parts.append(jnp.pad(f, (0, r * FLAT_COLS - f.shape[0])).reshape(r, FLAT_COLS))
        used += r
    if total_rows > used:
        parts.append(jnp.zeros((total_rows - used, FLAT_COLS), dtype))
    return jnp.concatenate(parts, axis=0)


def _slots_of(g, axis):
    w = g.shape[axis] // N_CHIPS
    flat = jnp.stack([lax.slice_in_dim(g, q * w, (q + 1) * w, axis=axis).reshape(-1) for q in range(N_CHIPS)])
    rows = _rows_of(flat.shape[1:])
    return jnp.pad(flat, ((0, 0), (0, rows * FLAT_COLS - flat.shape[1]))).reshape(N_CHIPS, rows, FLAT_COLS)


def _unpack(buf, shapes):
    out, r0 = [], 0
    for shp in shapes:
        r = _rows_of(shp)
        n = 1
        for s in shp:
            n *= s
        out.append(buf[r0:r0 + r].reshape(-1)[:n].reshape(shp))
        r0 += r
    return out


def _round_up(n, m):
    return -(-n // m) * m


def kernel(x, norm_mix_g, norm_mlp_g, mlp_w_up, mlp_w_down, norm_final_g, a_w_in, a_conv_w, a_conv_b, a_gate_w, a_gate_b, a_lambda, a_w_out, b_w_in, b_ln_g, b_ln_b, b_w_s, b_b_s, b_w_out, c_w_in, c_conv_w, c_a_log, c_dt_bias, c_norm_g, c_w_out, loss_target, m_norm_mix_g, m_norm_mlp_g, m_mlp_w_up, m_mlp_w_down, m_norm_final_g, m_a_w_in, m_a_conv_w, m_a_conv_b, m_a_gate_w, m_a_gate_b, m_a_lambda, m_a_w_out, m_b_w_in, m_b_ln_g, m_b_ln_b, m_b_w_s, m_b_b_s, m_b_w_out, m_c_w_in, m_c_conv_w, m_c_a_log, m_c_dt_bias, m_c_norm_g, m_c_w_out, v_norm_mix_g, v_norm_mlp_g, v_mlp_w_up, v_mlp_w_down, v_norm_final_g, v_a_w_in, v_a_conv_w, v_a_conv_b, v_a_gate_w, v_a_gate_b, v_a_lambda, v_a_w_out, v_b_w_in, v_b_ln_g, v_b_ln_b, v_b_w_s, v_b_b_s, v_b_w_out, v_c_w_in, v_c_conv_w, v_c_a_log, v_c_dt_bias, v_c_norm_g, v_c_w_out):
    given = dict(locals())
    w_loc = {n: given[n] for n in WEIGHTS}
    m_loc = {n: given["m_" + n] for n in WEIGHTS}
    v_loc = {n: given["v_" + n] for n in WEIGHTS}

    mlp = ["mlp_w_up", "mlp_w_down"]
    rest = [n for n in MATMUL_WEIGHTS if n not in mlp]
    rest_rows = _round_up(sum(_rows_of(w_loc[n].shape) for n in rest), 512)
    small_rows = _round_up(sum(_rows_of(w_loc[n].shape) for n in SMALL_SHARDED), 16)
    as_rows = lambda w: w.astype(BF).reshape(-1, w.shape[-1])
    up_all, down_all, rest_all, small_all = _all_gather_xy(
        [as_rows(mlp_w_up), as_rows(mlp_w_down), _pack([w_loc[n] for n in rest], rest_rows, BF),
         _pack([w_loc[n] for n in SMALL_SHARDED], small_rows, F32)], "gather_weights")
    W = {n: w_loc[n] for n in REPLICATED}
    W["mlp_up_slots"] = up_all.reshape((N_CHIPS,) + mlp_w_up.shape)
    W["mlp_down_slots"] = down_all.reshape((N_CHIPS,) + mlp_w_down.shape)
    for names, buf in ((rest, rest_all), (SMALL_SHARDED, small_all)):
        per_chip = [_unpack(buf[q], [w_loc[n].shape for n in names]) for q in range(N_CHIPS)]
        for i, n in enumerate(names):
            W[n] = jnp.concatenate([per_chip[q][i] for q in range(N_CHIPS)], axis=SHARD_AXIS[n])

    loss, grad_x, G = _local_step(x, loss_target, W, by_chip=True)
    loss = lax.psum(loss, ("x", "y", "c"))

    sharded = rest + SMALL_SHARDED
    rep_rows = _round_up(sum(_rows_of(w_loc[n].shape) for n in REPLICATED), N_CHIPS * 16)
    rep_flat = _pack([G[n] for n in REPLICATED], rep_rows, F32).reshape(N_CHIPS, rep_rows // N_CHIPS, FLAT_COLS)
    shard_rows = sum(_rows_of(w_loc[n].shape) for n in sharded)
    total_rows = _round_up(shard_rows + rep_rows // N_CHIPS, 1024)
    parts = [_slots_of(G[n], SHARD_AXIS[n]) for n in sharded] + [rep_flat]
    parts.append(jnp.zeros((N_CHIPS, total_rows - shard_rows - rep_rows // N_CHIPS, FLAT_COLS), F32))
    mlp_bufs = [g.reshape(N_CHIPS, -1, g.shape[-1]) for n in mlp for g in G[n]]
    *mlp_red, red = _reduce_scatter(mlp_bufs + [jnp.concatenate(parts, axis=1)], "grads")
    n_layers = mlp_w_up.shape[0]
    g_loc = {"mlp_w_up": jnp.stack(mlp_red[:n_layers]).reshape(mlp_w_up.shape),
             "mlp_w_down": jnp.stack(mlp_red[n_layers:]).reshape(mlp_w_down.shape)}
    g_loc.update(zip(sharded, _unpack(red, [w_loc[n].shape for n in sharded])))
    rep_quarter = red[shard_rows:shard_rows + rep_rows // N_CHIPS]
    (rep_all,) = _all_gather_xy([rep_quarter], "gather_replicated_grads")
    g_loc.update(zip(REPLICATED, _unpack(rep_all.reshape(rep_rows, FLAT_COLS), [w_loc[n].shape for n in REPLICATED])))

    delta, new_m, new_v = {}, {}, {}
    big = [n for n in MATMUL_WEIGHTS if w_loc[n].size % FLAT_COLS == 0]
    for n in big:
        shp = w_loc[n].shape
        two_d = lambda a: a.reshape(-1, FLAT_COLS)
        d, nm, nv = _adamw(two_d(w_loc[n]), two_d(g_loc[n]), two_d(m_loc[n]), two_d(v_loc[n]), f"adamw_{n}")
        delta[n], new_m[n], new_v[n] = d.reshape(shp), nm.reshape(shp), nv.reshape(shp)
    small = [n for n in WEIGHTS if n not in big]
    small_shapes = [w_loc[n].shape for n in small]
    rows = _round_up(sum(_rows_of(s) for s in small_shapes), 8)
    packed = [_pack([src[n] for n in small], rows, F32) for src in (w_loc, g_loc, m_loc, v_loc)]
    for dst, buf in zip((delta, new_m, new_v), _adamw(*packed, "adamw_small")):
        dst.update(zip(small, _unpack(buf, small_shapes)))

    return (loss, grad_x, *[g_loc[n] for n in WEIGHTS], *[delta[n] for n in WEIGHTS],
            *[new_m[n] for n in WEIGHTS], *[new_v[n] for n in WEIGHTS])
```

```python
import functools

import jax
import jax.numpy as jnp
from jax import lax
from jax.experimental import pallas as pl
from jax.experimental.pallas import tpu as pltpu

F32 = jnp.float32
BF = jnp.bfloat16

LANES = 128
VMEM_LIMIT = 56 * 1024 * 1024
RG_C = 8.0
SGU_CHUNK = 128
GDN_CHUNK = 64
CONV_W = 4
N_MIXERS = 3

ADAM_LR = 0.001
ADAM_B1 = 0.9
ADAM_B2 = 0.999
ADAM_EPS = 1e-08
ADAM_WD = 0.01
ADAM_STEP = 10


def _params(sem=None):
    return pltpu.CompilerParams(dimension_semantics=sem, vmem_limit_bytes=VMEM_LIMIT)


def _shift_impl(x, s):
    if s == 0:
        return x
    n = x.shape[0]
    row = lax.broadcasted_iota(jnp.int32, x.shape, 0)
    if s > 0:
        return jnp.where(row >= s, pltpu.roll(x, s, 0), 0.0)
    return jnp.where(row < n + s, pltpu.roll(x, n + s, 0), 0.0)


@functools.partial(jax.custom_vjp, nondiff_argnums=(1,))
def _shift(x, s):
    return _shift_impl(x, s)


def _shift_fwd(x, s):
    return _shift_impl(x, s), None


def _shift_bwd(s, _, g):
    return (_shift_impl(g, -s),)


_shift.defvjp(_shift_fwd, _shift_bwd)


def _chunk_cumsum_impl(x, rev, chunk):
    n = x.shape[0]
    rc = lax.broadcasted_iota(jnp.int32, x.shape, 0) & (chunk - 1)
    sh = 1
    while sh < chunk:
        if rev:
            x = x + jnp.where(rc < chunk - sh, pltpu.roll(x, n - sh, 0), 0.0)
        else:
            x = x + jnp.where(rc >= sh, pltpu.roll(x, sh, 0), 0.0)
        sh *= 2
    return x


@functools.partial(jax.custom_vjp, nondiff_argnums=(1, 2))
def _chunk_cumsum(x, rev, chunk):
    return _chunk_cumsum_impl(x, rev, chunk)


def _chunk_cumsum_fwd(x, rev, chunk):
    return _chunk_cumsum_impl(x, rev, chunk), None


def _chunk_cumsum_bwd(rev, chunk, _, g):
    return (_chunk_cumsum_impl(g, not rev, chunk),)


_chunk_cumsum.defvjp(_chunk_cumsum_fwd, _chunk_cumsum_bwd)


def _rms(x, g, eps=1e-6):
    return x * lax.rsqrt(jnp.mean(x * x, axis=-1, keepdims=True) + eps) * g


def _sigmoid(x):
    return 0.5 * jnp.tanh(0.5 * x) + 0.5


def _silu(x):
    return x * _sigmoid(x)


def _softplus(x):
    return jnp.maximum(x, 0.0) + jnp.log1p(jnp.exp(-jnp.abs(x)))


def _neg_expm1(y, ey):
    series = -y * (1.0 + y * (1 / 2) * (1.0 + y * (1 / 3) * (1.0 + y * (1 / 4))))
    return jnp.where(y > -1 / 32, series, 1.0 - ey)


@jax.custom_vjp
def _sqrt_one_minus_sq(log_a, a):
    t = _neg_expm1(2.0 * log_a, a * a)
    return t * lax.rsqrt(jnp.maximum(t, 1e-30))


def _sqrt_one_minus_sq_fwd(log_a, a):
    t = _neg_expm1(2.0 * log_a, a * a)
    rs = lax.rsqrt(jnp.maximum(t, 1e-30))
    return t * rs, (a, rs)


def _sqrt_one_minus_sq_bwd(res, g):
    a, rs = res
    return -g * (a * a) * rs, jnp.zeros_like(a)


_sqrt_one_minus_sq.defvjp(_sqrt_one_minus_sq_fwd, _sqrt_one_minus_sq_bwd)


def _bmm_raw(a, b, form):
    r = a.ndim - 2
    con = {"nn": ((r + 1,), (r,)), "nt": ((r + 1,), (r + 1,)), "tn": ((r,), (r,))}[form]
    batch = ((0,), (0,)) if r else ((), ())
    return lax.dot_general(a.astype(BF), b.astype(BF), (con, batch), preferred_element_type=F32)


@functools.partial(jax.custom_vjp, nondiff_argnums=(2,))
def _bmm(a, b, form):
    return _bmm_raw(a, b, form)


def _bmm_fwd(a, b, form):
    return _bmm_raw(a, b, form), (a, b)


def _bmm_bwd(form, res, g):
    a, b = res
    if form == "nn":
        da, db = _bmm_raw(g, b, "nt"), _bmm_raw(a, g, "tn")
    elif form == "nt":
        da, db = _bmm_raw(g, b, "nn"), _bmm_raw(g, a, "tn")
    else:
        da, db = _bmm_raw(b, g, "nt"), _bmm_raw(a, g, "nn")
    return da.astype(a.dtype), db.astype(b.dtype)


_bmm.defvjp(_bmm_fwd, _bmm_bwd)


def _conv4(z, rows):
    out = rows[0] * _shift(z, 2)
    for k in range(1, CONV_W):
        out = out + rows[k] * _shift(z, 2 - k)
    return out


_DIMS = {"nn": ((1,), (0,)), "nt": ((1,), (1,)), "tn": ((0,), (0,))}


def _mm(a, b, mode, name, *, out_dtype=F32, epi=None, extra=None, tm=2048, tn=1024, tk=1024, b_index=None, n_cols=None,
        by_chip=None):
    if mode == "tn":
        K, M = a.shape
    else:
        M, K = a.shape
    N = n_cols if b_index is not None else (b.shape[0] if mode == "nt" else b.shape[1])
    if epi == "add":
        tm = min(tm, 1024)
    tm, tn, tk = min(tm, M), min(tn, N), min(tk, K)
    assert M % tm == 0 and N % tn == 0 and K % tk == 0, (name, M, N, K)
    nk = K // tk
    a_spec = pl.BlockSpec((tk, tm), lambda i, j, k: (k, i)) if mode == "tn" else pl.BlockSpec((tm, tk), lambda i, j, k: (i, k))
    b_block = (tn, tk) if mode == "nt" else (tk, tn)
    if b_index is not None:
        b_spec = pl.BlockSpec((None,) * (b.ndim - 2) + b_block, lambda i, j, k: b_index(j, k))
    elif mode == "nt":
        b_spec = pl.BlockSpec(b_block, lambda i, j, k: (j, k))
    else:
        b_spec = pl.BlockSpec(b_block, lambda i, j, k: (k, j))
    o_spec = pl.BlockSpec((tm, tn), lambda i, j, k: (i, j))
    ins, specs = [a, b], [a_spec, b_spec]
    if epi in ("add", "relu2_bwd"):
        ins.append(extra)
        specs.append(o_spec)
    o_shape = (M, N)
    if by_chip == "cols":
        assert tn == N // N_CHIPS and epi is None
        o_shape, o_spec = (N_CHIPS, M, tn), pl.BlockSpec((None, tm, tn), lambda i, j, k: (j, i, 0))

    def body(*refs):
        a_ref, b_ref = refs[0], refs[1]
        e_ref = refs[2] if len(ins) == 3 else None
        o_ref = refs[len(ins)]

        def product():
            return lax.dot_general(a_ref[...].astype(BF), b_ref[...].astype(BF), (_DIMS[mode], ((), ())),
                                   preferred_element_type=F32)

        def finish(r):
            if epi == "relu2":
                r = jnp.square(jnp.maximum(r, 0.0))
            elif epi == "add":
                r = r + e_ref[...]
            elif epi == "relu2_bwd":
                e = e_ref[...].astype(F32)
                r = r * (2.0 * e * lax.rsqrt(jnp.maximum(e, 1e-30)))
            o_ref[...] = r.astype(out_dtype)

        if nk == 1:
            finish(product())
            return
        acc = refs[-1]
        k = pl.program_id(2)

        @pl.when(k == 0)
        def _():
            acc[...] = product()

        @pl.when(k > 0)
        def _():
            acc[...] += product()

        @pl.when(k == nk - 1)
        def _():
            finish(acc[...])

    return pl.pallas_call(
        body, name=name, grid=(M // tm, N // tn, nk), in_specs=specs, out_specs=o_spec,
        out_shape=jax.ShapeDtypeStruct(o_shape, out_dtype),
        scratch_shapes=[pltpu.VMEM((tm, tn), F32)] if nk > 1 else [],
        compiler_params=_params(("parallel", "parallel", "arbitrary")))(*ins)


def _rows_tile(T):
    return min(512, T)


def _rms_fwd(x, g, name):
    T, D = x.shape
    tr = _rows_tile(T)

    def body(x_ref, g_ref, o_ref):
        o_ref[...] = _rms(x_ref[...], g_ref[...]).astype(BF)

    return pl.pallas_call(
        body, name=name, grid=(T // tr,),
        in_specs=[pl.BlockSpec((tr, D), lambda i: (i, 0)), pl.BlockSpec((1, D), lambda i: (0, 0))],
        out_specs=pl.BlockSpec((tr, D), lambda i: (i, 0)), out_shape=jax.ShapeDtypeStruct((T, D), BF),
        compiler_params=_params(("parallel",)))(x, g.reshape(1, D))


def _rms_bwd(x, g, dhn, dres, name):
    T, D = x.shape
    tr = _rows_tile(T)

    def body(x_ref, g_ref, dhn_ref, dres_ref, dx_ref, dg_ref):
        _, vjp = jax.vjp(_rms, x_ref[...], g_ref[...])
        dx, dg = vjp(dhn_ref[...])
        dx_ref[...] = dres_ref[...] + dx

        @pl.when(pl.program_id(0) == 0)
        def _():
            dg_ref[...] = jnp.zeros_like(dg_ref)

        dg_ref[...] += dg

    row = pl.BlockSpec((tr, D), lambda i: (i, 0))
    vec = pl.BlockSpec((1, D), lambda i: (0, 0))
    dx, dg = pl.pallas_call(
        body, name=name, grid=(T // tr,), in_specs=[row, vec, row, row], out_specs=[row, vec],
        out_shape=[jax.ShapeDtypeStruct((T, D), F32), jax.ShapeDtypeStruct((1, D), F32)],
        compiler_params=_params(("arbitrary",)))(x, g.reshape(1, D), dhn, dres)
    return dx, dg.reshape(D)


def _final_loss(x, g, tgt, name):
    T, D = x.shape
    tr = _rows_tile(T)

    def body(x_ref, g_ref, t_ref, l_ref, dx_ref, dg_ref):
        y, vjp = jax.vjp(_rms, x_ref[...], g_ref[...])
        err = y - t_ref[...]
        dx, dg = vjp(err * (1.0 / D))
        dx_ref[...] = dx

        @pl.when(pl.program_id(0) == 0)
        def _():
            dg_ref[...] = jnp.zeros_like(dg_ref)
            l_ref[...] = jnp.zeros_like(l_ref)

        dg_ref[...] += dg
        l_ref[...] += (0.5 / D) * jnp.sum(jnp.sum(err * err, axis=1, keepdims=True), axis=0, keepdims=True)

    row = pl.BlockSpec((tr, D), lambda i: (i, 0))
    vec = pl.BlockSpec((1, D), lambda i: (0, 0))
    loss, dx, dg = pl.pallas_call(
        body, name=name, grid=(T // tr,), in_specs=[row, vec, row],
        out_specs=[pl.BlockSpec((1, LANES), lambda i: (0, 0)), row, vec],
        out_shape=[jax.ShapeDtypeStruct((1, LANES), F32), jax.ShapeDtypeStruct((T, D), F32),
                   jax.ShapeDtypeStruct((1, D), F32)],
        compiler_params=_params(("arbitrary",)))(x, g.reshape(1, D), tgt)
    return loss[0, 0], dx, dg.reshape(D)


def _a_pre(zx, cws, cb, gws, gbs, lams):
    xr = _conv4(zx, cws) + cb
    out = []
    for d in range(2):
        r = _sigmoid(_bmm(xr, gws[2 * d], "nn") + gbs[2 * d])
        ig = _sigmoid(_bmm(xr, gws[2 * d + 1], "nn") + gbs[2 * d + 1])
        log_a = -RG_C * r * _softplus(-lams[d])
        a = jnp.exp(log_a)
        out += [a, _sqrt_one_minus_sq(log_a, a) * ig * xr]
    return tuple(out)


def _a_post(h0, h1, zg):
    return (h0 + h1) * jax.nn.gelu(zg)


SUBLANES = 8
SCAN_TILES = 8


def _scan_jobs(jobs):
    S, C = jobs[0][0].shape
    U = min(SCAN_TILES, S // SUBLANES)
    rows = U * SUBLANES
    row = lax.broadcasted_iota(jnp.int32, (SUBLANES, C), 0)

    def prefix(a, b, reverse):
        for sh in (1, 2, 4):
            if reverse:
                m, r = row < SUBLANES - sh, SUBLANES - sh
            else:
                m, r = row >= sh, sh
            a_s = jnp.where(m, pltpu.roll(a, r, 0), 1.0)
            b_s = jnp.where(m, pltpu.roll(b, r, 0), 0.0)
            b = a * b_s + b
            a = a * a_s
        return a, b

    def step(i, carries):
        out = []
        for (a_ref, b_ref, h_ref, reverse), c in zip(jobs, carries):
            blk = (S // rows - 1 - i) if reverse else i
            t0 = pl.multiple_of(blk * rows, rows)
            order = range(U - 1, -1, -1) if reverse else range(U)
            edge = slice(0, 1) if reverse else slice(SUBLANES - 1, SUBLANES)
            for j in order:
                sl = pl.ds(t0 + j * SUBLANES, SUBLANES)
                a, b = prefix(a_ref[sl, :], b_ref[sl, :], reverse)
                h_ref[sl, :] = a * jnp.broadcast_to(c, (SUBLANES, C)) + b
                c = a[edge, :] * c + b[edge, :]
            out.append(c)
        return tuple(out)

    lax.fori_loop(0, S // rows, step, tuple(jnp.zeros((1, C), F32) for _ in jobs))


def _a_load_params(cw_ref, cb_ref, gw_ref, gb_ref, lam_ref):
    cws = [cw_ref[k:k + 1, :] for k in range(CONV_W)]
    gws = [gw_ref[d, g] for d in range(2) for g in range(2)]
    gbs = [gb_ref[d, g] for d in range(2) for g in range(2)]
    lams = [lam_ref[d:d + 1, :] for d in range(2)]
    return cws, cb_ref[...], gws, gbs, lams


def _a_in_specs(S, H):
    zg = pl.BlockSpec((None, S, LANES), lambda h, b: (b, 0, h))
    zx = pl.BlockSpec((None, S, LANES), lambda h, b: (b, 0, H + h))
    cw = pl.BlockSpec((CONV_W, LANES), lambda h, b: (0, h))
    cb = pl.BlockSpec((1, LANES), lambda h, b: (0, h))
    gw = pl.BlockSpec((2, 2, None, LANES, LANES), lambda h, b: (0, 0, h, 0, 0))
    gb = pl.BlockSpec((2, 2, None, 1, LANES), lambda h, b: (0, 0, h, 0, 0))
    lam = pl.BlockSpec((2, LANES), lambda h, b: (0, h))
    return zg, zx, cw, cb, gw, gb, lam


def _a_core_fwd(z, cw, cb, gw, gb, lam, name):
    Bq, S, D2 = z.shape
    D = D2 // 2
    H = D // LANES

    def body(zg_ref, zx_ref, cw_ref, cb_ref, gw_ref, gb_ref, lam_ref, y_ref, a_s, b_s, h_s):
        ab = _a_pre(zx_ref[...], *_a_load_params(cw_ref, cb_ref, gw_ref, gb_ref, lam_ref))
        for d in range(2):
            a_s[d] = ab[2 * d]
            b_s[d] = ab[2 * d + 1]
        _scan_jobs([(a_s.at[d], b_s.at[d], h_s.at[d], d == 1) for d in range(2)])
        y_ref[...] = _a_post(h_s[0], h_s[1], zg_ref[...]).astype(BF)

    seq = pltpu.VMEM((2, S, LANES), F32)
    return pl.pallas_call(
        body, name=name, grid=(H, Bq), in_specs=list(_a_in_specs(S, H)),
        out_specs=pl.BlockSpec((None, S, LANES), lambda h, b: (b, 0, h)),
        out_shape=jax.ShapeDtypeStruct((Bq, S, D), BF), scratch_shapes=[seq, seq, seq],
        compiler_params=_params(("parallel", "arbitrary")))(z, z, cw, cb.reshape(1, D), gw, gb.reshape(2, 2, H, 1, LANES), lam)


def _a_core_bwd(z, dy, cw, cb, gw, gb, lam, name):
    Bq, S, D2 = z.shape
    D = D2 // 2
    H = D // LANES

    def body(zg_ref, zx_ref, dy_ref, cw_ref, cb_ref, gw_ref, gb_ref, lam_ref,
             dzg_ref, dzx_ref, dcw_ref, dcb_ref, dgw_ref, dgb_ref, dlam_ref, a_s, b_s, h_s, l_s):
        prm = _a_load_params(cw_ref, cb_ref, gw_ref, gb_ref, lam_ref)
        ab = _a_pre(zx_ref[...], *prm)
        for d in range(2):
            a_s[d] = ab[2 * d]
            b_s[d] = ab[2 * d + 1]
        _scan_jobs([(a_s.at[d], b_s.at[d], h_s.at[d], d == 1) for d in range(2)])
        _, post_vjp = jax.vjp(_a_post, h_s[0], h_s[1], zg_ref[...])
        dh0, dh1, dzg = post_vjp(dy_ref[...])
        dzg_ref[...] = dzg.astype(BF)
        for d, dh in ((0, dh0), (1, dh1)):
            b_s[d] = dh
            a_s[d] = _shift(a_s[d], -1 if d == 0 else 1)
        _scan_jobs([(a_s.at[d], b_s.at[d], l_s.at[d], d == 0) for d in range(2)])
        cot = []
        for d in range(2):
            cot += [l_s[d] * _shift(h_s[d], 1 if d == 0 else -1), l_s[d]]
        _, pre_vjp = jax.vjp(_a_pre, zx_ref[...], *prm)
        dzx, dcws, dcb, dgws, dgbs, dlams = pre_vjp(tuple(cot))
        dzx_ref[...] = dzx.astype(BF)

        @pl.when(pl.program_id(1) == 0)
        def _():
            for r in (dcw_ref, dcb_ref, dgw_ref, dgb_ref, dlam_ref):
                r[...] = jnp.zeros_like(r)

        for k in range(CONV_W):
            dcw_ref[k:k + 1, :] += dcws[k]
        dcb_ref[...] += dcb
        for d in range(2):
            dlam_ref[d:d + 1, :] += dlams[d]
            for g in range(2):
                dgw_ref[d, g] += dgws[2 * d + g]
                dgb_ref[d, g] += dgbs[2 * d + g]

    zg, zx, cws, cbs, gws, gbs, lams = _a_in_specs(S, H)
    dyspec = pl.BlockSpec((None, S, LANES), lambda h, b: (b, 0, h))
    seq = pltpu.VMEM((2, S, LANES), F32)
    dzg, dzx, dcw, dcb, dgw, dgb, dlam = pl.pallas_call(
        body, name=name, grid=(H, Bq), in_specs=[zg, zx, dyspec, cws, cbs, gws, gbs, lams],
        out_specs=[dyspec, dyspec, cws, cbs, gws, gbs, lams],
        out_shape=[jax.ShapeDtypeStruct((Bq, S, D), BF), jax.ShapeDtypeStruct((Bq, S, D), BF),
                   jax.ShapeDtypeStruct((CONV_W, D), F32), jax.ShapeDtypeStruct((1, D), F32),
                   jax.ShapeDtypeStruct((2, 2, H, LANES, LANES), F32), jax.ShapeDtypeStruct((2, 2, H, 1, LANES), F32),
                   jax.ShapeDtypeStruct((2, D), F32)],
        scratch_shapes=[seq, seq, seq, seq],
        compiler_params=_params(("parallel", "arbitrary")))(z, z, dy, cw, cb.reshape(1, D), gw, gb.reshape(2, 2, H, 1, LANES), lam)
    dz = jnp.concatenate([dzg, dzx], axis=-1)
    return dz, dcw, dcb.reshape(D), dgw, dgb.reshape(2, 2, H, LANES), dlam


def _b_fn(z, lng, lnb, wss, bsf):
    D = z.shape[1] // 2
    zz = jax.nn.gelu(z)
    u, v = zz[:, :D], zz[:, D:]
    mu = jnp.mean(v, axis=-1, keepdims=True)
    var = jnp.mean(jnp.square(v - mu), axis=-1, keepdims=True)
    vn = (v - mu) * lax.rsqrt(var + 1e-5) * lng + lnb
    vs = jnp.concatenate([_bmm(wss[g], vn[:, g * LANES:(g + 1) * LANES], "nn") for g in range(D // LANES)], axis=1)
    return u * (vs + bsf)


def _b_specs(D, GB):
    row = lambda w: pl.BlockSpec((SGU_CHUNK, w), lambda i: (i, 0))
    vec = pl.BlockSpec((1, D), lambda i: (0, 0))
    ws = pl.BlockSpec((GB, SGU_CHUNK, SGU_CHUNK), lambda i: (0, 0, 0))
    bsf = pl.BlockSpec((SGU_CHUNK, D), lambda i: (0, 0))
    return row, vec, ws, bsf


def _b_core_fwd(z, lng, lnb, ws, bsf, name):
    T, D2 = z.shape
    D = D2 // 2
    GB = D // LANES
    row, vec, wspec, bspec = _b_specs(D, GB)

    def body(z_ref, lng_ref, lnb_ref, ws_ref, bsf_ref, y_ref):
        wss = [ws_ref[g] for g in range(GB)]
        y_ref[...] = _b_fn(z_ref[...], lng_ref[...], lnb_ref[...], wss, bsf_ref[...]).astype(BF)

    return pl.pallas_call(
        body, name=name, grid=(T // SGU_CHUNK,), in_specs=[row(D2), vec, vec, wspec, bspec], out_specs=row(D),
        out_shape=jax.ShapeDtypeStruct((T, D), BF), compiler_params=_params(("parallel",)))(
            z, lng.reshape(1, D), lnb.reshape(1, D), ws, bsf)


def _b_core_bwd(z, dy, lng, lnb, ws, bsf, name):
    T, D2 = z.shape
    D = D2 // 2
    GB = D // LANES
    row, vec, wspec, bspec = _b_specs(D, GB)

    def body(z_ref, dy_ref, lng_ref, lnb_ref, ws_ref, bsf_ref, dz_ref, dlng_ref, dlnb_ref, dws_ref, dbsf_ref):
        wss = [ws_ref[g] for g in range(GB)]
        _, vjp = jax.vjp(_b_fn, z_ref[...], lng_ref[...], lnb_ref[...], wss, bsf_ref[...])
        dz, dlng, dlnb, dwss, dbsf = vjp(dy_ref[...])
        dz_ref[...] = dz.astype(BF)

        @pl.when(pl.program_id(0) == 0)
        def _():
            for r in (dlng_ref, dlnb_ref, dws_ref, dbsf_ref):
                r[...] = jnp.zeros_like(r)

        dlng_ref[...] += dlng
        dlnb_ref[...] += dlnb
        dbsf_ref[...] += dbsf
        for g in range(GB):
            dws_ref[g] += dwss[g]

    dz, dlng, dlnb, dws, dbsf = pl.pallas_call(
        body, name=name, grid=(T // SGU_CHUNK,), in_specs=[row(D2), row(D), vec, vec, wspec, bspec],
        out_specs=[row(D2), vec, vec, wspec, bspec],
        out_shape=[jax.ShapeDtypeStruct((T, D2), BF), jax.ShapeDtypeStruct((1, D), F32), jax.ShapeDtypeStruct((1, D), F32),
                   jax.ShapeDtypeStruct((GB, SGU_CHUNK, SGU_CHUNK), F32), jax.ShapeDtypeStruct((SGU_CHUNK, D), F32)],
        compiler_params=_params(("arbitrary",)))(z, dy, lng.reshape(1, D), lnb.reshape(1, D), ws, bsf)
    return dz, dlng.reshape(D), dlnb.reshape(D), dws, dbsf


def _lane_is(j):
    return lax.broadcasted_iota(jnp.int32, (1, LANES), 1) == j


def _lane_col(x, j):
    return jnp.sum(jnp.where(_lane_is(j), x, 0.0), axis=1, keepdims=True)


def _c_pre(zq, zk, zv, zs, cwq, cwk, cwv, pcs, head, HC):
    q = _silu(_conv4(zq, cwq))
    k = _silu(_conv4(zk, cwk))
    v = _silu(_conv4(zv, cwv))
    q = q * lax.rsqrt(jnp.sum(q * q, axis=-1, keepdims=True) + 1e-6) * (LANES ** -0.5)
    k = k * lax.rsqrt(jnp.sum(k * k, axis=-1, keepdims=True) + 1e-6)
    gbp = jnp.zeros_like(zs)
    for d in range(2):
        a_logit = _lane_col(zs, d * HC + head)
        b_logit = _lane_col(zs, 2 * HC + d * HC + head)
        g = -jnp.exp(pcs[d]) * _softplus(a_logit + pcs[2 + d])
        beta = jnp.broadcast_to(_sigmoid(b_logit), g.shape)
        gbp = gbp + jnp.where(_lane_is(2 * d), g, 0.0) + jnp.where(_lane_is(2 * d + 1), beta, 0.0)
    return q, k, v, gbp


def _mm3(x, y):
    xh, yh = x.astype(BF), y.astype(BF)
    xl, yl = (x - xh.astype(F32)).astype(BF), (y - yh.astype(F32)).astype(BF)
    return _bmm_raw(xh, yh, "nn") + _bmm_raw(xh, yl, "nn") + _bmm_raw(xl, yh, "nn")


def _tri_inv_impl(a):
    C = a.shape[-1]
    eye = (lax.broadcasted_iota(jnp.int32, (1, C, C), 1) == lax.broadcasted_iota(jnp.int32, (1, C, C), 2)).astype(F32)
    r = eye - a
    p = a
    n = 2
    while n < C:
        p = _mm3(p, p)
        r = r + _mm3(r, p)
        n *= 2
    return r


@jax.custom_vjp
def _tri_inv(a):
    return _tri_inv_impl(a)


def _tri_inv_fwd(a):
    t = _tri_inv_impl(a)
    return t, t


def _tri_inv_bwd(t, g):
    tt = jnp.swapaxes(t, 1, 2)
    return (-_bmm_raw(_bmm_raw(tt, g, "nn"), tt, "nn"),)


_tri_inv.defvjp(_tri_inv_fwd, _tri_inv_bwd)


@jax.custom_vjp
def _pair_diff(gc3):
    m = gc3[:, :, :gc3.shape[1]]
    return m - jnp.swapaxes(m, 1, 2)


def _pair_diff_fwd(gc3):
    return _pair_diff(gc3), None


def _pair_diff_bwd(_, g):
    d = jnp.sum(g, axis=2, keepdims=True) - jnp.sum(jnp.swapaxes(g, 1, 2), axis=2, keepdims=True)
    return (jnp.broadcast_to(d * (1.0 / LANES), d.shape[:2] + (LANES,)),)


_pair_diff.defvjp(_pair_diff_fwd, _pair_diff_bwd)


@jax.custom_vjp
def _tri_inv_saved(a, t):
    return t


def _tri_inv_saved_fwd(a, t):
    return t, t


def _tri_inv_saved_bwd(t, g):
    return _tri_inv_bwd(t, g)[0], jnp.zeros_like(t)


_tri_inv_saved.defvjp(_tri_inv_saved_fwd, _tri_inv_saved_bwd)


def _c_phase1(q, k, v, gbp, rev, t_saved=None):
    S = q.shape[0]
    C = GDN_CHUNK
    N = S // C
    col = 2 if rev else 0
    gB = jnp.broadcast_to(_lane_col(gbp, col), (S, LANES))
    bB = jnp.broadcast_to(_lane_col(gbp, col + 1), (S, LANES))
    r3 = lambda t: t.reshape(N, C, LANES)
    gc3 = r3(_chunk_cumsum(gB, rev, C))
    q3, k3, v3, b3, g3 = r3(q), r3(k), r3(v), r3(bB), r3(gB)
    ri = lax.broadcasted_iota(jnp.int32, (1, C, C), 1)
    ci = lax.broadcasted_iota(jnp.int32, (1, C, C), 2)
    incl = (ri <= ci) if rev else (ri >= ci)
    strict = (ri < ci) if rev else (ri > ci)
    decay = jnp.where(incl, jnp.exp(jnp.where(incl, _pair_diff(gc3), 0.0)), 0.0)
    kb = k3 * b3
    vb = v3 * b3
    A = jnp.where(strict, _bmm(kb, k3, "nt") * decay, 0.0)
    T = _tri_inv(A) if t_saved is None else _tri_inv_saved(A, t_saved)
    egc = jnp.exp(gc3)
    u = _bmm(T, vb, "nn")
    w = _bmm(T, kb * egc, "nn")
    qk = _bmm(q3, k3, "nt") * decay
    glast = jnp.sum(g3, axis=1, keepdims=True)
    kd = k3 * jnp.exp(glast - gc3)
    k2 = _bmm(kd, w, "tn")
    z = _bmm(kd, u, "tn")
    qe2 = q3 * egc - _bmm(qk, w, "nn")
    o0 = _bmm(qk, u, "nn")
    return k2, z, jnp.exp(glast), qe2, o0, T


def _c_next_state(state, k2, z, eg):
    return state * eg - _bmm(k2, state, "nn") + z


def _c_out(state, qe2, o0):
    return _bmm(qe2, state, "nn") + o0


def _c_post(o, zg, ng):
    return _rms(o, ng) * _silu(zg)


def _c_pre_specs(S, H):
    col = lambda c0: pl.BlockSpec((None, S, LANES), lambda h, b: (b, 0, c0 * H + h))
    zs = pl.BlockSpec((None, S, LANES), lambda h, b: (b, 0, 0))
    cw = lambda c0: pl.BlockSpec((None, CONV_W, LANES), lambda h, b: (c0, 0, h))
    pc = pl.BlockSpec((None, 4, LANES), lambda h, b: (h, 0, 0))
    return col, zs, cw, pc


def _c_pre_fwd(z, zs, cw3, pc, name):
    Bq, S, D4 = z.shape
    D = D4 // 4
    H = D // LANES
    col, zss, cw, pcs = _c_pre_specs(S, H)

    def body(zq_ref, zk_ref, zv_ref, zs_ref, cwq_ref, cwk_ref, cwv_ref, pc_ref, q_ref, k_ref, v_ref, gbp_ref):
        rows = lambda r: [r[i:i + 1, :] for i in range(r.shape[0])]
        q, k, v, gbp = _c_pre(zq_ref[...], zk_ref[...], zv_ref[...], zs_ref[...], rows(cwq_ref), rows(cwk_ref),
                              rows(cwv_ref), rows(pc_ref), pl.program_id(0), H)
        q_ref[...] = q
        k_ref[...] = k
        v_ref[...] = v
        gbp_ref[...] = gbp

    out = pl.BlockSpec((None, S, LANES), lambda h, b: (b, 0, h))
    shp = jax.ShapeDtypeStruct((Bq, S, D), F32)
    return pl.pallas_call(
        body, name=name, grid=(H, Bq), in_specs=[col(0), col(1), col(2), zss, cw(0), cw(1), cw(2), pcs],
        out_specs=[out] * 4, out_shape=[shp] * 4, compiler_params=_params(("parallel", "arbitrary")))(
            z, z, z, zs, cw3, cw3, cw3, pc)


def _c_pre_bwd(z, zs, cw3, pc, dq, dk, dv, dgbp, name):
    Bq, S, D4 = z.shape
    D = D4 // 4
    H = D // LANES
    col, zss, cw, pcs = _c_pre_specs(S, H)

    def body(zq_ref, zk_ref, zv_ref, zs_ref, cwq_ref, cwk_ref, cwv_ref, pc_ref, dq_ref, dk_ref, dv_ref, dgbp_ref,
             dzq_ref, dzk_ref, dzv_ref, dzs_ref, dcw_ref, dpc_ref):
        rows = lambda r: [r[i:i + 1, :] for i in range(r.shape[0])]
        fn = functools.partial(_c_pre, head=pl.program_id(0), HC=H)
        _, vjp = jax.vjp(fn, zq_ref[...], zk_ref[...], zv_ref[...], zs_ref[...], rows(cwq_ref), rows(cwk_ref),
                         rows(cwv_ref), rows(pc_ref))
        dzq, dzk, dzv, dzs, dcwq, dcwk, dcwv, dpcs = vjp((dq_ref[...], dk_ref[...], dv_ref[...], dgbp_ref[...]))
        dzq_ref[...] = dzq.astype(BF)
        dzk_ref[...] = dzk.astype(BF)
        dzv_ref[...] = dzv.astype(BF)
        dzs_ref[...] = dzs

        @pl.when(pl.program_id(1) == 0)
        def _():
            dcw_ref[...] = jnp.zeros_like(dcw_ref)
            dpc_ref[...] = jnp.zeros_like(dpc_ref)

        for c, dc in enumerate((dcwq, dcwk, dcwv)):
            for i in range(CONV_W):
                dcw_ref[c, i:i + 1, :] += dc[i]
        for i in range(4):
            dpc_ref[i:i + 1, :] += dpcs[i]

    out = pl.BlockSpec((None, S, LANES), lambda h, b: (b, 0, h))
    dzs_spec = pl.BlockSpec((None, None, S, LANES), lambda h, b: (h, b, 0, 0))
    dcw_spec = pl.BlockSpec((3, CONV_W, LANES), lambda h, b: (0, 0, h))
    bshape = jax.ShapeDtypeStruct((Bq, S, D), BF)
    dzq, dzk, dzv, dzs, dcw3, dpc = pl.pallas_call(
        body, name=name, grid=(H, Bq),
        in_specs=[col(0), col(1), col(2), zss, cw(0), cw(1), cw(2), pcs, out, out, out, out],
        out_specs=[out, out, out, dzs_spec, dcw_spec, pcs],
        out_shape=[bshape, bshape, bshape, jax.ShapeDtypeStruct((H, Bq, S, LANES), F32),
                   jax.ShapeDtypeStruct((3, CONV_W, D), F32), jax.ShapeDtypeStruct((H, 4, LANES), F32)],
        compiler_params=_params(("parallel", "arbitrary")))(z, z, z, zs, cw3, cw3, cw3, pc, dq, dk, dv, dgbp)
    return dzq, dzk, dzv, dzs, dcw3, dpc


def _c_saved_shapes(Bq, H, S):
    N, C = S // GDN_CHUNK, GDN_CHUNK
    return [(Bq, H, 2, N, LANES, LANES), (Bq, H, 2, N, LANES, LANES), (Bq, H, 2, N, 1, LANES), (Bq, H, 2, N, C, LANES),
            (Bq, H, 2, N, C, C)]


def _c_saved_scratch(S):
    return [pltpu.VMEM(shp[3:], F32) for shp in _c_saved_shapes(1, 1, S)]


PHASE1_CHUNKS = 8


def _c_blocks(S, fn):
    nb = min(PHASE1_CHUNKS, S // GDN_CHUNK)
    rows = nb * GDN_CHUNK

    def blk(i, carry):
        fn(pl.ds(pl.multiple_of(i * rows, rows), rows), pl.ds(pl.multiple_of(i * nb, nb), nb))
        return carry

    lax.fori_loop(0, S // rows, blk, 0)


def _c_phase1_blocks(in_refs, k2_ref, z_ref, eg_ref, qe2_ref, t_ref, o_ref, rev):
    def fn(rows, chunks):
        k2, z, eg, qe2, o0, t = _c_phase1(*[r[rows, :] for r in in_refs], rev)
        k2_ref[chunks] = k2
        z_ref[chunks] = z
        eg_ref[chunks] = eg
        qe2_ref[chunks] = qe2
        t_ref[chunks] = t
        o_ref[chunks] += o0

    _c_blocks(in_refs[0].shape[0], fn)


def _c_sweep(jobs):
    N = jobs[0][3].shape[0]

    def step(i, states):
        out = []
        for (k2_ref, z_ref, eg_ref, st_ref, rev), state in zip(jobs, states):
            n = (N - 1 - i) if rev else i
            st_ref[n] = state
            out.append(_c_next_state(state, k2_ref[n], z_ref[n], eg_ref[n]))
        return tuple(out)

    lax.fori_loop(0, N, step, tuple(jnp.zeros((LANES, LANES), F32) for _ in jobs))


def _c_sweep_adjoint(jobs):
    N = jobs[0][3].shape[0]

    def step(i, gs):
        out = []
        for (k2_ref, eg_ref, dso_ref, gs_ref, rev), g in zip(jobs, gs):
            n = i if rev else (N - 1 - i)
            gs_ref[n] = g
            out.append(dso_ref[n] + g * eg_ref[n] - _bmm_raw(k2_ref[n], g, "tn"))
        return tuple(out)

    lax.fori_loop(0, N, step, tuple(jnp.zeros((LANES, LANES), F32) for _ in jobs))


def _c_mid_fwd(q, k, v, gbp, name):
    Bq, S, D = q.shape
    H = D // LANES
    N = S // GDN_CHUNK
    blk = pl.BlockSpec((None, S, LANES), lambda h, b: (b, 0, h))
    blk3 = pl.BlockSpec((None, N, GDN_CHUNK, LANES), lambda h, b: (b, 0, 0, h))
    n_saved = len(_c_saved_shapes(Bq, H, S))

    def body(q_ref, k_ref, v_ref, gbp_ref, o3, *rest):
        saved_hbm, scr = rest[:n_saved], rest[n_saved:]
        per_dir = n_saved + 1
        sems = scr[2 * per_dir]
        o3[...] = jnp.zeros_like(o3)
        sets = [scr[d * per_dir:(d + 1) * per_dir] for d in range(2)]
        copies = []

        def keep(d, i):
            copies.append(pltpu.make_async_copy(sets[d][i], saved_hbm[i].at[pl.program_id(1), pl.program_id(0), d],
                                                sems.at[d, i]))
            copies[-1].start()

        for d, rev in enumerate((False, True)):
            k2_ref, st_ref, eg_ref, qe2_ref, t_ref, z_ref = sets[d]
            _c_phase1_blocks((q_ref, k_ref, v_ref, gbp_ref), k2_ref, z_ref, eg_ref, qe2_ref, t_ref, o3, rev)
            for i in (0, 2, 3, 4):
                keep(d, i)
        _c_sweep([(sets[d][0], sets[d][5], sets[d][2], sets[d][1], d == 1) for d in range(2)])
        for d in range(2):
            keep(d, 1)
            qe2_ref, st_ref = sets[d][3], sets[d][1]

            def add_out(rows, chunks, qe2_ref=qe2_ref, st_ref=st_ref):
                o3[chunks] += _bmm_raw(qe2_ref[chunks], st_ref[chunks], "nn")

            _c_blocks(S, add_out)
        for cp in copies:
            cp.wait()

    one_dir = _c_saved_scratch(S) + [pltpu.VMEM((N, LANES, LANES), F32)]
    outs = pl.pallas_call(
        body, name=name, grid=(H, Bq), in_specs=[blk] * 4,
        out_specs=[blk3] + [pl.BlockSpec(memory_space=pltpu.HBM)] * n_saved,
        out_shape=[jax.ShapeDtypeStruct((Bq, N, GDN_CHUNK, D), F32)]
        + [jax.ShapeDtypeStruct(shp, F32) for shp in _c_saved_shapes(Bq, H, S)],
        scratch_shapes=one_dir + one_dir + [pltpu.SemaphoreType.DMA((2, n_saved))],
        compiler_params=_params(("parallel", "parallel")))(q, k, v, gbp)
    return outs[0].reshape(Bq, S, D), tuple(outs[1:])


def _c_mid_bwd(q, k, v, gbp, do, saved, name):
    Bq, S, D = q.shape
    H = D // LANES
    N = S // GDN_CHUNK
    blk = pl.BlockSpec((None, S, LANES), lambda h, b: (b, 0, h))
    blk3 = pl.BlockSpec((None, N, GDN_CHUNK, LANES), lambda h, b: (b, 0, 0, h))
    n_saved = len(saved)

    def body(q_ref, k_ref, v_ref, gbp_ref, do3, *rest):
        saved_hbm = rest[:n_saved]
        dq_ref, dk_ref, dv_ref, dgbp_ref = rest[n_saved:n_saved + 4]
        scr = rest[n_saved + 4:]
        sets = (scr[:n_saved], scr[n_saved:2 * n_saved])
        dso_refs, gs_refs, sems = scr[2 * n_saved:2 * n_saved + 2], scr[2 * n_saved + 2:2 * n_saved + 4], scr[-1]
        in_refs = (q_ref, k_ref, v_ref, gbp_ref)
        out_refs = (dq_ref, dk_ref, dv_ref, dgbp_ref)
        copies = [pltpu.make_async_copy(src.at[pl.program_id(1), pl.program_id(0), d], dst, sems.at[d, i])
                  for d in range(2) for i, (src, dst) in enumerate(zip(saved_hbm, sets[d]))]
        for cp in copies:
            cp.start()
        for cp in copies:
            cp.wait()
        for d in range(2):
            qe2_ref, dso_ref = sets[d][3], dso_refs[d]

            def out_to_state(rows, chunks, qe2_ref=qe2_ref, dso_ref=dso_ref):
                dso_ref[chunks] = _bmm_raw(qe2_ref[chunks], do3[chunks], "tn")

            _c_blocks(S, out_to_state)
        _c_sweep_adjoint([(sets[d][0], sets[d][2], dso_refs[d], gs_refs[d], d == 1) for d in range(2)])
        for d, rev in enumerate((False, True)):
            k2_ref, st_ref, eg_ref, qe2_ref, t_ref = sets[d]
            gs_ref = gs_refs[d]

            def block_vjp(rows, chunks):
                states, t_saved = st_ref[chunks], t_ref[chunks]

                def chunk_fn(q_, k_, v_, gbp_):
                    k2, z, eg, qe2, o0, _ = _c_phase1(q_, k_, v_, gbp_, rev, t_saved)
                    return _c_next_state(states, k2, z, eg), _c_out(states, qe2, o0)

                _, vjp = jax.vjp(chunk_fn, *[r[rows, :] for r in in_refs])
                for r, c in zip(out_refs, vjp((gs_ref[chunks], do3[chunks]))):
                    if rev:
                        r[rows, :] += c
                    else:
                        r[rows, :] = c

            _c_blocks(S, block_vjp)

    shp = jax.ShapeDtypeStruct((Bq, S, D), F32)
    mat = pltpu.VMEM((N, LANES, LANES), F32)
    return pl.pallas_call(
        body, name=name, grid=(H, Bq), in_specs=[blk] * 4 + [blk3] + [pl.BlockSpec(memory_space=pltpu.HBM)] * n_saved,
        out_specs=[blk] * 4, out_shape=[shp] * 4,
        scratch_shapes=_c_saved_scratch(S) + _c_saved_scratch(S) + [mat] * 4 + [pltpu.SemaphoreType.DMA((2, n_saved))],
        compiler_params=_params(("parallel", "parallel")))(q, k, v, gbp, do.reshape(Bq, N, GDN_CHUNK, D), *saved)


def _c_post_fwd(o, z, ng, name):
    Bq, S, D = o.shape
    H = D // LANES
    blk = pl.BlockSpec((None, S, LANES), lambda h, b: (b, 0, h))
    gate = pl.BlockSpec((None, S, LANES), lambda h, b: (b, 0, 3 * H + h))
    vec = pl.BlockSpec((1, LANES), lambda h, b: (0, 0))

    def body(o_ref, zg_ref, ng_ref, y_ref):
        y_ref[...] = _c_post(o_ref[...], zg_ref[...], ng_ref[...]).astype(BF)

    return pl.pallas_call(
        body, name=name, grid=(H, Bq), in_specs=[blk, gate, vec], out_specs=blk,
        out_shape=jax.ShapeDtypeStruct((Bq, S, D), BF), compiler_params=_params(("parallel", "parallel")))(
            o, z, ng.reshape(1, LANES))


def _c_post_bwd(o, z, ng, dy, name):
    Bq, S, D = o.shape
    H = D // LANES
    blk = pl.BlockSpec((None, S, LANES), lambda b, h: (b, 0, h))
    gate = pl.BlockSpec((None, S, LANES), lambda b, h: (b, 0, 3 * H + h))
    vec = pl.BlockSpec((1, LANES), lambda b, h: (0, 0))

    def body(o_ref, zg_ref, ng_ref, dy_ref, do_ref, dzg_ref, dng_ref):
        _, vjp = jax.vjp(_c_post, o_ref[...], zg_ref[...], ng_ref[...])
        do, dzg, dng = vjp(dy_ref[...])
        do_ref[...] = do
        dzg_ref[...] = dzg.astype(BF)

        @pl.when((pl.program_id(0) == 0) & (pl.program_id(1) == 0))
        def _():
            dng_ref[...] = jnp.zeros_like(dng_ref)

        dng_ref[...] += dng

    do, dzg, dng = pl.pallas_call(
        body, name=name, grid=(Bq, H), in_specs=[blk, gate, vec, blk], out_specs=[blk, blk, vec],
        out_shape=[jax.ShapeDtypeStruct((Bq, S, D), F32), jax.ShapeDtypeStruct((Bq, S, D), BF),
                   jax.ShapeDtypeStruct((1, LANES), F32)],
        compiler_params=_params(("arbitrary", "arbitrary")))(o, z, ng.reshape(1, LANES), dy)
    return do, dzg, dng.reshape(LANES)


def _c_param_rows(a_log, dt_bias):
    p = jnp.concatenate([a_log, dt_bias], axis=0).T
    return jnp.broadcast_to(p[:, :, None], p.shape + (LANES,)).astype(F32)


def _local_step(x, tgt, W, by_chip=False):
    Bq, S, D = x.shape
    T = Bq * S
    H = D // LANES
    L = W["norm_mix_g"].shape[0]
    seq = lambda t: t.reshape(Bq, S, t.shape[-1])
    flat = lambda t: t.reshape(T, t.shape[-1])
    if "mlp_up_slots" in W:
        up4, down4 = W["mlp_up_slots"], W["mlp_down_slots"]
    else:
        up4 = W["mlp_w_up"].reshape(L, D, N_CHIPS, -1).transpose(2, 0, 1, 3)
        down4 = W["mlp_w_down"].reshape(L, N_CHIPS, -1, D).transpose(1, 0, 2, 3)
    sw = up4.shape[-1]
    F = N_CHIPS * sw

    xs = flat(x)
    saved = []
    for i in range(L):
        kind, j = i % N_MIXERS, i // N_MIXERS
        tag = f"l{i}"
        sv = {"x": xs}
        hn = _rms_fwd(xs, W["norm_mix_g"][i], f"{tag}_mix_norm")
        sv["hn"] = hn
        if kind == 0:
            z = _mm(hn, W["a_w_in"][j], "nn", f"{tag}_a_in")
            y = _a_core_fwd(seq(z), W["a_conv_w"][j], W["a_conv_b"][j], W["a_gate_w"][j], W["a_gate_b"][j],
                            W["a_lambda"][j], f"{tag}_a_core")
            sv["z"] = z
            w_out = W["a_w_out"][j]
        elif kind == 1:
            z = _mm(hn, W["b_w_in"][j], "nn", f"{tag}_b_in")
            bsf = jnp.repeat(W["b_b_s"][j].T, LANES, axis=1)
            y = _b_core_fwd(z, W["b_ln_g"][j], W["b_ln_b"][j], W["b_w_s"][j], bsf, f"{tag}_b_core")
            sv["z"], sv["bsf"] = z, bsf
            w_out = W["b_w_out"][j]
        else:
            w_in = W["c_w_in"][j]
            w_small = jnp.pad(w_in[:, 4 * D:], ((0, 0), (0, LANES - 4 * H)))
            z = _mm(hn, w_in[:, :4 * D], "nn", f"{tag}_c_in")
            zs = _mm(hn, w_small, "nn", f"{tag}_c_in_small")
            cw3 = W["c_conv_w"][j].reshape(CONV_W, 3, D).transpose(1, 0, 2)
            pc = _c_param_rows(W["c_a_log"][j], W["c_dt_bias"][j])
            q, k, v, gbp = _c_pre_fwd(seq(z), seq(zs), cw3, pc, f"{tag}_c_pre")
            o, sv["mid"] = _c_mid_fwd(q, k, v, gbp, f"{tag}_c_mid")
            y = _c_post_fwd(o, seq(z), W["c_norm_g"][j], f"{tag}_c_post")
            sv.update(z=z, zs=zs, cw3=cw3, pc=pc, q=q, k=k, v=v, gbp=gbp, o=o, w_small=w_small)
            w_out = W["c_w_out"][j]
        y = flat(y)
        sv["y"] = y
        x1 = _mm(y, w_out, "nn", f"{tag}_mix_out", epi="add", extra=xs)
        sv["x1"] = x1
        hn2 = _rms_fwd(x1, W["norm_mlp_g"][i], f"{tag}_mlp_norm")
        act = _mm(hn2, up4, "nn", f"{tag}_mlp_up", out_dtype=BF, epi="relu2", tm=2048, tn=sw, tk=D, n_cols=F,
                  b_index=lambda j, k, i=i: (j, i, 0, 0))
        xs = _mm(act, down4, "nn", f"{tag}_mlp_down", epi="add", extra=x1, tn=D, tk=sw, n_cols=D,
                 b_index=lambda j, k, i=i: (k, i, 0, j))
        sv["hn2"], sv["act"] = hn2, act
        saved.append(sv)

    loss, dx, dgf = _final_loss(xs, W["norm_final_g"], flat(tgt), "final_loss")

    G = {"norm_final_g": dgf}
    per_layer = {n: [None] * L for n in ("norm_mix_g", "norm_mlp_g", "mlp_w_up", "mlp_w_down")}
    mixer = {}
    for i in reversed(range(L)):
        kind, j = i % N_MIXERS, i // N_MIXERS
        tag = f"l{i}"
        sv = saved[i]
        dhid = _mm(dx, down4, "nt", f"{tag}_mlp_dhid", out_dtype=BF, epi="relu2_bwd", extra=sv["act"], tm=2048, tn=sw, tk=D,
                   n_cols=F, b_index=lambda j, k, i=i: (j, i, 0, 0))
        per_layer["mlp_w_down"][i] = _mm(sv["act"], dx, "tn", f"{tag}_mlp_dwdown").reshape(N_CHIPS, sw, D)
        per_layer["mlp_w_up"][i] = _mm(sv["hn2"], dhid, "tn", f"{tag}_mlp_dwup", tn=sw, by_chip="cols")
        dhn2 = _mm(dhid, up4, "nt", f"{tag}_mlp_dhn", tn=D, tk=sw, n_cols=D, b_index=lambda j, k, i=i: (k, i, j, 0))
        dx, per_layer["norm_mlp_g"][i] = _rms_bwd(sv["x1"], W["norm_mlp_g"][i], dhn2, dx, f"{tag}_mlp_norm_bwd")
        g = {}
        if kind == 0:
            dy = _mm(dx, W["a_w_out"][j], "nt", f"{tag}_a_dy")
            g["a_w_out"] = _mm(sv["y"], dx, "tn", f"{tag}_a_dwout")
            dz, g["a_conv_w"], g["a_conv_b"], g["a_gate_w"], g["a_gate_b"], g["a_lambda"] = _a_core_bwd(
                seq(sv["z"]), seq(dy), W["a_conv_w"][j], W["a_conv_b"][j], W["a_gate_w"][j], W["a_gate_b"][j],
                W["a_lambda"][j], f"{tag}_a_core_bwd")
            dz = flat(dz)
            g["a_w_in"] = _mm(sv["hn"], dz, "tn", f"{tag}_a_dwin")
            dhn = _mm(dz, W["a_w_in"][j], "nt", f"{tag}_a_dhn")
        elif kind == 1:
            dy = _mm(dx, W["b_w_out"][j], "nt", f"{tag}_b_dy")
            g["b_w_out"] = _mm(sv["y"], dx, "tn", f"{tag}_b_dwout")
            dz, g["b_ln_g"], g["b_ln_b"], g["b_w_s"], dbsf = _b_core_bwd(
                sv["z"], dy, W["b_ln_g"][j], W["b_ln_b"][j], W["b_w_s"][j], sv["bsf"], f"{tag}_b_core_bwd")
            g["b_b_s"] = dbsf.reshape(SGU_CHUNK, H, LANES).sum(-1).T
            g["b_w_in"] = _mm(sv["hn"], dz, "tn", f"{tag}_b_dwin")
            dhn = _mm(dz, W["b_w_in"][j], "nt", f"{tag}_b_dhn")
        else:
            dy = _mm(dx, W["c_w_out"][j], "nt", f"{tag}_c_dy")
            g["c_w_out"] = _mm(sv["y"], dx, "tn", f"{tag}_c_dwout")
            do, dzg, g["c_norm_g"] = _c_post_bwd(sv["o"], seq(sv["z"]), W["c_norm_g"][j], seq(dy), f"{tag}_c_post_bwd")
            dq, dk, dv, dgbp = _c_mid_bwd(sv["q"], sv["k"], sv["v"], sv["gbp"], do, sv["mid"], f"{tag}_c_mid_bwd")
            dzq, dzk, dzv, dzs_h, dcw3, dpc = _c_pre_bwd(seq(sv["z"]), seq(sv["zs"]), sv["cw3"], sv["pc"], dq, dk, dv, dgbp,
                                                         f"{tag}_c_pre_bwd")
            dz = flat(jnp.concatenate([dzq, dzk, dzv, dzg], axis=-1))
            dzs = flat(dzs_h.sum(0)).astype(BF)
            g["c_conv_w"] = dcw3.transpose(1, 0, 2).reshape(CONV_W, 3 * D)
            dpc = dpc.sum(-1)
            g["c_a_log"], g["c_dt_bias"] = dpc[:, :2].T, dpc[:, 2:].T
            dw_main = _mm(sv["hn"], dz, "tn", f"{tag}_c_dwin")
            dw_small = _mm(sv["hn"], dzs, "tn", f"{tag}_c_dwin_small")
            g["c_w_in"] = jnp.concatenate([dw_main, dw_small[:, :4 * H]], axis=1)
            dhn = _mm(dz, W["c_w_in"][j][:, :4 * D], "nt", f"{tag}_c_dhn")
            dhn = _mm(dzs, sv["w_small"], "nt", f"{tag}_c_dhn_small", epi="add", extra=dhn)
        dx, per_layer["norm_mix_g"][i] = _rms_bwd(sv["x"], W["norm_mix_g"][i], dhn, dx, f"{tag}_mix_norm_bwd")
        for n, val in g.items():
            mixer.setdefault(n, {})[j] = val

    for n, vals in per_layer.items():
        if n not in ("mlp_w_up", "mlp_w_down"):
            G[n] = jnp.stack(vals)
        elif by_chip:
            G[n] = vals
        elif n == "mlp_w_up":
            G[n] = jnp.stack(vals).transpose(0, 2, 1, 3).reshape(L, D, F)
        else:
            G[n] = jnp.stack(vals).reshape(L, F, D)
    for n, by_j in mixer.items():
        G[n] = jnp.stack([by_j[j] for j in sorted(by_j)])
    return loss, dx.reshape(Bq, S, D), G


MESH = pl.DeviceIdType.MESH
N_CHIPS = 4
HBM_SPEC = pl.BlockSpec(memory_space=pltpu.HBM)


def _place():
    x, y, c = lax.axis_index("x"), lax.axis_index("y"), lax.axis_index("c")
    others = [(1 - x, y), (x, 1 - y), (1 - x, 1 - y)]
    return x, y, c, others


def _all_gather_xy(bufs, name):
    n = len(bufs)
    pieces = [_stage_rows(b.shape[0], b.shape[1] * b.dtype.itemsize) for b in bufs]

    def body(*refs):
        ins, outs = refs[:n], refs[n:2 * n]
        send, recv, fsend, frecv = refs[2 * n:2 * n + 4]
        stages = refs[2 * n + 4:]
        x, y, c, others = _place()
        p = 2 * x + y
        half = lambda b, cc: pl.ds(cc * (ins[b].shape[0] // 2), ins[b].shape[0] // 2)

        def ici(b, j):
            qx, qy = others[j]
            return pltpu.make_async_remote_copy(
                src_ref=ins[b].at[half(b, c)], dst_ref=outs[b].at[p, half(b, c)], send_sem=send.at[b, j],
                recv_sem=recv.at[b, j], device_id=(qx, qy, c), device_id_type=MESH)

        def landed(b, j, cc):
            qx, qy = others[j]
            return outs[b].at[2 * qx + qy, half(b, cc)]

        def d2d(b, j):
            return pltpu.make_async_remote_copy(
                src_ref=landed(b, j, c), dst_ref=landed(b, j, c), send_sem=fsend.at[b, j], recv_sem=frecv.at[b, j],
                device_id=(x, y, 1 - c), device_id_type=MESH)

        pairs = [(b, j) for b in range(n) for j in range(3)]
        for b, j in pairs:
            ici(b, j).start()
        for b in range(n):
            def own_piece(i, carry, b=b):
                rows = pl.ds(pl.multiple_of(i * pieces[b], pieces[b]), pieces[b])
                pltpu.sync_copy(ins[b].at[rows], stages[b])
                pltpu.sync_copy(stages[b], outs[b].at[p, rows])
                return carry

            lax.fori_loop(0, ins[b].shape[0] // pieces[b], own_piece, 0)
        for b, j in pairs:
            pltpu.make_async_remote_copy(
                src_ref=ins[b].at[half(b, c)], dst_ref=landed(b, j, c), send_sem=send.at[b, j], recv_sem=recv.at[b, j],
                device_id=(x, y, c), device_id_type=MESH).wait_recv()
            d2d(b, j).start()
        for b, j in pairs:
            pltpu.make_async_remote_copy(
                src_ref=landed(b, j, 1 - c), dst_ref=landed(b, j, 1 - c), send_sem=fsend.at[b, j], recv_sem=frecv.at[b, j],
                device_id=(x, y, 1 - c), device_id_type=MESH).wait_recv()
        for b, j in pairs:
            ici(b, j).wait_send()
            d2d(b, j).wait_send()

    return pl.pallas_call(
        body, name=name, in_specs=[HBM_SPEC] * n, out_specs=[HBM_SPEC] * n,
        out_shape=[jax.ShapeDtypeStruct((N_CHIPS,) + b.shape, b.dtype) for b in bufs],
        scratch_shapes=[pltpu.SemaphoreType.DMA((n, 3))] * 4
        + [pltpu.VMEM((r, b.shape[1]), b.dtype) for r, b in zip(pieces, bufs)],
        compiler_params=pltpu.CompilerParams(has_side_effects=True, vmem_limit_bytes=VMEM_LIMIT))(*bufs)


STAGE_BYTES = 2 * 1024 * 1024


def _stage_rows(rows, row_bytes):
    for d in range(min(rows, max(1, STAGE_BYTES // row_bytes)), 0, -1):
        if rows % d == 0 and (d % 16 == 0 or d == rows):
            return d
    return rows


def _swap_halves(gs, name):
    n = len(gs)

    def body(*refs):
        g_refs, o_refs, send, recv = refs[:n], refs[n:2 * n], refs[2 * n], refs[2 * n + 1]
        x, y, c, _ = _place()
        cps = [pltpu.make_async_remote_copy(src_ref=g_refs[b].at[:, 1 - c], dst_ref=o_refs[b], send_sem=send.at[b],
                                            recv_sem=recv.at[b], device_id=(x, y, 1 - c), device_id_type=MESH)
               for b in range(n)]
        for cp in cps:
            cp.start()
        for cp in cps:
            cp.wait()

    return pl.pallas_call(
        body, name=name, in_specs=[HBM_SPEC] * n, out_specs=[HBM_SPEC] * n,
        out_shape=[jax.ShapeDtypeStruct((g.shape[0],) + g.shape[2:], g.dtype) for g in gs],
        scratch_shapes=[pltpu.SemaphoreType.DMA((n,)), pltpu.SemaphoreType.DMA((n,))],
        compiler_params=pltpu.CompilerParams(has_side_effects=True))(*gs)


def _pair_sum(g, got, out_dtype, name):
    nq, _, h, cols = g.shape
    tr = min(512, h)

    def body(c_ref, g_ref, r_ref, o_ref):
        o_ref[...] = (g_ref[...] + r_ref[...]).astype(o_ref.dtype)

    spec = pl.BlockSpec((None, tr, cols), lambda q, i, c_ref: (q, i, 0))
    return pl.pallas_call(
        body, name=name,
        grid_spec=pltpu.PrefetchScalarGridSpec(
            num_scalar_prefetch=1, grid=(nq, h // tr),
            in_specs=[pl.BlockSpec((None, None, tr, cols), lambda q, i, c_ref: (q, c_ref[0], i, 0)), spec], out_specs=spec),
        out_shape=jax.ShapeDtypeStruct((nq, h, cols), out_dtype),
        compiler_params=_params(("parallel", "parallel")))(lax.axis_index("c").astype(jnp.int32).reshape(1), g, got)


def _scatter_xy(p_sums, name):
    n = len(p_sums)

    def body(*refs):
        p_refs, o_refs, send, recv = refs[:n], refs[n:2 * n], refs[2 * n], refs[2 * n + 1]
        x, y, c, others = _place()
        me = 2 * x + y
        cps = []
        for b in range(n):
            for j, (qx, qy) in enumerate(others):
                cps.append(pltpu.make_async_remote_copy(
                    src_ref=p_refs[b].at[2 * qx + qy], dst_ref=o_refs[b].at[me], send_sem=send.at[b, j],
                    recv_sem=recv.at[b, j], device_id=(qx, qy, c), device_id_type=MESH))
                cps[-1].start()
        for b in range(n):
            for j, (qx, qy) in enumerate(others):
                pltpu.make_async_remote_copy(
                    src_ref=p_refs[b].at[me], dst_ref=o_refs[b].at[2 * qx + qy], send_sem=send.at[b, j],
                    recv_sem=recv.at[b, j], device_id=(qx, qy, c), device_id_type=MESH).wait_recv()
        for cp in cps:
            cp.wait_send()

    return pl.pallas_call(
        body, name=name, in_specs=[HBM_SPEC] * n, out_specs=[HBM_SPEC] * n,
        out_shape=[jax.ShapeDtypeStruct(p.shape, p.dtype) for p in p_sums],
        scratch_shapes=[pltpu.SemaphoreType.DMA((n, 3)), pltpu.SemaphoreType.DMA((n, 3))],
        compiler_params=pltpu.CompilerParams(has_side_effects=True))(*p_sums)


def _chip_sum(r4, p_sum, name):
    nq, h, cols = r4.shape
    tr = min(512, h)

    def body(r_ref, p_ref, o_ref):
        me = 2 * lax.axis_index("x") + lax.axis_index("y")
        f = lambda q: jnp.where(me == q, p_ref[q], r_ref[q]).astype(F32)
        o_ref[...] = ((f(0) + f(1)) + f(2)) + f(3)

    spec = pl.BlockSpec((nq, tr, cols), lambda i: (0, i, 0))
    return pl.pallas_call(
        body, name=name, grid=(h // tr,), in_specs=[spec, spec],
        out_specs=pl.BlockSpec((tr, cols), lambda i: (i, 0)), out_shape=jax.ShapeDtypeStruct((h, cols), F32),
        compiler_params=_params(("parallel",)))(r4, p_sum)


def _join_halves(rs, name):
    n = len(rs)

    def body(*refs):
        r_refs, o_refs, send, recv = refs[:n], refs[n:2 * n], refs[2 * n], refs[2 * n + 1]
        x, y, c, _ = _place()
        cps = [pltpu.make_async_remote_copy(src_ref=r_refs[b], dst_ref=o_refs[b].at[c], send_sem=send.at[b],
                                            recv_sem=recv.at[b], device_id=(x, y, 1 - c), device_id_type=MESH)
               for b in range(n)]
        for cp in cps:
            cp.start()
        for b in range(n):
            pltpu.make_async_remote_copy(src_ref=r_refs[b], dst_ref=o_refs[b].at[1 - c], send_sem=send.at[b],
                                         recv_sem=recv.at[b], device_id=(x, y, 1 - c), device_id_type=MESH).wait_recv()
        for cp in cps:
            cp.wait_send()

    gots = pl.pallas_call(
        body, name=name, in_specs=[HBM_SPEC] * n, out_specs=[HBM_SPEC] * n,
        out_shape=[jax.ShapeDtypeStruct((2,) + r.shape, r.dtype) for r in rs],
        scratch_shapes=[pltpu.SemaphoreType.DMA((n,)), pltpu.SemaphoreType.DMA((n,))],
        compiler_params=pltpu.CompilerParams(has_side_effects=True))(*rs)
    c = lax.axis_index("c")
    return [jnp.stack([jnp.where(c == s, r, got[s]) for s in range(2)]) for r, got in zip(rs, gots)]


def _reduce_scatter(gs, tag):
    gs = [g.reshape(g.shape[0], 2, g.shape[1] // 2, g.shape[2]) for g in gs]
    gots = _swap_halves(gs, f"{tag}_swap")
    pairs = [_pair_sum(g, got, BF, f"{tag}_pair_sum{i}") for i, (g, got) in enumerate(zip(gs, gots))]
    r4s = _scatter_xy(pairs, f"{tag}_scatter")
    rs = [_chip_sum(r4, pair, f"{tag}_chip_sum{i}") for i, (r4, pair) in enumerate(zip(r4s, pairs))]
    return [j.reshape(-1, j.shape[-1]) for j in _join_halves(rs, f"{tag}_join")]


def _adamw(w, g, m, v, name):
    rows, cols = w.shape
    tr = rows
    for cand in (512, 344, 256, 128, 64, 32, 16, 8):
        if rows % cand == 0:
            tr = cand
            break

    def body(w_ref, g_ref, m_ref, v_ref, d_ref, nm_ref, nv_ref):
        g_ = g_ref[...]
        m_ = ADAM_B1 * m_ref[...] + (1.0 - ADAM_B1) * g_
        v_ = ADAM_B2 * v_ref[...] + (1.0 - ADAM_B2) * jnp.square(g_)
        m_hat = m_ / (1.0 - ADAM_B1 ** ADAM_STEP)
        v_hat = v_ / (1.0 - ADAM_B2 ** ADAM_STEP)
        d_ref[...] = -ADAM_LR * (m_hat / (jnp.sqrt(v_hat) + ADAM_EPS) + ADAM_WD * w_ref[...])
        nm_ref[...] = m_
        nv_ref[...] = v_

    spec = pl.BlockSpec((tr, cols), lambda i: (i, 0))
    shp = jax.ShapeDtypeStruct((rows, cols), F32)
    return pl.pallas_call(body, name=name, grid=(rows // tr,), in_specs=[spec] * 4, out_specs=[spec] * 3,
                          out_shape=[shp] * 3, compiler_params=_params(("parallel",)))(w, g, m, v)


WEIGHTS = ["norm_mix_g", "norm_mlp_g", "mlp_w_up", "mlp_w_down", "norm_final_g", "a_w_in", "a_conv_w", "a_conv_b",
           "a_gate_w", "a_gate_b", "a_lambda", "a_w_out", "b_w_in", "b_ln_g", "b_ln_b", "b_w_s", "b_b_s", "b_w_out",
           "c_w_in", "c_conv_w", "c_a_log", "c_dt_bias", "c_norm_g", "c_w_out"]
SHARD_AXIS = {"mlp_w_up": 2, "mlp_w_down": 1, "a_w_in": 2, "a_conv_w": 2, "a_conv_b": 1, "a_lambda": 2, "a_w_out": 1,
              "b_w_in": 2, "b_w_out": 1, "c_w_in": 2, "c_conv_w": 2, "c_w_out": 1}
MATMUL_WEIGHTS = ["mlp_w_up", "mlp_w_down", "a_w_in", "a_w_out", "b_w_in", "b_w_out", "c_w_out", "c_w_in"]
SMALL_SHARDED = ["a_conv_w", "a_conv_b", "a_lambda", "c_conv_w"]
REPLICATED = [n for n in WEIGHTS if n not in SHARD_AXIS]
FLAT_COLS = 1024


def _rows_of(shape):
    n = 1
    for s in shape:
        n *= s
    return -(-n // FLAT_COLS)


def _pack(arrays, total_rows, dtype):
    parts = []
    used = 0
    for a in arrays:
        r = _rows_of(a.shape)
        f = a.reshape(-1).astype(dtype)
        parts.append(jnp.pad(f, (0, r * FLAT_COLS - f.shape[0])).reshape(r, FLAT_COLS))
        used += r
    if total_rows > used:
        parts.append(jnp.zeros((total_rows - used, FLAT_COLS), dtype))
    return jnp.concatenate(parts, axis=0)


def _slots_of(g, axis):
    w = g.shape[axis] // N_CHIPS
    flat = jnp.stack([lax.slice_in_dim(g, q * w, (q + 1) * w, axis=axis).reshape(-1) for q in range(N_CHIPS)])
    rows = _rows_of(flat.shape[1:])
    return jnp.pad(flat, ((0, 0), (0, rows * FLAT_COLS - flat.shape[1]))).reshape(N_CHIPS, rows, FLAT_COLS)


def _unpack(buf, shapes):
    out, r0 = [], 0
    for shp in shapes:
        r = _rows_of(shp)
        n = 1
        for s in shp:
            n *= s
        out.append(buf[r0:r0 + r].reshape(-1)[:n].reshape(shp))
        r0 += r
    return out


def _round_up(n, m):
    return -(-n // m) * m


def kernel(x, norm_mix_g, norm_mlp_g, mlp_w_up, mlp_w_down, norm_final_g, a_w_in, a_conv_w, a_conv_b, a_gate_w, a_gate_b, a_lambda, a_w_out, b_w_in, b_ln_g, b_ln_b, b_w_s, b_b_s, b_w_out, c_w_in, c_conv_w, c_a_log, c_dt_bias, c_norm_g, c_w_out, loss_target, m_norm_mix_g, m_norm_mlp_g, m_mlp_w_up, m_mlp_w_down, m_norm_final_g, m_a_w_in, m_a_conv_w, m_a_conv_b, m_a_gate_w, m_a_gate_b, m_a_lambda, m_a_w_out, m_b_w_in, m_b_ln_g, m_b_ln_b, m_b_w_s, m_b_b_s, m_b_w_out, m_c_w_in, m_c_conv_w, m_c_a_log, m_c_dt_bias, m_c_norm_g, m_c_w_out, v_norm_mix_g, v_norm_mlp_g, v_mlp_w_up, v_mlp_w_down, v_norm_final_g, v_a_w_in, v_a_conv_w, v_a_conv_b, v_a_gate_w, v_a_gate_b, v_a_lambda, v_a_w_out, v_b_w_in, v_b_ln_g, v_b_ln_b, v_b_w_s, v_b_b_s, v_b_w_out, v_c_w_in, v_c_conv_w, v_c_a_log, v_c_dt_bias, v_c_norm_g, v_c_w_out):
    given = dict(locals())
    w_loc = {n: given[n] for n in WEIGHTS}
    m_loc = {n: given["m_" + n] for n in WEIGHTS}
    v_loc = {n: given["v_" + n] for n in WEIGHTS}

    mlp = ["mlp_w_up", "mlp_w_down"]
    rest = [n for n in MATMUL_WEIGHTS if n not in mlp]
    rest_rows = _round_up(sum(_rows_of(w_loc[n].shape) for n in rest), 512)
    small_rows = _round_up(sum(_rows_of(w_loc[n].shape) for n in SMALL_SHARDED), 16)
    as_rows = lambda w: w.astype(BF).reshape(-1, w.shape[-1])
    up_all, down_all, rest_all, small_all = _all_gather_xy(
        [as_rows(mlp_w_up), as_rows(mlp_w_down), _pack([w_loc[n] for n in rest], rest_rows, BF),
         _pack([w_loc[n] for n in SMALL_SHARDED], small_rows, F32)], "gather_weights")
    W = {n: w_loc[n] for n in REPLICATED}
    W["mlp_up_slots"] = up_all.reshape((N_CHIPS,) + mlp_w_up.shape)
    W["mlp_down_slots"] = down_all.reshape((N_CHIPS,) + mlp_w_down.shape)
    for names, buf in ((rest, rest_all), (SMALL_SHARDED, small_all)):
        per_chip = [_unpack(buf[q], [w_loc[n].shape for n in names]) for q in range(N_CHIPS)]
        for i, n in enumerate(names):
            W[n] = jnp.concatenate([per_chip[q][i] for q in range(N_CHIPS)], axis=SHARD_AXIS[n])

    loss, grad_x, G = _local_step(x, loss_target, W, by_chip=True)
    loss = lax.psum(loss, ("x", "y", "c"))

    sharded = rest + SMALL_SHARDED
    rep_rows = _round_up(sum(_rows_of(w_loc[n].shape) for n in REPLICATED), N_CHIPS * 16)
    rep_flat = _pack([G[n] for n in REPLICATED], rep_rows, F32).reshape(N_CHIPS, rep_rows // N_CHIPS, FLAT_COLS)
    shard_rows = sum(_rows_of(w_loc[n].shape) for n in sharded)
    total_rows = _round_up(shard_rows + rep_rows // N_CHIPS, 1024)
    parts = [_slots_of(G[n], SHARD_AXIS[n]) for n in sharded] + [rep_flat]
    parts.append(jnp.zeros((N_CHIPS, total_rows - shard_rows - rep_rows // N_CHIPS, FLAT_COLS), F32))
    mlp_bufs = [g.reshape(N_CHIPS, -1, g.shape[-1]) for n in mlp for g in G[n]]
    *mlp_red, red = _reduce_scatter(mlp_bufs + [jnp.concatenate(parts, axis=1)], "grads")
    n_layers = mlp_w_up.shape[0]
    g_loc = {"mlp_w_up": jnp.stack(mlp_red[:n_layers]).reshape(mlp_w_up.shape),
             "mlp_w_down": jnp.stack(mlp_red[n_layers:]).reshape(mlp_w_down.shape)}
    g_loc.update(zip(sharded, _unpack(red, [w_loc[n].shape for n in sharded])))
    rep_quarter = red[shard_rows:shard_rows + rep_rows // N_CHIPS]
    (rep_all,) = _all_gather_xy([rep_quarter], "gather_replicated_grads")
    g_loc.update(zip(REPLICATED, _unpack(rep_all.reshape(rep_rows, FLAT_COLS), [w_loc[n].shape for n in REPLICATED])))

    delta, new_m, new_v = {}, {}, {}
    big = [n for n in MATMUL_WEIGHTS if w_loc[n].size % FLAT_COLS == 0]
    for n in big:
        shp = w_loc[n].shape
        two_d = lambda a: a.reshape(-1, FLAT_COLS)
        d, nm, nv = _adamw(two_d(w_loc[n]), two_d(g_loc[n]), two_d(m_loc[n]), two_d(v_loc[n]), f"adamw_{n}")
        delta[n], new_m[n], new_v[n] = d.reshape(shp), nm.reshape(shp), nv.reshape(shp)
    small = [n for n in WEIGHTS if n not in big]
    small_shapes = [w_loc[n].shape for n in small]
    rows = _round_up(sum(_rows_of(s) for s in small_shapes), 8)
    packed = [_pack([src[n] for n in small], rows, F32) for src in (w_loc, g_loc, m_loc, v_loc)]
    for dst, buf in zip((delta, new_m, new_v), _adamw(*packed, "adamw_small")):
        dst.update(zip(small, _unpack(buf, small_shapes)))

    return (loss, grad_x, *[g_loc[n] for n in WEIGHTS], *[delta[n] for n in WEIGHTS],
            *[new_m[n] for n in WEIGHTS], *[new_v[n] for n in WEIGHTS])
```

```python
import functools

import jax
import jax.numpy as jnp
from jax import lax
from jax.experimental import pallas as pl
from jax.experimental.pallas import tpu as pltpu

F32 = jnp.float32
BF = jnp.bfloat16

LANES = 128
VMEM_LIMIT = 56 * 1024 * 1024
RG_C = 8.0
SGU_CHUNK = 128
GDN_CHUNK = 64
CONV_W = 4
N_MIXERS = 3

ADAM_LR = 0.001
ADAM_B1 = 0.9
ADAM_B2 = 0.999
ADAM_EPS = 1e-08
ADAM_WD = 0.01
ADAM_STEP = 10


VMEM_LIMIT_HIGH = 60 * 1024 * 1024


def _params(sem=None, vmem=VMEM_LIMIT):
    return pltpu.CompilerParams(dimension_semantics=sem, vmem_limit_bytes=vmem)


def _shift_impl(x, s):
    if s == 0:
        return x
    n = x.shape[0]
    row = lax.broadcasted_iota(jnp.int32, x.shape, 0)
    if s > 0:
        return jnp.where(row >= s, pltpu.roll(x, s, 0), 0.0)
    return jnp.where(row < n + s, pltpu.roll(x, n + s, 0), 0.0)


@functools.partial(jax.custom_vjp, nondiff_argnums=(1,))
def _shift(x, s):
    return _shift_impl(x, s)


def _shift_fwd(x, s):
    return _shift_impl(x, s), None


def _shift_bwd(s, _, g):
    return (_shift_impl(g, -s),)


_shift.defvjp(_shift_fwd, _shift_bwd)


def _chunk_cumsum_impl(x, rev, chunk):
    n = x.shape[0]
    rc = lax.broadcasted_iota(jnp.int32, x.shape, 0) & (chunk - 1)
    sh = 1
    while sh < chunk:
        if rev:
            x = x + jnp.where(rc < chunk - sh, pltpu.roll(x, n - sh, 0), 0.0)
        else:
            x = x + jnp.where(rc >= sh, pltpu.roll(x, sh, 0), 0.0)
        sh *= 2
    return x


@functools.partial(jax.custom_vjp, nondiff_argnums=(1, 2))
def _chunk_cumsum(x, rev, chunk):
    return _chunk_cumsum_impl(x, rev, chunk)


def _chunk_cumsum_fwd(x, rev, chunk):
    return _chunk_cumsum_impl(x, rev, chunk), None


def _chunk_cumsum_bwd(rev, chunk, _, g):
    return (_chunk_cumsum_impl(g, not rev, chunk),)


_chunk_cumsum.defvjp(_chunk_cumsum_fwd, _chunk_cumsum_bwd)


def _rms(x, g, eps=1e-6):
    return x * lax.rsqrt(jnp.mean(x * x, axis=-1, keepdims=True) + eps) * g


def _sigmoid(x):
    return 0.5 * jnp.tanh(0.5 * x) + 0.5


def _silu(x):
    return x * _sigmoid(x)


def _softplus(x):
    return jnp.maximum(x, 0.0) + jnp.log1p(jnp.exp(-jnp.abs(x)))


def _neg_expm1(y, ey):
    series = -y * (1.0 + y * (1 / 2) * (1.0 + y * (1 / 3) * (1.0 + y * (1 / 4))))
    return jnp.where(y > -1 / 32, series, 1.0 - ey)


@jax.custom_vjp
def _sqrt_one_minus_sq(log_a, a):
    t = _neg_expm1(2.0 * log_a, a * a)
    return t * lax.rsqrt(jnp.maximum(t, 1e-30))


def _sqrt_one_minus_sq_fwd(log_a, a):
    t = _neg_expm1(2.0 * log_a, a * a)
    rs = lax.rsqrt(jnp.maximum(t, 1e-30))
    return t * rs, (a, rs)


def _sqrt_one_minus_sq_bwd(res, g):
    a, rs = res
    return -g * (a * a) * rs, jnp.zeros_like(a)


_sqrt_one_minus_sq.defvjp(_sqrt_one_minus_sq_fwd, _sqrt_one_minus_sq_bwd)


def _bmm_raw(a, b, form):
    r = a.ndim - 2
    con = {"nn": ((r + 1,), (r,)), "nt": ((r + 1,), (r + 1,)), "tn": ((r,), (r,))}[form]
    batch = ((0,), (0,)) if r else ((), ())
    return lax.dot_general(a.astype(BF), b.astype(BF), (con, batch), preferred_element_type=F32)


@functools.partial(jax.custom_vjp, nondiff_argnums=(2,))
def _bmm(a, b, form):
    return _bmm_raw(a, b, form)


def _bmm_fwd(a, b, form):
    return _bmm_raw(a, b, form), (a, b)


def _bmm_bwd(form, res, g):
    a, b = res
    if form == "nn":
        da, db = _bmm_raw(g, b, "nt"), _bmm_raw(a, g, "tn")
    elif form == "nt":
        da, db = _bmm_raw(g, b, "nn"), _bmm_raw(g, a, "tn")
    else:
        da, db = _bmm_raw(b, g, "nt"), _bmm_raw(a, g, "nn")
    return da.astype(a.dtype), db.astype(b.dtype)


_bmm.defvjp(_bmm_fwd, _bmm_bwd)


def _conv4(z, rows):
    out = rows[0] * _shift(z, 2)
    for k in range(1, CONV_W):
        out = out + rows[k] * _shift(z, 2 - k)
    return out


_DIMS = {"nn": ((1,), (0,)), "nt": ((1,), (1,)), "tn": ((0,), (0,))}


def _mm(a, b, mode, name, *, out_dtype=F32, epi=None, extra=None, tm=2048, tn=1024, tk=1024, b_index=None, n_cols=None,
        by_chip=None):
    if mode == "tn":
        K, M = a.shape
    else:
        M, K = a.shape
    N = n_cols if b_index is not None else (b.shape[0] if mode == "nt" else b.shape[1])
    if epi == "add":
        tm = min(tm, 1024)
    tm, tn, tk = min(tm, M), min(tn, N), min(tk, K)
    assert M % tm == 0 and N % tn == 0 and K % tk == 0, (name, M, N, K)
    nk = K // tk
    a_spec = pl.BlockSpec((tk, tm), lambda i, j, k: (k, i)) if mode == "tn" else pl.BlockSpec((tm, tk), lambda i, j, k: (i, k))
    b_block = (tn, tk) if mode == "nt" else (tk, tn)
    if b_index is not None:
        b_spec = pl.BlockSpec((None,) * (b.ndim - 2) + b_block, lambda i, j, k: b_index(j, k))
    elif mode == "nt":
        b_spec = pl.BlockSpec(b_block, lambda i, j, k: (j, k))
    else:
        b_spec = pl.BlockSpec(b_block, lambda i, j, k: (k, j))
    o_spec = pl.BlockSpec((tm, tn), lambda i, j, k: (i, j))
    ins, specs = [a, b], [a_spec, b_spec]
    if epi in ("add", "relu2_bwd"):
        ins.append(extra)
        specs.append(o_spec)
    o_shape = (M, N)
    if by_chip == "cols":
        assert tn == N // N_CHIPS and epi is None
        o_shape, o_spec = (N_CHIPS, M, tn), pl.BlockSpec((None, tm, tn), lambda i, j, k: (j, i, 0))

    def body(*refs):
        a_ref, b_ref = refs[0], refs[1]
        e_ref = refs[2] if len(ins) == 3 else None
        o_ref = refs[len(ins)]

        def product():
            return lax.dot_general(a_ref[...].astype(BF), b_ref[...].astype(BF), (_DIMS[mode], ((), ())),
                                   preferred_element_type=F32)

        def finish(r):
            if epi == "relu2":
                r = jnp.square(jnp.maximum(r, 0.0))
            elif epi == "add":
                r = r + e_ref[...]
            elif epi == "relu2_bwd":
                e = e_ref[...].astype(F32)
                r = r * (2.0 * e * lax.rsqrt(jnp.maximum(e, 1e-30)))
            o_ref[...] = r.astype(out_dtype)

        if nk == 1:
            finish(product())
            return
        acc = refs[-1]
        k = pl.program_id(2)

        @pl.when(k == 0)
        def _():
            acc[...] = product()

        @pl.when(k > 0)
        def _():
            acc[...] += product()

        @pl.when(k == nk - 1)
        def _():
            finish(acc[...])

    return pl.pallas_call(
        body, name=name, grid=(M // tm, N // tn, nk), in_specs=specs, out_specs=o_spec,
        out_shape=jax.ShapeDtypeStruct(o_shape, out_dtype),
        scratch_shapes=[pltpu.VMEM((tm, tn), F32)] if nk > 1 else [],
        compiler_params=_params(("parallel", "parallel", "arbitrary")))(*ins)


def _rows_tile(T):
    return min(512, T)


def _rms_fwd(x, g, name):
    T, D = x.shape
    tr = _rows_tile(T)

    def body(x_ref, g_ref, o_ref):
        o_ref[...] = _rms(x_ref[...], g_ref[...]).astype(BF)

    return pl.pallas_call(
        body, name=name, grid=(T // tr,),
        in_specs=[pl.BlockSpec((tr, D), lambda i: (i, 0)), pl.BlockSpec((1, D), lambda i: (0, 0))],
        out_specs=pl.BlockSpec((tr, D), lambda i: (i, 0)), out_shape=jax.ShapeDtypeStruct((T, D), BF),
        compiler_params=_params(("parallel",)))(x, g.reshape(1, D))


def _rms_bwd(x, g, dhn, dres, name):
    T, D = x.shape
    tr = _rows_tile(T)

    def body(x_ref, g_ref, dhn_ref, dres_ref, dx_ref, dg_ref):
        _, vjp = jax.vjp(_rms, x_ref[...], g_ref[...])
        dx, dg = vjp(dhn_ref[...])
        dx_ref[...] = dres_ref[...] + dx

        @pl.when(pl.program_id(0) == 0)
        def _():
            dg_ref[...] = jnp.zeros_like(dg_ref)

        dg_ref[...] += dg

    row = pl.BlockSpec((tr, D), lambda i: (i, 0))
    vec = pl.BlockSpec((1, D), lambda i: (0, 0))
    dx, dg = pl.pallas_call(
        body, name=name, grid=(T // tr,), in_specs=[row, vec, row, row], out_specs=[row, vec],
        out_shape=[jax.ShapeDtypeStruct((T, D), F32), jax.ShapeDtypeStruct((1, D), F32)],
        compiler_params=_params(("arbitrary",)))(x, g.reshape(1, D), dhn, dres)
    return dx, dg.reshape(D)


def _final_loss(x, g, tgt, name):
    T, D = x.shape
    tr = _rows_tile(T)

    def body(x_ref, g_ref, t_ref, l_ref, dx_ref, dg_ref):
        y, vjp = jax.vjp(_rms, x_ref[...], g_ref[...])
        err = y - t_ref[...]
        dx, dg = vjp(err * (1.0 / D))
        dx_ref[...] = dx

        @pl.when(pl.program_id(0) == 0)
        def _():
            dg_ref[...] = jnp.zeros_like(dg_ref)
            l_ref[...] = jnp.zeros_like(l_ref)

        dg_ref[...] += dg
        l_ref[...] += (0.5 / D) * jnp.sum(jnp.sum(err * err, axis=1, keepdims=True), axis=0, keepdims=True)

    row = pl.BlockSpec((tr, D), lambda i: (i, 0))
    vec = pl.BlockSpec((1, D), lambda i: (0, 0))
    loss, dx, dg = pl.pallas_call(
        body, name=name, grid=(T // tr,), in_specs=[row, vec, row],
        out_specs=[pl.BlockSpec((1, LANES), lambda i: (0, 0)), row, vec],
        out_shape=[jax.ShapeDtypeStruct((1, LANES), F32), jax.ShapeDtypeStruct((T, D), F32),
                   jax.ShapeDtypeStruct((1, D), F32)],
        compiler_params=_params(("arbitrary",)))(x, g.reshape(1, D), tgt)
    return loss[0, 0], dx, dg.reshape(D)


def _a_pre(zx, cws, cb, gws, gbs, lams):
    xr = _conv4(zx, cws) + cb
    out = []
    for d in range(2):
        r = _sigmoid(_bmm(xr, gws[2 * d], "nn") + gbs[2 * d])
        ig = _sigmoid(_bmm(xr, gws[2 * d + 1], "nn") + gbs[2 * d + 1])
        log_a = -RG_C * r * _softplus(-lams[d])
        a = jnp.exp(log_a)
        out += [a, _sqrt_one_minus_sq(log_a, a) * ig * xr]
    return tuple(out)


def _a_post(h0, h1, zg):
    return (h0 + h1) * jax.nn.gelu(zg)


SUBLANES = 8
SCAN_TILES = 8


def _scan_jobs(jobs):
    S, C = jobs[0][0].shape
    U = min(SCAN_TILES, S // SUBLANES)
    rows = U * SUBLANES
    row = lax.broadcasted_iota(jnp.int32, (SUBLANES, C), 0)

    def prefix(a, b, reverse):
        for sh in (1, 2, 4):
            if reverse:
                m, r = row < SUBLANES - sh, SUBLANES - sh
            else:
                m, r = row >= sh, sh
            a_s = jnp.where(m, pltpu.roll(a, r, 0), 1.0)
            b_s = jnp.where(m, pltpu.roll(b, r, 0), 0.0)
            b = a * b_s + b
            a = a * a_s
        return a, b

    def step(i, carries):
        out = []
        for (a_ref, b_ref, h_ref, reverse), c in zip(jobs, carries):
            blk = (S // rows - 1 - i) if reverse else i
            t0 = pl.multiple_of(blk * rows, rows)
            order = range(U - 1, -1, -1) if reverse else range(U)
            edge = slice(0, 1) if reverse else slice(SUBLANES - 1, SUBLANES)
            for j in order:
                sl = pl.ds(t0 + j * SUBLANES, SUBLANES)
                a, b = prefix(a_ref[sl, :], b_ref[sl, :], reverse)
                h_ref[sl, :] = a * jnp.broadcast_to(c, (SUBLANES, C)) + b
                c = a[edge, :] * c + b[edge, :]
            out.append(c)
        return tuple(out)

    lax.fori_loop(0, S // rows, step, tuple(jnp.zeros((1, C), F32) for _ in jobs))


def _a_load_params(cw_ref, cb_ref, gw_ref, gb_ref, lam_ref):
    cws = [cw_ref[k:k + 1, :] for k in range(CONV_W)]
    gws = [gw_ref[d, g] for d in range(2) for g in range(2)]
    gbs = [gb_ref[d, g] for d in range(2) for g in range(2)]
    lams = [lam_ref[d:d + 1, :] for d in range(2)]
    return cws, cb_ref[...], gws, gbs, lams


def _a_in_specs(S, H):
    zg = pl.BlockSpec((None, S, LANES), lambda h, b: (b, 0, h))
    zx = pl.BlockSpec((None, S, LANES), lambda h, b: (b, 0, H + h))
    cw = pl.BlockSpec((CONV_W, LANES), lambda h, b: (0, h))
    cb = pl.BlockSpec((1, LANES), lambda h, b: (0, h))
    gw = pl.BlockSpec((2, 2, None, LANES, LANES), lambda h, b: (0, 0, h, 0, 0))
    gb = pl.BlockSpec((2, 2, None, 1, LANES), lambda h, b: (0, 0, h, 0, 0))
    lam = pl.BlockSpec((2, LANES), lambda h, b: (0, h))
    return zg, zx, cw, cb, gw, gb, lam


def _a_core_fwd(z, cw, cb, gw, gb, lam, name):
    Bq, S, D2 = z.shape
    D = D2 // 2
    H = D // LANES

    def body(zg_ref, zx_ref, cw_ref, cb_ref, gw_ref, gb_ref, lam_ref, y_ref, a_s, b_s, h_s):
        ab = _a_pre(zx_ref[...], *_a_load_params(cw_ref, cb_ref, gw_ref, gb_ref, lam_ref))
        for d in range(2):
            a_s[d] = ab[2 * d]
            b_s[d] = ab[2 * d + 1]
        _scan_jobs([(a_s.at[d], b_s.at[d], h_s.at[d], d == 1) for d in range(2)])
        y_ref[...] = _a_post(h_s[0], h_s[1], zg_ref[...]).astype(BF)

    seq = pltpu.VMEM((2, S, LANES), F32)
    return pl.pallas_call(
        body, name=name, grid=(H, Bq), in_specs=list(_a_in_specs(S, H)),
        out_specs=pl.BlockSpec((None, S, LANES), lambda h, b: (b, 0, h)),
        out_shape=jax.ShapeDtypeStruct((Bq, S, D), BF), scratch_shapes=[seq, seq, seq],
        compiler_params=_params(("parallel", "arbitrary")))(z, z, cw, cb.reshape(1, D), gw, gb.reshape(2, 2, H, 1, LANES), lam)


def _a_core_bwd(z, dy, cw, cb, gw, gb, lam, name):
    Bq, S, D2 = z.shape
    D = D2 // 2
    H = D // LANES

    def body(zg_ref, zx_ref, dy_ref, cw_ref, cb_ref, gw_ref, gb_ref, lam_ref,
             dzg_ref, dzx_ref, dcw_ref, dcb_ref, dgw_ref, dgb_ref, dlam_ref, a_s, b_s, h_s, l_s):
        prm = _a_load_params(cw_ref, cb_ref, gw_ref, gb_ref, lam_ref)
        ab, pre_vjp = jax.vjp(_a_pre, zx_ref[...], *prm)
        for d in range(2):
            a_s[d] = ab[2 * d]
            b_s[d] = ab[2 * d + 1]
        _scan_jobs([(a_s.at[d], b_s.at[d], h_s.at[d], d == 1) for d in range(2)])
        _, post_vjp = jax.vjp(_a_post, h_s[0], h_s[1], zg_ref[...])
        dh0, dh1, dzg = post_vjp(dy_ref[...])
        dzg_ref[...] = dzg.astype(BF)
        for d, dh in ((0, dh0), (1, dh1)):
            b_s[d] = dh
            a_s[d] = _shift(a_s[d], -1 if d == 0 else 1)
        _scan_jobs([(a_s.at[d], b_s.at[d], l_s.at[d], d == 0) for d in range(2)])
        cot = []
        for d in range(2):
            cot += [l_s[d] * _shift(h_s[d], 1 if d == 0 else -1), l_s[d]]
        dzx, dcws, dcb, dgws, dgbs, dlams = pre_vjp(tuple(cot))
        dzx_ref[...] = dzx.astype(BF)

        @pl.when(pl.program_id(1) == 0)
        def _():
            for r in (dcw_ref, dcb_ref, dgw_ref, dgb_ref, dlam_ref):
                r[...] = jnp.zeros_like(r)

        for k in range(CONV_W):
            dcw_ref[k:k + 1, :] += dcws[k]
        dcb_ref[...] += dcb
        for d in range(2):
            dlam_ref[d:d + 1, :] += dlams[d]
            for g in range(2):
                dgw_ref[d, g] += dgws[2 * d + g]
                dgb_ref[d, g] += dgbs[2 * d + g]

    zg, zx, cws, cbs, gws, gbs, lams = _a_in_specs(S, H)
    dyspec = pl.BlockSpec((None, S, LANES), lambda h, b: (b, 0, h))
    seq = pltpu.VMEM((2, S, LANES), F32)
    dzg, dzx, dcw, dcb, dgw, dgb, dlam = pl.pallas_call(
        body, name=name, grid=(H, Bq), in_specs=[zg, zx, dyspec, cws, cbs, gws, gbs, lams],
        out_specs=[dyspec, dyspec, cws, cbs, gws, gbs, lams],
        out_shape=[jax.ShapeDtypeStruct((Bq, S, D), BF), jax.ShapeDtypeStruct((Bq, S, D), BF),
                   jax.ShapeDtypeStruct((CONV_W, D), F32), jax.ShapeDtypeStruct((1, D), F32),
                   jax.ShapeDtypeStruct((2, 2, H, LANES, LANES), F32), jax.ShapeDtypeStruct((2, 2, H, 1, LANES), F32),
                   jax.ShapeDtypeStruct((2, D), F32)],
        scratch_shapes=[seq, seq, seq, seq],
        compiler_params=_params(("parallel", "arbitrary"), VMEM_LIMIT_HIGH))(
            z, z, dy, cw, cb.reshape(1, D), gw, gb.reshape(2, 2, H, 1, LANES), lam)
    dz = jnp.concatenate([dzg, dzx], axis=-1)
    return dz, dcw, dcb.reshape(D), dgw, dgb.reshape(2, 2, H, LANES), dlam


def _b_fn(z, lng, lnb, wss, bsf):
    D = z.shape[1] // 2
    zz = jax.nn.gelu(z)
    u, v = zz[:, :D], zz[:, D:]
    mu = jnp.mean(v, axis=-1, keepdims=True)
    var = jnp.mean(jnp.square(v - mu), axis=-1, keepdims=True)
    vn = (v - mu) * lax.rsqrt(var + 1e-5) * lng + lnb
    vs = jnp.concatenate([_bmm(wss[g], vn[:, g * LANES:(g + 1) * LANES], "nn") for g in range(D // LANES)], axis=1)
    return u * (vs + bsf)


def _b_specs(D, GB):
    row = lambda w: pl.BlockSpec((SGU_CHUNK, w), lambda i: (i, 0))
    vec = pl.BlockSpec((1, D), lambda i: (0, 0))
    ws = pl.BlockSpec((GB, SGU_CHUNK, SGU_CHUNK), lambda i: (0, 0, 0))
    bsf = pl.BlockSpec((SGU_CHUNK, D), lambda i: (0, 0))
    return row, vec, ws, bsf


def _b_core_fwd(z, lng, lnb, ws, bsf, name):
    T, D2 = z.shape
    D = D2 // 2
    GB = D // LANES
    row, vec, wspec, bspec = _b_specs(D, GB)

    def body(z_ref, lng_ref, lnb_ref, ws_ref, bsf_ref, y_ref):
        wss = [ws_ref[g] for g in range(GB)]
        y_ref[...] = _b_fn(z_ref[...], lng_ref[...], lnb_ref[...], wss, bsf_ref[...]).astype(BF)

    return pl.pallas_call(
        body, name=name, grid=(T // SGU_CHUNK,), in_specs=[row(D2), vec, vec, wspec, bspec], out_specs=row(D),
        out_shape=jax.ShapeDtypeStruct((T, D), BF), compiler_params=_params(("parallel",)))(
            z, lng.reshape(1, D), lnb.reshape(1, D), ws, bsf)


def _b_core_bwd(z, dy, lng, lnb, ws, bsf, name):
    T, D2 = z.shape
    D = D2 // 2
    GB = D // LANES
    row, vec, wspec, bspec = _b_specs(D, GB)

    def body(z_ref, dy_ref, lng_ref, lnb_ref, ws_ref, bsf_ref, dz_ref, dlng_ref, dlnb_ref, dws_ref, dbsf_ref):
        wss = [ws_ref[g] for g in range(GB)]
        _, vjp = jax.vjp(_b_fn, z_ref[...], lng_ref[...], lnb_ref[...], wss, bsf_ref[...])
        dz, dlng, dlnb, dwss, dbsf = vjp(dy_ref[...])
        dz_ref[...] = dz.astype(BF)

        @pl.when(pl.program_id(0) == 0)
        def _():
            for r in (dlng_ref, dlnb_ref, dws_ref, dbsf_ref):
                r[...] = jnp.zeros_like(r)

        dlng_ref[...] += dlng
        dlnb_ref[...] += dlnb
        dbsf_ref[...] += dbsf
        for g in range(GB):
            dws_ref[g] += dwss[g]

    dz, dlng, dlnb, dws, dbsf = pl.pallas_call(
        body, name=name, grid=(T // SGU_CHUNK,), in_specs=[row(D2), row(D), vec, vec, wspec, bspec],
        out_specs=[row(D2), vec, vec, wspec, bspec],
        out_shape=[jax.ShapeDtypeStruct((T, D2), BF), jax.ShapeDtypeStruct((1, D), F32), jax.ShapeDtypeStruct((1, D), F32),
                   jax.ShapeDtypeStruct((GB, SGU_CHUNK, SGU_CHUNK), F32), jax.ShapeDtypeStruct((SGU_CHUNK, D), F32)],
        compiler_params=_params(("arbitrary",)))(z, dy, lng.reshape(1, D), lnb.reshape(1, D), ws, bsf)
    return dz, dlng.reshape(D), dlnb.reshape(D), dws, dbsf


def _lane_is(j):
    return lax.broadcasted_iota(jnp.int32, (1, LANES), 1) == j


def _lane_col(x, j):
    return jnp.sum(jnp.where(_lane_is(j), x, 0.0), axis=1, keepdims=True)


def _c_pre(zq, zk, zv, zs, cwq, cwk, cwv, pcs, head, HC):
    q = _silu(_conv4(zq, cwq))
    k = _silu(_conv4(zk, cwk))
    v = _silu(_conv4(zv, cwv))
    q = q * lax.rsqrt(jnp.sum(q * q, axis=-1, keepdims=True) + 1e-6) * (LANES ** -0.5)
    k = k * lax.rsqrt(jnp.sum(k * k, axis=-1, keepdims=True) + 1e-6)
    gbp = jnp.zeros_like(zs)
    for d in range(2):
        a_logit = _lane_col(zs, d * HC + head)
        b_logit = _lane_col(zs, 2 * HC + d * HC + head)
        g = -jnp.exp(pcs[d]) * _softplus(a_logit + pcs[2 + d])
        beta = jnp.broadcast_to(_sigmoid(b_logit), g.shape)
        gbp = gbp + jnp.where(_lane_is(2 * d), g, 0.0) + jnp.where(_lane_is(2 * d + 1), beta, 0.0)
    return q, k, v, gbp


def _mm3(x, y):
    xh, yh = x.astype(BF), y.astype(BF)
    xl, yl = (x - xh.astype(F32)).astype(BF), (y - yh.astype(F32)).astype(BF)
    return _bmm_raw(xh, yh, "nn") + _bmm_raw(xh, yl, "nn") + _bmm_raw(xl, yh, "nn")


def _tri_inv_impl(a):
    C = a.shape[-1]
    eye = (lax.broadcasted_iota(jnp.int32, (1, C, C), 1) == lax.broadcasted_iota(jnp.int32, (1, C, C), 2)).astype(F32)
    r = eye - a
    p = a
    n = 2
    while n < C:
        p = _mm3(p, p)
        r = r + _mm3(r, p)
        n *= 2
    return r


@jax.custom_vjp
def _tri_inv(a):
    return _tri_inv_impl(a)


def _tri_inv_fwd(a):
    t = _tri_inv_impl(a)
    return t, t


def _tri_inv_bwd(t, g):
    tt = jnp.swapaxes(t, 1, 2)
    return (-_bmm_raw(_bmm_raw(tt, g, "nn"), tt, "nn"),)


_tri_inv.defvjp(_tri_inv_fwd, _tri_inv_bwd)


@jax.custom_vjp
def _pair_diff(gc3):
    m = gc3[:, :, :gc3.shape[1]]
    return m - jnp.swapaxes(m, 1, 2)


def _pair_diff_fwd(gc3):
    return _pair_diff(gc3), None


def _pair_diff_bwd(_, g):
    d = jnp.sum(g, axis=2, keepdims=True) - jnp.sum(jnp.swapaxes(g, 1, 2), axis=2, keepdims=True)
    return (jnp.broadcast_to(d * (1.0 / LANES), d.shape[:2] + (LANES,)),)


_pair_diff.defvjp(_pair_diff_fwd, _pair_diff_bwd)


@jax.custom_vjp
def _tri_inv_saved(a, t):
    return t


def _tri_inv_saved_fwd(a, t):
    return t, t


def _tri_inv_saved_bwd(t, g):
    return _tri_inv_bwd(t, g)[0], jnp.zeros_like(t)


_tri_inv_saved.defvjp(_tri_inv_saved_fwd, _tri_inv_saved_bwd)


def _c_phase1(q, k, v, gbp, rev, t_saved=None):
    S = q.shape[0]
    C = GDN_CHUNK
    N = S // C
    col = 2 if rev else 0
    gB = jnp.broadcast_to(_lane_col(gbp, col), (S, LANES))
    bB = jnp.broadcast_to(_lane_col(gbp, col + 1), (S, LANES))
    r3 = lambda t: t.reshape(N, C, LANES)
    gc3 = r3(_chunk_cumsum(gB, rev, C))
    q3, k3, v3, b3, g3 = r3(q), r3(k), r3(v), r3(bB), r3(gB)
    ri = lax.broadcasted_iota(jnp.int32, (1, C, C), 1)
    ci = lax.broadcasted_iota(jnp.int32, (1, C, C), 2)
    incl = (ri <= ci) if rev else (ri >= ci)
    strict = (ri < ci) if rev else (ri > ci)
    decay = jnp.where(incl, jnp.exp(jnp.where(incl, _pair_diff(gc3), 0.0)), 0.0)
    kb = k3 * b3
    vb = v3 * b3
    A = jnp.where(strict, _bmm(kb, k3, "nt") * decay, 0.0)
    T = _tri_inv(A) if t_saved is None else _tri_inv_saved(A, t_saved)
    egc = jnp.exp(gc3)
    u = _bmm(T, vb, "nn")
    w = _bmm(T, kb * egc, "nn")
    qk = _bmm(q3, k3, "nt") * decay
    glast = jnp.sum(g3, axis=1, keepdims=True)
    kd = k3 * jnp.exp(glast - gc3)
    k2 = _bmm(kd, w, "tn")
    z = _bmm(kd, u, "tn")
    qe2 = q3 * egc - _bmm(qk, w, "nn")
    o0 = _bmm(qk, u, "nn")
    return k2, z, jnp.exp(glast), qe2, o0, T


def _c_next_state(state, k2, z, eg):
    return state * eg - _bmm(k2, state, "nn") + z


def _c_out(state, qe2, o0):
    return _bmm(qe2, state, "nn") + o0


def _c_post(o, zg, ng):
    return _rms(o, ng) * _silu(zg)


def _c_pre_specs(S, H):
    col = lambda c0: pl.BlockSpec((None, S, LANES), lambda h, b: (b, 0, c0 * H + h))
    zs = pl.BlockSpec((None, S, LANES), lambda h, b: (b, 0, 0))
    cw = lambda c0: pl.BlockSpec((None, CONV_W, LANES), lambda h, b: (c0, 0, h))
    pc = pl.BlockSpec((None, 4, LANES), lambda h, b: (h, 0, 0))
    return col, zs, cw, pc


def _c_pre_fwd(z, zs, cw3, pc, name):
    Bq, S, D4 = z.shape
    D = D4 // 4
    H = D // LANES
    col, zss, cw, pcs = _c_pre_specs(S, H)

    def body(zq_ref, zk_ref, zv_ref, zs_ref, cwq_ref, cwk_ref, cwv_ref, pc_ref, q_ref, k_ref, v_ref, gbp_ref):
        rows = lambda r: [r[i:i + 1, :] for i in range(r.shape[0])]
        q, k, v, gbp = _c_pre(zq_ref[...], zk_ref[...], zv_ref[...], zs_ref[...], rows(cwq_ref), rows(cwk_ref),
                              rows(cwv_ref), rows(pc_ref), pl.program_id(0), H)
        q_ref[...] = q
        k_ref[...] = k
        v_ref[...] = v
        gbp_ref[...] = gbp

    out = pl.BlockSpec((None, S, LANES), lambda h, b: (b, 0, h))
    shp = jax.ShapeDtypeStruct((Bq, S, D), F32)
    return pl.pallas_call(
        body, name=name, grid=(H, Bq), in_specs=[col(0), col(1), col(2), zss, cw(0), cw(1), cw(2), pcs],
        out_specs=[out] * 4, out_shape=[shp] * 4, compiler_params=_params(("parallel", "arbitrary")))(
            z, z, z, zs, cw3, cw3, cw3, pc)


def _c_pre_bwd(z, zs, cw3, pc, dq, dk, dv, dgbp, name):
    Bq, S, D4 = z.shape
    D = D4 // 4
    H = D // LANES
    col, zss, cw, pcs = _c_pre_specs(S, H)

    def body(zq_ref, zk_ref, zv_ref, zs_ref, cwq_ref, cwk_ref, cwv_ref, pc_ref, dq_ref, dk_ref, dv_ref, dgbp_ref,
             dzq_ref, dzk_ref, dzv_ref, dzs_ref, dcw_ref, dpc_ref):
        rows = lambda r: [r[i:i + 1, :] for i in range(r.shape[0])]
        fn = functools.partial(_c_pre, head=pl.program_id(0), HC=H)
        _, vjp = jax.vjp(fn, zq_ref[...], zk_ref[...], zv_ref[...], zs_ref[...], rows(cwq_ref), rows(cwk_ref),
                         rows(cwv_ref), rows(pc_ref))
        dzq, dzk, dzv, dzs, dcwq, dcwk, dcwv, dpcs = vjp((dq_ref[...], dk_ref[...], dv_ref[...], dgbp_ref[...]))
        dzq_ref[...] = dzq.astype(BF)
        dzk_ref[...] = dzk.astype(BF)
        dzv_ref[...] = dzv.astype(BF)
        dzs_ref[...] = dzs

        @pl.when(pl.program_id(1) == 0)
        def _():
            dcw_ref[...] = jnp.zeros_like(dcw_ref)
            dpc_ref[...] = jnp.zeros_like(dpc_ref)

        for c, dc in enumerate((dcwq, dcwk, dcwv)):
            for i in range(CONV_W):
                dcw_ref[c, i:i + 1, :] += dc[i]
        for i in range(4):
            dpc_ref[i:i + 1, :] += dpcs[i]

    out = pl.BlockSpec((None, S, LANES), lambda h, b: (b, 0, h))
    dzs_spec = pl.BlockSpec((None, None, S, LANES), lambda h, b: (h, b, 0, 0))
    dcw_spec = pl.BlockSpec((3, CONV_W, LANES), lambda h, b: (0, 0, h))
    bshape = jax.ShapeDtypeStruct((Bq, S, D), BF)
    dzq, dzk, dzv, dzs, dcw3, dpc = pl.pallas_call(
        body, name=name, grid=(H, Bq),
        in_specs=[col(0), col(1), col(2), zss, cw(0), cw(1), cw(2), pcs, out, out, out, out],
        out_specs=[out, out, out, dzs_spec, dcw_spec, pcs],
        out_shape=[bshape, bshape, bshape, jax.ShapeDtypeStruct((H, Bq, S, LANES), F32),
                   jax.ShapeDtypeStruct((3, CONV_W, D), F32), jax.ShapeDtypeStruct((H, 4, LANES), F32)],
        compiler_params=_params(("parallel", "arbitrary")))(z, z, z, zs, cw3, cw3, cw3, pc, dq, dk, dv, dgbp)
    return dzq, dzk, dzv, dzs, dcw3, dpc


def _c_saved_shapes(Bq, H, S):
    N, C = S // GDN_CHUNK, GDN_CHUNK
    return [(Bq, H, 2, N, LANES, LANES), (Bq, H, 2, N, LANES, LANES), (Bq, H, 2, N, 1, LANES), (Bq, H, 2, N, C, LANES),
            (Bq, H, 2, N, C, C)]


def _c_saved_scratch(S):
    return [pltpu.VMEM(shp[3:], F32) for shp in _c_saved_shapes(1, 1, S)]


PHASE1_CHUNKS = 16


def _c_blocks(S, fn):
    nb = min(PHASE1_CHUNKS, S // GDN_CHUNK)
    rows = nb * GDN_CHUNK

    def blk(i, carry):
        fn(pl.ds(pl.multiple_of(i * rows, rows), rows), pl.ds(pl.multiple_of(i * nb, nb), nb))
        return carry

    lax.fori_loop(0, S // rows, blk, 0)


def _c_phase1_blocks(in_refs, k2_ref, z_ref, eg_ref, qe2_ref, t_ref, o_ref, rev):
    def fn(rows, chunks):
        k2, z, eg, qe2, o0, t = _c_phase1(*[r[rows, :] for r in in_refs], rev)
        k2_ref[chunks] = k2
        z_ref[chunks] = z
        eg_ref[chunks] = eg
        qe2_ref[chunks] = qe2
        t_ref[chunks] = t
        o_ref[chunks] += o0

    _c_blocks(in_refs[0].shape[0], fn)


def _c_sweep(jobs):
    N = jobs[0][3].shape[0]

    def step(i, states):
        out = []
        for (k2_ref, z_ref, eg_ref, st_ref, rev), state in zip(jobs, states):
            n = (N - 1 - i) if rev else i
            st_ref[n] = state
            out.append(_c_next_state(state, k2_ref[n], z_ref[n], eg_ref[n]))
        return tuple(out)

    lax.fori_loop(0, N, step, tuple(jnp.zeros((LANES, LANES), F32) for _ in jobs))


def _c_sweep_adjoint(jobs):
    N = jobs[0][3].shape[0]

    def step(i, gs):
        out = []
        for (k2_ref, eg_ref, dso_ref, gs_ref, rev), g in zip(jobs, gs):
            n = i if rev else (N - 1 - i)
            gs_ref[n] = g
            out.append(dso_ref[n] + g * eg_ref[n] - _bmm_raw(k2_ref[n], g, "tn"))
        return tuple(out)

    lax.fori_loop(0, N, step, tuple(jnp.zeros((LANES, LANES), F32) for _ in jobs))


def _c_mid_fwd(q, k, v, gbp, name):
    Bq, S, D = q.shape
    H = D // LANES
    N = S // GDN_CHUNK
    blk = pl.BlockSpec((None, S, LANES), lambda h, b: (b, 0, h))
    blk3 = pl.BlockSpec((None, N, GDN_CHUNK, LANES), lambda h, b: (b, 0, 0, h))
    n_saved = len(_c_saved_shapes(Bq, H, S))

    def body(q_ref, k_ref, v_ref, gbp_ref, o3, *rest):
        saved_hbm, scr = rest[:n_saved], rest[n_saved:]
        per_dir = n_saved + 1
        sems = scr[2 * per_dir]
        o3[...] = jnp.zeros_like(o3)
        sets = [scr[d * per_dir:(d + 1) * per_dir] for d in range(2)]
        copies = []

        def keep(d, i):
            copies.append(pltpu.make_async_copy(sets[d][i], saved_hbm[i].at[pl.program_id(1), pl.program_id(0), d],
                                                sems.at[d, i]))
            copies[-1].start()

        for d, rev in enumerate((False, True)):
            k2_ref, st_ref, eg_ref, qe2_ref, t_ref, z_ref = sets[d]
            _c_phase1_blocks((q_ref, k_ref, v_ref, gbp_ref), k2_ref, z_ref, eg_ref, qe2_ref, t_ref, o3, rev)
            for i in (0, 2, 3, 4):
                keep(d, i)
        _c_sweep([(sets[d][0], sets[d][5], sets[d][2], sets[d][1], d == 1) for d in range(2)])
        for d in range(2):
            keep(d, 1)
            qe2_ref, st_ref = sets[d][3], sets[d][1]

            def add_out(rows, chunks, qe2_ref=qe2_ref, st_ref=st_ref):
                o3[chunks] += _bmm_raw(qe2_ref[chunks], st_ref[chunks], "nn")

            _c_blocks(S, add_out)
        for cp in copies:
            cp.wait()

    one_dir = _c_saved_scratch(S) + [pltpu.VMEM((N, LANES, LANES), F32)]
    outs = pl.pallas_call(
        body, name=name, grid=(H, Bq), in_specs=[blk] * 4,
        out_specs=[blk3] + [pl.BlockSpec(memory_space=pltpu.HBM)] * n_saved,
        out_shape=[jax.ShapeDtypeStruct((Bq, N, GDN_CHUNK, D), F32)]
        + [jax.ShapeDtypeStruct(shp, F32) for shp in _c_saved_shapes(Bq, H, S)],
        scratch_shapes=one_dir + one_dir + [pltpu.SemaphoreType.DMA((2, n_saved))],
        compiler_params=_params(("parallel", "parallel")))(q, k, v, gbp)
    return outs[0].reshape(Bq, S, D), tuple(outs[1:])


def _c_mid_bwd(q, k, v, gbp, do, saved, name):
    Bq, S, D = q.shape
    H = D // LANES
    N = S // GDN_CHUNK
    blk = pl.BlockSpec((None, S, LANES), lambda h, b: (b, 0, h))
    blk3 = pl.BlockSpec((None, N, GDN_CHUNK, LANES), lambda h, b: (b, 0, 0, h))
    n_saved = len(saved)

    def body(q_ref, k_ref, v_ref, gbp_ref, do3, *rest):
        saved_hbm = rest[:n_saved]
        dq_ref, dk_ref, dv_ref, dgbp_ref = rest[n_saved:n_saved + 4]
        scr = rest[n_saved + 4:]
        sets = (scr[:n_saved], scr[n_saved:2 * n_saved])
        dso_refs, gs_refs, sems = scr[2 * n_saved:2 * n_saved + 2], scr[2 * n_saved + 2:2 * n_saved + 4], scr[-1]
        in_refs = (q_ref, k_ref, v_ref, gbp_ref)
        out_refs = (dq_ref, dk_ref, dv_ref, dgbp_ref)
        copies = [pltpu.make_async_copy(src.at[pl.program_id(1), pl.program_id(0), d], dst, sems.at[d, i])
                  for d in range(2) for i, (src, dst) in enumerate(zip(saved_hbm, sets[d]))]
        for cp in copies:
            cp.start()
        for cp in copies:
            cp.wait()
        for d in range(2):
            qe2_ref, dso_ref = sets[d][3], dso_refs[d]

            def out_to_state(rows, chunks, qe2_ref=qe2_ref, dso_ref=dso_ref):
                dso_ref[chunks] = _bmm_raw(qe2_ref[chunks], do3[chunks], "tn")

            _c_blocks(S, out_to_state)
        _c_sweep_adjoint([(sets[d][0], sets[d][2], dso_refs[d], gs_refs[d], d == 1) for d in range(2)])
        for d, rev in enumerate((False, True)):
            k2_ref, st_ref, eg_ref, qe2_ref, t_ref = sets[d]
            gs_ref = gs_refs[d]

            def block_vjp(rows, chunks):
                states, t_saved = st_ref[chunks], t_ref[chunks]

                def chunk_fn(q_, k_, v_, gbp_):
                    k2, z, eg, qe2, o0, _ = _c_phase1(q_, k_, v_, gbp_, rev, t_saved)
                    return _c_next_state(states, k2, z, eg), _c_out(states, qe2, o0)

                _, vjp = jax.vjp(chunk_fn, *[r[rows, :] for r in in_refs])
                for r, c in zip(out_refs, vjp((gs_ref[chunks], do3[chunks]))):
                    if rev:
                        r[rows, :] += c
                    else:
                        r[rows, :] = c

            _c_blocks(S, block_vjp)

    shp = jax.ShapeDtypeStruct((Bq, S, D), F32)
    mat = pltpu.VMEM((N, LANES, LANES), F32)
    return pl.pallas_call(
        body, name=name, grid=(H, Bq), in_specs=[blk] * 4 + [blk3] + [pl.BlockSpec(memory_space=pltpu.HBM)] * n_saved,
        out_specs=[blk] * 4, out_shape=[shp] * 4,
        scratch_shapes=_c_saved_scratch(S) + _c_saved_scratch(S) + [mat] * 4 + [pltpu.SemaphoreType.DMA((2, n_saved))],
        compiler_params=_params(("parallel", "parallel")))(q, k, v, gbp, do.reshape(Bq, N, GDN_CHUNK, D), *saved)


def _c_post_fwd(o, z, ng, name):
    Bq, S, D = o.shape
    H = D // LANES
    blk = pl.BlockSpec((None, S, LANES), lambda h, b: (b, 0, h))
    gate = pl.BlockSpec((None, S, LANES), lambda h, b: (b, 0, 3 * H + h))
    vec = pl.BlockSpec((1, LANES), lambda h, b: (0, 0))

    def body(o_ref, zg_ref, ng_ref, y_ref):
        y_ref[...] = _c_post(o_ref[...], zg_ref[...], ng_ref[...]).astype(BF)

    return pl.pallas_call(
        body, name=name, grid=(H, Bq), in_specs=[blk, gate, vec], out_specs=blk,
        out_shape=jax.ShapeDtypeStruct((Bq, S, D), BF), compiler_params=_params(("parallel", "parallel")))(
            o, z, ng.reshape(1, LANES))


def _c_post_bwd(o, z, ng, dy, name):
    Bq, S, D = o.shape
    H = D // LANES
    blk = pl.BlockSpec((None, S, LANES), lambda b, h: (b, 0, h))
    gate = pl.BlockSpec((None, S, LANES), lambda b, h: (b, 0, 3 * H + h))
    vec = pl.BlockSpec((1, LANES), lambda b, h: (0, 0))

    def body(o_ref, zg_ref, ng_ref, dy_ref, do_ref, dzg_ref, dng_ref):
        _, vjp = jax.vjp(_c_post, o_ref[...], zg_ref[...], ng_ref[...])
        do, dzg, dng = vjp(dy_ref[...])
        do_ref[...] = do
        dzg_ref[...] = dzg.astype(BF)

        @pl.when((pl.program_id(0) == 0) & (pl.program_id(1) == 0))
        def _():
            dng_ref[...] = jnp.zeros_like(dng_ref)

        dng_ref[...] += dng

    do, dzg, dng = pl.pallas_call(
        body, name=name, grid=(Bq, H), in_specs=[blk, gate, vec, blk], out_specs=[blk, blk, vec],
        out_shape=[jax.ShapeDtypeStruct((Bq, S, D), F32), jax.ShapeDtypeStruct((Bq, S, D), BF),
                   jax.ShapeDtypeStruct((1, LANES), F32)],
        compiler_params=_params(("arbitrary", "arbitrary")))(o, z, ng.reshape(1, LANES), dy)
    return do, dzg, dng.reshape(LANES)


def _c_param_rows(a_log, dt_bias):
    p = jnp.concatenate([a_log, dt_bias], axis=0).T
    return jnp.broadcast_to(p[:, :, None], p.shape + (LANES,)).astype(F32)


def _local_step(x, tgt, W, by_chip=False):
    Bq, S, D = x.shape
    T = Bq * S
    H = D // LANES
    L = W["norm_mix_g"].shape[0]
    seq = lambda t: t.reshape(Bq, S, t.shape[-1])
    flat = lambda t: t.reshape(T, t.shape[-1])
    if "mlp_up_slots" in W:
        up4, down4 = W["mlp_up_slots"], W["mlp_down_slots"]
    else:
        up4 = W["mlp_w_up"].reshape(L, D, N_CHIPS, -1).transpose(2, 0, 1, 3)
        down4 = W["mlp_w_down"].reshape(L, N_CHIPS, -1, D).transpose(1, 0, 2, 3)
    sw = up4.shape[-1]
    F = N_CHIPS * sw

    xs = flat(x)
    saved = []
    for i in range(L):
        kind, j = i % N_MIXERS, i // N_MIXERS
        tag = f"l{i}"
        sv = {"x": xs}
        hn = _rms_fwd(xs, W["norm_mix_g"][i], f"{tag}_mix_norm")
        sv["hn"] = hn
        if kind == 0:
            z = _mm(hn, W["a_w_in"][j], "nn", f"{tag}_a_in")
            y = _a_core_fwd(seq(z), W["a_conv_w"][j], W["a_conv_b"][j], W["a_gate_w"][j], W["a_gate_b"][j],
                            W["a_lambda"][j], f"{tag}_a_core")
            sv["z"] = z
            w_out = W["a_w_out"][j]
        elif kind == 1:
            z = _mm(hn, W["b_w_in"][j], "nn", f"{tag}_b_in")
            bsf = jnp.repeat(W["b_b_s"][j].T, LANES, axis=1)
            y = _b_core_fwd(z, W["b_ln_g"][j], W["b_ln_b"][j], W["b_w_s"][j], bsf, f"{tag}_b_core")
            sv["z"], sv["bsf"] = z, bsf
            w_out = W["b_w_out"][j]
        else:
            w_in = W["c_w_in"][j]
            w_small = jnp.pad(w_in[:, 4 * D:], ((0, 0), (0, LANES - 4 * H)))
            z = _mm(hn, w_in[:, :4 * D], "nn", f"{tag}_c_in")
            zs = _mm(hn, w_small, "nn", f"{tag}_c_in_small")
            cw3 = W["c_conv_w"][j].reshape(CONV_W, 3, D).transpose(1, 0, 2)
            pc = _c_param_rows(W["c_a_log"][j], W["c_dt_bias"][j])
            q, k, v, gbp = _c_pre_fwd(seq(z), seq(zs), cw3, pc, f"{tag}_c_pre")
            o, sv["mid"] = _c_mid_fwd(q, k, v, gbp, f"{tag}_c_mid")
            y = _c_post_fwd(o, seq(z), W["c_norm_g"][j], f"{tag}_c_post")
            sv.update(z=z, zs=zs, cw3=cw3, pc=pc, q=q, k=k, v=v, gbp=gbp, o=o, w_small=w_small)
            w_out = W["c_w_out"][j]
        y = flat(y)
        sv["y"] = y
        x1 = _mm(y, w_out, "nn", f"{tag}_mix_out", epi="add", extra=xs)
        sv["x1"] = x1
        hn2 = _rms_fwd(x1, W["norm_mlp_g"][i], f"{tag}_mlp_norm")
        act = _mm(hn2, up4, "nn", f"{tag}_mlp_up", out_dtype=BF, epi="relu2", tm=2048, tn=sw, tk=D, n_cols=F,
                  b_index=lambda j, k, i=i: (j, i, 0, 0))
        xs = _mm(act, down4, "nn", f"{tag}_mlp_down", epi="add", extra=x1, tn=D, tk=sw, n_cols=D,
                 b_index=lambda j, k, i=i: (k, i, 0, j))
        sv["hn2"], sv["act"] = hn2, act
        saved.append(sv)

    loss, dx, dgf = _final_loss(xs, W["norm_final_g"], flat(tgt), "final_loss")

    G = {"norm_final_g": dgf}
    per_layer = {n: [None] * L for n in ("norm_mix_g", "norm_mlp_g", "mlp_w_up", "mlp_w_down")}
    mixer = {}
    for i in reversed(range(L)):
        kind, j = i % N_MIXERS, i // N_MIXERS
        tag = f"l{i}"
        sv = saved[i]
        dhid = _mm(dx, down4, "nt", f"{tag}_mlp_dhid", out_dtype=BF, epi="relu2_bwd", extra=sv["act"], tm=2048, tn=sw, tk=D,
                   n_cols=F, b_index=lambda j, k, i=i: (j, i, 0, 0))
        per_layer["mlp_w_down"][i] = _mm(sv["act"], dx, "tn", f"{tag}_mlp_dwdown").reshape(N_CHIPS, sw, D)
        per_layer["mlp_w_up"][i] = _mm(sv["hn2"], dhid, "tn", f"{tag}_mlp_dwup", tn=sw, by_chip="cols")
        dhn2 = _mm(dhid, up4, "nt", f"{tag}_mlp_dhn", tn=D, tk=sw, n_cols=D, b_index=lambda j, k, i=i: (k, i, j, 0))
        dx, per_layer["norm_mlp_g"][i] = _rms_bwd(sv["x1"], W["norm_mlp_g"][i], dhn2, dx, f"{tag}_mlp_norm_bwd")
        g = {}
        if kind == 0:
            dy = _mm(dx, W["a_w_out"][j], "nt", f"{tag}_a_dy")
            g["a_w_out"] = _mm(sv["y"], dx, "tn", f"{tag}_a_dwout")
            dz, g["a_conv_w"], g["a_conv_b"], g["a_gate_w"], g["a_gate_b"], g["a_lambda"] = _a_core_bwd(
                seq(sv["z"]), seq(dy), W["a_conv_w"][j], W["a_conv_b"][j], W["a_gate_w"][j], W["a_gate_b"][j],
                W["a_lambda"][j], f"{tag}_a_core_bwd")
            dz = flat(dz)
            g["a_w_in"] = _mm(sv["hn"], dz, "tn", f"{tag}_a_dwin")
            dhn = _mm(dz, W["a_w_in"][j], "nt", f"{tag}_a_dhn")
        elif kind == 1:
            dy = _mm(dx, W["b_w_out"][j], "nt", f"{tag}_b_dy")
            g["b_w_out"] = _mm(sv["y"], dx, "tn", f"{tag}_b_dwout")
            dz, g["b_ln_g"], g["b_ln_b"], g["b_w_s"], dbsf = _b_core_bwd(
                sv["z"], dy, W["b_ln_g"][j], W["b_ln_b"][j], W["b_w_s"][j], sv["bsf"], f"{tag}_b_core_bwd")
            g["b_b_s"] = dbsf.reshape(SGU_CHUNK, H, LANES).sum(-1).T
            g["b_w_in"] = _mm(sv["hn"], dz, "tn", f"{tag}_b_dwin")
            dhn = _mm(dz, W["b_w_in"][j], "nt", f"{tag}_b_dhn")
        else:
            dy = _mm(dx, W["c_w_out"][j], "nt", f"{tag}_c_dy")
            g["c_w_out"] = _mm(sv["y"], dx, "tn", f"{tag}_c_dwout")
            do, dzg, g["c_norm_g"] = _c_post_bwd(sv["o"], seq(sv["z"]), W["c_norm_g"][j], seq(dy), f"{tag}_c_post_bwd")
            dq, dk, dv, dgbp = _c_mid_bwd(sv["q"], sv["k"], sv["v"], sv["gbp"], do, sv["mid"], f"{tag}_c_mid_bwd")
            dzq, dzk, dzv, dzs_h, dcw3, dpc = _c_pre_bwd(seq(sv["z"]), seq(sv["zs"]), sv["cw3"], sv["pc"], dq, dk, dv, dgbp,
                                                         f"{tag}_c_pre_bwd")
            dz = flat(jnp.concatenate([dzq, dzk, dzv, dzg], axis=-1))
            dzs = flat(dzs_h.sum(0)).astype(BF)
            g["c_conv_w"] = dcw3.transpose(1, 0, 2).reshape(CONV_W, 3 * D)
            dpc = dpc.sum(-1)
            g["c_a_log"], g["c_dt_bias"] = dpc[:, :2].T, dpc[:, 2:].T
            dw_main = _mm(sv["hn"], dz, "tn", f"{tag}_c_dwin")
            dw_small = _mm(sv["hn"], dzs, "tn", f"{tag}_c_dwin_small")
            g["c_w_in"] = jnp.concatenate([dw_main, dw_small[:, :4 * H]], axis=1)
            dhn = _mm(dz, W["c_w_in"][j][:, :4 * D], "nt", f"{tag}_c_dhn")
            dhn = _mm(dzs, sv["w_small"], "nt", f"{tag}_c_dhn_small", epi="add", extra=dhn)
        dx, per_layer["norm_mix_g"][i] = _rms_bwd(sv["x"], W["norm_mix_g"][i], dhn, dx, f"{tag}_mix_norm_bwd")
        for n, val in g.items():
            mixer.setdefault(n, {})[j] = val

    for n, vals in per_layer.items():
        if n not in ("mlp_w_up", "mlp_w_down"):
            G[n] = jnp.stack(vals)
        elif by_chip:
            G[n] = vals
        elif n == "mlp_w_up":
            G[n] = jnp.stack(vals).transpose(0, 2, 1, 3).reshape(L, D, F)
        else:
            G[n] = jnp.stack(vals).reshape(L, F, D)
    for n, by_j in mixer.items():
        G[n] = jnp.stack([by_j[j] for j in sorted(by_j)])
    return loss, dx.reshape(Bq, S, D), G


MESH = pl.DeviceIdType.MESH
N_CHIPS = 4
HBM_SPEC = pl.BlockSpec(memory_space=pltpu.HBM)


def _place():
    x, y, c = lax.axis_index("x"), lax.axis_index("y"), lax.axis_index("c")
    others = [(1 - x, y), (x, 1 - y), (1 - x, 1 - y)]
    return x, y, c, others


def _all_gather_xy(bufs, name):
    n = len(bufs)
    pieces = [_stage_rows(b.shape[0], b.shape[1] * b.dtype.itemsize) for b in bufs]

    def body(*refs):
        ins, outs = refs[:n], refs[n:2 * n]
        send, recv, fsend, frecv = refs[2 * n:2 * n + 4]
        stages = refs[2 * n + 4:]
        x, y, c, others = _place()
        p = 2 * x + y
        half = lambda b, cc: pl.ds(cc * (ins[b].shape[0] // 2), ins[b].shape[0] // 2)

        def ici(b, j):
            qx, qy = others[j]
            return pltpu.make_async_remote_copy(
                src_ref=ins[b].at[half(b, c)], dst_ref=outs[b].at[p, half(b, c)], send_sem=send.at[b, j],
                recv_sem=recv.at[b, j], device_id=(qx, qy, c), device_id_type=MESH)

        def landed(b, j, cc):
            qx, qy = others[j]
            return outs[b].at[2 * qx + qy, half(b, cc)]

        def d2d(b, j):
            return pltpu.make_async_remote_copy(
                src_ref=landed(b, j, c), dst_ref=landed(b, j, c), send_sem=fsend.at[b, j], recv_sem=frecv.at[b, j],
                device_id=(x, y, 1 - c), device_id_type=MESH)

        pairs = [(b, j) for b in range(n) for j in range(3)]
        for b, j in pairs:
            ici(b, j).start()
        for b in range(n):
            def own_piece(i, carry, b=b):
                rows = pl.ds(pl.multiple_of(i * pieces[b], pieces[b]), pieces[b])
                pltpu.sync_copy(ins[b].at[rows], stages[b])
                pltpu.sync_copy(stages[b], outs[b].at[p, rows])
                return carry

            lax.fori_loop(0, ins[b].shape[0] // pieces[b], own_piece, 0)
        for b, j in pairs:
            pltpu.make_async_remote_copy(
                src_ref=ins[b].at[half(b, c)], dst_ref=landed(b, j, c), send_sem=send.at[b, j], recv_sem=recv.at[b, j],
                device_id=(x, y, c), device_id_type=MESH).wait_recv()
            d2d(b, j).start()
        for b, j in pairs:
            pltpu.make_async_remote_copy(
                src_ref=landed(b, j, 1 - c), dst_ref=landed(b, j, 1 - c), send_sem=fsend.at[b, j], recv_sem=frecv.at[b, j],
                device_id=(x, y, 1 - c), device_id_type=MESH).wait_recv()
        for b, j in pairs:
            ici(b, j).wait_send()
            d2d(b, j).wait_send()

    return pl.pallas_call(
        body, name=name, in_specs=[HBM_SPEC] * n, out_specs=[HBM_SPEC] * n,
        out_shape=[jax.ShapeDtypeStruct((N_CHIPS,) + b.shape, b.dtype) for b in bufs],
        scratch_shapes=[pltpu.SemaphoreType.DMA((n, 3))] * 4
        + [pltpu.VMEM((r, b.shape[1]), b.dtype) for r, b in zip(pieces, bufs)],
        compiler_params=pltpu.CompilerParams(has_side_effects=True, vmem_limit_bytes=VMEM_LIMIT))(*bufs)


STAGE_BYTES = 2 * 1024 * 1024


def _stage_rows(rows, row_bytes):
    for d in range(min(rows, max(1, STAGE_BYTES // row_bytes)), 0, -1):
        if rows % d == 0 and (d % 16 == 0 or d == rows):
            return d
    return rows


def _swap_halves(gs, name):
    n = len(gs)

    def body(*refs):
        g_refs, o_refs, send, recv = refs[:n], refs[n:2 * n], refs[2 * n], refs[2 * n + 1]
        x, y, c, _ = _place()
        cps = [pltpu.make_async_remote_copy(src_ref=g_refs[b].at[:, 1 - c], dst_ref=o_refs[b], send_sem=send.at[b],
                                            recv_sem=recv.at[b], device_id=(x, y, 1 - c), device_id_type=MESH)
               for b in range(n)]
        for cp in cps:
            cp.start()
        for cp in cps:
            cp.wait()

    return pl.pallas_call(
        body, name=name, in_specs=[HBM_SPEC] * n, out_specs=[HBM_SPEC] * n,
        out_shape=[jax.ShapeDtypeStruct((g.shape[0],) + g.shape[2:], g.dtype) for g in gs],
        scratch_shapes=[pltpu.SemaphoreType.DMA((n,)), pltpu.SemaphoreType.DMA((n,))],
        compiler_params=pltpu.CompilerParams(has_side_effects=True))(*gs)


def _pair_sum(g, got, out_dtype, name):
    nq, _, h, cols = g.shape
    tr = min(512, h)

    def body(c_ref, g_ref, r_ref, o_ref):
        o_ref[...] = (g_ref[...] + r_ref[...]).astype(o_ref.dtype)

    spec = pl.BlockSpec((None, tr, cols), lambda q, i, c_ref: (q, i, 0))
    return pl.pallas_call(
        body, name=name,
        grid_spec=pltpu.PrefetchScalarGridSpec(
            num_scalar_prefetch=1, grid=(nq, h // tr),
            in_specs=[pl.BlockSpec((None, None, tr, cols), lambda q, i, c_ref: (q, c_ref[0], i, 0)), spec], out_specs=spec),
        out_shape=jax.ShapeDtypeStruct((nq, h, cols), out_dtype),
        compiler_params=_params(("parallel", "parallel")))(lax.axis_index("c").astype(jnp.int32).reshape(1), g, got)


def _scatter_xy(p_sums, name):
    n = len(p_sums)

    def body(*refs):
        p_refs, o_refs, send, recv = refs[:n], refs[n:2 * n], refs[2 * n], refs[2 * n + 1]
        x, y, c, others = _place()
        me = 2 * x + y
        cps = []
        for b in range(n):
            for j, (qx, qy) in enumerate(others):
                cps.append(pltpu.make_async_remote_copy(
                    src_ref=p_refs[b].at[2 * qx + qy], dst_ref=o_refs[b].at[me], send_sem=send.at[b, j],
                    recv_sem=recv.at[b, j], device_id=(qx, qy, c), device_id_type=MESH))
                cps[-1].start()
        for b in range(n):
            for j, (qx, qy) in enumerate(others):
                pltpu.make_async_remote_copy(
                    src_ref=p_refs[b].at[me], dst_ref=o_refs[b].at[2 * qx + qy], send_sem=send.at[b, j],
                    recv_sem=recv.at[b, j], device_id=(qx, qy, c), device_id_type=MESH).wait_recv()
        for cp in cps:
            cp.wait_send()

    return pl.pallas_call(
        body, name=name, in_specs=[HBM_SPEC] * n, out_specs=[HBM_SPEC] * n,
        out_shape=[jax.ShapeDtypeStruct(p.shape, p.dtype) for p in p_sums],
        scratch_shapes=[pltpu.SemaphoreType.DMA((n, 3)), pltpu.SemaphoreType.DMA((n, 3))],
        compiler_params=pltpu.CompilerParams(has_side_effects=True))(*p_sums)


def _chip_sum(r4, p_sum, name):
    nq, h, cols = r4.shape
    tr = min(512, h)

    def body(r_ref, p_ref, o_ref):
        me = 2 * lax.axis_index("x") + lax.axis_index("y")
        f = lambda q: jnp.where(me == q, p_ref[q], r_ref[q]).astype(F32)
        o_ref[...] = ((f(0) + f(1)) + f(2)) + f(3)

    spec = pl.BlockSpec((nq, tr, cols), lambda i: (0, i, 0))
    return pl.pallas_call(
        body, name=name, grid=(h // tr,), in_specs=[spec, spec],
        out_specs=pl.BlockSpec((tr, cols), lambda i: (i, 0)), out_shape=jax.ShapeDtypeStruct((h, cols), F32),
        compiler_params=_params(("parallel",)))(r4, p_sum)


def _join_halves(rs, name):
    n = len(rs)

    def body(*refs):
        r_refs, o_refs, send, recv = refs[:n], refs[n:2 * n], refs[2 * n], refs[2 * n + 1]
        x, y, c, _ = _place()
        cps = [pltpu.make_async_remote_copy(src_ref=r_refs[b], dst_ref=o_refs[b].at[c], send_sem=send.at[b],
                                            recv_sem=recv.at[b], device_id=(x, y, 1 - c), device_id_type=MESH)
               for b in range(n)]
        for cp in cps:
            cp.start()
        for b in range(n):
            pltpu.make_async_remote_copy(src_ref=r_refs[b], dst_ref=o_refs[b].at[1 - c], send_sem=send.at[b],
                                         recv_sem=recv.at[b], device_id=(x, y, 1 - c), device_id_type=MESH).wait_recv()
        for cp in cps:
            cp.wait_send()

    gots = pl.pallas_call(
        body, name=name, in_specs=[HBM_SPEC] * n, out_specs=[HBM_SPEC] * n,
        out_shape=[jax.ShapeDtypeStruct((2,) + r.shape, r.dtype) for r in rs],
        scratch_shapes=[pltpu.SemaphoreType.DMA((n,)), pltpu.SemaphoreType.DMA((n,))],
        compiler_params=pltpu.CompilerParams(has_side_effects=True))(*rs)
    c = lax.axis_index("c")
    return [jnp.stack([jnp.where(c == s, r, got[s]) for s in range(2)]) for r, got in zip(rs, gots)]


def _reduce_scatter(gs, tag):
    gs = [g.reshape(g.shape[0], 2, g.shape[1] // 2, g.shape[2]) for g in gs]
    gots = _swap_halves(gs, f"{tag}_swap")
    pairs = [_pair_sum(g, got, BF, f"{tag}_pair_sum{i}") for i, (g, got) in enumerate(zip(gs, gots))]
    r4s = _scatter_xy(pairs, f"{tag}_scatter")
    rs = [_chip_sum(r4, pair, f"{tag}_chip_sum{i}") for i, (r4, pair) in enumerate(zip(r4s, pairs))]
    return [j.reshape(-1, j.shape[-1]) for j in _join_halves(rs, f"{tag}_join")]


def _adamw(w, g, m, v, name):
    rows, cols = w.shape
    tr = rows
    for cand in (512, 344, 256, 128, 64, 32, 16, 8):
        if rows % cand == 0:
            tr = cand
            break

    def body(w_ref, g_ref, m_ref, v_ref, d_ref, nm_ref, nv_ref):
        g_ = g_ref[...]
        m_ = ADAM_B1 * m_ref[...] + (1.0 - ADAM_B1) * g_
        v_ = ADAM_B2 * v_ref[...] + (1.0 - ADAM_B2) * jnp.square(g_)
        m_hat = m_ / (1.0 - ADAM_B1 ** ADAM_STEP)
        v_hat = v_ / (1.0 - ADAM_B2 ** ADAM_STEP)
        d_ref[...] = -ADAM_LR * (m_hat / (jnp.sqrt(v_hat) + ADAM_EPS) + ADAM_WD * w_ref[...])
        nm_ref[...] = m_
        nv_ref[...] = v_

    spec = pl.BlockSpec((tr, cols), lambda i: (i, 0))
    shp = jax.ShapeDtypeStruct((rows, cols), F32)
    return pl.pallas_call(body, name=name, grid=(rows // tr,), in_specs=[spec] * 4, out_specs=[spec] * 3,
                          out_shape=[shp] * 3, compiler_params=_params(("parallel",)))(w, g, m, v)


WEIGHTS = ["norm_mix_g", "norm_mlp_g", "mlp_w_up", "mlp_w_down", "norm_final_g", "a_w_in", "a_conv_w", "a_conv_b",
           "a_gate_w", "a_gate_b", "a_lambda", "a_w_out", "b_w_in", "b_ln_g", "b_ln_b", "b_w_s", "b_b_s", "b_w_out",
           "c_w_in", "c_conv_w", "c_a_log", "c_dt_bias", "c_norm_g", "c_w_out"]
SHARD_AXIS = {"mlp_w_up": 2, "mlp_w_down": 1, "a_w_in": 2, "a_conv_w": 2, "a_conv_b": 1, "a_lambda": 2, "a_w_out": 1,
              "b_w_in": 2, "b_w_out": 1, "c_w_in": 2, "c_conv_w": 2, "c_w_out": 1}
MATMUL_WEIGHTS = ["mlp_w_up", "mlp_w_down", "a_w_in", "a_w_out", "b_w_in", "b_w_out", "c_w_out", "c_w_in"]
SMALL_SHARDED = ["a_conv_w", "a_conv_b", "a_lambda", "c_conv_w"]
REPLICATED = [n for n in WEIGHTS if n not in SHARD_AXIS]
FLAT_COLS = 1024


def _rows_of(shape):
    n = 1
    for s in shape:
        n *= s
    return -(-n // FLAT_COLS)


def _pack(arrays, total_rows, dtype):
    parts = []
    used = 0
    for a in arrays:
        r = _rows_of(a.shape)
        f = a.reshape(-1).astype(dtype)
        parts.append(jnp.pad(f, (0, r * FLAT_COLS - f.shape[0])).reshape(r, FLAT_COLS))
        used += r
    if total_rows > used:
        parts.append(jnp.zeros((total_rows - used, FLAT_COLS), dtype))
    return jnp.concatenate(parts, axis=0)


def _slots_of(g, axis):
    w = g.shape[axis] // N_CHIPS
    flat = jnp.stack([lax.slice_in_dim(g, q * w, (q + 1) * w, axis=axis).reshape(-1) for q in range(N_CHIPS)])
    rows = _rows_of(flat.shape[1:])
    return jnp.pad(flat, ((0, 0), (0, rows * FLAT_COLS - flat.shape[1]))).reshape(N_CHIPS, rows, FLAT_COLS)


def _unpack(buf, shapes):
    out, r0 = [], 0
    for shp in shapes:
        r = _rows_of(shp)
        n = 1
        for s in shp:
            n *= s
        out.append(buf[r0:r0 + r].reshape(-1)[:n].reshape(shp))
        r0 += r
    return out


def _round_up(n, m):
    return -(-n // m) * m


def kernel(x, norm_mix_g, norm_mlp_g, mlp_w_up, mlp_w_down, norm_final_g, a_w_in, a_conv_w, a_conv_b, a_gate_w, a_gate_b, a_lambda, a_w_out, b_w_in, b_ln_g, b_ln_b, b_w_s, b_b_s, b_w_out, c_w_in, c_conv_w, c_a_log, c_dt_bias, c_norm_g, c_w_out, loss_target, m_norm_mix_g, m_norm_mlp_g, m_mlp_w_up, m_mlp_w_down, m_norm_final_g, m_a_w_in, m_a_conv_w, m_a_conv_b, m_a_gate_w, m_a_gate_b, m_a_lambda, m_a_w_out, m_b_w_in, m_b_ln_g, m_b_ln_b, m_b_w_s, m_b_b_s, m_b_w_out, m_c_w_in, m_c_conv_w, m_c_a_log, m_c_dt_bias, m_c_norm_g, m_c_w_out, v_norm_mix_g, v_norm_mlp_g, v_mlp_w_up, v_mlp_w_down, v_norm_final_g, v_a_w_in, v_a_conv_w, v_a_conv_b, v_a_gate_w, v_a_gate_b, v_a_lambda, v_a_w_out, v_b_w_in, v_b_ln_g, v_b_ln_b, v_b_w_s, v_b_b_s, v_b_w_out, v_c_w_in, v_c_conv_w, v_c_a_log, v_c_dt_bias, v_c_norm_g, v_c_w_out):
    given = dict(locals())
    w_loc = {n: given[n] for n in WEIGHTS}
    m_loc = {n: given["m_" + n] for n in WEIGHTS}
    v_loc = {n: given["v_" + n] for n in WEIGHTS}

    mlp = ["mlp_w_up", "mlp_w_down"]
    rest = [n for n in MATMUL_WEIGHTS if n not in mlp]
    rest_rows = _round_up(sum(_rows_of(w_loc[n].shape) for n in rest), 512)
    small_rows = _round_up(sum(_rows_of(w_loc[n].shape) for n in SMALL_SHARDED), 16)
    as_rows = lambda w: w.astype(BF).reshape(-1, w.shape[-1])
    up_all, down_all, rest_all, small_all = _all_gather_xy(
        [as_rows(mlp_w_up), as_rows(mlp_w_down), _pack([w_loc[n] for n in rest], rest_rows, BF),
         _pack([w_loc[n] for n in SMALL_SHARDED], small_rows, F32)], "gather_weights")
    W = {n: w_loc[n] for n in REPLICATED}
    W["mlp_up_slots"] = up_all.reshape((N_CHIPS,) + mlp_w_up.shape)
    W["mlp_down_slots"] = down_all.reshape((N_CHIPS,) + mlp_w_down.shape)
    for names, buf in ((rest, rest_all), (SMALL_SHARDED, small_all)):
        per_chip = [_unpack(buf[q], [w_loc[n].shape for n in names]) for q in range(N_CHIPS)]
        for i, n in enumerate(names):
            W[n] = jnp.concatenate([per_chip[q][i] for q in range(N_CHIPS)], axis=SHARD_AXIS[n])

    loss, grad_x, G = _local_step(x, loss_target, W, by_chip=True)
    loss = lax.psum(loss, ("x", "y", "c"))

    sharded = rest + SMALL_SHARDED
    rep_rows = _round_up(sum(_rows_of(w_loc[n].shape) for n in REPLICATED), N_CHIPS * 16)
    rep_flat = _pack([G[n] for n in REPLICATED], rep_rows, F32).reshape(N_CHIPS, rep_rows // N_CHIPS, FLAT_COLS)
    shard_rows = sum(_rows_of(w_loc[n].shape) for n in sharded)
    total_rows = _round_up(shard_rows + rep_rows // N_CHIPS, 1024)
    parts = [_slots_of(G[n], SHARD_AXIS[n]) for n in sharded] + [rep_flat]
    parts.append(jnp.zeros((N_CHIPS, total_rows - shard_rows - rep_rows // N_CHIPS, FLAT_COLS), F32))
    mlp_bufs = [g.reshape(N_CHIPS, -1, g.shape[-1]) for n in mlp for g in G[n]]
    *mlp_red, red = _reduce_scatter(mlp_bufs + [jnp.concatenate(parts, axis=1)], "grads")
    n_layers = mlp_w_up.shape[0]
    g_loc = {"mlp_w_up": jnp.stack(mlp_red[:n_layers]).reshape(mlp_w_up.shape),
             "mlp_w_down": jnp.stack(mlp_red[n_layers:]).reshape(mlp_w_down.shape)}
    g_loc.update(zip(sharded, _unpack(red, [w_loc[n].shape for n in sharded])))
    rep_quarter = red[shard_rows:shard_rows + rep_rows // N_CHIPS]
    (rep_all,) = _all_gather_xy([rep_quarter], "gather_replicated_grads")
    g_loc.update(zip(REPLICATED, _unpack(rep_all.reshape(rep_rows, FLAT_COLS), [w_loc[n].shape for n in REPLICATED])))

    delta, new_m, new_v = {}, {}, {}
    big = [n for n in MATMUL_WEIGHTS if w_loc[n].size % FLAT_COLS == 0]
    for n in big:
        shp = w_loc[n].shape
        two_d = lambda a: a.reshape(-1, FLAT_COLS)
        d, nm, nv = _adamw(two_d(w_loc[n]), two_d(g_loc[n]), two_d(m_loc[n]), two_d(v_loc[n]), f"adamw_{n}")
        delta[n], new_m[n], new_v[n] = d.reshape(shp), nm.reshape(shp), nv.reshape(shp)
    small = [n for n in WEIGHTS if n not in big]
    small_shapes = [w_loc[n].shape for n in small]
    rows = _round_up(sum(_rows_of(s) for s in small_shapes), 8)
    packed = [_pack([src[n] for n in small], rows, F32) for src in (w_loc, g_loc, m_loc, v_loc)]
    for dst, buf in zip((delta, new_m, new_v), _adamw(*packed, "adamw_small")):
        dst.update(zip(small, _unpack(buf, small_shapes)))

    return (loss, grad_x, *[g_loc[n] for n in WEIGHTS], *[delta[n] for n in WEIGHTS],
            *[new_m[n] for n in WEIGHTS], *[new_v[n] for n in WEIGHTS])
```

```python
import functools

import jax
import jax.numpy as jnp
from jax import lax
from jax.experimental import pallas as pl
from jax.experimental.pallas import tpu as pltpu

F32 = jnp.float32
BF = jnp.bfloat16

LANES = 128
VMEM_LIMIT = 56 * 1024 * 1024
RG_C = 8.0
SGU_CHUNK = 128
GDN_CHUNK = 64
CONV_W = 4
N_MIXERS = 3

ADAM_LR = 0.001
ADAM_B1 = 0.9
ADAM_B2 = 0.999
ADAM_EPS = 1e-08
ADAM_WD = 0.01
ADAM_STEP = 10


VMEM_LIMIT_HIGH = 60 * 1024 * 1024


def _params(sem=None, vmem=VMEM_LIMIT):
    return pltpu.CompilerParams(dimension_semantics=sem, vmem_limit_bytes=vmem)


def _shift_impl(x, s):
    if s == 0:
        return x
    n = x.shape[0]
    row = lax.broadcasted_iota(jnp.int32, x.shape, 0)
    if s > 0:
        return jnp.where(row >= s, pltpu.roll(x, s, 0), 0.0)
    return jnp.where(row < n + s, pltpu.roll(x, n + s, 0), 0.0)


@functools.partial(jax.custom_vjp, nondiff_argnums=(1,))
def _shift(x, s):
    return _shift_impl(x, s)


def _shift_fwd(x, s):
    return _shift_impl(x, s), None


def _shift_bwd(s, _, g):
    return (_shift_impl(g, -s),)


_shift.defvjp(_shift_fwd, _shift_bwd)


def _chunk_cumsum_impl(x, rev, chunk):
    n = x.shape[0]
    rc = lax.broadcasted_iota(jnp.int32, x.shape, 0) & (chunk - 1)
    sh = 1
    while sh < chunk:
        if rev:
            x = x + jnp.where(rc < chunk - sh, pltpu.roll(x, n - sh, 0), 0.0)
        else:
            x = x + jnp.where(rc >= sh, pltpu.roll(x, sh, 0), 0.0)
        sh *= 2
    return x


@functools.partial(jax.custom_vjp, nondiff_argnums=(1, 2))
def _chunk_cumsum(x, rev, chunk):
    return _chunk_cumsum_impl(x, rev, chunk)


def _chunk_cumsum_fwd(x, rev, chunk):
    return _chunk_cumsum_impl(x, rev, chunk), None


def _chunk_cumsum_bwd(rev, chunk, _, g):
    return (_chunk_cumsum_impl(g, not rev, chunk),)


_chunk_cumsum.defvjp(_chunk_cumsum_fwd, _chunk_cumsum_bwd)


def _rms(x, g, eps=1e-6):
    return x * lax.rsqrt(jnp.mean(x * x, axis=-1, keepdims=True) + eps) * g


def _sigmoid(x):
    return 0.5 * jnp.tanh(0.5 * x) + 0.5


def _silu(x):
    return x * _sigmoid(x)


def _softplus(x):
    return jnp.maximum(x, 0.0) + jnp.log1p(jnp.exp(-jnp.abs(x)))


def _neg_expm1(y, ey):
    series = -y * (1.0 + y * (1 / 2) * (1.0 + y * (1 / 3) * (1.0 + y * (1 / 4))))
    return jnp.where(y > -1 / 32, series, 1.0 - ey)


@jax.custom_vjp
def _sqrt_one_minus_sq(log_a, a):
    t = _neg_expm1(2.0 * log_a, a * a)
    return t * lax.rsqrt(jnp.maximum(t, 1e-30))


def _sqrt_one_minus_sq_fwd(log_a, a):
    t = _neg_expm1(2.0 * log_a, a * a)
    rs = lax.rsqrt(jnp.maximum(t, 1e-30))
    return t * rs, (a, rs)


def _sqrt_one_minus_sq_bwd(res, g):
    a, rs = res
    return -g * (a * a) * rs, jnp.zeros_like(a)


_sqrt_one_minus_sq.defvjp(_sqrt_one_minus_sq_fwd, _sqrt_one_minus_sq_bwd)


def _bmm_raw(a, b, form):
    r = a.ndim - 2
    con = {"nn": ((r + 1,), (r,)), "nt": ((r + 1,), (r + 1,)), "tn": ((r,), (r,))}[form]
    batch = ((0,), (0,)) if r else ((), ())
    return lax.dot_general(a.astype(BF), b.astype(BF), (con, batch), preferred_element_type=F32)


@functools.partial(jax.custom_vjp, nondiff_argnums=(2,))
def _bmm(a, b, form):
    return _bmm_raw(a, b, form)


def _bmm_fwd(a, b, form):
    return _bmm_raw(a, b, form), (a, b)


def _bmm_bwd(form, res, g):
    a, b = res
    if form == "nn":
        da, db = _bmm_raw(g, b, "nt"), _bmm_raw(a, g, "tn")
    elif form == "nt":
        da, db = _bmm_raw(g, b, "nn"), _bmm_raw(g, a, "tn")
    else:
        da, db = _bmm_raw(b, g, "nt"), _bmm_raw(a, g, "nn")
    return da.astype(a.dtype), db.astype(b.dtype)


_bmm.defvjp(_bmm_fwd, _bmm_bwd)


def _conv4(z, rows):
    out = rows[0] * _shift(z, 2)
    for k in range(1, CONV_W):
        out = out + rows[k] * _shift(z, 2 - k)
    return out


_DIMS = {"nn": ((1,), (0,)), "nt": ((1,), (1,)), "tn": ((0,), (0,))}


def _mm(a, b, mode, name, *, out_dtype=F32, epi=None, extra=None, tm=2048, tn=1024, tk=1024, b_index=None, n_cols=None,
        by_chip=None):
    if mode == "tn":
        K, M = a.shape
    else:
        M, K = a.shape
    N = n_cols if b_index is not None else (b.shape[0] if mode == "nt" else b.shape[1])
    if epi == "add":
        tm = min(tm, 1024)
    tm, tn, tk = min(tm, M), min(tn, N), min(tk, K)
    assert M % tm == 0 and N % tn == 0 and K % tk == 0, (name, M, N, K)
    nk = K // tk
    a_spec = pl.BlockSpec((tk, tm), lambda i, j, k: (k, i)) if mode == "tn" else pl.BlockSpec((tm, tk), lambda i, j, k: (i, k))
    b_block = (tn, tk) if mode == "nt" else (tk, tn)
    if b_index is not None:
        b_spec = pl.BlockSpec((None,) * (b.ndim - 2) + b_block, lambda i, j, k: b_index(j, k))
    elif mode == "nt":
        b_spec = pl.BlockSpec(b_block, lambda i, j, k: (j, k))
    else:
        b_spec = pl.BlockSpec(b_block, lambda i, j, k: (k, j))
    o_spec = pl.BlockSpec((tm, tn), lambda i, j, k: (i, j))
    ins, specs = [a, b], [a_spec, b_spec]
    if epi in ("add", "relu2_bwd"):
        ins.append(extra)
        specs.append(o_spec)
    o_shape = (M, N)
    if by_chip == "cols":
        assert tn == N // N_CHIPS and epi is None
        o_shape, o_spec = (N_CHIPS, M, tn), pl.BlockSpec((None, tm, tn), lambda i, j, k: (j, i, 0))

    def body(*refs):
        a_ref, b_ref = refs[0], refs[1]
        e_ref = refs[2] if len(ins) == 3 else None
        o_ref = refs[len(ins)]

        def product():
            return lax.dot_general(a_ref[...].astype(BF), b_ref[...].astype(BF), (_DIMS[mode], ((), ())),
                                   preferred_element_type=F32)

        def finish(r):
            if epi == "relu2":
                r = jnp.square(jnp.maximum(r, 0.0))
            elif epi == "add":
                r = r + e_ref[...]
            elif epi == "relu2_bwd":
                e = e_ref[...].astype(F32)
                r = r * (2.0 * e * lax.rsqrt(jnp.maximum(e, 1e-30)))
            o_ref[...] = r.astype(out_dtype)

        if nk == 1:
            finish(product())
            return
        acc = refs[-1]
        k = pl.program_id(2)

        @pl.when(k == 0)
        def _():
            acc[...] = product()

        @pl.when(k > 0)
        def _():
            acc[...] += product()

        @pl.when(k == nk - 1)
        def _():
            finish(acc[...])

    return pl.pallas_call(
        body, name=name, grid=(M // tm, N // tn, nk), in_specs=specs, out_specs=o_spec,
        out_shape=jax.ShapeDtypeStruct(o_shape, out_dtype),
        scratch_shapes=[pltpu.VMEM((tm, tn), F32)] if nk > 1 else [],
        compiler_params=_params(("parallel", "parallel", "arbitrary")))(*ins)


def _rows_tile(T):
    return min(512, T)


def _rms_fwd(x, g, name):
    T, D = x.shape
    tr = _rows_tile(T)

    def body(x_ref, g_ref, o_ref):
        o_ref[...] = _rms(x_ref[...], g_ref[...]).astype(BF)

    return pl.pallas_call(
        body, name=name, grid=(T // tr,),
        in_specs=[pl.BlockSpec((tr, D), lambda i: (i, 0)), pl.BlockSpec((1, D), lambda i: (0, 0))],
        out_specs=pl.BlockSpec((tr, D), lambda i: (i, 0)), out_shape=jax.ShapeDtypeStruct((T, D), BF),
        compiler_params=_params(("parallel",)))(x, g.reshape(1, D))


def _rms_bwd(x, g, dhn, dres, name):
    T, D = x.shape
    tr = _rows_tile(T)

    def body(x_ref, g_ref, dhn_ref, dres_ref, dx_ref, dg_ref):
        _, vjp = jax.vjp(_rms, x_ref[...], g_ref[...])
        dx, dg = vjp(dhn_ref[...])
        dx_ref[...] = dres_ref[...] + dx

        @pl.when(pl.program_id(0) == 0)
        def _():
            dg_ref[...] = jnp.zeros_like(dg_ref)

        dg_ref[...] += dg

    row = pl.BlockSpec((tr, D), lambda i: (i, 0))
    vec = pl.BlockSpec((1, D), lambda i: (0, 0))
    dx, dg = pl.pallas_call(
        body, name=name, grid=(T // tr,), in_specs=[row, vec, row, row], out_specs=[row, vec],
        out_shape=[jax.ShapeDtypeStruct((T, D), F32), jax.ShapeDtypeStruct((1, D), F32)],
        compiler_params=_params(("arbitrary",)))(x, g.reshape(1, D), dhn, dres)
    return dx, dg.reshape(D)


def _final_loss(x, g, tgt, name):
    T, D = x.shape
    tr = _rows_tile(T)

    def body(x_ref, g_ref, t_ref, l_ref, dx_ref, dg_ref):
        y, vjp = jax.vjp(_rms, x_ref[...], g_ref[...])
        err = y - t_ref[...]
        dx, dg = vjp(err * (1.0 / D))
        dx_ref[...] = dx

        @pl.when(pl.program_id(0) == 0)
        def _():
            dg_ref[...] = jnp.zeros_like(dg_ref)
            l_ref[...] = jnp.zeros_like(l_ref)

        dg_ref[...] += dg
        l_ref[...] += (0.5 / D) * jnp.sum(jnp.sum(err * err, axis=1, keepdims=True), axis=0, keepdims=True)

    row = pl.BlockSpec((tr, D), lambda i: (i, 0))
    vec = pl.BlockSpec((1, D), lambda i: (0, 0))
    loss, dx, dg = pl.pallas_call(
        body, name=name, grid=(T // tr,), in_specs=[row, vec, row],
        out_specs=[pl.BlockSpec((1, LANES), lambda i: (0, 0)), row, vec],
        out_shape=[jax.ShapeDtypeStruct((1, LANES), F32), jax.ShapeDtypeStruct((T, D), F32),
                   jax.ShapeDtypeStruct((1, D), F32)],
        compiler_params=_params(("arbitrary",)))(x, g.reshape(1, D), tgt)
    return loss[0, 0], dx, dg.reshape(D)


def _a_pre(zx, cws, cb, gws, gbs, lams):
    xr = _conv4(zx, cws) + cb
    out = []
    for d in range(2):
        r = _sigmoid(_bmm(xr, gws[2 * d], "nn") + gbs[2 * d])
        ig = _sigmoid(_bmm(xr, gws[2 * d + 1], "nn") + gbs[2 * d + 1])
        log_a = -RG_C * r * _softplus(-lams[d])
        a = jnp.exp(log_a)
        out += [a, _sqrt_one_minus_sq(log_a, a) * ig * xr]
    return tuple(out)


def _a_post(h0, h1, zg):
    return (h0 + h1) * jax.nn.gelu(zg)


SUBLANES = 8
SCAN_TILES = 8


def _scan_jobs(jobs):
    S, C = jobs[0][0].shape
    U = min(SCAN_TILES, S // SUBLANES)
    rows = U * SUBLANES
    row = lax.broadcasted_iota(jnp.int32, (SUBLANES, C), 0)

    def prefix(a, b, reverse):
        for sh in (1, 2, 4):
            if reverse:
                m, r = row < SUBLANES - sh, SUBLANES - sh
            else:
                m, r = row >= sh, sh
            a_s = jnp.where(m, pltpu.roll(a, r, 0), 1.0)
            b_s = jnp.where(m, pltpu.roll(b, r, 0), 0.0)
            b = a * b_s + b
            a = a * a_s
        return a, b

    def step(i, carries):
        out = []
        for (a_ref, b_ref, h_ref, reverse), c in zip(jobs, carries):
            blk = (S // rows - 1 - i) if reverse else i
            t0 = pl.multiple_of(blk * rows, rows)
            order = range(U - 1, -1, -1) if reverse else range(U)
            edge = slice(0, 1) if reverse else slice(SUBLANES - 1, SUBLANES)
            for j in order:
                sl = pl.ds(t0 + j * SUBLANES, SUBLANES)
                a, b = prefix(a_ref[sl, :], b_ref[sl, :], reverse)
                h_ref[sl, :] = a * jnp.broadcast_to(c, (SUBLANES, C)) + b
                c = a[edge, :] * c + b[edge, :]
            out.append(c)
        return tuple(out)

    lax.fori_loop(0, S // rows, step, tuple(jnp.zeros((1, C), F32) for _ in jobs))


def _a_load_params(cw_ref, cb_ref, gw_ref, gb_ref, lam_ref):
    cws = [cw_ref[k:k + 1, :] for k in range(CONV_W)]
    gws = [gw_ref[d, g] for d in range(2) for g in range(2)]
    gbs = [gb_ref[d, g] for d in range(2) for g in range(2)]
    lams = [lam_ref[d:d + 1, :] for d in range(2)]
    return cws, cb_ref[...], gws, gbs, lams


def _a_in_specs(S, H):
    zg = pl.BlockSpec((None, S, LANES), lambda h, b: (b, 0, h))
    zx = pl.BlockSpec((None, S, LANES), lambda h, b: (b, 0, H + h))
    cw = pl.BlockSpec((CONV_W, LANES), lambda h, b: (0, h))
    cb = pl.BlockSpec((1, LANES), lambda h, b: (0, h))
    gw = pl.BlockSpec((2, 2, None, LANES, LANES), lambda h, b: (0, 0, h, 0, 0))
    gb = pl.BlockSpec((2, 2, None, 1, LANES), lambda h, b: (0, 0, h, 0, 0))
    lam = pl.BlockSpec((2, LANES), lambda h, b: (0, h))
    return zg, zx, cw, cb, gw, gb, lam


def _a_core_fwd(z, cw, cb, gw, gb, lam, name):
    Bq, S, D2 = z.shape
    D = D2 // 2
    H = D // LANES

    def body(zg_ref, zx_ref, cw_ref, cb_ref, gw_ref, gb_ref, lam_ref, y_ref, a_s, b_s, h_s):
        ab = _a_pre(zx_ref[...], *_a_load_params(cw_ref, cb_ref, gw_ref, gb_ref, lam_ref))
        for d in range(2):
            a_s[d] = ab[2 * d]
            b_s[d] = ab[2 * d + 1]
        _scan_jobs([(a_s.at[d], b_s.at[d], h_s.at[d], d == 1) for d in range(2)])
        y_ref[...] = _a_post(h_s[0], h_s[1], zg_ref[...]).astype(BF)

    seq = pltpu.VMEM((2, S, LANES), F32)
    return pl.pallas_call(
        body, name=name, grid=(H, Bq), in_specs=list(_a_in_specs(S, H)),
        out_specs=pl.BlockSpec((None, S, LANES), lambda h, b: (b, 0, h)),
        out_shape=jax.ShapeDtypeStruct((Bq, S, D), BF), scratch_shapes=[seq, seq, seq],
        compiler_params=_params(("parallel", "arbitrary")))(z, z, cw, cb.reshape(1, D), gw, gb.reshape(2, 2, H, 1, LANES), lam)


def _a_core_bwd(z, dy, cw, cb, gw, gb, lam, name):
    Bq, S, D2 = z.shape
    D = D2 // 2
    H = D // LANES

    def body(zg_ref, zx_ref, dy_ref, cw_ref, cb_ref, gw_ref, gb_ref, lam_ref,
             dzg_ref, dzx_ref, dcw_ref, dcb_ref, dgw_ref, dgb_ref, dlam_ref, a_s, b_s, h_s, l_s):
        prm = _a_load_params(cw_ref, cb_ref, gw_ref, gb_ref, lam_ref)
        ab, pre_vjp = jax.vjp(_a_pre, zx_ref[...], *prm)
        for d in range(2):
            a_s[d] = ab[2 * d]
            b_s[d] = ab[2 * d + 1]
        _scan_jobs([(a_s.at[d], b_s.at[d], h_s.at[d], d == 1) for d in range(2)])
        _, post_vjp = jax.vjp(_a_post, h_s[0], h_s[1], zg_ref[...])
        dh0, dh1, dzg = post_vjp(dy_ref[...])
        dzg_ref[...] = dzg.astype(BF)
        for d, dh in ((0, dh0), (1, dh1)):
            b_s[d] = dh
            a_s[d] = _shift(a_s[d], -1 if d == 0 else 1)
        _scan_jobs([(a_s.at[d], b_s.at[d], l_s.at[d], d == 0) for d in range(2)])
        cot = []
        for d in range(2):
            cot += [l_s[d] * _shift(h_s[d], 1 if d == 0 else -1), l_s[d]]
        dzx, dcws, dcb, dgws, dgbs, dlams = pre_vjp(tuple(cot))
        dzx_ref[...] = dzx.astype(BF)

        @pl.when(pl.program_id(1) == 0)
        def _():
            for r in (dcw_ref, dcb_ref, dgw_ref, dgb_ref, dlam_ref):
                r[...] = jnp.zeros_like(r)

        for k in range(CONV_W):
            dcw_ref[k:k + 1, :] += dcws[k]
        dcb_ref[...] += dcb
        for d in range(2):
            dlam_ref[d:d + 1, :] += dlams[d]
            for g in range(2):
                dgw_ref[d, g] += dgws[2 * d + g]
                dgb_ref[d, g] += dgbs[2 * d + g]

    zg, zx, cws, cbs, gws, gbs, lams = _a_in_specs(S, H)
    dyspec = pl.BlockSpec((None, S, LANES), lambda h, b: (b, 0, h))
    seq = pltpu.VMEM((2, S, LANES), F32)
    dzg, dzx, dcw, dcb, dgw, dgb, dlam = pl.pallas_call(
        body, name=name, grid=(H, Bq), in_specs=[zg, zx, dyspec, cws, cbs, gws, gbs, lams],
        out_specs=[dyspec, dyspec, cws, cbs, gws, gbs, lams],
        out_shape=[jax.ShapeDtypeStruct((Bq, S, D), BF), jax.ShapeDtypeStruct((Bq, S, D), BF),
                   jax.ShapeDtypeStruct((CONV_W, D), F32), jax.ShapeDtypeStruct((1, D), F32),
                   jax.ShapeDtypeStruct((2, 2, H, LANES, LANES), F32), jax.ShapeDtypeStruct((2, 2, H, 1, LANES), F32),
                   jax.ShapeDtypeStruct((2, D), F32)],
        scratch_shapes=[seq, seq, seq, seq],
        compiler_params=_params(("parallel", "arbitrary"), VMEM_LIMIT_HIGH))(
            z, z, dy, cw, cb.reshape(1, D), gw, gb.reshape(2, 2, H, 1, LANES), lam)
    dz = jnp.concatenate([dzg, dzx], axis=-1)
    return dz, dcw, dcb.reshape(D), dgw, dgb.reshape(2, 2, H, LANES), dlam


def _b_fn(z, lng, lnb, wss, bsf):
    D = z.shape[1] // 2
    zz = jax.nn.gelu(z)
    u, v = zz[:, :D], zz[:, D:]
    mu = jnp.mean(v, axis=-1, keepdims=True)
    var = jnp.mean(jnp.square(v - mu), axis=-1, keepdims=True)
    vn = (v - mu) * lax.rsqrt(var + 1e-5) * lng + lnb
    vs = jnp.concatenate([_bmm(wss[g], vn[:, g * LANES:(g + 1) * LANES], "nn") for g in range(D // LANES)], axis=1)
    return u * (vs + bsf)


def _b_specs(D, GB):
    row = lambda w: pl.BlockSpec((SGU_CHUNK, w), lambda i: (i, 0))
    vec = pl.BlockSpec((1, D), lambda i: (0, 0))
    ws = pl.BlockSpec((GB, SGU_CHUNK, SGU_CHUNK), lambda i: (0, 0, 0))
    bsf = pl.BlockSpec((SGU_CHUNK, D), lambda i: (0, 0))
    return row, vec, ws, bsf


def _b_core_fwd(z, lng, lnb, ws, bsf, name):
    T, D2 = z.shape
    D = D2 // 2
    GB = D // LANES
    row, vec, wspec, bspec = _b_specs(D, GB)

    def body(z_ref, lng_ref, lnb_ref, ws_ref, bsf_ref, y_ref):
        wss = [ws_ref[g] for g in range(GB)]
        y_ref[...] = _b_fn(z_ref[...], lng_ref[...], lnb_ref[...], wss, bsf_ref[...]).astype(BF)

    return pl.pallas_call(
        body, name=name, grid=(T // SGU_CHUNK,), in_specs=[row(D2), vec, vec, wspec, bspec], out_specs=row(D),
        out_shape=jax.ShapeDtypeStruct((T, D), BF), compiler_params=_params(("parallel",)))(
            z, lng.reshape(1, D), lnb.reshape(1, D), ws, bsf)


def _b_core_bwd(z, dy, lng, lnb, ws, bsf, name):
    T, D2 = z.shape
    D = D2 // 2
    GB = D // LANES
    row, vec, wspec, bspec = _b_specs(D, GB)

    def body(z_ref, dy_ref, lng_ref, lnb_ref, ws_ref, bsf_ref, dz_ref, dlng_ref, dlnb_ref, dws_ref, dbsf_ref):
        wss = [ws_ref[g] for g in range(GB)]
        _, vjp = jax.vjp(_b_fn, z_ref[...], lng_ref[...], lnb_ref[...], wss, bsf_ref[...])
        dz, dlng, dlnb, dwss, dbsf = vjp(dy_ref[...])
        dz_ref[...] = dz.astype(BF)

        @pl.when(pl.program_id(0) == 0)
        def _():
            for r in (dlng_ref, dlnb_ref, dws_ref, dbsf_ref):
                r[...] = jnp.zeros_like(r)

        dlng_ref[...] += dlng
        dlnb_ref[...] += dlnb
        dbsf_ref[...] += dbsf
        for g in range(GB):
            dws_ref[g] += dwss[g]

    dz, dlng, dlnb, dws, dbsf = pl.pallas_call(
        body, name=name, grid=(T // SGU_CHUNK,), in_specs=[row(D2), row(D), vec, vec, wspec, bspec],
        out_specs=[row(D2), vec, vec, wspec, bspec],
        out_shape=[jax.ShapeDtypeStruct((T, D2), BF), jax.ShapeDtypeStruct((1, D), F32), jax.ShapeDtypeStruct((1, D), F32),
                   jax.ShapeDtypeStruct((GB, SGU_CHUNK, SGU_CHUNK), F32), jax.ShapeDtypeStruct((SGU_CHUNK, D), F32)],
        compiler_params=_params(("arbitrary",)))(z, dy, lng.reshape(1, D), lnb.reshape(1, D), ws, bsf)
    return dz, dlng.reshape(D), dlnb.reshape(D), dws, dbsf


def _lane_is(j):
    return lax.broadcasted_iota(jnp.int32, (1, LANES), 1) == j


def _lane_col(x, j):
    return jnp.sum(jnp.where(_lane_is(j), x, 0.0), axis=1, keepdims=True)


def _c_pre(zq, zk, zv, zs, cwq, cwk, cwv, pcs, head, HC):
    q = _silu(_conv4(zq, cwq))
    k = _silu(_conv4(zk, cwk))
    v = _silu(_conv4(zv, cwv))
    q = q * lax.rsqrt(jnp.sum(q * q, axis=-1, keepdims=True) + 1e-6) * (LANES ** -0.5)
    k = k * lax.rsqrt(jnp.sum(k * k, axis=-1, keepdims=True) + 1e-6)
    gbp = jnp.zeros_like(zs)
    for d in range(2):
        a_logit = _lane_col(zs, d * HC + head)
        b_logit = _lane_col(zs, 2 * HC + d * HC + head)
        g = -jnp.exp(pcs[d]) * _softplus(a_logit + pcs[2 + d])
        beta = jnp.broadcast_to(_sigmoid(b_logit), g.shape)
        gbp = gbp + jnp.where(_lane_is(2 * d), g, 0.0) + jnp.where(_lane_is(2 * d + 1), beta, 0.0)
    return q, k, v, gbp


def _mm3(x, y):
    xh, yh = x.astype(BF), y.astype(BF)
    xl, yl = (x - xh.astype(F32)).astype(BF), (y - yh.astype(F32)).astype(BF)
    return _bmm_raw(xh, yh, "nn") + _bmm_raw(xh, yl, "nn") + _bmm_raw(xl, yh, "nn")


def _tri_inv_impl(a):
    C = a.shape[-1]
    eye = (lax.broadcasted_iota(jnp.int32, (1, C, C), 1) == lax.broadcasted_iota(jnp.int32, (1, C, C), 2)).astype(F32)
    r = eye - a
    p = a
    n = 2
    while n < C:
        p = _mm3(p, p)
        r = r + _mm3(r, p)
        n *= 2
    return r


@jax.custom_vjp
def _tri_inv(a):
    return _tri_inv_impl(a)


def _tri_inv_fwd(a):
    t = _tri_inv_impl(a)
    return t, t


def _tri_inv_bwd(t, g):
    tt = jnp.swapaxes(t, 1, 2)
    return (-_bmm_raw(_bmm_raw(tt, g, "nn"), tt, "nn"),)


_tri_inv.defvjp(_tri_inv_fwd, _tri_inv_bwd)


@jax.custom_vjp
def _pair_diff(gc3):
    m = gc3[:, :, :gc3.shape[1]]
    return m - jnp.swapaxes(m, 1, 2)


def _pair_diff_fwd(gc3):
    return _pair_diff(gc3), None


def _pair_diff_bwd(_, g):
    d = jnp.sum(g, axis=2, keepdims=True) - jnp.sum(jnp.swapaxes(g, 1, 2), axis=2, keepdims=True)
    return (jnp.broadcast_to(d * (1.0 / LANES), d.shape[:2] + (LANES,)),)


_pair_diff.defvjp(_pair_diff_fwd, _pair_diff_bwd)


@jax.custom_vjp
def _tri_inv_saved(a, t):
    return t


def _tri_inv_saved_fwd(a, t):
    return t, t


def _tri_inv_saved_bwd(t, g):
    return _tri_inv_bwd(t, g)[0], jnp.zeros_like(t)


_tri_inv_saved.defvjp(_tri_inv_saved_fwd, _tri_inv_saved_bwd)


def _c_phase1(q, k, v, gbp, rev, t_saved=None):
    S = q.shape[0]
    C = GDN_CHUNK
    N = S // C
    col = 2 if rev else 0
    gB = jnp.broadcast_to(_lane_col(gbp, col), (S, LANES))
    bB = jnp.broadcast_to(_lane_col(gbp, col + 1), (S, LANES))
    r3 = lambda t: t.reshape(N, C, LANES)
    gc3 = r3(_chunk_cumsum(gB, rev, C))
    q3, k3, v3, b3, g3 = r3(q), r3(k), r3(v), r3(bB), r3(gB)
    ri = lax.broadcasted_iota(jnp.int32, (1, C, C), 1)
    ci = lax.broadcasted_iota(jnp.int32, (1, C, C), 2)
    incl = (ri <= ci) if rev else (ri >= ci)
    strict = (ri < ci) if rev else (ri > ci)
    decay = jnp.where(incl, jnp.exp(jnp.where(incl, _pair_diff(gc3), 0.0)), 0.0)
    kb = k3 * b3
    vb = v3 * b3
    A = jnp.where(strict, _bmm(kb, k3, "nt") * decay, 0.0)
    T = _tri_inv(A) if t_saved is None else _tri_inv_saved(A, t_saved)
    egc = jnp.exp(gc3)
    u = _bmm(T, vb, "nn")
    w = _bmm(T, kb * egc, "nn")
    qk = _bmm(q3, k3, "nt") * decay
    glast = jnp.sum(g3, axis=1, keepdims=True)
    kd = k3 * jnp.exp(glast - gc3)
    k2 = _bmm(kd, w, "tn")
    z = _bmm(kd, u, "tn")
    qe2 = q3 * egc - _bmm(qk, w, "nn")
    o0 = _bmm(qk, u, "nn")
    return k2, z, jnp.exp(glast), qe2, o0, T


def _c_next_state(state, k2, z, eg):
    return state * eg - _bmm(k2, state, "nn") + z


def _c_out(state, qe2, o0):
    return _bmm(qe2, state, "nn") + o0


def _c_post(o, zg, ng):
    return _rms(o, ng) * _silu(zg)


def _c_pre_specs(S, H):
    col = lambda c0: pl.BlockSpec((None, S, LANES), lambda h, b: (b, 0, c0 * H + h))
    zs = pl.BlockSpec((None, S, LANES), lambda h, b: (b, 0, 0))
    cw = lambda c0: pl.BlockSpec((None, CONV_W, LANES), lambda h, b: (c0, 0, h))
    pc = pl.BlockSpec((None, 4, LANES), lambda h, b: (h, 0, 0))
    return col, zs, cw, pc


def _c_pre_fwd(z, zs, cw3, pc, name):
    Bq, S, D4 = z.shape
    D = D4 // 4
    H = D // LANES
    col, zss, cw, pcs = _c_pre_specs(S, H)

    def body(zq_ref, zk_ref, zv_ref, zs_ref, cwq_ref, cwk_ref, cwv_ref, pc_ref, q_ref, k_ref, v_ref, gbp_ref):
        rows = lambda r: [r[i:i + 1, :] for i in range(r.shape[0])]
        q, k, v, gbp = _c_pre(zq_ref[...], zk_ref[...], zv_ref[...], zs_ref[...], rows(cwq_ref), rows(cwk_ref),
                              rows(cwv_ref), rows(pc_ref), pl.program_id(0), H)
        q_ref[...] = q
        k_ref[...] = k
        v_ref[...] = v
        gbp_ref[...] = gbp

    out = pl.BlockSpec((None, S, LANES), lambda h, b: (b, 0, h))
    shp = jax.ShapeDtypeStruct((Bq, S, D), F32)
    return pl.pallas_call(
        body, name=name, grid=(H, Bq), in_specs=[col(0), col(1), col(2), zss, cw(0), cw(1), cw(2), pcs],
        out_specs=[out] * 4, out_shape=[shp] * 4, compiler_params=_params(("parallel", "arbitrary")))(
            z, z, z, zs, cw3, cw3, cw3, pc)


def _c_pre_bwd(z, zs, cw3, pc, dq, dk, dv, dgbp, name):
    Bq, S, D4 = z.shape
    D = D4 // 4
    H = D // LANES
    col, zss, cw, pcs = _c_pre_specs(S, H)

    def body(zq_ref, zk_ref, zv_ref, zs_ref, cwq_ref, cwk_ref, cwv_ref, pc_ref, dq_ref, dk_ref, dv_ref, dgbp_ref,
             dzq_ref, dzk_ref, dzv_ref, dzs_ref, dcw_ref, dpc_ref):
        rows = lambda r: [r[i:i + 1, :] for i in range(r.shape[0])]
        fn = functools.partial(_c_pre, head=pl.program_id(0), HC=H)
        _, vjp = jax.vjp(fn, zq_ref[...], zk_ref[...], zv_ref[...], zs_ref[...], rows(cwq_ref), rows(cwk_ref),
                         rows(cwv_ref), rows(pc_ref))
        dzq, dzk, dzv, dzs, dcwq, dcwk, dcwv, dpcs = vjp((dq_ref[...], dk_ref[...], dv_ref[...], dgbp_ref[...]))
        dzq_ref[...] = dzq.astype(BF)
        dzk_ref[...] = dzk.astype(BF)
        dzv_ref[...] = dzv.astype(BF)
        dzs_ref[...] = dzs

        @pl.when(pl.program_id(1) == 0)
        def _():
            dcw_ref[...] = jnp.zeros_like(dcw_ref)
            dpc_ref[...] = jnp.zeros_like(dpc_ref)

        for c, dc in enumerate((dcwq, dcwk, dcwv)):
            for i in range(CONV_W):
                dcw_ref[c, i:i + 1, :] += dc[i]
        for i in range(4):
            dpc_ref[i:i + 1, :] += dpcs[i]

    out = pl.BlockSpec((None, S, LANES), lambda h, b: (b, 0, h))
    dzs_spec = pl.BlockSpec((None, None, S, LANES), lambda h, b: (h, b, 0, 0))
    dcw_spec = pl.BlockSpec((3, CONV_W, LANES), lambda h, b: (0, 0, h))
    bshape = jax.ShapeDtypeStruct((Bq, S, D), BF)
    dzq, dzk, dzv, dzs, dcw3, dpc = pl.pallas_call(
        body, name=name, grid=(H, Bq),
        in_specs=[col(0), col(1), col(2), zss, cw(0), cw(1), cw(2), pcs, out, out, out, out],
        out_specs=[out, out, out, dzs_spec, dcw_spec, pcs],
        out_shape=[bshape, bshape, bshape, jax.ShapeDtypeStruct((H, Bq, S, LANES), F32),
                   jax.ShapeDtypeStruct((3, CONV_W, D), F32), jax.ShapeDtypeStruct((H, 4, LANES), F32)],
        compiler_params=_params(("parallel", "arbitrary")))(z, z, z, zs, cw3, cw3, cw3, pc, dq, dk, dv, dgbp)
    return dzq, dzk, dzv, dzs, dcw3, dpc


def _c_saved_shapes(Bq, H, S):
    N, C = S // GDN_CHUNK, GDN_CHUNK
    return [(Bq, H, 2, N, LANES, LANES), (Bq, H, 2, N, LANES, LANES), (Bq, H, 2, N, 1, LANES), (Bq, H, 2, N, C, LANES),
            (Bq, H, 2, N, C, C)]


def _c_saved_scratch(S):
    return [pltpu.VMEM(shp[3:], F32) for shp in _c_saved_shapes(1, 1, S)]


PHASE1_CHUNKS = 16


PHASE1_CHUNKS_FWD = 32


def _c_blocks(S, fn, chunks=PHASE1_CHUNKS):
    nb = min(chunks, S // GDN_CHUNK)
    rows = nb * GDN_CHUNK

    def blk(i, carry):
        fn(pl.ds(pl.multiple_of(i * rows, rows), rows), pl.ds(pl.multiple_of(i * nb, nb), nb))
        return carry

    lax.fori_loop(0, S // rows, blk, 0)


def _c_phase1_blocks(in_refs, k2_ref, z_ref, eg_ref, qe2_ref, t_ref, o_ref, rev):
    def fn(rows, chunks):
        k2, z, eg, qe2, o0, t = _c_phase1(*[r[rows, :] for r in in_refs], rev)
        k2_ref[chunks] = k2
        z_ref[chunks] = z
        eg_ref[chunks] = eg
        qe2_ref[chunks] = qe2
        t_ref[chunks] = t
        o_ref[chunks] += o0

    _c_blocks(in_refs[0].shape[0], fn, PHASE1_CHUNKS_FWD)


def _c_sweep(jobs):
    N = jobs[0][3].shape[0]

    def step(i, states):
        out = []
        for (k2_ref, z_ref, eg_ref, st_ref, rev), state in zip(jobs, states):
            n = (N - 1 - i) if rev else i
            st_ref[n] = state
            out.append(_c_next_state(state, k2_ref[n], z_ref[n], eg_ref[n]))
        return tuple(out)

    lax.fori_loop(0, N, step, tuple(jnp.zeros((LANES, LANES), F32) for _ in jobs))


def _c_sweep_adjoint(jobs):
    N = jobs[0][3].shape[0]

    def step(i, gs):
        out = []
        for (k2_ref, eg_ref, dso_ref, gs_ref, rev), g in zip(jobs, gs):
            n = i if rev else (N - 1 - i)
            gs_ref[n] = g
            out.append(dso_ref[n] + g * eg_ref[n] - _bmm_raw(k2_ref[n], g, "tn"))
        return tuple(out)

    lax.fori_loop(0, N, step, tuple(jnp.zeros((LANES, LANES), F32) for _ in jobs))


def _c_mid_fwd(q, k, v, gbp, name):
    Bq, S, D = q.shape
    H = D // LANES
    N = S // GDN_CHUNK
    blk = pl.BlockSpec((None, S, LANES), lambda h, b: (b, 0, h))
    blk3 = pl.BlockSpec((None, N, GDN_CHUNK, LANES), lambda h, b: (b, 0, 0, h))
    n_saved = len(_c_saved_shapes(Bq, H, S))

    def body(q_ref, k_ref, v_ref, gbp_ref, o3, *rest):
        saved_hbm, scr = rest[:n_saved], rest[n_saved:]
        per_dir = n_saved + 1
        sems = scr[2 * per_dir]
        o3[...] = jnp.zeros_like(o3)
        sets = [scr[d * per_dir:(d + 1) * per_dir] for d in range(2)]
        copies = []

        def keep(d, i):
            copies.append(pltpu.make_async_copy(sets[d][i], saved_hbm[i].at[pl.program_id(1), pl.program_id(0), d],
                                                sems.at[d, i]))
            copies[-1].start()

        for d, rev in enumerate((False, True)):
            k2_ref, st_ref, eg_ref, qe2_ref, t_ref, z_ref = sets[d]
            _c_phase1_blocks((q_ref, k_ref, v_ref, gbp_ref), k2_ref, z_ref, eg_ref, qe2_ref, t_ref, o3, rev)
            for i in (0, 2, 3, 4):
                keep(d, i)
        _c_sweep([(sets[d][0], sets[d][5], sets[d][2], sets[d][1], d == 1) for d in range(2)])
        for d in range(2):
            keep(d, 1)
            qe2_ref, st_ref = sets[d][3], sets[d][1]

            def add_out(rows, chunks, qe2_ref=qe2_ref, st_ref=st_ref):
                o3[chunks] += _bmm_raw(qe2_ref[chunks], st_ref[chunks], "nn")

            _c_blocks(S, add_out)
        for cp in copies:
            cp.wait()

    one_dir = _c_saved_scratch(S) + [pltpu.VMEM((N, LANES, LANES), F32)]
    outs = pl.pallas_call(
        body, name=name, grid=(H, Bq), in_specs=[blk] * 4,
        out_specs=[blk3] + [pl.BlockSpec(memory_space=pltpu.HBM)] * n_saved,
        out_shape=[jax.ShapeDtypeStruct((Bq, N, GDN_CHUNK, D), F32)]
        + [jax.ShapeDtypeStruct(shp, F32) for shp in _c_saved_shapes(Bq, H, S)],
        scratch_shapes=one_dir + one_dir + [pltpu.SemaphoreType.DMA((2, n_saved))],
        compiler_params=_params(("parallel", "parallel")))(q, k, v, gbp)
    return outs[0].reshape(Bq, S, D), tuple(outs[1:])


def _c_mid_bwd(q, k, v, gbp, do, saved, name):
    Bq, S, D = q.shape
    H = D // LANES
    N = S // GDN_CHUNK
    blk = pl.BlockSpec((None, S, LANES), lambda h, b: (b, 0, h))
    blk3 = pl.BlockSpec((None, N, GDN_CHUNK, LANES), lambda h, b: (b, 0, 0, h))
    n_saved = len(saved)

    def body(q_ref, k_ref, v_ref, gbp_ref, do3, *rest):
        saved_hbm = rest[:n_saved]
        dq_ref, dk_ref, dv_ref, dgbp_ref = rest[n_saved:n_saved + 4]
        scr = rest[n_saved + 4:]
        sets = (scr[:n_saved], scr[n_saved:2 * n_saved])
        dso_refs, gs_refs, sems = scr[2 * n_saved:2 * n_saved + 2], scr[2 * n_saved + 2:2 * n_saved + 4], scr[-1]
        in_refs = (q_ref, k_ref, v_ref, gbp_ref)
        out_refs = (dq_ref, dk_ref, dv_ref, dgbp_ref)
        copies = [pltpu.make_async_copy(src.at[pl.program_id(1), pl.program_id(0), d], dst, sems.at[d, i])
                  for d in range(2) for i, (src, dst) in enumerate(zip(saved_hbm, sets[d]))]
        for cp in copies:
            cp.start()
        for cp in copies:
            cp.wait()
        for d in range(2):
            qe2_ref, dso_ref = sets[d][3], dso_refs[d]

            def out_to_state(rows, chunks, qe2_ref=qe2_ref, dso_ref=dso_ref):
                dso_ref[chunks] = _bmm_raw(qe2_ref[chunks], do3[chunks], "tn")

            _c_blocks(S, out_to_state)
        _c_sweep_adjoint([(sets[d][0], sets[d][2], dso_refs[d], gs_refs[d], d == 1) for d in range(2)])
        for d, rev in enumerate((False, True)):
            k2_ref, st_ref, eg_ref, qe2_ref, t_ref = sets[d]
            gs_ref = gs_refs[d]

            def block_vjp(rows, chunks):
                states, t_saved = st_ref[chunks], t_ref[chunks]

                def chunk_fn(q_, k_, v_, gbp_):
                    k2, z, eg, qe2, o0, _ = _c_phase1(q_, k_, v_, gbp_, rev, t_saved)
                    return _c_next_state(states, k2, z, eg), _c_out(states, qe2, o0)

                _, vjp = jax.vjp(chunk_fn, *[r[rows, :] for r in in_refs])
                for r, c in zip(out_refs, vjp((gs_ref[chunks], do3[chunks]))):
                    if rev:
                        r[rows, :] += c
                    else:
                        r[rows, :] = c

            _c_blocks(S, block_vjp)

    shp = jax.ShapeDtypeStruct((Bq, S, D), F32)
    mat = pltpu.VMEM((N, LANES, LANES), F32)
    return pl.pallas_call(
        body, name=name, grid=(H, Bq), in_specs=[blk] * 4 + [blk3] + [pl.BlockSpec(memory_space=pltpu.HBM)] * n_saved,
        out_specs=[blk] * 4, out_shape=[shp] * 4,
        scratch_shapes=_c_saved_scratch(S) + _c_saved_scratch(S) + [mat] * 4 + [pltpu.SemaphoreType.DMA((2, n_saved))],
        compiler_params=_params(("parallel", "parallel")))(q, k, v, gbp, do.reshape(Bq, N, GDN_CHUNK, D), *saved)


def _c_post_fwd(o, z, ng, name):
    Bq, S, D = o.shape
    H = D // LANES
    blk = pl.BlockSpec((None, S, LANES), lambda h, b: (b, 0, h))
    gate = pl.BlockSpec((None, S, LANES), lambda h, b: (b, 0, 3 * H + h))
    vec = pl.BlockSpec((1, LANES), lambda h, b: (0, 0))

    def body(o_ref, zg_ref, ng_ref, y_ref):
        y_ref[...] = _c_post(o_ref[...], zg_ref[...], ng_ref[...]).astype(BF)

    return pl.pallas_call(
        body, name=name, grid=(H, Bq), in_specs=[blk, gate, vec], out_specs=blk,
        out_shape=jax.ShapeDtypeStruct((Bq, S, D), BF), compiler_params=_params(("parallel", "parallel")))(
            o, z, ng.reshape(1, LANES))


def _c_post_bwd(o, z, ng, dy, name):
    Bq, S, D = o.shape
    H = D // LANES
    blk = pl.BlockSpec((None, S, LANES), lambda b, h: (b, 0, h))
    gate = pl.BlockSpec((None, S, LANES), lambda b, h: (b, 0, 3 * H + h))
    vec = pl.BlockSpec((1, LANES), lambda b, h: (0, 0))

    def body(o_ref, zg_ref, ng_ref, dy_ref, do_ref, dzg_ref, dng_ref):
        _, vjp = jax.vjp(_c_post, o_ref[...], zg_ref[...], ng_ref[...])
        do, dzg, dng = vjp(dy_ref[...])
        do_ref[...] = do
        dzg_ref[...] = dzg.astype(BF)

        @pl.when((pl.program_id(0) == 0) & (pl.program_id(1) == 0))
        def _():
            dng_ref[...] = jnp.zeros_like(dng_ref)

        dng_ref[...] += dng

    do, dzg, dng = pl.pallas_call(
        body, name=name, grid=(Bq, H), in_specs=[blk, gate, vec, blk], out_specs=[blk, blk, vec],
        out_shape=[jax.ShapeDtypeStruct((Bq, S, D), F32), jax.ShapeDtypeStruct((Bq, S, D), BF),
                   jax.ShapeDtypeStruct((1, LANES), F32)],
        compiler_params=_params(("arbitrary", "arbitrary")))(o, z, ng.reshape(1, LANES), dy)
    return do, dzg, dng.reshape(LANES)


def _c_param_rows(a_log, dt_bias):
    p = jnp.concatenate([a_log, dt_bias], axis=0).T
    return jnp.broadcast_to(p[:, :, None], p.shape + (LANES,)).astype(F32)


def _local_step(x, tgt, W, by_chip=False):
    Bq, S, D = x.shape
    T = Bq * S
    H = D // LANES
    L = W["norm_mix_g"].shape[0]
    seq = lambda t: t.reshape(Bq, S, t.shape[-1])
    flat = lambda t: t.reshape(T, t.shape[-1])
    if "mlp_up_slots" in W:
        up4, down4 = W["mlp_up_slots"], W["mlp_down_slots"]
    else:
        up4 = W["mlp_w_up"].reshape(L, D, N_CHIPS, -1).transpose(2, 0, 1, 3)
        down4 = W["mlp_w_down"].reshape(L, N_CHIPS, -1, D).transpose(1, 0, 2, 3)
    sw = up4.shape[-1]
    F = N_CHIPS * sw

    xs = flat(x)
    saved = []
    for i in range(L):
        kind, j = i % N_MIXERS, i // N_MIXERS
        tag = f"l{i}"
        sv = {"x": xs}
        hn = _rms_fwd(xs, W["norm_mix_g"][i], f"{tag}_mix_norm")
        sv["hn"] = hn
        if kind == 0:
            z = _mm(hn, W["a_w_in"][j], "nn", f"{tag}_a_in")
            y = _a_core_fwd(seq(z), W["a_conv_w"][j], W["a_conv_b"][j], W["a_gate_w"][j], W["a_gate_b"][j],
                            W["a_lambda"][j], f"{tag}_a_core")
            sv["z"] = z
            w_out = W["a_w_out"][j]
        elif kind == 1:
            z = _mm(hn, W["b_w_in"][j], "nn", f"{tag}_b_in")
            bsf = jnp.repeat(W["b_b_s"][j].T, LANES, axis=1)
            y = _b_core_fwd(z, W["b_ln_g"][j], W["b_ln_b"][j], W["b_w_s"][j], bsf, f"{tag}_b_core")
            sv["z"], sv["bsf"] = z, bsf
            w_out = W["b_w_out"][j]
        else:
            w_in = W["c_w_in"][j]
            w_small = jnp.pad(w_in[:, 4 * D:], ((0, 0), (0, LANES - 4 * H)))
            z = _mm(hn, w_in[:, :4 * D], "nn", f"{tag}_c_in")
            zs = _mm(hn, w_small, "nn", f"{tag}_c_in_small")
            cw3 = W["c_conv_w"][j].reshape(CONV_W, 3, D).transpose(1, 0, 2)
            pc = _c_param_rows(W["c_a_log"][j], W["c_dt_bias"][j])
            q, k, v, gbp = _c_pre_fwd(seq(z), seq(zs), cw3, pc, f"{tag}_c_pre")
            o, sv["mid"] = _c_mid_fwd(q, k, v, gbp, f"{tag}_c_mid")
            y = _c_post_fwd(o, seq(z), W["c_norm_g"][j], f"{tag}_c_post")
            sv.update(z=z, zs=zs, cw3=cw3, pc=pc, q=q, k=k, v=v, gbp=gbp, o=o, w_small=w_small)
            w_out = W["c_w_out"][j]
        y = flat(y)
        sv["y"] = y
        x1 = _mm(y, w_out, "nn", f"{tag}_mix_out", epi="add", extra=xs)
        sv["x1"] = x1
        hn2 = _rms_fwd(x1, W["norm_mlp_g"][i], f"{tag}_mlp_norm")
        act = _mm(hn2, up4, "nn", f"{tag}_mlp_up", out_dtype=BF, epi="relu2", tm=2048, tn=sw, tk=D, n_cols=F,
                  b_index=lambda j, k, i=i: (j, i, 0, 0))
        xs = _mm(act, down4, "nn", f"{tag}_mlp_down", epi="add", extra=x1, tn=D, tk=sw, n_cols=D,
                 b_index=lambda j, k, i=i: (k, i, 0, j))
        sv["hn2"], sv["act"] = hn2, act
        saved.append(sv)

    loss, dx, dgf = _final_loss(xs, W["norm_final_g"], flat(tgt), "final_loss")

    G = {"norm_final_g": dgf}
    per_layer = {n: [None] * L for n in ("norm_mix_g", "norm_mlp_g", "mlp_w_up", "mlp_w_down")}
    mixer = {}
    for i in reversed(range(L)):
        kind, j = i % N_MIXERS, i // N_MIXERS
        tag = f"l{i}"
        sv = saved[i]
        dhid = _mm(dx, down4, "nt", f"{tag}_mlp_dhid", out_dtype=BF, epi="relu2_bwd", extra=sv["act"], tm=2048, tn=sw, tk=D,
                   n_cols=F, b_index=lambda j, k, i=i: (j, i, 0, 0))
        per_layer["mlp_w_down"][i] = _mm(sv["act"], dx, "tn", f"{tag}_mlp_dwdown").reshape(N_CHIPS, sw, D)
        per_layer["mlp_w_up"][i] = _mm(sv["hn2"], dhid, "tn", f"{tag}_mlp_dwup", tn=sw, by_chip="cols")
        dhn2 = _mm(dhid, up4, "nt", f"{tag}_mlp_dhn", tn=D, tk=sw, n_cols=D, b_index=lambda j, k, i=i: (k, i, j, 0))
        dx, per_layer["norm_mlp_g"][i] = _rms_bwd(sv["x1"], W["norm_mlp_g"][i], dhn2, dx, f"{tag}_mlp_norm_bwd")
        g = {}
        if kind == 0:
            dy = _mm(dx, W["a_w_out"][j], "nt", f"{tag}_a_dy")
            g["a_w_out"] = _mm(sv["y"], dx, "tn", f"{tag}_a_dwout")
            dz, g["a_conv_w"], g["a_conv_b"], g["a_gate_w"], g["a_gate_b"], g["a_lambda"] = _a_core_bwd(
                seq(sv["z"]), seq(dy), W["a_conv_w"][j], W["a_conv_b"][j], W["a_gate_w"][j], W["a_gate_b"][j],
                W["a_lambda"][j], f"{tag}_a_core_bwd")
            dz = flat(dz)
            g["a_w_in"] = _mm(sv["hn"], dz, "tn", f"{tag}_a_dwin")
            dhn = _mm(dz, W["a_w_in"][j], "nt", f"{tag}_a_dhn")
        elif kind == 1:
            dy = _mm(dx, W["b_w_out"][j], "nt", f"{tag}_b_dy")
            g["b_w_out"] = _mm(sv["y"], dx, "tn", f"{tag}_b_dwout")
            dz, g["b_ln_g"], g["b_ln_b"], g["b_w_s"], dbsf = _b_core_bwd(
                sv["z"], dy, W["b_ln_g"][j], W["b_ln_b"][j], W["b_w_s"][j], sv["bsf"], f"{tag}_b_core_bwd")
            g["b_b_s"] = dbsf.reshape(SGU_CHUNK, H, LANES).sum(-1).T
            g["b_w_in"] = _mm(sv["hn"], dz, "tn", f"{tag}_b_dwin")
            dhn = _mm(dz, W["b_w_in"][j], "nt", f"{tag}_b_dhn")
        else:
            dy = _mm(dx, W["c_w_out"][j], "nt", f"{tag}_c_dy")
            g["c_w_out"] = _mm(sv["y"], dx, "tn", f"{tag}_c_dwout")
            do, dzg, g["c_norm_g"] = _c_post_bwd(sv["o"], seq(sv["z"]), W["c_norm_g"][j], seq(dy), f"{tag}_c_post_bwd")
            dq, dk, dv, dgbp = _c_mid_bwd(sv["q"], sv["k"], sv["v"], sv["gbp"], do, sv["mid"], f"{tag}_c_mid_bwd")
            dzq, dzk, dzv, dzs_h, dcw3, dpc = _c_pre_bwd(seq(sv["z"]), seq(sv["zs"]), sv["cw3"], sv["pc"], dq, dk, dv, dgbp,
                                                         f"{tag}_c_pre_bwd")
            dz = flat(jnp.concatenate([dzq, dzk, dzv, dzg], axis=-1))
            dzs = flat(dzs_h.sum(0)).astype(BF)
            g["c_conv_w"] = dcw3.transpose(1, 0, 2).reshape(CONV_W, 3 * D)
            dpc = dpc.sum(-1)
            g["c_a_log"], g["c_dt_bias"] = dpc[:, :2].T, dpc[:, 2:].T
            dw_main = _mm(sv["hn"], dz, "tn", f"{tag}_c_dwin")
            dw_small = _mm(sv["hn"], dzs, "tn", f"{tag}_c_dwin_small")
            g["c_w_in"] = jnp.concatenate([dw_main, dw_small[:, :4 * H]], axis=1)
            dhn = _mm(dz, W["c_w_in"][j][:, :4 * D], "nt", f"{tag}_c_dhn")
            dhn = _mm(dzs, sv["w_small"], "nt", f"{tag}_c_dhn_small", epi="add", extra=dhn)
        dx, per_layer["norm_mix_g"][i] = _rms_bwd(sv["x"], W["norm_mix_g"][i], dhn, dx, f"{tag}_mix_norm_bwd")
        for n, val in g.items():
            mixer.setdefault(n, {})[j] = val

    for n, vals in per_layer.items():
        if n not in ("mlp_w_up", "mlp_w_down"):
            G[n] = jnp.stack(vals)
        elif by_chip:
            G[n] = vals
        elif n == "mlp_w_up":
            G[n] = jnp.stack(vals).transpose(0, 2, 1, 3).reshape(L, D, F)
        else:
            G[n] = jnp.stack(vals).reshape(L, F, D)
    for n, by_j in mixer.items():
        G[n] = jnp.stack([by_j[j] for j in sorted(by_j)])
    return loss, dx.reshape(Bq, S, D), G


MESH = pl.DeviceIdType.MESH
N_CHIPS = 4
HBM_SPEC = pl.BlockSpec(memory_space=pltpu.HBM)


def _place():
    x, y, c = lax.axis_index("x"), lax.axis_index("y"), lax.axis_index("c")
    others = [(1 - x, y), (x, 1 - y), (1 - x, 1 - y)]
    return x, y, c, others


def _all_gather_xy(bufs, name):
    n = len(bufs)
    pieces = [_stage_rows(b.shape[0], b.shape[1] * b.dtype.itemsize) for b in bufs]

    def body(*refs):
        ins, outs = refs[:n], refs[n:2 * n]
        send, recv, fsend, frecv = refs[2 * n:2 * n + 4]
        stages = refs[2 * n + 4:]
        x, y, c, others = _place()
        p = 2 * x + y
        half = lambda b, cc: pl.ds(cc * (ins[b].shape[0] // 2), ins[b].shape[0] // 2)

        def ici(b, j):
            qx, qy = others[j]
            return pltpu.make_async_remote_copy(
                src_ref=ins[b].at[half(b, c)], dst_ref=outs[b].at[p, half(b, c)], send_sem=send.at[b, j],
                recv_sem=recv.at[b, j], device_id=(qx, qy, c), device_id_type=MESH)

        def landed(b, j, cc):
            qx, qy = others[j]
            return outs[b].at[2 * qx + qy, half(b, cc)]

        def d2d(b, j):
            return pltpu.make_async_remote_copy(
                src_ref=landed(b, j, c), dst_ref=landed(b, j, c), send_sem=fsend.at[b, j], recv_sem=frecv.at[b, j],
                device_id=(x, y, 1 - c), device_id_type=MESH)

        pairs = [(b, j) for b in range(n) for j in range(3)]
        for b, j in pairs:
            ici(b, j).start()
        for b in range(n):
            def own_piece(i, carry, b=b):
                rows = pl.ds(pl.multiple_of(i * pieces[b], pieces[b]), pieces[b])
                pltpu.sync_copy(ins[b].at[rows], stages[b])
                pltpu.sync_copy(stages[b], outs[b].at[p, rows])
                return carry

            lax.fori_loop(0, ins[b].shape[0] // pieces[b], own_piece, 0)
        for b, j in pairs:
            pltpu.make_async_remote_copy(
                src_ref=ins[b].at[half(b, c)], dst_ref=landed(b, j, c), send_sem=send.at[b, j], recv_sem=recv.at[b, j],
                device_id=(x, y, c), device_id_type=MESH).wait_recv()
            d2d(b, j).start()
        for b, j in pairs:
            pltpu.make_async_remote_copy(
                src_ref=landed(b, j, 1 - c), dst_ref=landed(b, j, 1 - c), send_sem=fsend.at[b, j], recv_sem=frecv.at[b, j],
                device_id=(x, y, 1 - c), device_id_type=MESH).wait_recv()
        for b, j in pairs:
            ici(b, j).wait_send()
            d2d(b, j).wait_send()

    return pl.pallas_call(
        body, name=name, in_specs=[HBM_SPEC] * n, out_specs=[HBM_SPEC] * n,
        out_shape=[jax.ShapeDtypeStruct((N_CHIPS,) + b.shape, b.dtype) for b in bufs],
        scratch_shapes=[pltpu.SemaphoreType.DMA((n, 3))] * 4
        + [pltpu.VMEM((r, b.shape[1]), b.dtype) for r, b in zip(pieces, bufs)],
        compiler_params=pltpu.CompilerParams(has_side_effects=True, vmem_limit_bytes=VMEM_LIMIT))(*bufs)


STAGE_BYTES = 2 * 1024 * 1024


def _stage_rows(rows, row_bytes):
    for d in range(min(rows, max(1, STAGE_BYTES // row_bytes)), 0, -1):
        if rows % d == 0 and (d % 16 == 0 or d == rows):
            return d
    return rows


def _swap_halves(gs, name):
    n = len(gs)

    def body(*refs):
        g_refs, o_refs, send, recv = refs[:n], refs[n:2 * n], refs[2 * n], refs[2 * n + 1]
        x, y, c, _ = _place()
        cps = [pltpu.make_async_remote_copy(src_ref=g_refs[b].at[:, 1 - c], dst_ref=o_refs[b], send_sem=send.at[b],
                                            recv_sem=recv.at[b], device_id=(x, y, 1 - c), device_id_type=MESH)
               for b in range(n)]
        for cp in cps:
            cp.start()
        for cp in cps:
            cp.wait()

    return pl.pallas_call(
        body, name=name, in_specs=[HBM_SPEC] * n, out_specs=[HBM_SPEC] * n,
        out_shape=[jax.ShapeDtypeStruct((g.shape[0],) + g.shape[2:], g.dtype) for g in gs],
        scratch_shapes=[pltpu.SemaphoreType.DMA((n,)), pltpu.SemaphoreType.DMA((n,))],
        compiler_params=pltpu.CompilerParams(has_side_effects=True))(*gs)


def _pair_sum(g, got, out_dtype, name):
    nq, _, h, cols = g.shape
    tr = min(512, h)

    def body(c_ref, g_ref, r_ref, o_ref):
        o_ref[...] = (g_ref[...] + r_ref[...]).astype(o_ref.dtype)

    spec = pl.BlockSpec((None, tr, cols), lambda q, i, c_ref: (q, i, 0))
    return pl.pallas_call(
        body, name=name,
        grid_spec=pltpu.PrefetchScalarGridSpec(
            num_scalar_prefetch=1, grid=(nq, h // tr),
            in_specs=[pl.BlockSpec((None, None, tr, cols), lambda q, i, c_ref: (q, c_ref[0], i, 0)), spec], out_specs=spec),
        out_shape=jax.ShapeDtypeStruct((nq, h, cols), out_dtype),
        compiler_params=_params(("parallel", "parallel")))(lax.axis_index("c").astype(jnp.int32).reshape(1), g, got)


def _scatter_xy(p_sums, name):
    n = len(p_sums)

    def body(*refs):
        p_refs, o_refs, send, recv = refs[:n], refs[n:2 * n], refs[2 * n], refs[2 * n + 1]
        x, y, c, others = _place()
        me = 2 * x + y
        cps = []
        for b in range(n):
            for j, (qx, qy) in enumerate(others):
                cps.append(pltpu.make_async_remote_copy(
                    src_ref=p_refs[b].at[2 * qx + qy], dst_ref=o_refs[b].at[me], send_sem=send.at[b, j],
                    recv_sem=recv.at[b, j], device_id=(qx, qy, c), device_id_type=MESH))
                cps[-1].start()
        for b in range(n):
            for j, (qx, qy) in enumerate(others):
                pltpu.make_async_remote_copy(
                    src_ref=p_refs[b].at[me], dst_ref=o_refs[b].at[2 * qx + qy], send_sem=send.at[b, j],
                    recv_sem=recv.at[b, j], device_id=(qx, qy, c), device_id_type=MESH).wait_recv()
        for cp in cps:
            cp.wait_send()

    return pl.pallas_call(
        body, name=name, in_specs=[HBM_SPEC] * n, out_specs=[HBM_SPEC] * n,
        out_shape=[jax.ShapeDtypeStruct(p.shape, p.dtype) for p in p_sums],
        scratch_shapes=[pltpu.SemaphoreType.DMA((n, 3)), pltpu.SemaphoreType.DMA((n, 3))],
        compiler_params=pltpu.CompilerParams(has_side_effects=True))(*p_sums)


def _chip_sum(r4, p_sum, name):
    nq, h, cols = r4.shape
    tr = min(512, h)

    def body(r_ref, p_ref, o_ref):
        me = 2 * lax.axis_index("x") + lax.axis_index("y")
        f = lambda q: jnp.where(me == q, p_ref[q], r_ref[q]).astype(F32)
        o_ref[...] = ((f(0) + f(1)) + f(2)) + f(3)

    spec = pl.BlockSpec((nq, tr, cols), lambda i: (0, i, 0))
    return pl.pallas_call(
        body, name=name, grid=(h // tr,), in_specs=[spec, spec],
        out_specs=pl.BlockSpec((tr, cols), lambda i: (i, 0)), out_shape=jax.ShapeDtypeStruct((h, cols), F32),
        compiler_params=_params(("parallel",)))(r4, p_sum)


def _join_halves(rs, name):
    n = len(rs)

    def body(*refs):
        r_refs, o_refs, send, recv = refs[:n], refs[n:2 * n], refs[2 * n], refs[2 * n + 1]
        x, y, c, _ = _place()
        cps = [pltpu.make_async_remote_copy(src_ref=r_refs[b], dst_ref=o_refs[b].at[c], send_sem=send.at[b],
                                            recv_sem=recv.at[b], device_id=(x, y, 1 - c), device_id_type=MESH)
               for b in range(n)]
        for cp in cps:
            cp.start()
        for b in range(n):
            pltpu.make_async_remote_copy(src_ref=r_refs[b], dst_ref=o_refs[b].at[1 - c], send_sem=send.at[b],
                                         recv_sem=recv.at[b], device_id=(x, y, 1 - c), device_id_type=MESH).wait_recv()
        for cp in cps:
            cp.wait_send()

    gots = pl.pallas_call(
        body, name=name, in_specs=[HBM_SPEC] * n, out_specs=[HBM_SPEC] * n,
        out_shape=[jax.ShapeDtypeStruct((2,) + r.shape, r.dtype) for r in rs],
        scratch_shapes=[pltpu.SemaphoreType.DMA((n,)), pltpu.SemaphoreType.DMA((n,))],
        compiler_params=pltpu.CompilerParams(has_side_effects=True))(*rs)
    c = lax.axis_index("c")
    return [jnp.stack([jnp.where(c == s, r, got[s]) for s in range(2)]) for r, got in zip(rs, gots)]


def _reduce_scatter(gs, tag):
    gs = [g.reshape(g.shape[0], 2, g.shape[1] // 2, g.shape[2]) for g in gs]
    gots = _swap_halves(gs, f"{tag}_swap")
    pairs = [_pair_sum(g, got, BF, f"{tag}_pair_sum{i}") for i, (g, got) in enumerate(zip(gs, gots))]
    r4s = _scatter_xy(pairs, f"{tag}_scatter")
    rs = [_chip_sum(r4, pair, f"{tag}_chip_sum{i}") for i, (r4, pair) in enumerate(zip(r4s, pairs))]
    return [j.reshape(-1, j.shape[-1]) for j in _join_halves(rs, f"{tag}_join")]


def _adamw(w, g, m, v, name):
    rows, cols = w.shape
    tr = rows
    for cand in (512, 344, 256, 128, 64, 32, 16, 8):
        if rows % cand == 0:
            tr = cand
            break

    def body(w_ref, g_ref, m_ref, v_ref, d_ref, nm_ref, nv_ref):
        g_ = g_ref[...]
        m_ = ADAM_B1 * m_ref[...] + (1.0 - ADAM_B1) * g_
        v_ = ADAM_B2 * v_ref[...] + (1.0 - ADAM_B2) * jnp.square(g_)
        m_hat = m_ / (1.0 - ADAM_B1 ** ADAM_STEP)
        v_hat = v_ / (1.0 - ADAM_B2 ** ADAM_STEP)
        d_ref[...] = -ADAM_LR * (m_hat / (jnp.sqrt(v_hat) + ADAM_EPS) + ADAM_WD * w_ref[...])
        nm_ref[...] = m_
        nv_ref[...] = v_

    spec = pl.BlockSpec((tr, cols), lambda i: (i, 0))
    shp = jax.ShapeDtypeStruct((rows, cols), F32)
    return pl.pallas_call(body, name=name, grid=(rows // tr,), in_specs=[spec] * 4, out_specs=[spec] * 3,
                          out_shape=[shp] * 3, compiler_params=_params(("parallel",)))(w, g, m, v)


WEIGHTS = ["norm_mix_g", "norm_mlp_g", "mlp_w_up", "mlp_w_down", "norm_final_g", "a_w_in", "a_conv_w", "a_conv_b",
           "a_gate_w", "a_gate_b", "a_lambda", "a_w_out", "b_w_in", "b_ln_g", "b_ln_b", "b_w_s", "b_b_s", "b_w_out",
           "c_w_in", "c_conv_w", "c_a_log", "c_dt_bias", "c_norm_g", "c_w_out"]
SHARD_AXIS = {"mlp_w_up": 2, "mlp_w_down": 1, "a_w_in": 2, "a_conv_w": 2, "a_conv_b": 1, "a_lambda": 2, "a_w_out": 1,
              "b_w_in": 2, "b_w_out": 1, "c_w_in": 2, "c_conv_w": 2, "c_w_out": 1}
MATMUL_WEIGHTS = ["mlp_w_up", "mlp_w_down", "a_w_in", "a_w_out", "b_w_in", "b_w_out", "c_w_out", "c_w_in"]
SMALL_SHARDED = ["a_conv_w", "a_conv_b", "a_lambda", "c_conv_w"]
REPLICATED = [n for n in WEIGHTS if n not in SHARD_AXIS]
FLAT_COLS = 1024


def _rows_of(shape):
    n = 1
    for s in shape:
        n *= s
    return -(-n // FLAT_COLS)


def _pack(arrays, total_rows, dtype):
    parts = []
    used = 0
    for a in arrays:
        r = _rows_of(a.shape)
        f = a.reshape(-1).astype(dtype)
        parts.append(jnp.pad(f, (0, r * FLAT_COLS - f.shape[0])).reshape(r, FLAT_COLS))
        used += r
    if total_rows > used:
        parts.append(jnp.zeros((total_rows - used, FLAT_COLS), dtype))
    return jnp.concatenate(parts, axis=0)


def _slots_of(g, axis):
    w = g.shape[axis] // N_CHIPS
    flat = jnp.stack([lax.slice_in_dim(g, q * w, (q + 1) * w, axis=axis).reshape(-1) for q in range(N_CHIPS)])
    rows = _rows_of(flat.shape[1:])
    return jnp.pad(flat, ((0, 0), (0, rows * FLAT_COLS - flat.shape[1]))).reshape(N_CHIPS, rows, FLAT_COLS)


def _unpack(buf, shapes):
    out, r0 = [], 0
    for shp in shapes:
        r = _rows_of(shp)
        n = 1
        for s in shp:
            n *= s
        out.append(buf[r0:r0 + r].reshape(-1)[:n].reshape(shp))
        r0 += r
    return out


def _round_up(n, m):
    return -(-n // m) * m


def kernel(x, norm_mix_g, norm_mlp_g, mlp_w_up, mlp_w_down, norm_final_g, a_w_in, a_conv_w, a_conv_b, a_gate_w, a_gate_b, a_lambda, a_w_out, b_w_in, b_ln_g, b_ln_b, b_w_s, b_b_s, b_w_out, c_w_in, c_conv_w, c_a_log, c_dt_bias, c_norm_g, c_w_out, loss_target, m_norm_mix_g, m_norm_mlp_g, m_mlp_w_up, m_mlp_w_down, m_norm_final_g, m_a_w_in, m_a_conv_w, m_a_conv_b, m_a_gate_w, m_a_gate_b, m_a_lambda, m_a_w_out, m_b_w_in, m_b_ln_g, m_b_ln_b, m_b_w_s, m_b_b_s, m_b_w_out, m_c_w_in, m_c_conv_w, m_c_a_log, m_c_dt_bias, m_c_norm_g, m_c_w_out, v_norm_mix_g, v_norm_mlp_g, v_mlp_w_up, v_mlp_w_down, v_norm_final_g, v_a_w_in, v_a_conv_w, v_a_conv_b, v_a_gate_w, v_a_gate_b, v_a_lambda, v_a_w_out, v_b_w_in, v_b_ln_g, v_b_ln_b, v_b_w_s, v_b_b_s, v_b_w_out, v_c_w_in, v_c_conv_w, v_c_a_log, v_c_dt_bias, v_c_norm_g, v_c_w_out):
    given = dict(locals())
    w_loc = {n: given[n] for n in WEIGHTS}
    m_loc = {n: given["m_" + n] for n in WEIGHTS}
    v_loc = {n: given["v_" + n] for n in WEIGHTS}

    mlp = ["mlp_w_up", "mlp_w_down"]
    rest = [n for n in MATMUL_WEIGHTS if n not in mlp]
    rest_rows = _round_up(sum(_rows_of(w_loc[n].shape) for n in rest), 512)
    small_rows = _round_up(sum(_rows_of(w_loc[n].shape) for n in SMALL_SHARDED), 16)
    as_rows = lambda w: w.astype(BF).reshape(-1, w.shape[-1])
    up_all, down_all, rest_all, small_all = _all_gather_xy(
        [as_rows(mlp_w_up), as_rows(mlp_w_down), _pack([w_loc[n] for n in rest], rest_rows, BF),
         _pack([w_loc[n] for n in SMALL_SHARDED], small_rows, F32)], "gather_weights")
    W = {n: w_loc[n] for n in REPLICATED}
    W["mlp_up_slots"] = up_all.reshape((N_CHIPS,) + mlp_w_up.shape)
    W["mlp_down_slots"] = down_all.reshape((N_CHIPS,) + mlp_w_down.shape)
    for names, buf in ((rest, rest_all), (SMALL_SHARDED, small_all)):
        per_chip = [_unpack(buf[q], [w_loc[n].shape for n in names]) for q in range(N_CHIPS)]
        for i, n in enumerate(names):
            W[n] = jnp.concatenate([per_chip[q][i] for q in range(N_CHIPS)], axis=SHARD_AXIS[n])

    loss, grad_x, G = _local_step(x, loss_target, W, by_chip=True)
    loss = lax.psum(loss, ("x", "y", "c"))

    sharded = rest + SMALL_SHARDED
    rep_rows = _round_up(sum(_rows_of(w_loc[n].shape) for n in REPLICATED), N_CHIPS * 16)
    rep_flat = _pack([G[n] for n in REPLICATED], rep_rows, F32).reshape(N_CHIPS, rep_rows // N_CHIPS, FLAT_COLS)
    shard_rows = sum(_rows_of(w_loc[n].shape) for n in sharded)
    total_rows = _round_up(shard_rows + rep_rows // N_CHIPS, 1024)
    parts = [_slots_of(G[n], SHARD_AXIS[n]) for n in sharded] + [rep_flat]
    parts.append(jnp.zeros((N_CHIPS, total_rows - shard_rows - rep_rows // N_CHIPS, FLAT_COLS), F32))
    mlp_bufs = [g.reshape(N_CHIPS, -1, g.shape[-1]) for n in mlp for g in G[n]]
    *mlp_red, red = _reduce_scatter(mlp_bufs + [jnp.concatenate(parts, axis=1)], "grads")
    n_layers = mlp_w_up.shape[0]
    g_loc = {"mlp_w_up": jnp.stack(mlp_red[:n_layers]).reshape(mlp_w_up.shape),
             "mlp_w_down": jnp.stack(mlp_red[n_layers:]).reshape(mlp_w_down.shape)}
    g_loc.update(zip(sharded, _unpack(red, [w_loc[n].shape for n in sharded])))
    rep_quarter = red[shard_rows:shard_rows + rep_rows // N_CHIPS]
    (rep_all,) = _all_gather_xy([rep_quarter], "gather_replicated_grads")
    g_loc.update(zip(REPLICATED, _unpack(rep_all.reshape(rep_rows, FLAT_COLS), [w_loc[n].shape for n in REPLICATED])))

    delta, new_m, new_v = {}, {}, {}
    big = [n for n in MATMUL_WEIGHTS if w_loc[n].size % FLAT_COLS == 0]
    for n in big:
        shp = w_loc[n].shape
        two_d = lambda a: a.reshape(-1, FLAT_COLS)
        d, nm, nv = _adamw(two_d(w_loc[n]), two_d(g_loc[n]), two_d(m_loc[n]), two_d(v_loc[n]), f"adamw_{n}")
        delta[n], new_m[n], new_v[n] = d.reshape(shp), nm.reshape(shp), nv.reshape(shp)
    small = [n for n in WEIGHTS if n not in big]
    small_shapes = [w_loc[n].shape for n in small]
    rows = _round_up(sum(_rows_of(s) for s in small_shapes), 8)
    packed = [_pack([src[n] for n in small], rows, F32) for src in (w_loc, g_loc, m_loc, v_loc)]
    for dst, buf in zip((delta, new_m, new_v), _adamw(*packed, "adamw_small")):
        dst.update(zip(small, _unpack(buf, small_shapes)))

    return (loss, grad_x, *[g_loc[n] for n in WEIGHTS], *[delta[n] for n in WEIGHTS],
            *[new_m[n] for n in WEIGHTS], *[new_v[n] for n in WEIGHTS])
```

```python
import functools

import jax
import jax.numpy as jnp
from jax import lax
from jax.experimental import pallas as pl
from jax.experimental.pallas import tpu as pltpu

F32 = jnp.float32
BF = jnp.bfloat16

LANES = 128
VMEM_LIMIT = 56 * 1024 * 1024
RG_C = 8.0
SGU_CHUNK = 128
GDN_CHUNK = 64
CONV_W = 4
N_MIXERS = 3

ADAM_LR = 0.001
ADAM_B1 = 0.9
ADAM_B2 = 0.999
ADAM_EPS = 1e-08
ADAM_WD = 0.01
ADAM_STEP = 10


VMEM_LIMIT_HIGH = 60 * 1024 * 1024


def _params(sem=None, vmem=VMEM_LIMIT):
    return pltpu.CompilerParams(dimension_semantics=sem, vmem_limit_bytes=vmem)


def _shift_impl(x, s):
    if s == 0:
        return x
    n = x.shape[0]
    row = lax.broadcasted_iota(jnp.int32, x.shape, 0)
    if s > 0:
        return jnp.where(row >= s, pltpu.roll(x, s, 0), 0.0)
    return jnp.where(row < n + s, pltpu.roll(x, n + s, 0), 0.0)


@functools.partial(jax.custom_vjp, nondiff_argnums=(1,))
def _shift(x, s):
    return _shift_impl(x, s)


def _shift_fwd(x, s):
    return _shift_impl(x, s), None


def _shift_bwd(s, _, g):
    return (_shift_impl(g, -s),)


_shift.defvjp(_shift_fwd, _shift_bwd)


def _chunk_cumsum_impl(x, rev, chunk):
    n = x.shape[0]
    rc = lax.broadcasted_iota(jnp.int32, x.shape, 0) & (chunk - 1)
    sh = 1
    while sh < chunk:
        if rev:
            x = x + jnp.where(rc < chunk - sh, pltpu.roll(x, n - sh, 0), 0.0)
        else:
            x = x + jnp.where(rc >= sh, pltpu.roll(x, sh, 0), 0.0)
        sh *= 2
    return x


@functools.partial(jax.custom_vjp, nondiff_argnums=(1, 2))
def _chunk_cumsum(x, rev, chunk):
    return _chunk_cumsum_impl(x, rev, chunk)


def _chunk_cumsum_fwd(x, rev, chunk):
    return _chunk_cumsum_impl(x, rev, chunk), None


def _chunk_cumsum_bwd(rev, chunk, _, g):
    return (_chunk_cumsum_impl(g, not rev, chunk),)


_chunk_cumsum.defvjp(_chunk_cumsum_fwd, _chunk_cumsum_bwd)


def _rms(x, g, eps=1e-6):
    return x * lax.rsqrt(jnp.mean(x * x, axis=-1, keepdims=True) + eps) * g


def _sigmoid(x):
    return 0.5 * jnp.tanh(0.5 * x) + 0.5


def _silu(x):
    return x * _sigmoid(x)


def _softplus(x):
    return jnp.maximum(x, 0.0) + jnp.log1p(jnp.exp(-jnp.abs(x)))


def _neg_expm1(y, ey):
    series = -y * (1.0 + y * (1 / 2) * (1.0 + y * (1 / 3) * (1.0 + y * (1 / 4))))
    return jnp.where(y > -1 / 32, series, 1.0 - ey)


@jax.custom_vjp
def _sqrt_one_minus_sq(log_a, a):
    t = _neg_expm1(2.0 * log_a, a * a)
    return t * lax.rsqrt(jnp.maximum(t, 1e-30))


def _sqrt_one_minus_sq_fwd(log_a, a):
    t = _neg_expm1(2.0 * log_a, a * a)
    rs = lax.rsqrt(jnp.maximum(t, 1e-30))
    return t * rs, (a, rs)


def _sqrt_one_minus_sq_bwd(res, g):
    a, rs = res
    return -g * (a * a) * rs, jnp.zeros_like(a)


_sqrt_one_minus_sq.defvjp(_sqrt_one_minus_sq_fwd, _sqrt_one_minus_sq_bwd)


def _bmm_raw(a, b, form):
    r = a.ndim - 2
    con = {"nn": ((r + 1,), (r,)), "nt": ((r + 1,), (r + 1,)), "tn": ((r,), (r,))}[form]
    batch = ((0,), (0,)) if r else ((), ())
    return lax.dot_general(a.astype(BF), b.astype(BF), (con, batch), preferred_element_type=F32)


@functools.partial(jax.custom_vjp, nondiff_argnums=(2,))
def _bmm(a, b, form):
    return _bmm_raw(a, b, form)


def _bmm_fwd(a, b, form):
    return _bmm_raw(a, b, form), (a, b)


def _bmm_bwd(form, res, g):
    a, b = res
    if form == "nn":
        da, db = _bmm_raw(g, b, "nt"), _bmm_raw(a, g, "tn")
    elif form == "nt":
        da, db = _bmm_raw(g, b, "nn"), _bmm_raw(g, a, "tn")
    else:
        da, db = _bmm_raw(b, g, "nt"), _bmm_raw(a, g, "nn")
    return da.astype(a.dtype), db.astype(b.dtype)


_bmm.defvjp(_bmm_fwd, _bmm_bwd)


def _conv4(z, rows):
    out = rows[0] * _shift(z, 2)
    for k in range(1, CONV_W):
        out = out + rows[k] * _shift(z, 2 - k)
    return out


_DIMS = {"nn": ((1,), (0,)), "nt": ((1,), (1,)), "tn": ((0,), (0,))}


def _mm(a, b, mode, name, *, out_dtype=F32, epi=None, extra=None, tm=2048, tn=1024, tk=1024, b_index=None, n_cols=None,
        by_chip=None):
    if mode == "tn":
        K, M = a.shape
    else:
        M, K = a.shape
    N = n_cols if b_index is not None else (b.shape[0] if mode == "nt" else b.shape[1])
    if epi == "add":
        tm = min(tm, 1024)
    tm, tn, tk = min(tm, M), min(tn, N), min(tk, K)
    assert M % tm == 0 and N % tn == 0 and K % tk == 0, (name, M, N, K)
    nk = K // tk
    a_spec = pl.BlockSpec((tk, tm), lambda i, j, k: (k, i)) if mode == "tn" else pl.BlockSpec((tm, tk), lambda i, j, k: (i, k))
    b_block = (tn, tk) if mode == "nt" else (tk, tn)
    if b_index is not None:
        b_spec = pl.BlockSpec((None,) * (b.ndim - 2) + b_block, lambda i, j, k: b_index(j, k))
    elif mode == "nt":
        b_spec = pl.BlockSpec(b_block, lambda i, j, k: (j, k))
    else:
        b_spec = pl.BlockSpec(b_block, lambda i, j, k: (k, j))
    o_spec = pl.BlockSpec((tm, tn), lambda i, j, k: (i, j))
    ins, specs = [a, b], [a_spec, b_spec]
    if epi in ("add", "relu2_bwd"):
        ins.append(extra)
        specs.append(o_spec)
    o_shape = (M, N)
    if by_chip == "cols":
        assert tn == N // N_CHIPS and epi is None
        o_shape, o_spec = (N_CHIPS, M, tn), pl.BlockSpec((None, tm, tn), lambda i, j, k: (j, i, 0))

    def body(*refs):
        a_ref, b_ref = refs[0], refs[1]
        e_ref = refs[2] if len(ins) == 3 else None
        o_ref = refs[len(ins)]

        def product():
            return lax.dot_general(a_ref[...].astype(BF), b_ref[...].astype(BF), (_DIMS[mode], ((), ())),
                                   preferred_element_type=F32)

        def finish(r):
            if epi == "relu2":
                r = jnp.square(jnp.maximum(r, 0.0))
            elif epi == "add":
                r = r + e_ref[...]
            elif epi == "relu2_bwd":
                e = e_ref[...].astype(F32)
                r = r * (2.0 * e * lax.rsqrt(jnp.maximum(e, 1e-30)))
            o_ref[...] = r.astype(out_dtype)

        if nk == 1:
            finish(product())
            return
        acc = refs[-1]
        k = pl.program_id(2)

        @pl.when(k == 0)
        def _():
            acc[...] = product()

        @pl.when(k > 0)
        def _():
            acc[...] += product()

        @pl.when(k == nk - 1)
        def _():
            finish(acc[...])

    return pl.pallas_call(
        body, name=name, grid=(M // tm, N // tn, nk), in_specs=specs, out_specs=o_spec,
        out_shape=jax.ShapeDtypeStruct(o_shape, out_dtype),
        scratch_shapes=[pltpu.VMEM((tm, tn), F32)] if nk > 1 else [],
        compiler_params=_params(("parallel", "parallel", "arbitrary")))(*ins)


def _rows_tile(T):
    return min(512, T)


def _rms_fwd(x, g, name):
    T, D = x.shape
    tr = _rows_tile(T)

    def body(x_ref, g_ref, o_ref):
        o_ref[...] = _rms(x_ref[...], g_ref[...]).astype(BF)

    return pl.pallas_call(
        body, name=name, grid=(T // tr,),
        in_specs=[pl.BlockSpec((tr, D), lambda i: (i, 0)), pl.BlockSpec((1, D), lambda i: (0, 0))],
        out_specs=pl.BlockSpec((tr, D), lambda i: (i, 0)), out_shape=jax.ShapeDtypeStruct((T, D), BF),
        compiler_params=_params(("parallel",)))(x, g.reshape(1, D))


def _rms_bwd(x, g, dhn, dres, name):
    T, D = x.shape
    tr = _rows_tile(T)

    def body(x_ref, g_ref, dhn_ref, dres_ref, dx_ref, dxb_ref, dg_ref):
        _, vjp = jax.vjp(_rms, x_ref[...], g_ref[...])
        dx, dg = vjp(dhn_ref[...])
        dx = dres_ref[...] + dx
        dx_ref[...] = dx
        dxb_ref[...] = dx.astype(BF)

        @pl.when(pl.program_id(0) == 0)
        def _():
            dg_ref[...] = jnp.zeros_like(dg_ref)

        dg_ref[...] += dg

    row = pl.BlockSpec((tr, D), lambda i: (i, 0))
    vec = pl.BlockSpec((1, D), lambda i: (0, 0))
    dx, dxb, dg = pl.pallas_call(
        body, name=name, grid=(T // tr,), in_specs=[row, vec, row, row], out_specs=[row, row, vec],
        out_shape=[jax.ShapeDtypeStruct((T, D), F32), jax.ShapeDtypeStruct((T, D), BF), jax.ShapeDtypeStruct((1, D), F32)],
        compiler_params=_params(("arbitrary",)))(x, g.reshape(1, D), dhn, dres)
    return (dx, dxb), dg.reshape(D)


def _final_loss(x, g, tgt, name):
    T, D = x.shape
    tr = _rows_tile(T)

    def body(x_ref, g_ref, t_ref, l_ref, dx_ref, dxb_ref, dg_ref):
        y, vjp = jax.vjp(_rms, x_ref[...], g_ref[...])
        err = y - t_ref[...]
        dx, dg = vjp(err * (1.0 / D))
        dx_ref[...] = dx
        dxb_ref[...] = dx.astype(BF)

        @pl.when(pl.program_id(0) == 0)
        def _():
            dg_ref[...] = jnp.zeros_like(dg_ref)
            l_ref[...] = jnp.zeros_like(l_ref)

        dg_ref[...] += dg
        l_ref[...] += (0.5 / D) * jnp.sum(jnp.sum(err * err, axis=1, keepdims=True), axis=0, keepdims=True)

    row = pl.BlockSpec((tr, D), lambda i: (i, 0))
    vec = pl.BlockSpec((1, D), lambda i: (0, 0))
    loss, dx, dxb, dg = pl.pallas_call(
        body, name=name, grid=(T // tr,), in_specs=[row, vec, row],
        out_specs=[pl.BlockSpec((1, LANES), lambda i: (0, 0)), row, row, vec],
        out_shape=[jax.ShapeDtypeStruct((1, LANES), F32), jax.ShapeDtypeStruct((T, D), F32),
                   jax.ShapeDtypeStruct((T, D), BF), jax.ShapeDtypeStruct((1, D), F32)],
        compiler_params=_params(("arbitrary",)))(x, g.reshape(1, D), tgt)
    return loss[0, 0], (dx, dxb), dg.reshape(D)


def _a_pre(zx, cws, cb, gws, gbs, lams):
    xr = _conv4(zx, cws) + cb
    out = []
    for d in range(2):
        r = _sigmoid(_bmm(xr, gws[2 * d], "nn") + gbs[2 * d])
        ig = _sigmoid(_bmm(xr, gws[2 * d + 1], "nn") + gbs[2 * d + 1])
        log_a = -RG_C * r * _softplus(-lams[d])
        a = jnp.exp(log_a)
        out += [a, _sqrt_one_minus_sq(log_a, a) * ig * xr]
    return tuple(out)


def _a_post(h0, h1, zg):
    return (h0 + h1) * jax.nn.gelu(zg)


SUBLANES = 8
SCAN_TILES = 8


def _scan_jobs(jobs):
    S, C = jobs[0][0].shape
    U = min(SCAN_TILES, S // SUBLANES)
    rows = U * SUBLANES
    row = lax.broadcasted_iota(jnp.int32, (SUBLANES, C), 0)

    def prefix(a, b, reverse):
        for sh in (1, 2, 4):
            if reverse:
                m, r = row < SUBLANES - sh, SUBLANES - sh
            else:
                m, r = row >= sh, sh
            a_s = jnp.where(m, pltpu.roll(a, r, 0), 1.0)
            b_s = jnp.where(m, pltpu.roll(b, r, 0), 0.0)
            b = a * b_s + b
            a = a * a_s
        return a, b

    def step(i, carries):
        out = []
        for (a_ref, b_ref, h_ref, reverse), c in zip(jobs, carries):
            blk = (S // rows - 1 - i) if reverse else i
            t0 = pl.multiple_of(blk * rows, rows)
            order = range(U - 1, -1, -1) if reverse else range(U)
            edge = slice(0, 1) if reverse else slice(SUBLANES - 1, SUBLANES)
            for j in order:
                sl = pl.ds(t0 + j * SUBLANES, SUBLANES)
                a, b = prefix(a_ref[sl, :], b_ref[sl, :], reverse)
                h_ref[sl, :] = a * jnp.broadcast_to(c, (SUBLANES, C)) + b
                c = a[edge, :] * c + b[edge, :]
            out.append(c)
        return tuple(out)

    lax.fori_loop(0, S // rows, step, tuple(jnp.zeros((1, C), F32) for _ in jobs))


def _a_load_params(cw_ref, cb_ref, gw_ref, gb_ref, lam_ref):
    cws = [cw_ref[k:k + 1, :] for k in range(CONV_W)]
    gws = [gw_ref[d, g] for d in range(2) for g in range(2)]
    gbs = [gb_ref[d, g] for d in range(2) for g in range(2)]
    lams = [lam_ref[d:d + 1, :] for d in range(2)]
    return cws, cb_ref[...], gws, gbs, lams


def _a_in_specs(S, H):
    zg = pl.BlockSpec((None, S, LANES), lambda h, b: (b, 0, h))
    zx = pl.BlockSpec((None, S, LANES), lambda h, b: (b, 0, H + h))
    cw = pl.BlockSpec((CONV_W, LANES), lambda h, b: (0, h))
    cb = pl.BlockSpec((1, LANES), lambda h, b: (0, h))
    gw = pl.BlockSpec((2, 2, None, LANES, LANES), lambda h, b: (0, 0, h, 0, 0))
    gb = pl.BlockSpec((2, 2, None, 1, LANES), lambda h, b: (0, 0, h, 0, 0))
    lam = pl.BlockSpec((2, LANES), lambda h, b: (0, h))
    return zg, zx, cw, cb, gw, gb, lam


def _a_core_fwd(z, cw, cb, gw, gb, lam, name):
    Bq, S, D2 = z.shape
    D = D2 // 2
    H = D // LANES

    def body(zg_ref, zx_ref, cw_ref, cb_ref, gw_ref, gb_ref, lam_ref, y_ref, a_s, b_s, h_s):
        ab = _a_pre(zx_ref[...], *_a_load_params(cw_ref, cb_ref, gw_ref, gb_ref, lam_ref))
        for d in range(2):
            a_s[d] = ab[2 * d]
            b_s[d] = ab[2 * d + 1]
        _scan_jobs([(a_s.at[d], b_s.at[d], h_s.at[d], d == 1) for d in range(2)])
        y_ref[...] = _a_post(h_s[0], h_s[1], zg_ref[...]).astype(BF)

    seq = pltpu.VMEM((2, S, LANES), F32)
    return pl.pallas_call(
        body, name=name, grid=(H, Bq), in_specs=list(_a_in_specs(S, H)),
        out_specs=pl.BlockSpec((None, S, LANES), lambda h, b: (b, 0, h)),
        out_shape=jax.ShapeDtypeStruct((Bq, S, D), BF), scratch_shapes=[seq, seq, seq],
        compiler_params=_params(("parallel", "arbitrary")))(z, z, cw, cb.reshape(1, D), gw, gb.reshape(2, 2, H, 1, LANES), lam)


def _a_core_bwd(z, dy, cw, cb, gw, gb, lam, name):
    Bq, S, D2 = z.shape
    D = D2 // 2
    H = D // LANES

    def body(zg_ref, zx_ref, dy_ref, cw_ref, cb_ref, gw_ref, gb_ref, lam_ref,
             dzg_ref, dzx_ref, dcw_ref, dcb_ref, dgw_ref, dgb_ref, dlam_ref, a_s, b_s, h_s, l_s):
        prm = _a_load_params(cw_ref, cb_ref, gw_ref, gb_ref, lam_ref)
        ab, pre_vjp = jax.vjp(_a_pre, zx_ref[...], *prm)
        for d in range(2):
            a_s[d] = ab[2 * d]
            b_s[d] = ab[2 * d + 1]
        _scan_jobs([(a_s.at[d], b_s.at[d], h_s.at[d], d == 1) for d in range(2)])
        _, post_vjp = jax.vjp(_a_post, h_s[0], h_s[1], zg_ref[...])
        dh0, dh1, dzg = post_vjp(dy_ref[...])
        dzg_ref[...] = dzg.astype(BF)
        for d, dh in ((0, dh0), (1, dh1)):
            b_s[d] = dh
            a_s[d] = _shift(a_s[d], -1 if d == 0 else 1)
        _scan_jobs([(a_s.at[d], b_s.at[d], l_s.at[d], d == 0) for d in range(2)])
        cot = []
        for d in range(2):
            cot += [l_s[d] * _shift(h_s[d], 1 if d == 0 else -1), l_s[d]]
        dzx, dcws, dcb, dgws, dgbs, dlams = pre_vjp(tuple(cot))
        dzx_ref[...] = dzx.astype(BF)

        @pl.when(pl.program_id(1) == 0)
        def _():
            for r in (dcw_ref, dcb_ref, dgw_ref, dgb_ref, dlam_ref):
                r[...] = jnp.zeros_like(r)

        for k in range(CONV_W):
            dcw_ref[k:k + 1, :] += dcws[k]
        dcb_ref[...] += dcb
        for d in range(2):
            dlam_ref[d:d + 1, :] += dlams[d]
            for g in range(2):
                dgw_ref[d, g] += dgws[2 * d + g]
                dgb_ref[d, g] += dgbs[2 * d + g]

    zg, zx, cws, cbs, gws, gbs, lams = _a_in_specs(S, H)
    dyspec = pl.BlockSpec((None, S, LANES), lambda h, b: (b, 0, h))
    seq = pltpu.VMEM((2, S, LANES), F32)
    dzg, dzx, dcw, dcb, dgw, dgb, dlam = pl.pallas_call(
        body, name=name, grid=(H, Bq), in_specs=[zg, zx, dyspec, cws, cbs, gws, gbs, lams],
        out_specs=[dyspec, dyspec, cws, cbs, gws, gbs, lams],
        out_shape=[jax.ShapeDtypeStruct((Bq, S, D), BF), jax.ShapeDtypeStruct((Bq, S, D), BF),
                   jax.ShapeDtypeStruct((CONV_W, D), F32), jax.ShapeDtypeStruct((1, D), F32),
                   jax.ShapeDtypeStruct((2, 2, H, LANES, LANES), F32), jax.ShapeDtypeStruct((2, 2, H, 1, LANES), F32),
                   jax.ShapeDtypeStruct((2, D), F32)],
        scratch_shapes=[seq, seq, seq, seq],
        compiler_params=_params(("parallel", "arbitrary"), VMEM_LIMIT_HIGH))(
            z, z, dy, cw, cb.reshape(1, D), gw, gb.reshape(2, 2, H, 1, LANES), lam)
    dz = jnp.concatenate([dzg, dzx], axis=-1)
    return dz, dcw, dcb.reshape(D), dgw, dgb.reshape(2, 2, H, LANES), dlam


def _b_fn(z, lng, lnb, wss, bsf):
    D = z.shape[1] // 2
    zz = jax.nn.gelu(z)
    u, v = zz[:, :D], zz[:, D:]
    mu = jnp.mean(v, axis=-1, keepdims=True)
    var = jnp.mean(jnp.square(v - mu), axis=-1, keepdims=True)
    vn = (v - mu) * lax.rsqrt(var + 1e-5) * lng + lnb
    vs = jnp.concatenate([_bmm(wss[g], vn[:, g * LANES:(g + 1) * LANES], "nn") for g in range(D // LANES)], axis=1)
    return u * (vs + bsf)


def _b_specs(D, GB):
    row = lambda w: pl.BlockSpec((SGU_CHUNK, w), lambda i: (i, 0))
    vec = pl.BlockSpec((1, D), lambda i: (0, 0))
    ws = pl.BlockSpec((GB, SGU_CHUNK, SGU_CHUNK), lambda i: (0, 0, 0))
    bsf = pl.BlockSpec((SGU_CHUNK, D), lambda i: (0, 0))
    return row, vec, ws, bsf


def _b_core_fwd(z, lng, lnb, ws, bsf, name):
    T, D2 = z.shape
    D = D2 // 2
    GB = D // LANES
    row, vec, wspec, bspec = _b_specs(D, GB)

    def body(z_ref, lng_ref, lnb_ref, ws_ref, bsf_ref, y_ref):
        wss = [ws_ref[g] for g in range(GB)]
        y_ref[...] = _b_fn(z_ref[...], lng_ref[...], lnb_ref[...], wss, bsf_ref[...]).astype(BF)

    return pl.pallas_call(
        body, name=name, grid=(T // SGU_CHUNK,), in_specs=[row(D2), vec, vec, wspec, bspec], out_specs=row(D),
        out_shape=jax.ShapeDtypeStruct((T, D), BF), compiler_params=_params(("parallel",)))(
            z, lng.reshape(1, D), lnb.reshape(1, D), ws, bsf)


def _b_core_bwd(z, dy, lng, lnb, ws, bsf, name):
    T, D2 = z.shape
    D = D2 // 2
    GB = D // LANES
    row, vec, wspec, bspec = _b_specs(D, GB)

    def body(z_ref, dy_ref, lng_ref, lnb_ref, ws_ref, bsf_ref, dz_ref, dlng_ref, dlnb_ref, dws_ref, dbsf_ref):
        wss = [ws_ref[g] for g in range(GB)]
        _, vjp = jax.vjp(_b_fn, z_ref[...], lng_ref[...], lnb_ref[...], wss, bsf_ref[...])
        dz, dlng, dlnb, dwss, dbsf = vjp(dy_ref[...])
        dz_ref[...] = dz.astype(BF)

        @pl.when(pl.program_id(0) == 0)
        def _():
            for r in (dlng_ref, dlnb_ref, dws_ref, dbsf_ref):
                r[...] = jnp.zeros_like(r)

        dlng_ref[...] += dlng
        dlnb_ref[...] += dlnb
        dbsf_ref[...] += dbsf
        for g in range(GB):
            dws_ref[g] += dwss[g]

    dz, dlng, dlnb, dws, dbsf = pl.pallas_call(
        body, name=name, grid=(T // SGU_CHUNK,), in_specs=[row(D2), row(D), vec, vec, wspec, bspec],
        out_specs=[row(D2), vec, vec, wspec, bspec],
        out_shape=[jax.ShapeDtypeStruct((T, D2), BF), jax.ShapeDtypeStruct((1, D), F32), jax.ShapeDtypeStruct((1, D), F32),
                   jax.ShapeDtypeStruct((GB, SGU_CHUNK, SGU_CHUNK), F32), jax.ShapeDtypeStruct((SGU_CHUNK, D), F32)],
        compiler_params=_params(("arbitrary",)))(z, dy, lng.reshape(1, D), lnb.reshape(1, D), ws, bsf)
    return dz, dlng.reshape(D), dlnb.reshape(D), dws, dbsf


def _lane_is(j):
    return lax.broadcasted_iota(jnp.int32, (1, LANES), 1) == j


def _lane_col(x, j):
    return jnp.sum(jnp.where(_lane_is(j), x, 0.0), axis=1, keepdims=True)


def _c_pre(zq, zk, zv, zs, cwq, cwk, cwv, pcs, head, HC):
    q = _silu(_conv4(zq, cwq))
    k = _silu(_conv4(zk, cwk))
    v = _silu(_conv4(zv, cwv))
    q = q * lax.rsqrt(jnp.sum(q * q, axis=-1, keepdims=True) + 1e-6) * (LANES ** -0.5)
    k = k * lax.rsqrt(jnp.sum(k * k, axis=-1, keepdims=True) + 1e-6)
    gbp = jnp.zeros_like(zs)
    for d in range(2):
        a_logit = _lane_col(zs, d * HC + head)
        b_logit = _lane_col(zs, 2 * HC + d * HC + head)
        g = -jnp.exp(pcs[d]) * _softplus(a_logit + pcs[2 + d])
        beta = jnp.broadcast_to(_sigmoid(b_logit), g.shape)
        gbp = gbp + jnp.where(_lane_is(2 * d), g, 0.0) + jnp.where(_lane_is(2 * d + 1), beta, 0.0)
    return q, k, v, gbp


def _mm3(x, y):
    xh, yh = x.astype(BF), y.astype(BF)
    xl, yl = (x - xh.astype(F32)).astype(BF), (y - yh.astype(F32)).astype(BF)
    return _bmm_raw(xh, yh, "nn") + _bmm_raw(xh, yl, "nn") + _bmm_raw(xl, yh, "nn")


def _tri_inv_impl(a):
    C = a.shape[-1]
    eye = (lax.broadcasted_iota(jnp.int32, (1, C, C), 1) == lax.broadcasted_iota(jnp.int32, (1, C, C), 2)).astype(F32)
    r = eye - a
    p = a
    n = 2
    while n < C:
        p = _mm3(p, p)
        r = r + _mm3(r, p)
        n *= 2
    return r


@jax.custom_vjp
def _tri_inv(a):
    return _tri_inv_impl(a)


def _tri_inv_fwd(a):
    t = _tri_inv_impl(a)
    return t, t


def _tri_inv_bwd(t, g):
    tt = jnp.swapaxes(t, 1, 2)
    return (-_bmm_raw(_bmm_raw(tt, g, "nn"), tt, "nn"),)


_tri_inv.defvjp(_tri_inv_fwd, _tri_inv_bwd)


@jax.custom_vjp
def _pair_diff(gc3):
    m = gc3[:, :, :gc3.shape[1]]
    return m - jnp.swapaxes(m, 1, 2)


def _pair_diff_fwd(gc3):
    return _pair_diff(gc3), None


def _pair_diff_bwd(_, g):
    d = jnp.sum(g, axis=2, keepdims=True) - jnp.sum(jnp.swapaxes(g, 1, 2), axis=2, keepdims=True)
    return (jnp.broadcast_to(d * (1.0 / LANES), d.shape[:2] + (LANES,)),)


_pair_diff.defvjp(_pair_diff_fwd, _pair_diff_bwd)


@jax.custom_vjp
def _tri_inv_saved(a, t):
    return t


def _tri_inv_saved_fwd(a, t):
    return t, t


def _tri_inv_saved_bwd(t, g):
    return _tri_inv_bwd(t, g)[0], jnp.zeros_like(t)


_tri_inv_saved.defvjp(_tri_inv_saved_fwd, _tri_inv_saved_bwd)


def _c_phase1(q, k, v, gbp, rev, t_saved=None):
    S = q.shape[0]
    C = GDN_CHUNK
    N = S // C
    col = 2 if rev else 0
    gB = jnp.broadcast_to(_lane_col(gbp, col), (S, LANES))
    bB = jnp.broadcast_to(_lane_col(gbp, col + 1), (S, LANES))
    r3 = lambda t: t.reshape(N, C, LANES)
    gc3 = r3(_chunk_cumsum(gB, rev, C))
    q3, k3, v3, b3, g3 = r3(q), r3(k), r3(v), r3(bB), r3(gB)
    ri = lax.broadcasted_iota(jnp.int32, (1, C, C), 1)
    ci = lax.broadcasted_iota(jnp.int32, (1, C, C), 2)
    incl = (ri <= ci) if rev else (ri >= ci)
    strict = (ri < ci) if rev else (ri > ci)
    decay = jnp.where(incl, jnp.exp(jnp.where(incl, _pair_diff(gc3), 0.0)), 0.0)
    kb = k3 * b3
    vb = v3 * b3
    A = jnp.where(strict, _bmm(kb, k3, "nt") * decay, 0.0)
    T = _tri_inv(A) if t_saved is None else _tri_inv_saved(A, t_saved)
    egc = jnp.exp(gc3)
    u = _bmm(T, vb, "nn")
    w = _bmm(T, kb * egc, "nn")
    qk = _bmm(q3, k3, "nt") * decay
    glast = jnp.sum(g3, axis=1, keepdims=True)
    kd = k3 * jnp.exp(glast - gc3)
    k2 = _bmm(kd, w, "tn")
    z = _bmm(kd, u, "tn")
    qe2 = q3 * egc - _bmm(qk, w, "nn")
    o0 = _bmm(qk, u, "nn")
    return k2, z, jnp.exp(glast), qe2, o0, T


def _c_next_state(state, k2, z, eg):
    return state * eg - _bmm(k2, state, "nn") + z


def _c_out(state, qe2, o0):
    return _bmm(qe2, state, "nn") + o0


def _c_post(o, zg, ng):
    return _rms(o, ng) * _silu(zg)


def _c_pre_specs(S, H):
    col = lambda c0: pl.BlockSpec((None, S, LANES), lambda h, b: (b, 0, c0 * H + h))
    zs = pl.BlockSpec((None, S, LANES), lambda h, b: (b, 0, 0))
    cw = lambda c0: pl.BlockSpec((None, CONV_W, LANES), lambda h, b: (c0, 0, h))
    pc = pl.BlockSpec((None, 4, LANES), lambda h, b: (h, 0, 0))
    return col, zs, cw, pc


def _c_pre_fwd(z, zs, cw3, pc, name):
    Bq, S, D4 = z.shape
    D = D4 // 4
    H = D // LANES
    col, zss, cw, pcs = _c_pre_specs(S, H)

    def body(zq_ref, zk_ref, zv_ref, zs_ref, cwq_ref, cwk_ref, cwv_ref, pc_ref, q_ref, k_ref, v_ref, gbp_ref):
        rows = lambda r: [r[i:i + 1, :] for i in range(r.shape[0])]
        q, k, v, gbp = _c_pre(zq_ref[...], zk_ref[...], zv_ref[...], zs_ref[...], rows(cwq_ref), rows(cwk_ref),
                              rows(cwv_ref), rows(pc_ref), pl.program_id(0), H)
        q_ref[...] = q
        k_ref[...] = k
        v_ref[...] = v
        gbp_ref[...] = gbp

    out = pl.BlockSpec((None, S, LANES), lambda h, b: (b, 0, h))
    shp = jax.ShapeDtypeStruct((Bq, S, D), F32)
    return pl.pallas_call(
        body, name=name, grid=(H, Bq), in_specs=[col(0), col(1), col(2), zss, cw(0), cw(1), cw(2), pcs],
        out_specs=[out] * 4, out_shape=[shp] * 4, compiler_params=_params(("parallel", "arbitrary")))(
            z, z, z, zs, cw3, cw3, cw3, pc)


def _c_pre_bwd(z, zs, cw3, pc, dq, dk, dv, dgbp, name):
    Bq, S, D4 = z.shape
    D = D4 // 4
    H = D // LANES
    col, zss, cw, pcs = _c_pre_specs(S, H)

    def body(zq_ref, zk_ref, zv_ref, zs_ref, cwq_ref, cwk_ref, cwv_ref, pc_ref, dq_ref, dk_ref, dv_ref, dgbp_ref,
             dzq_ref, dzk_ref, dzv_ref, dzs_ref, dcw_ref, dpc_ref):
        rows = lambda r: [r[i:i + 1, :] for i in range(r.shape[0])]
        fn = functools.partial(_c_pre, head=pl.program_id(0), HC=H)
        _, vjp = jax.vjp(fn, zq_ref[...], zk_ref[...], zv_ref[...], zs_ref[...], rows(cwq_ref), rows(cwk_ref),
                         rows(cwv_ref), rows(pc_ref))
        dzq, dzk, dzv, dzs, dcwq, dcwk, dcwv, dpcs = vjp((dq_ref[...], dk_ref[...], dv_ref[...], dgbp_ref[...]))
        dzq_ref[...] = dzq.astype(BF)
        dzk_ref[...] = dzk.astype(BF)
        dzv_ref[...] = dzv.astype(BF)
        dzs_ref[...] = dzs

        @pl.when(pl.program_id(1) == 0)
        def _():
            dcw_ref[...] = jnp.zeros_like(dcw_ref)
            dpc_ref[...] = jnp.zeros_like(dpc_ref)

        for c, dc in enumerate((dcwq, dcwk, dcwv)):
            for i in range(CONV_W):
                dcw_ref[c, i:i + 1, :] += dc[i]
        for i in range(4):
            dpc_ref[i:i + 1, :] += dpcs[i]

    out = pl.BlockSpec((None, S, LANES), lambda h, b: (b, 0, h))
    dzs_spec = pl.BlockSpec((None, None, S, LANES), lambda h, b: (h, b, 0, 0))
    dcw_spec = pl.BlockSpec((3, CONV_W, LANES), lambda h, b: (0, 0, h))
    bshape = jax.ShapeDtypeStruct((Bq, S, D), BF)
    dzq, dzk, dzv, dzs, dcw3, dpc = pl.pallas_call(
        body, name=name, grid=(H, Bq),
        in_specs=[col(0), col(1), col(2), zss, cw(0), cw(1), cw(2), pcs, out, out, out, out],
        out_specs=[out, out, out, dzs_spec, dcw_spec, pcs],
        out_shape=[bshape, bshape, bshape, jax.ShapeDtypeStruct((H, Bq, S, LANES), F32),
                   jax.ShapeDtypeStruct((3, CONV_W, D), F32), jax.ShapeDtypeStruct((H, 4, LANES), F32)],
        compiler_params=_params(("parallel", "arbitrary")))(z, z, z, zs, cw3, cw3, cw3, pc, dq, dk, dv, dgbp)
    return dzq, dzk, dzv, dzs, dcw3, dpc


def _c_saved_shapes(Bq, H, S):
    N, C = S // GDN_CHUNK, GDN_CHUNK
    return [(Bq, H, 2, N, LANES, LANES), (Bq, H, 2, N, LANES, LANES), (Bq, H, 2, N, 1, LANES), (Bq, H, 2, N, C, LANES),
            (Bq, H, 2, N, C, C)]


def _c_saved_scratch(S):
    return [pltpu.VMEM(shp[3:], F32) for shp in _c_saved_shapes(1, 1, S)]


PHASE1_CHUNKS = 16


PHASE1_CHUNKS_FWD = 32


def _c_blocks(S, fn, chunks=PHASE1_CHUNKS):
    nb = min(chunks, S // GDN_CHUNK)
    rows = nb * GDN_CHUNK

    def blk(i, carry):
        fn(pl.ds(pl.multiple_of(i * rows, rows), rows), pl.ds(pl.multiple_of(i * nb, nb), nb))
        return carry

    lax.fori_loop(0, S // rows, blk, 0)


def _c_phase1_blocks(in_refs, k2_ref, z_ref, eg_ref, qe2_ref, t_ref, o_ref, rev):
    def fn(rows, chunks):
        k2, z, eg, qe2, o0, t = _c_phase1(*[r[rows, :] for r in in_refs], rev)
        k2_ref[chunks] = k2
        z_ref[chunks] = z
        eg_ref[chunks] = eg
        qe2_ref[chunks] = qe2
        t_ref[chunks] = t
        o_ref[chunks] += o0

    _c_blocks(in_refs[0].shape[0], fn, PHASE1_CHUNKS_FWD)


def _c_sweep(jobs):
    N = jobs[0][3].shape[0]

    def step(i, states):
        out = []
        for (k2_ref, z_ref, eg_ref, st_ref, rev), state in zip(jobs, states):
            n = (N - 1 - i) if rev else i
            st_ref[n] = state
            out.append(_c_next_state(state, k2_ref[n], z_ref[n], eg_ref[n]))
        return tuple(out)

    lax.fori_loop(0, N, step, tuple(jnp.zeros((LANES, LANES), F32) for _ in jobs))


def _c_sweep_adjoint(jobs):
    N = jobs[0][3].shape[0]

    def step(i, gs):
        out = []
        for (k2_ref, eg_ref, dso_ref, gs_ref, rev), g in zip(jobs, gs):
            n = i if rev else (N - 1 - i)
            gs_ref[n] = g
            out.append(dso_ref[n] + g * eg_ref[n] - _bmm_raw(k2_ref[n], g, "tn"))
        return tuple(out)

    lax.fori_loop(0, N, step, tuple(jnp.zeros((LANES, LANES), F32) for _ in jobs))


def _c_mid_fwd(q, k, v, gbp, name):
    Bq, S, D = q.shape
    H = D // LANES
    N = S // GDN_CHUNK
    blk = pl.BlockSpec((None, S, LANES), lambda h, b: (b, 0, h))
    blk3 = pl.BlockSpec((None, N, GDN_CHUNK, LANES), lambda h, b: (b, 0, 0, h))
    n_saved = len(_c_saved_shapes(Bq, H, S))

    def body(q_ref, k_ref, v_ref, gbp_ref, o3, *rest):
        saved_hbm, scr = rest[:n_saved], rest[n_saved:]
        per_dir = n_saved + 1
        sems = scr[2 * per_dir]
        o3[...] = jnp.zeros_like(o3)
        sets = [scr[d * per_dir:(d + 1) * per_dir] for d in range(2)]
        copies = []

        def keep(d, i):
            copies.append(pltpu.make_async_copy(sets[d][i], saved_hbm[i].at[pl.program_id(1), pl.program_id(0), d],
                                                sems.at[d, i]))
            copies[-1].start()

        for d, rev in enumerate((False, True)):
            k2_ref, st_ref, eg_ref, qe2_ref, t_ref, z_ref = sets[d]
            _c_phase1_blocks((q_ref, k_ref, v_ref, gbp_ref), k2_ref, z_ref, eg_ref, qe2_ref, t_ref, o3, rev)
            for i in (0, 2, 3, 4):
                keep(d, i)
        _c_sweep([(sets[d][0], sets[d][5], sets[d][2], sets[d][1], d == 1) for d in range(2)])
        for d in range(2):
            keep(d, 1)
            qe2_ref, st_ref = sets[d][3], sets[d][1]

            def add_out(rows, chunks, qe2_ref=qe2_ref, st_ref=st_ref):
                o3[chunks] += _bmm_raw(qe2_ref[chunks], st_ref[chunks], "nn")

            _c_blocks(S, add_out)
        for cp in copies:
            cp.wait()

    one_dir = _c_saved_scratch(S) + [pltpu.VMEM((N, LANES, LANES), F32)]
    outs = pl.pallas_call(
        body, name=name, grid=(H, Bq), in_specs=[blk] * 4,
        out_specs=[blk3] + [pl.BlockSpec(memory_space=pltpu.HBM)] * n_saved,
        out_shape=[jax.ShapeDtypeStruct((Bq, N, GDN_CHUNK, D), F32)]
        + [jax.ShapeDtypeStruct(shp, F32) for shp in _c_saved_shapes(Bq, H, S)],
        scratch_shapes=one_dir + one_dir + [pltpu.SemaphoreType.DMA((2, n_saved))],
        compiler_params=_params(("parallel", "parallel")))(q, k, v, gbp)
    return outs[0].reshape(Bq, S, D), tuple(outs[1:])


def _c_mid_bwd(q, k, v, gbp, do, saved, name):
    Bq, S, D = q.shape
    H = D // LANES
    N = S // GDN_CHUNK
    blk = pl.BlockSpec((None, S, LANES), lambda h, b: (b, 0, h))
    blk3 = pl.BlockSpec((None, N, GDN_CHUNK, LANES), lambda h, b: (b, 0, 0, h))
    n_saved = len(saved)

    def body(q_ref, k_ref, v_ref, gbp_ref, do3, *rest):
        saved_hbm = rest[:n_saved]
        dq_ref, dk_ref, dv_ref, dgbp_ref = rest[n_saved:n_saved + 4]
        scr = rest[n_saved + 4:]
        sets = (scr[:n_saved], scr[n_saved:2 * n_saved])
        dso_refs, gs_refs, sems = scr[2 * n_saved:2 * n_saved + 2], scr[2 * n_saved + 2:2 * n_saved + 4], scr[-1]
        in_refs = (q_ref, k_ref, v_ref, gbp_ref)
        out_refs = (dq_ref, dk_ref, dv_ref, dgbp_ref)
        copies = [pltpu.make_async_copy(src.at[pl.program_id(1), pl.program_id(0), d], dst, sems.at[d, i])
                  for d in range(2) for i, (src, dst) in enumerate(zip(saved_hbm, sets[d]))]
        for cp in copies:
            cp.start()
        for cp in copies:
            cp.wait()
        for d in range(2):
            qe2_ref, dso_ref = sets[d][3], dso_refs[d]

            def out_to_state(rows, chunks, qe2_ref=qe2_ref, dso_ref=dso_ref):
                dso_ref[chunks] = _bmm_raw(qe2_ref[chunks], do3[chunks], "tn")

            _c_blocks(S, out_to_state)
        _c_sweep_adjoint([(sets[d][0], sets[d][2], dso_refs[d], gs_refs[d], d == 1) for d in range(2)])
        for d, rev in enumerate((False, True)):
            k2_ref, st_ref, eg_ref, qe2_ref, t_ref = sets[d]
            gs_ref = gs_refs[d]

            def block_vjp(rows, chunks):
                states, t_saved = st_ref[chunks], t_ref[chunks]

                def chunk_fn(q_, k_, v_, gbp_):
                    k2, z, eg, qe2, o0, _ = _c_phase1(q_, k_, v_, gbp_, rev, t_saved)
                    return _c_next_state(states, k2, z, eg), _c_out(states, qe2, o0)

                _, vjp = jax.vjp(chunk_fn, *[r[rows, :] for r in in_refs])
                for r, c in zip(out_refs, vjp((gs_ref[chunks], do3[chunks]))):
                    if rev:
                        r[rows, :] += c
                    else:
                        r[rows, :] = c

            _c_blocks(S, block_vjp)

    shp = jax.ShapeDtypeStruct((Bq, S, D), F32)
    mat = pltpu.VMEM((N, LANES, LANES), F32)
    return pl.pallas_call(
        body, name=name, grid=(H, Bq), in_specs=[blk] * 4 + [blk3] + [pl.BlockSpec(memory_space=pltpu.HBM)] * n_saved,
        out_specs=[blk] * 4, out_shape=[shp] * 4,
        scratch_shapes=_c_saved_scratch(S) + _c_saved_scratch(S) + [mat] * 4 + [pltpu.SemaphoreType.DMA((2, n_saved))],
        compiler_params=_params(("parallel", "parallel")))(q, k, v, gbp, do.reshape(Bq, N, GDN_CHUNK, D), *saved)


def _c_post_fwd(o, z, ng, name):
    Bq, S, D = o.shape
    H = D // LANES
    blk = pl.BlockSpec((None, S, LANES), lambda h, b: (b, 0, h))
    gate = pl.BlockSpec((None, S, LANES), lambda h, b: (b, 0, 3 * H + h))
    vec = pl.BlockSpec((1, LANES), lambda h, b: (0, 0))

    def body(o_ref, zg_ref, ng_ref, y_ref):
        y_ref[...] = _c_post(o_ref[...], zg_ref[...], ng_ref[...]).astype(BF)

    return pl.pallas_call(
        body, name=name, grid=(H, Bq), in_specs=[blk, gate, vec], out_specs=blk,
        out_shape=jax.ShapeDtypeStruct((Bq, S, D), BF), compiler_params=_params(("parallel", "parallel")))(
            o, z, ng.reshape(1, LANES))


def _c_post_bwd(o, z, ng, dy, name):
    Bq, S, D = o.shape
    H = D // LANES
    blk = pl.BlockSpec((None, S, LANES), lambda b, h: (b, 0, h))
    gate = pl.BlockSpec((None, S, LANES), lambda b, h: (b, 0, 3 * H + h))
    vec = pl.BlockSpec((1, LANES), lambda b, h: (0, 0))

    def body(o_ref, zg_ref, ng_ref, dy_ref, do_ref, dzg_ref, dng_ref):
        _, vjp = jax.vjp(_c_post, o_ref[...], zg_ref[...], ng_ref[...])
        do, dzg, dng = vjp(dy_ref[...])
        do_ref[...] = do
        dzg_ref[...] = dzg.astype(BF)

        @pl.when((pl.program_id(0) == 0) & (pl.program_id(1) == 0))
        def _():
            dng_ref[...] = jnp.zeros_like(dng_ref)

        dng_ref[...] += dng

    do, dzg, dng = pl.pallas_call(
        body, name=name, grid=(Bq, H), in_specs=[blk, gate, vec, blk], out_specs=[blk, blk, vec],
        out_shape=[jax.ShapeDtypeStruct((Bq, S, D), F32), jax.ShapeDtypeStruct((Bq, S, D), BF),
                   jax.ShapeDtypeStruct((1, LANES), F32)],
        compiler_params=_params(("arbitrary", "arbitrary")))(o, z, ng.reshape(1, LANES), dy)
    return do, dzg, dng.reshape(LANES)


def _c_param_rows(a_log, dt_bias):
    p = jnp.concatenate([a_log, dt_bias], axis=0).T
    return jnp.broadcast_to(p[:, :, None], p.shape + (LANES,)).astype(F32)


def _local_step(x, tgt, W, by_chip=False):
    Bq, S, D = x.shape
    T = Bq * S
    H = D // LANES
    L = W["norm_mix_g"].shape[0]
    seq = lambda t: t.reshape(Bq, S, t.shape[-1])
    flat = lambda t: t.reshape(T, t.shape[-1])
    if "mlp_up_slots" in W:
        up4, down4 = W["mlp_up_slots"], W["mlp_down_slots"]
    else:
        up4 = W["mlp_w_up"].reshape(L, D, N_CHIPS, -1).transpose(2, 0, 1, 3)
        down4 = W["mlp_w_down"].reshape(L, N_CHIPS, -1, D).transpose(1, 0, 2, 3)
    sw = up4.shape[-1]
    F = N_CHIPS * sw

    xs = flat(x)
    saved = []
    for i in range(L):
        kind, j = i % N_MIXERS, i // N_MIXERS
        tag = f"l{i}"
        sv = {"x": xs}
        hn = _rms_fwd(xs, W["norm_mix_g"][i], f"{tag}_mix_norm")
        sv["hn"] = hn
        if kind == 0:
            z = _mm(hn, W["a_w_in"][j], "nn", f"{tag}_a_in")
            y = _a_core_fwd(seq(z), W["a_conv_w"][j], W["a_conv_b"][j], W["a_gate_w"][j], W["a_gate_b"][j],
                            W["a_lambda"][j], f"{tag}_a_core")
            sv["z"] = z
            w_out = W["a_w_out"][j]
        elif kind == 1:
            z = _mm(hn, W["b_w_in"][j], "nn", f"{tag}_b_in")
            bsf = jnp.repeat(W["b_b_s"][j].T, LANES, axis=1)
            y = _b_core_fwd(z, W["b_ln_g"][j], W["b_ln_b"][j], W["b_w_s"][j], bsf, f"{tag}_b_core")
            sv["z"], sv["bsf"] = z, bsf
            w_out = W["b_w_out"][j]
        else:
            w_in = W["c_w_in"][j]
            w_small = jnp.pad(w_in[:, 4 * D:], ((0, 0), (0, LANES - 4 * H)))
            z = _mm(hn, w_in[:, :4 * D], "nn", f"{tag}_c_in")
            zs = _mm(hn, w_small, "nn", f"{tag}_c_in_small")
            cw3 = W["c_conv_w"][j].reshape(CONV_W, 3, D).transpose(1, 0, 2)
            pc = _c_param_rows(W["c_a_log"][j], W["c_dt_bias"][j])
            q, k, v, gbp = _c_pre_fwd(seq(z), seq(zs), cw3, pc, f"{tag}_c_pre")
            o, sv["mid"] = _c_mid_fwd(q, k, v, gbp, f"{tag}_c_mid")
            y = _c_post_fwd(o, seq(z), W["c_norm_g"][j], f"{tag}_c_post")
            sv.update(z=z, zs=zs, cw3=cw3, pc=pc, q=q, k=k, v=v, gbp=gbp, o=o, w_small=w_small)
            w_out = W["c_w_out"][j]
        y = flat(y)
        sv["y"] = y
        x1 = _mm(y, w_out, "nn", f"{tag}_mix_out", epi="add", extra=xs)
        sv["x1"] = x1
        hn2 = _rms_fwd(x1, W["norm_mlp_g"][i], f"{tag}_mlp_norm")
        act = _mm(hn2, up4, "nn", f"{tag}_mlp_up", out_dtype=BF, epi="relu2", tm=2048, tn=sw, tk=D, n_cols=F,
                  b_index=lambda j, k, i=i: (j, i, 0, 0))
        xs = _mm(act, down4, "nn", f"{tag}_mlp_down", epi="add", extra=x1, tn=D, tk=sw, n_cols=D,
                 b_index=lambda j, k, i=i: (k, i, 0, j))
        sv["hn2"], sv["act"] = hn2, act
        saved.append(sv)

    loss, dx, dgf = _final_loss(xs, W["norm_final_g"], flat(tgt), "final_loss")

    G = {"norm_final_g": dgf}
    per_layer = {n: [None] * L for n in ("norm_mix_g", "norm_mlp_g", "mlp_w_up", "mlp_w_down")}
    mixer = {}
    for i in reversed(range(L)):
        kind, j = i % N_MIXERS, i // N_MIXERS
        tag = f"l{i}"
        sv = saved[i]
        dhid = _mm(dx[1],down4, "nt", f"{tag}_mlp_dhid", out_dtype=BF, epi="relu2_bwd", extra=sv["act"], tm=2048, tn=sw, tk=D,
                   n_cols=F, b_index=lambda j, k, i=i: (j, i, 0, 0))
        per_layer["mlp_w_down"][i] = _mm(sv["act"], dx[1], "tn", f"{tag}_mlp_dwdown").reshape(N_CHIPS, sw, D)
        per_layer["mlp_w_up"][i] = _mm(sv["hn2"], dhid, "tn", f"{tag}_mlp_dwup", tn=sw, by_chip="cols")
        dhn2 = _mm(dhid, up4, "nt", f"{tag}_mlp_dhn", tn=D, tk=sw, n_cols=D, b_index=lambda j, k, i=i: (k, i, j, 0))
        dx, per_layer["norm_mlp_g"][i] = _rms_bwd(sv["x1"], W["norm_mlp_g"][i], dhn2, dx[0], f"{tag}_mlp_norm_bwd")
        g = {}
        if kind == 0:
            dy = _mm(dx[1],W["a_w_out"][j], "nt", f"{tag}_a_dy")
            g["a_w_out"] = _mm(sv["y"], dx[1], "tn", f"{tag}_a_dwout")
            dz, g["a_conv_w"], g["a_conv_b"], g["a_gate_w"], g["a_gate_b"], g["a_lambda"] = _a_core_bwd(
                seq(sv["z"]), seq(dy), W["a_conv_w"][j], W["a_conv_b"][j], W["a_gate_w"][j], W["a_gate_b"][j],
                W["a_lambda"][j], f"{tag}_a_core_bwd")
            dz = flat(dz)
            g["a_w_in"] = _mm(sv["hn"], dz, "tn", f"{tag}_a_dwin")
            dhn = _mm(dz, W["a_w_in"][j], "nt", f"{tag}_a_dhn")
        elif kind == 1:
            dy = _mm(dx[1],W["b_w_out"][j], "nt", f"{tag}_b_dy")
            g["b_w_out"] = _mm(sv["y"], dx[1], "tn", f"{tag}_b_dwout")
            dz, g["b_ln_g"], g["b_ln_b"], g["b_w_s"], dbsf = _b_core_bwd(
                sv["z"], dy, W["b_ln_g"][j], W["b_ln_b"][j], W["b_w_s"][j], sv["bsf"], f"{tag}_b_core_bwd")
            g["b_b_s"] = dbsf.reshape(SGU_CHUNK, H, LANES).sum(-1).T
            g["b_w_in"] = _mm(sv["hn"], dz, "tn", f"{tag}_b_dwin")
            dhn = _mm(dz, W["b_w_in"][j], "nt", f"{tag}_b_dhn")
        else:
            dy = _mm(dx[1],W["c_w_out"][j], "nt", f"{tag}_c_dy")
            g["c_w_out"] = _mm(sv["y"], dx[1], "tn", f"{tag}_c_dwout")
            do, dzg, g["c_norm_g"] = _c_post_bwd(sv["o"], seq(sv["z"]), W["c_norm_g"][j], seq(dy), f"{tag}_c_post_bwd")
            dq, dk, dv, dgbp = _c_mid_bwd(sv["q"], sv["k"], sv["v"], sv["gbp"], do, sv["mid"], f"{tag}_c_mid_bwd")
            dzq, dzk, dzv, dzs_h, dcw3, dpc = _c_pre_bwd(seq(sv["z"]), seq(sv["zs"]), sv["cw3"], sv["pc"], dq, dk, dv, dgbp,
                                                         f"{tag}_c_pre_bwd")
            dz = flat(jnp.concatenate([dzq, dzk, dzv, dzg], axis=-1))
            dzs = flat(dzs_h.sum(0)).astype(BF)
            g["c_conv_w"] = dcw3.transpose(1, 0, 2).reshape(CONV_W, 3 * D)
            dpc = dpc.sum(-1)
            g["c_a_log"], g["c_dt_bias"] = dpc[:, :2].T, dpc[:, 2:].T
            dw_main = _mm(sv["hn"], dz, "tn", f"{tag}_c_dwin")
            dw_small = _mm(sv["hn"], dzs, "tn", f"{tag}_c_dwin_small")
            g["c_w_in"] = jnp.concatenate([dw_main, dw_small[:, :4 * H]], axis=1)
            dhn = _mm(dz, W["c_w_in"][j][:, :4 * D], "nt", f"{tag}_c_dhn")
            dhn = _mm(dzs, sv["w_small"], "nt", f"{tag}_c_dhn_small", epi="add", extra=dhn)
        dx, per_layer["norm_mix_g"][i] = _rms_bwd(sv["x"], W["norm_mix_g"][i], dhn, dx[0], f"{tag}_mix_norm_bwd")
        for n, val in g.items():
            mixer.setdefault(n, {})[j] = val

    for n, vals in per_layer.items():
        if n not in ("mlp_w_up", "mlp_w_down"):
            G[n] = jnp.stack(vals)
        elif by_chip:
            G[n] = vals
        elif n == "mlp_w_up":
            G[n] = jnp.stack(vals).transpose(0, 2, 1, 3).reshape(L, D, F)
        else:
            G[n] = jnp.stack(vals).reshape(L, F, D)
    for n, by_j in mixer.items():
        G[n] = jnp.stack([by_j[j] for j in sorted(by_j)])
    return loss, dx[0].reshape(Bq, S, D), G


MESH = pl.DeviceIdType.MESH
N_CHIPS = 4
HBM_SPEC = pl.BlockSpec(memory_space=pltpu.HBM)


def _place():
    x, y, c = lax.axis_index("x"), lax.axis_index("y"), lax.axis_index("c")
    others = [(1 - x, y), (x, 1 - y), (1 - x, 1 - y)]
    return x, y, c, others


def _all_gather_xy(bufs, name):
    n = len(bufs)
    pieces = [_stage_rows(b.shape[0], b.shape[1] * b.dtype.itemsize) for b in bufs]

    def body(*refs):
        ins, outs = refs[:n], refs[n:2 * n]
        send, recv, fsend, frecv = refs[2 * n:2 * n + 4]
        stages = refs[2 * n + 4:]
        x, y, c, others = _place()
        p = 2 * x + y
        half = lambda b, cc: pl.ds(cc * (ins[b].shape[0] // 2), ins[b].shape[0] // 2)

        def ici(b, j):
            qx, qy = others[j]
            return pltpu.make_async_remote_copy(
                src_ref=ins[b].at[half(b, c)], dst_ref=outs[b].at[p, half(b, c)], send_sem=send.at[b, j],
                recv_sem=recv.at[b, j], device_id=(qx, qy, c), device_id_type=MESH)

        def landed(b, j, cc):
            qx, qy = others[j]
            return outs[b].at[2 * qx + qy, half(b, cc)]

        def d2d(b, j):
            return pltpu.make_async_remote_copy(
                src_ref=landed(b, j, c), dst_ref=landed(b, j, c), send_sem=fsend.at[b, j], recv_sem=frecv.at[b, j],
                device_id=(x, y, 1 - c), device_id_type=MESH)

        pairs = [(b, j) for b in range(n) for j in range(3)]
        for b, j in pairs:
            ici(b, j).start()
        for b in range(n):
            def own_piece(i, carry, b=b):
                rows = pl.ds(pl.multiple_of(i * pieces[b], pieces[b]), pieces[b])
                pltpu.sync_copy(ins[b].at[rows], stages[b])
                pltpu.sync_copy(stages[b], outs[b].at[p, rows])
                return carry

            lax.fori_loop(0, ins[b].shape[0] // pieces[b], own_piece, 0)
        for b, j in pairs:
            pltpu.make_async_remote_copy(
                src_ref=ins[b].at[half(b, c)], dst_ref=landed(b, j, c), send_sem=send.at[b, j], recv_sem=recv.at[b, j],
                device_id=(x, y, c), device_id_type=MESH).wait_recv()
            d2d(b, j).start()
        for b, j in pairs:
            pltpu.make_async_remote_copy(
                src_ref=landed(b, j, 1 - c), dst_ref=landed(b, j, 1 - c), send_sem=fsend.at[b, j], recv_sem=frecv.at[b, j],
                device_id=(x, y, 1 - c), device_id_type=MESH).wait_recv()
        for b, j in pairs:
            ici(b, j).wait_send()
            d2d(b, j).wait_send()

    return pl.pallas_call(
        body, name=name, in_specs=[HBM_SPEC] * n, out_specs=[HBM_SPEC] * n,
        out_shape=[jax.ShapeDtypeStruct((N_CHIPS,) + b.shape, b.dtype) for b in bufs],
        scratch_shapes=[pltpu.SemaphoreType.DMA((n, 3))] * 4
        + [pltpu.VMEM((r, b.shape[1]), b.dtype) for r, b in zip(pieces, bufs)],
        compiler_params=pltpu.CompilerParams(has_side_effects=True, vmem_limit_bytes=VMEM_LIMIT))(*bufs)


STAGE_BYTES = 2 * 1024 * 1024


def _stage_rows(rows, row_bytes):
    for d in range(min(rows, max(1, STAGE_BYTES // row_bytes)), 0, -1):
        if rows % d == 0 and (d % 16 == 0 or d == rows):
            return d
    return rows


def _swap_halves(gs, name):
    n = len(gs)

    def body(*refs):
        g_refs, o_refs, send, recv = refs[:n], refs[n:2 * n], refs[2 * n], refs[2 * n + 1]
        x, y, c, _ = _place()
        cps = [pltpu.make_async_remote_copy(src_ref=g_refs[b].at[:, 1 - c], dst_ref=o_refs[b], send_sem=send.at[b],
                                            recv_sem=recv.at[b], device_id=(x, y, 1 - c), device_id_type=MESH)
               for b in range(n)]
        for cp in cps:
            cp.start()
        for cp in cps:
            cp.wait()

    return pl.pallas_call(
        body, name=name, in_specs=[HBM_SPEC] * n, out_specs=[HBM_SPEC] * n,
        out_shape=[jax.ShapeDtypeStruct((g.shape[0],) + g.shape[2:], g.dtype) for g in gs],
        scratch_shapes=[pltpu.SemaphoreType.DMA((n,)), pltpu.SemaphoreType.DMA((n,))],
        compiler_params=pltpu.CompilerParams(has_side_effects=True))(*gs)


def _pair_sum(g, got, out_dtype, name):
    nq, _, h, cols = g.shape
    tr = min(512, h)

    def body(c_ref, g_ref, r_ref, o_ref):
        o_ref[...] = (g_ref[...] + r_ref[...]).astype(o_ref.dtype)

    spec = pl.BlockSpec((None, tr, cols), lambda q, i, c_ref: (q, i, 0))
    return pl.pallas_call(
        body, name=name,
        grid_spec=pltpu.PrefetchScalarGridSpec(
            num_scalar_prefetch=1, grid=(nq, h // tr),
            in_specs=[pl.BlockSpec((None, None, tr, cols), lambda q, i, c_ref: (q, c_ref[0], i, 0)), spec], out_specs=spec),
        out_shape=jax.ShapeDtypeStruct((nq, h, cols), out_dtype),
        compiler_params=_params(("parallel", "parallel")))(lax.axis_index("c").astype(jnp.int32).reshape(1), g, got)


def _scatter_xy(p_sums, name):
    n = len(p_sums)

    def body(*refs):
        p_refs, o_refs, send, recv = refs[:n], refs[n:2 * n], refs[2 * n], refs[2 * n + 1]
        x, y, c, others = _place()
        me = 2 * x + y
        cps = []
        for b in range(n):
            for j, (qx, qy) in enumerate(others):
                cps.append(pltpu.make_async_remote_copy(
                    src_ref=p_refs[b].at[2 * qx + qy], dst_ref=o_refs[b].at[me], send_sem=send.at[b, j],
                    recv_sem=recv.at[b, j], device_id=(qx, qy, c), device_id_type=MESH))
                cps[-1].start()
        for b in range(n):
            for j, (qx, qy) in enumerate(others):
                pltpu.make_async_remote_copy(
                    src_ref=p_refs[b].at[me], dst_ref=o_refs[b].at[2 * qx + qy], send_sem=send.at[b, j],
                    recv_sem=recv.at[b, j], device_id=(qx, qy, c), device_id_type=MESH).wait_recv()
        for cp in cps:
            cp.wait_send()

    return pl.pallas_call(
        body, name=name, in_specs=[HBM_SPEC] * n, out_specs=[HBM_SPEC] * n,
        out_shape=[jax.ShapeDtypeStruct(p.shape, p.dtype) for p in p_sums],
        scratch_shapes=[pltpu.SemaphoreType.DMA((n, 3)), pltpu.SemaphoreType.DMA((n, 3))],
        compiler_params=pltpu.CompilerParams(has_side_effects=True))(*p_sums)


def _chip_sum(r4, p_sum, name):
    nq, h, cols = r4.shape
    tr = min(512, h)

    def body(r_ref, p_ref, o_ref):
        me = 2 * lax.axis_index("x") + lax.axis_index("y")
        f = lambda q: jnp.where(me == q, p_ref[q], r_ref[q]).astype(F32)
        o_ref[...] = ((f(0) + f(1)) + f(2)) + f(3)

    spec = pl.BlockSpec((nq, tr, cols), lambda i: (0, i, 0))
    return pl.pallas_call(
        body, name=name, grid=(h // tr,), in_specs=[spec, spec],
        out_specs=pl.BlockSpec((tr, cols), lambda i: (i, 0)), out_shape=jax.ShapeDtypeStruct((h, cols), F32),
        compiler_params=_params(("parallel",)))(r4, p_sum)


def _join_halves(rs, name):
    n = len(rs)

    def body(*refs):
        r_refs, o_refs, send, recv = refs[:n], refs[n:2 * n], refs[2 * n], refs[2 * n + 1]
        x, y, c, _ = _place()
        cps = [pltpu.make_async_remote_copy(src_ref=r_refs[b], dst_ref=o_refs[b].at[c], send_sem=send.at[b],
                                            recv_sem=recv.at[b], device_id=(x, y, 1 - c), device_id_type=MESH)
               for b in range(n)]
        for cp in cps:
            cp.start()
        for b in range(n):
            pltpu.make_async_remote_copy(src_ref=r_refs[b], dst_ref=o_refs[b].at[1 - c], send_sem=send.at[b],
                                         recv_sem=recv.at[b], device_id=(x, y, 1 - c), device_id_type=MESH).wait_recv()
        for cp in cps:
            cp.wait_send()

    gots = pl.pallas_call(
        body, name=name, in_specs=[HBM_SPEC] * n, out_specs=[HBM_SPEC] * n,
        out_shape=[jax.ShapeDtypeStruct((2,) + r.shape, r.dtype) for r in rs],
        scratch_shapes=[pltpu.SemaphoreType.DMA((n,)), pltpu.SemaphoreType.DMA((n,))],
        compiler_params=pltpu.CompilerParams(has_side_effects=True))(*rs)
    c = lax.axis_index("c")
    return [jnp.stack([jnp.where(c == s, r, got[s]) for s in range(2)]) for r, got in zip(rs, gots)]


def _reduce_scatter(gs, tag):
    gs = [g.reshape(g.shape[0], 2, g.shape[1] // 2, g.shape[2]) for g in gs]
    gots = _swap_halves(gs, f"{tag}_swap")
    pairs = [_pair_sum(g, got, BF, f"{tag}_pair_sum{i}") for i, (g, got) in enumerate(zip(gs, gots))]
    r4s = _scatter_xy(pairs, f"{tag}_scatter")
    rs = [_chip_sum(r4, pair, f"{tag}_chip_sum{i}") for i, (r4, pair) in enumerate(zip(r4s, pairs))]
    return [j.reshape(-1, j.shape[-1]) for j in _join_halves(rs, f"{tag}_join")]


def _adamw(w, g, m, v, name):
    rows, cols = w.shape
    tr = rows
    for cand in (512, 344, 256, 128, 64, 32, 16, 8):
        if rows % cand == 0:
            tr = cand
            break

    def body(w_ref, g_ref, m_ref, v_ref, d_ref, nm_ref, nv_ref):
        g_ = g_ref[...]
        m_ = ADAM_B1 * m_ref[...] + (1.0 - ADAM_B1) * g_
        v_ = ADAM_B2 * v_ref[...] + (1.0 - ADAM_B2) * jnp.square(g_)
        m_hat = m_ / (1.0 - ADAM_B1 ** ADAM_STEP)
        v_hat = v_ / (1.0 - ADAM_B2 ** ADAM_STEP)
        d_ref[...] = -ADAM_LR * (m_hat / (jnp.sqrt(v_hat) + ADAM_EPS) + ADAM_WD * w_ref[...])
        nm_ref[...] = m_
        nv_ref[...] = v_

    spec = pl.BlockSpec((tr, cols), lambda i: (i, 0))
    shp = jax.ShapeDtypeStruct((rows, cols), F32)
    return pl.pallas_call(body, name=name, grid=(rows // tr,), in_specs=[spec] * 4, out_specs=[spec] * 3,
                          out_shape=[shp] * 3, compiler_params=_params(("parallel",)))(w, g, m, v)


WEIGHTS = ["norm_mix_g", "norm_mlp_g", "mlp_w_up", "mlp_w_down", "norm_final_g", "a_w_in", "a_conv_w", "a_conv_b",
           "a_gate_w", "a_gate_b", "a_lambda", "a_w_out", "b_w_in", "b_ln_g", "b_ln_b", "b_w_s", "b_b_s", "b_w_out",
           "c_w_in", "c_conv_w", "c_a_log", "c_dt_bias", "c_norm_g", "c_w_out"]
SHARD_AXIS = {"mlp_w_up": 2, "mlp_w_down": 1, "a_w_in": 2, "a_conv_w": 2, "a_conv_b": 1, "a_lambda": 2, "a_w_out": 1,
              "b_w_in": 2, "b_w_out": 1, "c_w_in": 2, "c_conv_w": 2, "c_w_out": 1}
MATMUL_WEIGHTS = ["mlp_w_up", "mlp_w_down", "a_w_in", "a_w_out", "b_w_in", "b_w_out", "c_w_out", "c_w_in"]
SMALL_SHARDED = ["a_conv_w", "a_conv_b", "a_lambda", "c_conv_w"]
REPLICATED = [n for n in WEIGHTS if n not in SHARD_AXIS]
FLAT_COLS = 1024


def _rows_of(shape):
    n = 1
    for s in shape:
        n *= s
    return -(-n // FLAT_COLS)


def _pack(arrays, total_rows, dtype):
    parts = []
    used = 0
    for a in arrays:
        r = _rows_of(a.shape)
        f = a.reshape(-1).astype(dtype)
        parts.append(jnp.pad(f, (0, r * FLAT_COLS - f.shape[0])).reshape(r, FLAT_COLS))
        used += r
    if total_rows > used:
        parts.append(jnp.zeros((total_rows - used, FLAT_COLS), dtype))
    return jnp.concatenate(parts, axis=0)


def _slots_of(g, axis):
    w = g.shape[axis] // N_CHIPS
    flat = jnp.stack([lax.slice_in_dim(g, q * w, (q + 1) * w, axis=axis).reshape(-1) for q in range(N_CHIPS)])
    rows = _rows_of(flat.shape[1:])
    return jnp.pad(flat, ((0, 0), (0, rows * FLAT_COLS - flat.shape[1]))).reshape(N_CHIPS, rows, FLAT_COLS)


def _unpack(buf, shapes):
    out, r0 = [], 0
    for shp in shapes:
        r = _rows_of(shp)
        n = 1
        for s in shp:
            n *= s
        out.append(buf[r0:r0 + r].reshape(-1)[:n].reshape(shp))
        r0 += r
    return out


def _round_up(n, m):
    return -(-n // m) * m


def kernel(x, norm_mix_g, norm_mlp_g, mlp_w_up, mlp_w_down, norm_final_g, a_w_in, a_conv_w, a_conv_b, a_gate_w, a_gate_b, a_lambda, a_w_out, b_w_in, b_ln_g, b_ln_b, b_w_s, b_b_s, b_w_out, c_w_in, c_conv_w, c_a_log, c_dt_bias, c_norm_g, c_w_out, loss_target, m_norm_mix_g, m_norm_mlp_g, m_mlp_w_up, m_mlp_w_down, m_norm_final_g, m_a_w_in, m_a_conv_w, m_a_conv_b, m_a_gate_w, m_a_gate_b, m_a_lambda, m_a_w_out, m_b_w_in, m_b_ln_g, m_b_ln_b, m_b_w_s, m_b_b_s, m_b_w_out, m_c_w_in, m_c_conv_w, m_c_a_log, m_c_dt_bias, m_c_norm_g, m_c_w_out, v_norm_mix_g, v_norm_mlp_g, v_mlp_w_up, v_mlp_w_down, v_norm_final_g, v_a_w_in, v_a_conv_w, v_a_conv_b, v_a_gate_w, v_a_gate_b, v_a_lambda, v_a_w_out, v_b_w_in, v_b_ln_g, v_b_ln_b, v_b_w_s, v_b_b_s, v_b_w_out, v_c_w_in, v_c_conv_w, v_c_a_log, v_c_dt_bias, v_c_norm_g, v_c_w_out):
    given = dict(locals())
    w_loc = {n: given[n] for n in WEIGHTS}
    m_loc = {n: given["m_" + n] for n in WEIGHTS}
    v_loc = {n: given["v_" + n] for n in WEIGHTS}

    mlp = ["mlp_w_up", "mlp_w_down"]
    rest = [n for n in MATMUL_WEIGHTS if n not in mlp]
    rest_rows = _round_up(sum(_rows_of(w_loc[n].shape) for n in rest), 512)
    small_rows = _round_up(sum(_rows_of(w_loc[n].shape) for n in SMALL_SHARDED), 16)
    as_rows = lambda w: w.astype(BF).reshape(-1, w.shape[-1])
    up_all, down_all, rest_all, small_all = _all_gather_xy(
        [as_rows(mlp_w_up), as_rows(mlp_w_down), _pack([w_loc[n] for n in rest], rest_rows, BF),
         _pack([w_loc[n] for n in SMALL_SHARDED], small_rows, F32)], "gather_weights")
    W = {n: w_loc[n] for n in REPLICATED}
    W["mlp_up_slots"] = up_all.reshape((N_CHIPS,) + mlp_w_up.shape)
    W["mlp_down_slots"] = down_all.reshape((N_CHIPS,) + mlp_w_down.shape)
    for names, buf in ((rest, rest_all), (SMALL_SHARDED, small_all)):
        per_chip = [_unpack(buf[q], [w_loc[n].shape for n in names]) for q in range(N_CHIPS)]
        for i, n in enumerate(names):
            W[n] = jnp.concatenate([per_chip[q][i] for q in range(N_CHIPS)], axis=SHARD_AXIS[n])

    loss, grad_x, G = _local_step(x, loss_target, W, by_chip=True)
    loss = lax.psum(loss, ("x", "y", "c"))

    sharded = rest + SMALL_SHARDED
    rep_rows = _round_up(sum(_rows_of(w_loc[n].shape) for n in REPLICATED), N_CHIPS * 16)
    rep_flat = _pack([G[n] for n in REPLICATED], rep_rows, F32).reshape(N_CHIPS, rep_rows // N_CHIPS, FLAT_COLS)
    shard_rows = sum(_rows_of(w_loc[n].shape) for n in sharded)
    total_rows = _round_up(shard_rows + rep_rows // N_CHIPS, 1024)
    parts = [_slots_of(G[n], SHARD_AXIS[n]) for n in sharded] + [rep_flat]
    parts.append(jnp.zeros((N_CHIPS, total_rows - shard_rows - rep_rows // N_CHIPS, FLAT_COLS), F32))
    mlp_bufs = [g.reshape(N_CHIPS, -1, g.shape[-1]) for n in mlp for g in G[n]]
    *mlp_red, red = _reduce_scatter(mlp_bufs + [jnp.concatenate(parts, axis=1)], "grads")
    n_layers = mlp_w_up.shape[0]
    g_loc = {"mlp_w_up": jnp.stack(mlp_red[:n_layers]).reshape(mlp_w_up.shape),
             "mlp_w_down": jnp.stack(mlp_red[n_layers:]).reshape(mlp_w_down.shape)}
    g_loc.update(zip(sharded, _unpack(red, [w_loc[n].shape for n in sharded])))
    rep_quarter = red[shard_rows:shard_rows + rep_rows // N_CHIPS]
    (rep_all,) = _all_gather_xy([rep_quarter], "gather_replicated_grads")
    g_loc.update(zip(REPLICATED, _unpack(rep_all.reshape(rep_rows, FLAT_COLS), [w_loc[n].shape for n in REPLICATED])))

    delta, new_m, new_v = {}, {}, {}
    big = [n for n in MATMUL_WEIGHTS if w_loc[n].size % FLAT_COLS == 0]
    for n in big:
        shp = w_loc[n].shape
        two_d = lambda a: a.reshape(-1, FLAT_COLS)
        d, nm, nv = _adamw(two_d(w_loc[n]), two_d(g_loc[n]), two_d(m_loc[n]), two_d(v_loc[n]), f"adamw_{n}")
        delta[n], new_m[n], new_v[n] = d.reshape(shp), nm.reshape(shp), nv.reshape(shp)
    small = [n for n in WEIGHTS if n not in big]
    small_shapes = [w_loc[n].shape for n in small]
    rows = _round_up(sum(_rows_of(s) for s in small_shapes), 8)
    packed = [_pack([src[n] for n in small], rows, F32) for src in (w_loc, g_loc, m_loc, v_loc)]
    for dst, buf in zip((delta, new_m, new_v), _adamw(*packed, "adamw_small")):
        dst.update(zip(small, _unpack(buf, small_shapes)))

    return (loss, grad_x, *[g_loc[n] for n in WEIGHTS], *[delta[n] for n in WEIGHTS],
            *[new_m[n] for n in WEIGHTS], *[new_v[n] for n in WEIGHTS])
```
